```python
import jax, jax.numpy as jnp
from jax import lax
import numpy as np


D_MODEL = 2048
BATCH = 8
SEQ = 4096
DEPTH = 1
DEC_BATCH = 32
DEC_SEQ = 64
PAST_LEN = 2048

CHUNK = 64
EPS = 1e-6
D_POOL = D_MODEL // 2
N_POOL_GROUPS = 4
POOL_GROUP = D_POOL // N_POOL_GROUPS
POOL_WINDOWS = (2, 4, 8, 16)
POOL_STATE = max(POOL_WINDOWS) - 1
POOL_OUT_GROUP = D_MODEL // N_POOL_GROUPS
SB_HEAD_DIM = 64
SB_HEADS = (D_MODEL // 2) // SB_HEAD_DIM
D_SB = SB_HEADS * SB_HEAD_DIM
SB_SCALE = SB_HEAD_DIM ** -0.5
Q_BLOCK = 128
D_IN = D_POOL + 3 * D_SB + 2 * D_MODEL
SPLITS = (D_POOL, D_POOL + D_SB, D_POOL + 2 * D_SB, D_POOL + 3 * D_SB, D_POOL + 3 * D_SB + D_MODEL)
N_EXPERT_GROUPS = 4
EXPERTS_PER_GROUP = 8
N_EXPERTS = N_EXPERT_GROUPS * EXPERTS_PER_GROUP
TOP_K_IN_GROUP = 2
D_EXPERT = D_MODEL // 4
MOE_BLOCK = 128

kernel_name = 'hybrid_pool_stickbreak_hmoe_stream_step'


def rmsnorm(x, g):
    xf = x.astype(jnp.float32)
    r = lax.rsqrt(jnp.mean(xf * xf, axis=-1, keepdims=True) + EPS)
    return (xf * r).astype(x.dtype) * g


def pool_mix(u_ext, pos0, w_pool, pool_scale):
    B, R, _ = u_ext.shape
    n = R - POOL_STATE
    uf = u_ext.astype(jnp.float32)
    cs = jnp.concatenate([jnp.zeros((B, 1, D_POOL), jnp.float32), jnp.cumsum(uf, axis=1)], axis=1)
    end = cs[:, POOL_STATE + 1:]
    pos = pos0 + jnp.arange(n)
    outs = []
    for g, w in enumerate(POOL_WINDOWS):
        lo, hi = g * POOL_GROUP, (g + 1) * POOL_GROUP
        start = cs[:, POOL_STATE + 1 - w: POOL_STATE + 1 - w + n, lo:hi]
        cnt = jnp.minimum(pos + 1, w).astype(jnp.float32)[None, :, None]
        diff = (end[..., lo:hi] - start) / cnt - uf[:, POOL_STATE:, lo:hi]
        outs.append(jnp.einsum('bnc,co->bno', diff.astype(u_ext.dtype), w_pool[g]))
    return jnp.concatenate(outs, axis=-1) * pool_scale


def stick_breaking_attend(q, k, v, pos0):
    B, Tq = q.shape[:2]
    Tk = k.shape[1]
    bq = min(Tq, Q_BLOCK)
    nb = Tq // bq
    qb = jnp.moveaxis(q.reshape(B, nb, bq, SB_HEADS, SB_HEAD_DIM), 1, 0)
    q_pos = (pos0 + jnp.arange(Tq)).reshape(nb, bq)
    k_pos = jnp.arange(Tk)

    def block(args):
        qi, qp = args
        z = jnp.einsum('bqhd,bkhd->bhqk', qi, k).astype(jnp.float32) * SB_SCALE
        mask = k_pos[None, :] < qp[:, None]
        log_keep = jnp.where(mask, -jax.nn.softplus(z), 0.0)
        later = lax.cumsum(log_keep, axis=3, reverse=True) - log_keep
        a = jnp.where(mask, jnp.exp(jax.nn.log_sigmoid(z) + later), 0.0)
        return jnp.einsum('bhqk,bkhd->bqhd', a.astype(v.dtype), v)

    o = lax.map(block, (qb, q_pos))
    return jnp.moveaxis(o, 0, 1).reshape(B, Tq, D_SB)


def hier_moe(h, w_rg, b_rg, w_re, b_re, w_g, w_u, w_d):
    B, L, D = h.shape
    T = B * L
    xt = h.reshape(T, D)
    g_logits = (xt @ w_rg).astype(jnp.float32) + b_rg.astype(jnp.float32)
    g_prob = jax.nn.softmax(g_logits, axis=-1)
    grp = jnp.argmax(g_logits, axis=-1)
    p_grp = jnp.take_along_axis(g_prob, grp[:, None], axis=1)
    e_logits = ((xt @ w_re).astype(jnp.float32) + b_re.astype(jnp.float32)).reshape(T, N_EXPERT_GROUPS, EXPERTS_PER_GROUP)
    e_logits = jnp.take_along_axis(e_logits, grp[:, None, None], axis=1)[:, 0]
    top_p, top_i = lax.top_k(jax.nn.softmax(e_logits, axis=-1), TOP_K_IN_GROUP)
    wts = p_grp * top_p / jnp.sum(top_p, axis=-1, keepdims=True)
    flat_e = (grp[:, None] * EXPERTS_PER_GROUP + top_i).reshape(-1).astype(jnp.int32)
    A = T * TOP_K_IN_GROUP
    order = jnp.argsort(flat_e)
    sorted_e = flat_e[order]
    tok = (order // TOP_K_IN_GROUP).astype(jnp.int32)
    w_sorted = wts.reshape(-1)[order]
    sizes = jnp.bincount(flat_e, length=N_EXPERTS)
    padded = (sizes + MOE_BLOCK - 1) // MOE_BLOCK * MOE_BLOCK
    pad_end = jnp.cumsum(padded)
    pad_start = pad_end - padded
    raw_start = jnp.cumsum(sizes) - sizes
    dest = pad_start[sorted_e] + jnp.arange(A) - raw_start[sorted_e]
    n_blocks = -(-A // MOE_BLOCK) + N_EXPERTS
    n_slots = n_blocks * MOE_BLOCK
    slot_tok = jnp.full((n_slots,), T, jnp.int32).at[dest].set(tok)
    slot_w = jnp.zeros((n_slots,), jnp.float32).at[dest].set(w_sorted)
    block_expert = jnp.minimum(jnp.searchsorted(pad_end, jnp.arange(n_blocks) * MOE_BLOCK, side='right'), N_EXPERTS - 1)
    x_pad = jnp.concatenate([xt, jnp.zeros((1, D), xt.dtype)], axis=0)
    xb = x_pad[slot_tok].reshape(n_blocks, MOE_BLOCK, D)

    def expert_block(args):
        e, xi = args
        return (jax.nn.silu(xi @ w_g[e]) * (xi @ w_u[e])) @ w_d[e]

    yb = lax.map(expert_block, (block_expert, xb)).reshape(n_slots, D)
    yb = yb * slot_w[:, None].astype(yb.dtype)
    out = jax.ops.segment_sum(yb, slot_tok, num_segments=T + 1)[:T]
    return out.reshape(B, L, D)


def hybrid_layer(x, pool_hist, k_hist, v_hist, norm_mix, w_in, w_pool, pool_scale, w_sb_out, w_out,
                 norm_ffn, w_rg, b_rg, w_re, b_re, w_g, w_u, w_d):
    B, L, _ = x.shape
    pos0 = k_hist.shape[1]
    h = rmsnorm(x, norm_mix)
    proj = h @ w_in
    u, q, k, v, gp, gs = jnp.split(proj, SPLITS, axis=-1)
    q = q.reshape(B, L, SB_HEADS, SB_HEAD_DIM)
    k = k.reshape(B, L, SB_HEADS, SB_HEAD_DIM)
    v = v.reshape(B, L, SB_HEADS, SB_HEAD_DIM)
    u_ext = jnp.concatenate([pool_hist, u], axis=1)
    pool_out = pool_mix(u_ext, pos0, w_pool, pool_scale)
    k_all = jnp.concatenate([k_hist, k], axis=1)
    v_all = jnp.concatenate([v_hist, v], axis=1)
    sb_out = stick_breaking_attend(q, k_all, v_all, pos0) @ w_sb_out
    mixed = jax.nn.sigmoid(gp) * pool_out + jax.nn.sigmoid(gs) * sb_out
    x = x + mixed @ w_out
    x = x + hier_moe(rmsnorm(x, norm_ffn), w_rg, b_rg, w_re, b_re, w_g, w_u, w_d)
    return x, k, v, u_ext[:, -POOL_STATE:]


def setup_inputs(seed: int = 0) -> dict:
    key = jax.random.key(seed)
    ks = jax.random.split(key, 20)

    def nrm(k, shape, scale):
        return jax.random.normal(k, shape, jnp.float32) * scale

    return {
        'x_prompt': nrm(ks[0], (BATCH, SEQ, D_MODEL), 1.0),
        'x_sample': nrm(ks[1], (DEC_BATCH, DEC_SEQ, D_MODEL), 1.0),
        'cache_sb_k': nrm(ks[2], (DEPTH, DEC_BATCH, PAST_LEN, SB_HEADS, SB_HEAD_DIM), 1.0),
        'cache_sb_v': nrm(ks[3], (DEPTH, DEC_BATCH, PAST_LEN, SB_HEADS, SB_HEAD_DIM), 1.0),
        'state_pool': nrm(ks[4], (DEPTH, DEC_BATCH, POOL_STATE, D_POOL), 1.0),
        'norm_mix': 1.0 + nrm(ks[5], (DEPTH, D_MODEL), 0.02),
        'w_in': nrm(ks[6], (DEPTH, D_MODEL, D_IN), D_MODEL ** -0.5),
        'w_pool': nrm(ks[7], (DEPTH, N_POOL_GROUPS, POOL_GROUP, POOL_OUT_GROUP), POOL_GROUP ** -0.5),
        'pool_scale': 1.0 + nrm(ks[8], (DEPTH, D_MODEL), 0.02),
        'w_sb_out': nrm(ks[9], (DEPTH, D_SB, D_MODEL), D_SB ** -0.5),
        'w_out': nrm(ks[10], (DEPTH, D_MODEL, D_MODEL), D_MODEL ** -0.5),
        'norm_ffn': 1.0 + nrm(ks[11], (DEPTH, D_MODEL), 0.02),
        'w_router_group': nrm(ks[12], (DEPTH, D_MODEL, N_EXPERT_GROUPS), D_MODEL ** -0.5),
        'b_router_group': nrm(ks[13], (DEPTH, N_EXPERT_GROUPS), 0.01),
        'w_router_expert': nrm(ks[14], (DEPTH, D_MODEL, N_EXPERTS), D_MODEL ** -0.5),
        'b_router_expert': nrm(ks[15], (DEPTH, N_EXPERTS), 0.01),
        'w_exp_gate': nrm(ks[16], (DEPTH, N_EXPERTS, D_MODEL, D_EXPERT), D_MODEL ** -0.5),
        'w_exp_up': nrm(ks[17], (DEPTH, N_EXPERTS, D_MODEL, D_EXPERT), D_MODEL ** -0.5),
        'w_exp_down': nrm(ks[18], (DEPTH, N_EXPERTS, D_EXPERT, D_MODEL), D_EXPERT ** -0.5),
        'norm_final': 1.0 + nrm(ks[19], (D_MODEL,), 0.02),
    }


def reference(x_prompt, x_sample, cache_sb_k, cache_sb_v, state_pool, norm_mix, w_in, w_pool, pool_scale,
              w_sb_out, w_out, norm_ffn, w_router_group, b_router_group, w_router_expert, b_router_expert,
              w_exp_gate, w_exp_up, w_exp_down, norm_final):
    xp, xs = x_prompt, x_sample
    kp_l, vp_l, pp_l, ks_l, vs_l, ps_l = [], [], [], [], [], []
    for l in range(DEPTH):
        lw = (norm_mix[l], w_in[l], w_pool[l], pool_scale[l], w_sb_out[l], w_out[l], norm_ffn[l],
              w_router_group[l], b_router_group[l], w_router_expert[l], b_router_expert[l],
              w_exp_gate[l], w_exp_up[l], w_exp_down[l])
        pool_hist0 = jnp.zeros((xp.shape[0], POOL_STATE, D_POOL), xp.dtype)
        kv_hist0 = jnp.zeros((xp.shape[0], 0, SB_HEADS, SB_HEAD_DIM), xp.dtype)
        xp, kp, vp, pp = hybrid_layer(xp, pool_hist0, kv_hist0, kv_hist0, *lw)
        xs, ks, vs, ps = hybrid_layer(xs, state_pool[l], cache_sb_k[l], cache_sb_v[l], *lw)
        kp_l.append(kp)
        vp_l.append(vp)
        pp_l.append(pp)
        ks_l.append(ks)
        vs_l.append(vs)
        ps_l.append(ps)
    y_prompt = rmsnorm(xp, norm_final)
    y_sample = rmsnorm(xs, norm_final)
    new_k_prompt = jnp.stack(kp_l)
    new_v_prompt = jnp.stack(vp_l)
    new_pool_prompt = jnp.stack(pp_l)
    new_k_sample = jnp.stack(ks_l)
    new_v_sample = jnp.stack(vs_l)
    new_pool_sample = jnp.stack(ps_l)
    return (y_prompt, y_sample, new_k_prompt, new_v_prompt, new_pool_prompt, new_k_sample, new_v_sample, new_pool_sample)
```

```python
import functools

import jax
import jax.numpy as jnp
from jax import lax
from jax.experimental import pallas as pl
from jax.experimental.pallas import tpu as pltpu

F32 = jnp.float32
BF16 = jnp.bfloat16

EPS = 1e-6
HEAD_DIM = 64
LANES = 128
POOL_WINDOWS = (2, 4, 8, 16)
POOL_HIST = 16
N_GROUPS = 4
PER_GROUP = 8
N_EXPERTS = N_GROUPS * PER_GROUP
VMEM_LIMIT = 56 * 1024 * 1024
NEG = -1e30


def _cparams(sem, **kw):
    return pltpu.CompilerParams(dimension_semantics=sem, vmem_limit_bytes=VMEM_LIMIT, **kw)


def _inproj_body(x_ref, g_ref, w_ref, u_ref, q_ref, k_ref, v_ref, gp_ref, gs_ref, h_scr, *, d_half):
    j = pl.program_id(1)

    @pl.when(j == 0)
    def _():
        x = x_ref[...]
        r = lax.rsqrt(jnp.mean(x * x, axis=-1, keepdims=True) + EPS)
        h_scr[...] = ((x * r) * g_ref[...]).astype(BF16)

    acc = jnp.dot(h_scr[...], w_ref[...], preferred_element_type=F32)

    @pl.when(j == 0)
    def _():
        u_ref[...] = acc

    @pl.when(j == 1)
    def _():
        q_ref[...] = (acc * (HEAD_DIM ** -0.5)).astype(BF16)

    @pl.when(j == 2)
    def _():
        k_ref[...] = acc

    @pl.when(j == 3)
    def _():
        v_ref[...] = acc

    for jj, ref in ((4, gp_ref), (6, gs_ref)):
        for half in range(2):
            @pl.when(j == jj + half)
            def _(ref=ref, half=half):
                ref[:, half * d_half:(half + 1) * d_half] = jax.nn.sigmoid(acc).astype(BF16)


def _inproj(x, g, w_bf, tm):
    m, d = x.shape
    d_half = d // 2
    assert w_bf.shape == (d, 8 * d_half) and m % tm == 0
    row = lambda i, j: (i, 0)
    outs = [jax.ShapeDtypeStruct((m, d_half), F32), jax.ShapeDtypeStruct((m, d_half), BF16),
            jax.ShapeDtypeStruct((m, d_half), F32), jax.ShapeDtypeStruct((m, d_half), F32),
            jax.ShapeDtypeStruct((m, d), BF16), jax.ShapeDtypeStruct((m, d), BF16)]
    return pl.pallas_call(
        functools.partial(_inproj_body, d_half=d_half),
        out_shape=outs,
        grid=(m // tm, 8),
        in_specs=[pl.BlockSpec((tm, d), row),
                  pl.BlockSpec((1, d), lambda i, j: (0, 0)),
                  pl.BlockSpec((d, d_half), lambda i, j: (0, j))],
        out_specs=[pl.BlockSpec((tm, d_half), row)] * 4 + [pl.BlockSpec((tm, d), row)] * 2,
        scratch_shapes=[pltpu.VMEM((tm, d), BF16)],
        compiler_params=_cparams(("parallel", "arbitrary")),
        name="inproj",
    )(x, g, w_bf)


def _pool_body(hist_ref, u_ref, o_ref, ext_scr, *, pos0, tn):
    s = pl.program_id(1)

    @pl.when(s == 0)
    def _():
        ext_scr[0:POOL_HIST, :] = hist_ref[0]

    ext_scr[POOL_HIST:POOL_HIST + tn, :] = u_ref[0]
    pos = pos0 + s * tn + lax.broadcasted_iota(jnp.int32, (tn, 1), 0)
    group = u_ref.shape[2] // len(POOL_WINDOWS)
    for g, w in enumerate(POOL_WINDOWS):
        lo, hi = g * group, (g + 1) * group
        cur = ext_scr[POOL_HIST:POOL_HIST + tn, lo:hi]
        tot = cur
        for dlt in range(1, w):
            tot = tot + ext_scr[POOL_HIST - dlt:POOL_HIST - dlt + tn, lo:hi]
        cnt = jnp.minimum(pos + 1, w).astype(F32)
        o_ref[0, :, lo:hi] = (tot / cnt - cur).astype(BF16)
    ext_scr[0:POOL_HIST, :] = ext_scr[tn:tn + POOL_HIST, :]


def _pool_diff(hist, u, pos0, tn):
    b, n, dp = u.shape
    assert n % tn == 0 and hist.shape == (b, POOL_HIST, dp)
    return pl.pallas_call(
        functools.partial(_pool_body, pos0=pos0, tn=tn),
        out_shape=jax.ShapeDtypeStruct((b, n, dp), BF16),
        grid=(b, n // tn),
        in_specs=[pl.BlockSpec((1, POOL_HIST, dp), lambda i, s: (i, 0, 0)),
                  pl.BlockSpec((1, tn, dp), lambda i, s: (i, s, 0))],
        out_specs=pl.BlockSpec((1, tn, dp), lambda i, s: (i, s, 0)),
        scratch_shapes=[pltpu.VMEM((POOL_HIST + tn, dp), F32)],
        compiler_params=_cparams(("parallel", "arbitrary")),
        name="pool_diff",
    )(hist, u)


def _softplus(z):
    return jnp.maximum(z, 0.0) + jnp.log(1.0 + jnp.exp(-jnp.abs(z)))


def _suffix_matrix(n):
    r = lax.broadcasted_iota(jnp.int32, (n, n), 0)
    c = lax.broadcasted_iota(jnp.int32, (n, n), 1)
    return jnp.where(r >= c, 1.0, 0.0).astype(BF16)


def _sb_block(z, v_blk, suffix, carry, mask):
    sp = _softplus(z)
    if mask is not None:
        sp = jnp.where(mask, sp, 0.0)
    hi = sp.astype(BF16)
    lo = (sp - hi.astype(F32)).astype(BF16)
    c = (jnp.dot(hi, suffix, preferred_element_type=F32)
         + jnp.dot(lo, suffix, preferred_element_type=F32))
    arg = z - c - carry
    if mask is not None:
        arg = jnp.where(mask, arg, NEG)
    a = jnp.exp(arg).astype(BF16)
    out = jnp.dot(a, v_blk, preferred_element_type=F32)
    return out, carry + c[:, 0:1]


def _attn_body(*refs, tq, tk, past, has_hist):
    if has_hist:
        q_ref, kn_ref, vn_ref, kh_ref, vh_ref, o_ref, kt_scr, v_scr, acc_scr, car_scr = refs
    else:
        q_ref, kn_ref, vn_ref, o_ref, kt_scr, v_scr, acc_scr, car_scr = refs
        kh_ref = vh_ref = None
    qi = pl.program_id(2)
    n_fill = kt_scr.shape[0]
    n_hist_blocks = past // tk

    @pl.when(qi == 0)
    def _fill():
        def put(c, k_src, v_src, row0):
            kt_scr[c] = k_src[0, pl.ds(row0, tk), :].T.astype(BF16)
            v_scr[c] = v_src[0, pl.ds(row0, tk), :].astype(BF16)

        def body(c, _):
            if has_hist:
                @pl.when(c < n_hist_blocks)
                def _():
                    put(c, kh_ref, vh_ref, pl.multiple_of(c * tk, tk))

                if n_fill > n_hist_blocks:
                    @pl.when(c >= n_hist_blocks)
                    def _():
                        put(c, kn_ref, vn_ref, pl.multiple_of((c - n_hist_blocks) * tk, tk))
            else:
                put(c, kn_ref, vn_ref, pl.multiple_of(c * tk, tk))
            return 0

        lax.fori_loop(0, n_fill, body, 0)

    q2 = q_ref[0]
    lane = lax.broadcasted_iota(jnp.int32, (tq, LANES), 1)
    zero = jnp.zeros_like(q2)
    q_heads = (jnp.where(lane < HEAD_DIM, q2, zero), jnp.where(lane >= HEAD_DIM, q2, zero))

    row0 = pl.multiple_of(qi * tq, tq)
    kd = kn_ref[0, pl.ds(row0, tq), :].astype(BF16)
    vd = vn_ref[0, pl.ds(row0, tq), :].astype(BF16)
    r = lax.broadcasted_iota(jnp.int32, (tq, tq), 0)
    c = lax.broadcasted_iota(jnp.int32, (tq, tq), 1)
    mask = c < r
    suffix_d = _suffix_matrix(tq)
    for h in range(2):
        z = lax.dot_general(q_heads[h], kd, (((1,), (1,)), ((), ())), preferred_element_type=F32)
        out, carry = _sb_block(z, vd, suffix_d, jnp.zeros((tq, 1), F32), mask)
        acc_scr[h] = out
        car_scr[h] = jnp.broadcast_to(carry, (tq, LANES))

    n_prev = (past + qi * tq) // tk
    suffix_e = _suffix_matrix(tk)

    def prev(it, _):
        blk = n_prev - 1 - it
        kt = kt_scr[blk]
        vb = v_scr[blk]
        for h in range(2):
            z = jnp.dot(q_heads[h], kt, preferred_element_type=F32)
            out, carry = _sb_block(z, vb, suffix_e, car_scr[h][:, 0:1], None)
            acc_scr[h] += out
            car_scr[h] = jnp.broadcast_to(carry, (tq, LANES))
        return 0

    lax.fori_loop(0, n_prev, prev, 0)
    o_ref[0] = jnp.where(lane < HEAD_DIM, acc_scr[0], acc_scr[1]).astype(o_ref.dtype)


def _attention(q, k_new, v_new, k_hist, v_hist, tq, tk):
    b, n, dm = q.shape
    has_hist = k_hist is not None
    past = k_hist.shape[1] if has_hist else 0
    assert n % tq == 0 and past % tk == 0 and (tq == tk or n == tq) and dm % LANES == 0
    n_fill = max((past + n - tq) // tk, 1)
    blk = lambda rows: pl.BlockSpec((1, rows, LANES), lambda i, p, s: (i, 0, p))
    in_specs = [pl.BlockSpec((1, tq, LANES), lambda i, p, s: (i, s, p)), blk(n), blk(n)]
    args = [q, k_new, v_new]
    if has_hist:
        in_specs += [blk(past), blk(past)]
        args += [k_hist, v_hist]
    return pl.pallas_call(
        functools.partial(_attn_body, tq=tq, tk=tk, past=past, has_hist=has_hist),
        out_shape=jax.ShapeDtypeStruct((b, n, dm), BF16),
        grid=(b, dm // LANES, n // tq),
        in_specs=in_specs,
        out_specs=pl.BlockSpec((1, tq, LANES), lambda i, p, s: (i, s, p)),
        scratch_shapes=[pltpu.VMEM((n_fill, LANES, tk), BF16), pltpu.VMEM((n_fill, tk, LANES), BF16),
                        pltpu.VMEM((2, tq, LANES), F32), pltpu.VMEM((2, tq, LANES), F32)],
        compiler_params=_cparams(("parallel", "parallel", "arbitrary")),
        name="sb_attention",
    )(*args)


def _mix_body(diff_ref, o_ref, gp_ref, gs_ref, x_ref, wp_ref, ps_ref, wsb_ref, wo_ref, nf_ref,
              wrh_ref, wrl_ref, br_ref, x1_ref, hp_ref, route_ref, cnt_ref, *, tm):
    n_pool = wp_ref.shape[0]
    group = wp_ref.shape[1]
    diff = diff_ref[...]
    pool = jnp.concatenate(
        [jnp.dot(diff[:, g * group:(g + 1) * group], wp_ref[g], preferred_element_type=F32)
         for g in range(n_pool)], axis=-1) * ps_ref[...]
    sb = jnp.dot(o_ref[...], wsb_ref[...], preferred_element_type=F32)
    mixed = gp_ref[...].astype(F32) * pool + gs_ref[...].astype(F32) * sb
    x1 = x_ref[...] + jnp.dot(mixed.astype(BF16), wo_ref[...], preferred_element_type=F32)
    x1_ref[...] = x1
    h = (x1 * lax.rsqrt(jnp.mean(x1 * x1, axis=-1, keepdims=True) + EPS)) * nf_ref[...]

    d_half = h.shape[1] // 2
    lo_bits = pltpu.bitcast(h[:, :d_half].astype(BF16).astype(F32), jnp.uint32)
    hi_bits = pltpu.bitcast(h[:, d_half:].astype(BF16).astype(F32), jnp.uint32)
    hp_ref[...] = (lo_bits >> 16) | (hi_bits & jnp.uint32(0xFFFF0000))

    hh = h.astype(BF16)
    hl = (h - hh.astype(F32)).astype(BF16)
    logits = (jnp.dot(hh, wrh_ref[...], preferred_element_type=F32)
              + jnp.dot(hl, wrh_ref[...], preferred_element_type=F32)
              + jnp.dot(hh, wrl_ref[...], preferred_element_type=F32)) + br_ref[...]
    lane = lax.broadcasted_iota(jnp.int32, (tm, LANES), 1)
    big = jnp.int32(LANES)

    def first_max(vals):
        m = jnp.max(vals, axis=-1, keepdims=True)
        idx = jnp.min(jnp.where(vals == m, lane, big), axis=-1, keepdims=True)
        return m, idx

    gl = jnp.where(lane < N_GROUPS, logits, NEG)
    gmax, grp = first_max(gl)
    p_grp = 1.0 / jnp.sum(jnp.exp(gl - gmax), axis=-1, keepdims=True)
    e_lo = N_GROUPS + grp * PER_GROUP
    el = jnp.where((lane >= e_lo) & (lane < e_lo + PER_GROUP), logits, NEG)
    m1, i1 = first_max(el)
    m2, i2 = first_max(jnp.where(lane == i1, NEG, el))
    t2 = jnp.exp(m2 - m1)
    w1 = p_grp / (1.0 + t2)
    w2 = w1 * t2
    e1 = i1 - N_GROUPS
    e2 = i2 - N_GROUPS

    oh1 = jnp.where(lane == e1, 1.0, 0.0).astype(BF16)
    oh2 = jnp.where(lane == e2, 1.0, 0.0).astype(BF16)
    rr = lax.broadcasted_iota(jnp.int32, (tm, tm), 0)
    cc = lax.broadcasted_iota(jnp.int32, (tm, tm), 1)
    before = jnp.where(cc < rr, 1.0, 0.0).astype(BF16)
    ones = jnp.ones((8, tm), BF16)
    pre1 = jnp.dot(before, oh1, preferred_element_type=F32)
    pre2 = jnp.dot(before, oh2, preferred_element_type=F32)
    c1 = jnp.dot(ones, oh1, preferred_element_type=F32)
    c2 = jnp.dot(ones, oh2, preferred_element_type=F32)
    rank1 = jnp.sum(jnp.where(lane == e1, pre1, 0.0), axis=-1, keepdims=True)
    rank2 = jnp.sum(jnp.where(lane == e2, pre2 + c1[0:1, :], 0.0), axis=-1, keepdims=True)
    cnt_ref[0] = c1 + c2

    route = jnp.where(lane == 0, e1.astype(F32), 0.0)
    route = jnp.where(lane == 1, e2.astype(F32), route)
    route = jnp.where(lane == 2, w1, route)
    route = jnp.where(lane == 3, w2, route)
    route = jnp.where(lane == 4, rank1, route)
    route = jnp.where(lane == 5, rank2, route)
    route_ref[...] = route


def _mix(diff, o, gp, gs, x, wp, ps, wsb, wo, nf, wrh, wrl, br, tm):
    m, d = x.shape
    assert m % tm == 0
    row = lambda i: (i, 0)
    full = lambda a: pl.BlockSpec(a.shape, lambda i: (0,) * a.ndim, pipeline_mode=pl.Buffered(1))
    acts = [diff, o, gp, gs, x]
    consts = [wp, ps, wsb, wo, nf, wrh, wrl, br]
    return pl.pallas_call(
        functools.partial(_mix_body, tm=tm),
        out_shape=[jax.ShapeDtypeStruct((m, d), F32), jax.ShapeDtypeStruct((m, d // 2), jnp.uint32),
                   jax.ShapeDtypeStruct((m, LANES), F32), jax.ShapeDtypeStruct((m // tm, 8, LANES), F32)],
        grid=(m // tm,),
        in_specs=[pl.BlockSpec((tm, a.shape[1]), row) for a in acts] + [full(a) for a in consts],
        out_specs=[pl.BlockSpec((tm, d), row), pl.BlockSpec((tm, d // 2), row),
                   pl.BlockSpec((tm, LANES), row), pl.BlockSpec((1, 8, LANES), lambda i: (i, 0, 0))],
        compiler_params=_cparams(("parallel",)),
        name="mix_outproj_router",
    )(*acts, *consts)


def _row_wait(src_ref, dst_ref, sem, n_rows):
    pltpu.make_async_copy(src_ref.at[pl.ds(0, n_rows)], dst_ref.at[pl.ds(0, n_rows)], sem).wait()


def _dispatch_body(slot_ref, hp_ref, xs_in_ref, xs_ref, sem, *, tm):
    del xs_in_ref
    base = pl.program_id(0) * tm

    def body(t, _):
        src = hp_ref.at[pl.ds(base + t, 1)]
        for kk in range(2):
            dst = xs_ref.at[pl.ds(slot_ref[2 * (base + t) + kk], 1)]
            pltpu.make_async_copy(src, dst, sem).start()
        return 0

    lax.fori_loop(0, tm, body, 0)
    _row_wait(hp_ref, xs_ref, sem, 2 * tm)


def _dispatch(slots_flat, hp, xs_zero, tm):
    m = hp.shape[0]
    assert m % tm == 0
    any_spec = pl.BlockSpec(memory_space=pl.ANY)
    return pl.pallas_call(
        functools.partial(_dispatch_body, tm=tm),
        out_shape=jax.ShapeDtypeStruct(xs_zero.shape, xs_zero.dtype),
        grid_spec=pltpu.PrefetchScalarGridSpec(
            num_scalar_prefetch=1, grid=(m // tm,),
            in_specs=[any_spec, any_spec], out_specs=any_spec,
            scratch_shapes=[pltpu.SemaphoreType.DMA]),
        input_output_aliases={2: 0},
        compiler_params=_cparams(("arbitrary",), disable_bounds_checks=True, has_side_effects=True),
        name="moe_dispatch",
    )(slots_flat, hp, xs_zero)


def _ffn_body(be_ref, nv_ref, xs_ref, wg_ref, wu_ref, wd_ref, y_ref):
    del be_ref

    @pl.when(pl.program_id(0) >= nv_ref[0])
    def _():
        y_ref[...] = jnp.zeros_like(y_ref)

    @pl.when(pl.program_id(0) < nv_ref[0])
    def _():
        words = xs_ref[...]
        d_half = words.shape[1]
        x_lo = pltpu.bitcast(words << 16, F32).astype(BF16)
        x_hi = pltpu.bitcast(words & jnp.uint32(0xFFFF0000), F32).astype(BF16)

        def proj(w_ref):
            return (jnp.dot(x_lo, w_ref[0, :d_half, :], preferred_element_type=F32)
                    + jnp.dot(x_hi, w_ref[0, d_half:, :], preferred_element_type=F32))

        hid = jax.nn.silu(proj(wg_ref)) * proj(wu_ref)
        y_ref[...] = jnp.dot(hid.astype(BF16), wd_ref[0], preferred_element_type=F32)


def _ffn(block_expert, n_valid, xs, wg, wu, wd, bm):
    n_slots, d_half = xs.shape
    d = 2 * d_half
    de = wg.shape[2]
    live = lambda b, be, nv: (jnp.minimum(b, nv[0] - 1), 0)
    wsel = lambda b, be, nv: (be[b], 0, 0)
    return pl.pallas_call(
        _ffn_body,
        out_shape=jax.ShapeDtypeStruct((n_slots, d), F32),
        grid_spec=pltpu.PrefetchScalarGridSpec(
            num_scalar_prefetch=2, grid=(n_slots // bm,),
            in_specs=[pl.BlockSpec((bm, d_half), live), pl.BlockSpec((1, d, de), wsel),
                      pl.BlockSpec((1, d, de), wsel), pl.BlockSpec((1, de, d), wsel)],
            out_specs=pl.BlockSpec((bm, d), lambda b, be, nv: (b, 0))),
        compiler_params=_cparams(("arbitrary",)),
        name="moe_ffn",
    )(block_expert, n_valid, xs, wg, wu, wd)


def _final_body(slot_ref, x1_ref, route_ref, g_ref, y_hbm, o_ref, buf, sem, *, tm):
    i = pl.program_id(0)
    n_steps = pl.num_programs(0)

    def issue(step, par):
        def body(t, _):
            for kk in range(2):
                src = y_hbm.at[pl.ds(slot_ref[2 * (step * tm + t) + kk], 1)]
                pltpu.make_async_copy(src, buf.at[par, kk, pl.ds(t, 1)], sem.at[par]).start()
            return 0
        lax.fori_loop(0, tm, body, 0)

    @pl.when(i == 0)
    def _():
        issue(0, 0)

    @pl.when(i + 1 < n_steps)
    def _():
        issue(i + 1, (i + 1) % 2)

    par = i % 2
    for kk in range(2):
        _row_wait(y_hbm, buf.at[par, kk], sem.at[par], tm)
    route = route_ref[...]
    x2 = x1_ref[...] + route[:, 2:3] * buf[par, 0] + route[:, 3:4] * buf[par, 1]
    o_ref[...] = (x2 * lax.rsqrt(jnp.mean(x2 * x2, axis=-1, keepdims=True) + EPS)) * g_ref[...]


def _final(slots_flat, x1, route, g, y, tm):
    m, d = x1.shape
    assert m % tm == 0
    row = lambda i, s: (i, 0)
    return pl.pallas_call(
        functools.partial(_final_body, tm=tm),
        out_shape=jax.ShapeDtypeStruct((m, d), F32),
        grid_spec=pltpu.PrefetchScalarGridSpec(
            num_scalar_prefetch=1, grid=(m // tm,),
            in_specs=[pl.BlockSpec((tm, d), row), pl.BlockSpec((tm, LANES), row),
                      pl.BlockSpec((1, d), lambda i, s: (0, 0)), pl.BlockSpec(memory_space=pl.ANY)],
            out_specs=pl.BlockSpec((tm, d), row),
            scratch_shapes=[pltpu.VMEM((2, 2, tm, d), F32), pltpu.SemaphoreType.DMA((2,))]),
        compiler_params=_cparams(("arbitrary",), disable_bounds_checks=True),
        name="moe_combine_final_norm",
    )(slots_flat, x1, route, g, y)


def _routing_tables(route, cnt, tm, bm):
    m = route.shape[0]
    n_blocks = (2 * m) // bm + N_EXPERTS
    counts = cnt[:, 0, :N_EXPERTS].astype(jnp.int32)
    sizes = jnp.sum(counts, axis=0)
    padded = (sizes + bm - 1) // bm * bm
    pad_end = jnp.cumsum(padded)
    base = (pad_end - padded)[None, :] + jnp.cumsum(counts, axis=0) - counts
    base_tok = jnp.repeat(base, tm, axis=0)
    e = route[:, 0:2].astype(jnp.int32)
    rank = route[:, 4:6].astype(jnp.int32)
    sel = e[:, :, None] == jnp.arange(N_EXPERTS, dtype=jnp.int32)[None, None, :]
    slots = jnp.sum(jnp.where(sel, base_tok[:, None, :], 0), axis=-1) + rank
    n_valid = (pad_end[-1] // bm).astype(jnp.int32)
    blk = jnp.minimum(jnp.arange(n_blocks, dtype=jnp.int32), n_valid - 1)
    block_expert = jnp.minimum(jnp.searchsorted(pad_end, blk * bm, side='right'), N_EXPERTS - 1)
    return slots.reshape(-1), block_expert.astype(jnp.int32), n_valid.reshape(1), n_blocks * bm


def _stream(x, pool_hist, k_hist, v_hist, p, *, tq, tk, tn, tm_in, tm_mix, tm_fin, bm):
    b, n, d = x.shape
    m = b * n
    past = 0 if k_hist is None else k_hist.shape[1]
    x2d = x.reshape(m, d)
    u, q, k, v, gp, gs = _inproj(x2d, p['norm_mix'], p['w_in'], tm_in)
    dp = u.shape[1]
    diff = _pool_diff(pool_hist, u.reshape(b, n, dp), past, tn).reshape(m, dp)
    shp = (b, n, k.shape[1])
    o = _attention(q.reshape(shp), k.reshape(shp), v.reshape(shp), k_hist, v_hist, tq, tk).reshape(m, -1)
    x1, hp, route, cnt = _mix(diff, o, gp, gs, x2d, p['w_pool'], p['pool_scale'], p['w_sb_out'], p['w_out'],
                              p['norm_ffn'], p['w_r_hi'], p['w_r_lo'], p['b_r'], tm_mix)
    slots, block_expert, n_valid, n_slots = _routing_tables(route, cnt, tm_mix, bm)
    xs = _dispatch(slots, hp, jnp.zeros((n_slots, d // 2), jnp.uint32), tm_mix)
    y = _ffn(block_expert, n_valid, xs, p['w_g'], p['w_u'], p['w_d'], bm)
    out = _final(slots, x1, route, p['norm_final'], y, tm_fin)
    return out.reshape(b, n, d), u.reshape(b, n, dp), k, v


def kernel(x_prompt, x_sample, cache_sb_k, cache_sb_v, state_pool, norm_mix, w_in, w_pool, pool_scale, w_sb_out,
           w_out, norm_ffn, w_router_group, b_router_group, w_router_expert, b_router_expert, w_exp_gate,
           w_exp_up, w_exp_down, norm_final):
    depth = w_in.shape[0]
    assert depth == 1
    bp, sp, d = x_prompt.shape
    bs, ss, _ = x_sample.shape
    past = cache_sb_k.shape[2]
    heads, hd = cache_sb_k.shape[3], cache_sb_k.shape[4]
    assert hd == HEAD_DIM
    dp = state_pool.shape[3]
    n_state = state_pool.shape[2]

    w_r = jnp.concatenate([w_router_group[0], w_router_expert[0]], axis=1)
    w_r = jnp.pad(w_r, ((0, 0), (0, LANES - w_r.shape[1])))
    w_r_hi = w_r.astype(BF16)
    b_r = jnp.concatenate([b_router_group[0], b_router_expert[0]])
    p = dict(
        norm_mix=norm_mix[0][None, :], w_in=w_in[0].astype(BF16), w_pool=w_pool[0].astype(BF16),
        pool_scale=pool_scale[0][None, :], w_sb_out=w_sb_out[0].astype(BF16), w_out=w_out[0].astype(BF16),
        norm_ffn=norm_ffn[0][None, :], w_r_hi=w_r_hi, w_r_lo=(w_r - w_r_hi.astype(F32)).astype(BF16),
        b_r=jnp.pad(b_r, (0, LANES - b_r.shape[0]))[None, :].astype(F32),
        w_g=w_exp_gate[0].astype(BF16), w_u=w_exp_up[0].astype(BF16), w_d=w_exp_down[0].astype(BF16),
        norm_final=norm_final[None, :])

    hist_p = jnp.zeros((bp, POOL_HIST, dp), F32)
    yp, up, kp, vp = _stream(x_prompt, hist_p, None, None, p, tq=256, tk=256, tn=min(sp, 512),
                             tm_in=min(512, bp * sp), tm_mix=256, tm_fin=256, bm=256)
    hist_s = jnp.pad(state_pool[0], ((0, 0), (POOL_HIST - n_state, 0), (0, 0)))
    kh = cache_sb_k[0].reshape(bs, past, heads * hd)
    vh = cache_sb_v[0].reshape(bs, past, heads * hd)
    ys, us, ks, vs = _stream(x_sample, hist_s, kh, vh, p, tq=ss, tk=256, tn=ss,
                             tm_in=min(512, bs * ss), tm_mix=256, tm_fin=256, bm=256)

    def pool_state(hist, u):
        return jnp.concatenate([hist[:, POOL_HIST - n_state:], u], axis=1)[:, -n_state:][None]

    return (yp, ys,
            kp.reshape(1, bp, sp, heads, hd), vp.reshape(1, bp, sp, heads, hd), pool_state(hist_p, up),
            ks.reshape(1, bs, ss, heads, hd), vs.reshape(1, bs, ss, heads, hd), pool_state(hist_s, us))
```

```python
import functools

import jax
import jax.numpy as jnp
from jax import lax
from jax.experimental import pallas as pl
from jax.experimental.pallas import tpu as pltpu

F32 = jnp.float32
BF16 = jnp.bfloat16

EPS = 1e-6
HEAD_DIM = 64
LANES = 128
POOL_WINDOWS = (2, 4, 8, 16)
POOL_HIST = 16
N_GROUPS = 4
PER_GROUP = 8
N_EXPERTS = N_GROUPS * PER_GROUP
VMEM_LIMIT = 56 * 1024 * 1024
NEG = -1e30
LOG2E = 1.4426950408889634


def _cparams(sem, **kw):
    return pltpu.CompilerParams(dimension_semantics=sem, vmem_limit_bytes=VMEM_LIMIT, **kw)


def _inproj_body(x_ref, g_ref, w_ref, u_ref, q_ref, k_ref, v_ref, gp_ref, gs_ref, h_scr, *, d_half):
    j = pl.program_id(1)

    @pl.when(j == 0)
    def _():
        x = x_ref[...]
        r = lax.rsqrt(jnp.mean(x * x, axis=-1, keepdims=True) + EPS)
        h_scr[...] = ((x * r) * g_ref[...]).astype(BF16)

    acc = jnp.dot(h_scr[...], w_ref[...], preferred_element_type=F32)

    @pl.when(j == 0)
    def _():
        u_ref[...] = acc

    @pl.when(j == 1)
    def _():
        q_ref[...] = (acc * (HEAD_DIM ** -0.5)).astype(BF16)

    @pl.when(j == 2)
    def _():
        k_ref[...] = acc

    @pl.when(j == 3)
    def _():
        v_ref[...] = acc

    for jj, ref in ((4, gp_ref), (6, gs_ref)):
        for half in range(2):
            @pl.when(j == jj + half)
            def _(ref=ref, half=half):
                ref[:, half * d_half:(half + 1) * d_half] = jax.nn.sigmoid(acc).astype(BF16)


def _inproj(x, g, w_bf, tm):
    m, d = x.shape
    d_half = d // 2
    assert w_bf.shape == (d, 8 * d_half) and m % tm == 0
    row = lambda i, j: (i, 0)
    outs = [jax.ShapeDtypeStruct((m, d_half), F32), jax.ShapeDtypeStruct((m, d_half), BF16),
            jax.ShapeDtypeStruct((m, d_half), F32), jax.ShapeDtypeStruct((m, d_half), F32),
            jax.ShapeDtypeStruct((m, d), BF16), jax.ShapeDtypeStruct((m, d), BF16)]
    return pl.pallas_call(
        functools.partial(_inproj_body, d_half=d_half),
        out_shape=outs,
        grid=(m // tm, 8),
        in_specs=[pl.BlockSpec((tm, d), row),
                  pl.BlockSpec((1, d), lambda i, j: (0, 0)),
                  pl.BlockSpec((d, d_half), lambda i, j: (0, j))],
        out_specs=[pl.BlockSpec((tm, d_half), row)] * 4 + [pl.BlockSpec((tm, d), row)] * 2,
        scratch_shapes=[pltpu.VMEM((tm, d), BF16)],
        compiler_params=_cparams(("parallel", "arbitrary")),
        name="inproj",
    )(x, g, w_bf)


def _pool_body(hist_ref, u_ref, o_ref, ext_scr, *, pos0, tn):
    s = pl.program_id(1)

    @pl.when(s == 0)
    def _():
        ext_scr[0:POOL_HIST, :] = hist_ref[0]

    ext_scr[POOL_HIST:POOL_HIST + tn, :] = u_ref[0]
    pos = pos0 + s * tn + lax.broadcasted_iota(jnp.int32, (tn, 1), 0)
    group = u_ref.shape[2] // len(POOL_WINDOWS)
    for g, w in enumerate(POOL_WINDOWS):
        lo, hi = g * group, (g + 1) * group
        cur = ext_scr[POOL_HIST:POOL_HIST + tn, lo:hi]
        tot = cur
        for dlt in range(1, w):
            tot = tot + ext_scr[POOL_HIST - dlt:POOL_HIST - dlt + tn, lo:hi]
        cnt = jnp.minimum(pos + 1, w).astype(F32)
        o_ref[0, :, lo:hi] = (tot / cnt - cur).astype(BF16)
    ext_scr[0:POOL_HIST, :] = ext_scr[tn:tn + POOL_HIST, :]


def _pool_diff(hist, u, pos0, tn):
    b, n, dp = u.shape
    assert n % tn == 0 and hist.shape == (b, POOL_HIST, dp)
    return pl.pallas_call(
        functools.partial(_pool_body, pos0=pos0, tn=tn),
        out_shape=jax.ShapeDtypeStruct((b, n, dp), BF16),
        grid=(b, n // tn),
        in_specs=[pl.BlockSpec((1, POOL_HIST, dp), lambda i, s: (i, 0, 0)),
                  pl.BlockSpec((1, tn, dp), lambda i, s: (i, s, 0))],
        out_specs=pl.BlockSpec((1, tn, dp), lambda i, s: (i, s, 0)),
        scratch_shapes=[pltpu.VMEM((POOL_HIST + tn, dp), F32)],
        compiler_params=_cparams(("parallel", "arbitrary")),
        name="pool_diff",
    )(hist, u)


def _softplus2(z2):
    neg_abs = pltpu.bitcast(pltpu.bitcast(z2, jnp.uint32) | jnp.uint32(0x80000000), F32)
    return jnp.maximum(z2, 0.0) + jnp.log2(1.0 + jnp.exp2(neg_abs))


def _suffix_matrix(n):
    r = lax.broadcasted_iota(jnp.int32, (n, n), 0)
    c = lax.broadcasted_iota(jnp.int32, (n, n), 1)
    return jnp.where(r >= c, 1.0, 0.0).astype(BF16)


def _stack_heads(q2):
    lane = lax.broadcasted_iota(jnp.int32, q2.shape, 1)
    zero = jnp.zeros_like(q2)
    return jnp.concatenate([jnp.where(lane < HEAD_DIM, q2, zero), jnp.where(lane >= HEAD_DIM, q2, zero)], axis=0)


def _unstack_heads(acc, t):
    lane = lax.broadcasted_iota(jnp.int32, (t, LANES), 1)
    return jnp.where(lane < HEAD_DIM, acc[:t], acc[t:])


def _causal_mask(t):
    r = lax.broadcasted_iota(jnp.int32, (t, t), 0)
    c = lax.broadcasted_iota(jnp.int32, (t, t), 1)
    m = c < r
    return jnp.concatenate([m, m], axis=0)


def _sb_group(q_st, k_blocks, v_blocks, suffix, carry, masks, transposed_keys):
    dn = (((1,), (0,)), ((), ())) if transposed_keys else (((1,), (1,)), ((), ()))
    zs = [lax.dot_general(q_st, kb, dn, preferred_element_type=F32) * LOG2E for kb in k_blocks]
    his, los = [], []
    for z, mask in zip(zs, masks):
        sp = _softplus2(z)
        if mask is not None:
            sp = jnp.where(mask, sp, 0.0)
        hi = pltpu.bitcast(pltpu.bitcast(sp, jnp.uint32) & jnp.uint32(0xFFFF0000), F32)
        his.append(hi.astype(BF16))
        los.append((sp - hi).astype(BF16))
    cs = [jnp.dot(hi, suffix, preferred_element_type=F32) + jnp.dot(lo, suffix, preferred_element_type=F32)
          for hi, lo in zip(his, los)]
    out = None
    for z, c, vb, mask in zip(zs, cs, v_blocks, masks):
        arg = z - c - carry
        if mask is not None:
            arg = jnp.where(mask, arg, NEG)
        o = jnp.dot(jnp.exp2(arg).astype(BF16), vb, preferred_element_type=F32)
        out = o if out is None else out + o
        carry = carry + c[:, 0:1]
    return out, carry


def _attn_self_body(q_ref, k_ref, v_ref, o_ref, kt_scr, v_scr, acc_scr, car_scr, *, t):
    qi = pl.program_id(2)

    @pl.when(qi == 0)
    def _fill():
        def body(c, _):
            row0 = pl.multiple_of(c * t, t)
            kt_scr[c] = k_ref[0, pl.ds(row0, t), :].T.astype(BF16)
            v_scr[c] = v_ref[0, pl.ds(row0, t), :].astype(BF16)
            return 0
        lax.fori_loop(0, kt_scr.shape[0], body, 0)

    q_st = _stack_heads(q_ref[0])
    suffix = _suffix_matrix(t)
    mask = _causal_mask(t)

    def run(blocks, masks, carry, first):
        out, carry = _sb_group(q_st, [kt_scr[b] for b in blocks], [v_scr[b] for b in blocks], suffix, carry,
                               masks, True)
        acc_scr[...] = out if first else acc_scr[...] + out
        car_scr[...] = jnp.broadcast_to(carry, car_scr.shape)

    zero = jnp.zeros((2 * t, 1), F32)
    odd = qi % 2

    @pl.when(odd == 0)
    def _():
        run([qi], [mask], zero, True)

    @pl.when(odd == 1)
    def _():
        run([qi, qi - 1], [mask, None], zero, True)

    top = qi - 1 - odd

    def pair(it, _):
        b0 = top - 2 * it
        run([b0, b0 - 1], [None, None], car_scr[:, 0:1], False)
        return 0

    lax.fori_loop(0, (top + 1) // 2, pair, 0)
    o_ref[0] = _unstack_heads(acc_scr[...], t).astype(o_ref.dtype)


def _attn_hist_body(q_ref, kn_ref, vn_ref, kh_ref, vh_ref, o_ref, *, tk, group):
    t = q_ref.shape[1]
    past = kh_ref.shape[1]
    q_st = _stack_heads(q_ref[0])
    out, carry = _sb_group(q_st, [kn_ref[0].astype(BF16)], [vn_ref[0].astype(BF16)], _suffix_matrix(t),
                           jnp.zeros((2 * t, 1), F32), [_causal_mask(t)], False)
    suffix = _suffix_matrix(tk)
    for top in range(past // tk, 0, -group):
        blocks = range(top - 1, top - 1 - group, -1)
        o, carry = _sb_group(q_st, [kh_ref[0, b * tk:(b + 1) * tk, :].astype(BF16) for b in blocks],
                             [vh_ref[0, b * tk:(b + 1) * tk, :].astype(BF16) for b in blocks],
                             suffix, carry, [None] * group, False)
        out = out + o
    o_ref[0] = _unstack_heads(out, t).astype(o_ref.dtype)


def _attention(q, k_new, v_new, k_hist, v_hist, t, tk, group):
    b, n, dm = q.shape
    assert n % t == 0 and dm % LANES == 0
    tile = pl.BlockSpec((1, t, LANES), lambda i, p, s: (i, s, p))
    seq = lambda rows: pl.BlockSpec((1, rows, LANES), lambda i, p, s: (i, 0, p))
    if k_hist is None:
        body = functools.partial(_attn_self_body, t=t)
        in_specs, args = [tile, seq(n), seq(n)], [q, k_new, v_new]
        scratch = [pltpu.VMEM((n // t, LANES, t), BF16), pltpu.VMEM((n // t, t, LANES), BF16),
                   pltpu.VMEM((2 * t, LANES), F32), pltpu.VMEM((2 * t, LANES), F32)]
    else:
        past = k_hist.shape[1]
        assert n == t and past % (tk * group) == 0
        body = functools.partial(_attn_hist_body, tk=tk, group=group)
        in_specs, args = [tile, seq(n), seq(n), seq(past), seq(past)], [q, k_new, v_new, k_hist, v_hist]
        scratch = []
    return pl.pallas_call(
        body,
        out_shape=jax.ShapeDtypeStruct((b, n, dm), BF16),
        grid=(b, dm // LANES, n // t),
        in_specs=in_specs,
        out_specs=tile,
        scratch_shapes=scratch,
        compiler_params=_cparams(("parallel", "parallel", "arbitrary")),
        name="sb_attention",
    )(*args)


def _mix_body(diff_ref, o_ref, gp_ref, gs_ref, x_ref, wp_ref, ps_ref, wsb_ref, wo_ref, nf_ref,
              wrh_ref, wrl_ref, br_ref, x1_ref, hp_ref, route_ref, cnt_ref, *, tm):
    n_pool = wp_ref.shape[0]
    group = wp_ref.shape[1]
    diff = diff_ref[...]
    pool = jnp.concatenate(
        [jnp.dot(diff[:, g * group:(g + 1) * group], wp_ref[g], preferred_element_type=F32)
         for g in range(n_pool)], axis=-1) * ps_ref[...]
    sb = jnp.dot(o_ref[...], wsb_ref[...], preferred_element_type=F32)
    mixed = gp_ref[...].astype(F32) * pool + gs_ref[...].astype(F32) * sb
    x1 = x_ref[...] + jnp.dot(mixed.astype(BF16), wo_ref[...], preferred_element_type=F32)
    x1_ref[...] = x1
    h = (x1 * lax.rsqrt(jnp.mean(x1 * x1, axis=-1, keepdims=True) + EPS)) * nf_ref[...]

    d_half = h.shape[1] // 2
    lo_bits = pltpu.bitcast(h[:, :d_half].astype(BF16).astype(F32), jnp.uint32)
    hi_bits = pltpu.bitcast(h[:, d_half:].astype(BF16).astype(F32), jnp.uint32)
    hp_ref[...] = (lo_bits >> 16) | (hi_bits & jnp.uint32(0xFFFF0000))

    hh = h.astype(BF16)
    hl = (h - hh.astype(F32)).astype(BF16)
    logits = (jnp.dot(hh, wrh_ref[...], preferred_element_type=F32)
              + jnp.dot(hl, wrh_ref[...], preferred_element_type=F32)
              + jnp.dot(hh, wrl_ref[...], preferred_element_type=F32)) + br_ref[...]
    lane = lax.broadcasted_iota(jnp.int32, (tm, LANES), 1)
    big = jnp.int32(LANES)

    def first_max(vals):
        m = jnp.max(vals, axis=-1, keepdims=True)
        idx = jnp.min(jnp.where(vals == m, lane, big), axis=-1, keepdims=True)
        return m, idx

    gl = jnp.where(lane < N_GROUPS, logits, NEG)
    gmax, grp = first_max(gl)
    p_grp = 1.0 / jnp.sum(jnp.exp(gl - gmax), axis=-1, keepdims=True)
    e_lo = N_GROUPS + grp * PER_GROUP
    el = jnp.where((lane >= e_lo) & (lane < e_lo + PER_GROUP), logits, NEG)
    m1, i1 = first_max(el)
    m2, i2 = first_max(jnp.where(lane == i1, NEG, el))
    t2 = jnp.exp(m2 - m1)
    w1 = p_grp / (1.0 + t2)
    w2 = w1 * t2
    e1 = i1 - N_GROUPS
    e2 = i2 - N_GROUPS

    oh1 = jnp.where(lane == e1, 1.0, 0.0).astype(BF16)
    oh2 = jnp.where(lane == e2, 1.0, 0.0).astype(BF16)
    rr = lax.broadcasted_iota(jnp.int32, (tm, tm), 0)
    cc = lax.broadcasted_iota(jnp.int32, (tm, tm), 1)
    before = jnp.where(cc < rr, 1.0, 0.0).astype(BF16)
    ones = jnp.ones((8, tm), BF16)
    pre1 = jnp.dot(before, oh1, preferred_element_type=F32)
    pre2 = jnp.dot(before, oh2, preferred_element_type=F32)
    c1 = jnp.dot(ones, oh1, preferred_element_type=F32)
    c2 = jnp.dot(ones, oh2, preferred_element_type=F32)
    rank1 = jnp.sum(jnp.where(lane == e1, pre1, 0.0), axis=-1, keepdims=True)
    rank2 = jnp.sum(jnp.where(lane == e2, pre2 + c1[0:1, :], 0.0), axis=-1, keepdims=True)
    cnt_ref[0] = c1 + c2

    route = jnp.where(lane == 0, e1.astype(F32), 0.0)
    route = jnp.where(lane == 1, e2.astype(F32), route)
    route = jnp.where(lane == 2, w1, route)
    route = jnp.where(lane == 3, w2, route)
    route = jnp.where(lane == 4, rank1, route)
    route = jnp.where(lane == 5, rank2, route)
    route_ref[...] = route


def _mix(diff, o, gp, gs, x, wp, ps, wsb, wo, nf, wrh, wrl, br, tm):
    m, d = x.shape
    assert m % tm == 0
    row = lambda i: (i, 0)
    full = lambda a: pl.BlockSpec(a.shape, lambda i: (0,) * a.ndim, pipeline_mode=pl.Buffered(1))
    acts = [diff, o, gp, gs, x]
    consts = [wp, ps, wsb, wo, nf, wrh, wrl, br]
    return pl.pallas_call(
        functools.partial(_mix_body, tm=tm),
        out_shape=[jax.ShapeDtypeStruct((m, d), F32), jax.ShapeDtypeStruct((m, d // 2), jnp.uint32),
                   jax.ShapeDtypeStruct((m, LANES), F32), jax.ShapeDtypeStruct((m // tm, 8, LANES), F32)],
        grid=(m // tm,),
        in_specs=[pl.BlockSpec((tm, a.shape[1]), row) for a in acts] + [full(a) for a in consts],
        out_specs=[pl.BlockSpec((tm, d), row), pl.BlockSpec((tm, d // 2), row),
                   pl.BlockSpec((tm, LANES), row), pl.BlockSpec((1, 8, LANES), lambda i: (i, 0, 0))],
        compiler_params=_cparams(("parallel",)),
        name="mix_outproj_router",
    )(*acts, *consts)


def _row_wait(src_ref, dst_ref, sem, n_rows):
    pltpu.make_async_copy(src_ref.at[pl.ds(0, n_rows)], dst_ref.at[pl.ds(0, n_rows)], sem).wait()


def _dispatch_body(slot_ref, hp_ref, xs_in_ref, xs_ref, buf, sem, *, tm):
    del xs_in_ref
    i = pl.program_id(0)
    last = pl.num_programs(0) - 1
    par = i % 2

    def drain(slot):
        for _ in range(2):
            _row_wait(buf.at[slot], xs_ref, sem.at[slot], tm)

    @pl.when(i >= 2)
    def _():
        drain(par)

    buf[par] = hp_ref[...]
    base = i * tm

    def body(t, _):
        src = buf.at[par, pl.ds(t, 1)]
        for kk in range(2):
            dst = xs_ref.at[pl.ds(slot_ref[2 * (base + t) + kk], 1)]
            pltpu.make_async_copy(src, dst, sem.at[par]).start()
        return 0

    lax.fori_loop(0, tm, body, 0, unroll=8)

    @pl.when(i == last)
    def _():
        drain(par)

        @pl.when(i >= 1)
        def _():
            drain(1 - par)


def _dispatch(slots_flat, hp, xs_zero, tm):
    m, dh = hp.shape
    assert m % tm == 0
    any_spec = pl.BlockSpec(memory_space=pl.ANY)
    return pl.pallas_call(
        functools.partial(_dispatch_body, tm=tm),
        out_shape=jax.ShapeDtypeStruct(xs_zero.shape, xs_zero.dtype),
        grid_spec=pltpu.PrefetchScalarGridSpec(
            num_scalar_prefetch=1, grid=(m // tm,),
            in_specs=[pl.BlockSpec((tm, dh), lambda i, s: (i, 0)), any_spec], out_specs=any_spec,
            scratch_shapes=[pltpu.VMEM((2, tm, dh), hp.dtype), pltpu.SemaphoreType.DMA((2,))]),
        input_output_aliases={2: 0},
        compiler_params=_cparams(("arbitrary",), disable_bounds_checks=True, has_side_effects=True),
        name="moe_dispatch",
    )(slots_flat, hp, xs_zero)


def _ffn_body(be_ref, nv_ref, xs_ref, wg_ref, wu_ref, wd_ref, y_ref):
    del be_ref

    @pl.when(pl.program_id(0) >= nv_ref[0])
    def _():
        y_ref[...] = jnp.zeros_like(y_ref)

    @pl.when(pl.program_id(0) < nv_ref[0])
    def _():
        words = xs_ref[...]
        d_half = words.shape[1]
        x_lo = pltpu.bitcast(words << 16, F32).astype(BF16)
        x_hi = pltpu.bitcast(words & jnp.uint32(0xFFFF0000), F32).astype(BF16)

        def proj(w_ref):
            return (jnp.dot(x_lo, w_ref[0, :d_half, :], preferred_element_type=F32)
                    + jnp.dot(x_hi, w_ref[0, d_half:, :], preferred_element_type=F32))

        hid = jax.nn.silu(proj(wg_ref)) * proj(wu_ref)
        y_ref[...] = jnp.dot(hid.astype(BF16), wd_ref[0], preferred_element_type=F32)


def _ffn(block_expert, n_valid, xs, wg, wu, wd, bm):
    n_slots, d_half = xs.shape
    d = 2 * d_half
    de = wg.shape[2]
    live = lambda b, be, nv: (jnp.minimum(b, nv[0] - 1), 0)
    wsel = lambda b, be, nv: (be[b], 0, 0)
    return pl.pallas_call(
        _ffn_body,
        out_shape=jax.ShapeDtypeStruct((n_slots, d), F32),
        grid_spec=pltpu.PrefetchScalarGridSpec(
            num_scalar_prefetch=2, grid=(n_slots // bm,),
            in_specs=[pl.BlockSpec((bm, d_half), live), pl.BlockSpec((1, d, de), wsel),
                      pl.BlockSpec((1, d, de), wsel), pl.BlockSpec((1, de, d), wsel)],
            out_specs=pl.BlockSpec((bm, d), lambda b, be, nv: (b, 0))),
        compiler_params=_cparams(("arbitrary",)),
        name="moe_ffn",
    )(block_expert, n_valid, xs, wg, wu, wd)


def _final_body(slot_ref, x1_ref, route_ref, g_ref, y_hbm, o_ref, buf, sem, *, tm):
    i = pl.program_id(0)
    n_steps = pl.num_programs(0)

    def issue(step, par):
        def body(t, _):
            for kk in range(2):
                src = y_hbm.at[pl.ds(slot_ref[2 * (step * tm + t) + kk], 1)]
                pltpu.make_async_copy(src, buf.at[par, kk, pl.ds(t, 1)], sem.at[par]).start()
            return 0
        lax.fori_loop(0, tm, body, 0, unroll=8)

    @pl.when(i == 0)
    def _():
        issue(0, 0)

    @pl.when(i + 1 < n_steps)
    def _():
        issue(i + 1, (i + 1) % 2)

    par = i % 2
    for kk in range(2):
        _row_wait(y_hbm, buf.at[par, kk], sem.at[par], tm)
    route = route_ref[...]
    x2 = x1_ref[...] + route[:, 2:3] * buf[par, 0] + route[:, 3:4] * buf[par, 1]
    o_ref[...] = (x2 * lax.rsqrt(jnp.mean(x2 * x2, axis=-1, keepdims=True) + EPS)) * g_ref[...]


def _final(slots_flat, x1, route, g, y, tm):
    m, d = x1.shape
    assert m % tm == 0
    row = lambda i, s: (i, 0)
    return pl.pallas_call(
        functools.partial(_final_body, tm=tm),
        out_shape=jax.ShapeDtypeStruct((m, d), F32),
        grid_spec=pltpu.PrefetchScalarGridSpec(
            num_scalar_prefetch=1, grid=(m // tm,),
            in_specs=[pl.BlockSpec((tm, d), row), pl.BlockSpec((tm, LANES), row),
                      pl.BlockSpec((1, d), lambda i, s: (0, 0)), pl.BlockSpec(memory_space=pl.ANY)],
            out_specs=pl.BlockSpec((tm, d), row),
            scratch_shapes=[pltpu.VMEM((2, 2, tm, d), F32), pltpu.SemaphoreType.DMA((2,))]),
        compiler_params=_cparams(("arbitrary",), disable_bounds_checks=True),
        name="moe_combine_final_norm",
    )(slots_flat, x1, route, g, y)


def _routing_tables(route, cnt, tm, bm):
    m = route.shape[0]
    n_blocks = (2 * m) // bm + N_EXPERTS
    counts = cnt[:, 0, :N_EXPERTS].astype(jnp.int32)
    sizes = jnp.sum(counts, axis=0)
    padded = (sizes + bm - 1) // bm * bm
    pad_end = jnp.cumsum(padded)
    base = (pad_end - padded)[None, :] + jnp.cumsum(counts, axis=0) - counts
    base_tok = jnp.repeat(base, tm, axis=0)
    e = route[:, 0:2].astype(jnp.int32)
    rank = route[:, 4:6].astype(jnp.int32)
    sel = e[:, :, None] == jnp.arange(N_EXPERTS, dtype=jnp.int32)[None, None, :]
    slots = jnp.sum(jnp.where(sel, base_tok[:, None, :], 0), axis=-1) + rank
    n_valid = (pad_end[-1] // bm).astype(jnp.int32)
    blk = jnp.minimum(jnp.arange(n_blocks, dtype=jnp.int32), n_valid - 1)
    block_expert = jnp.minimum(jnp.sum(pad_end[None, :] <= (blk * bm)[:, None], axis=1), N_EXPERTS - 1)
    return slots.reshape(-1), block_expert.astype(jnp.int32), n_valid.reshape(1), n_blocks * bm


def _stream(x, pool_hist, k_hist, v_hist, p, *, tq, tk, group, tn, tm_in, tm_mix, tm_fin, bm):
    b, n, d = x.shape
    m = b * n
    past = 0 if k_hist is None else k_hist.shape[1]
    x2d = x.reshape(m, d)
    u, q, k, v, gp, gs = _inproj(x2d, p['norm_mix'], p['w_in'], tm_in)
    dp = u.shape[1]
    diff = _pool_diff(pool_hist, u.reshape(b, n, dp), past, tn).reshape(m, dp)
    shp = (b, n, k.shape[1])
    o = _attention(q.reshape(shp), k.reshape(shp), v.reshape(shp), k_hist, v_hist, tq, tk, group).reshape(m, -1)
    x1, hp, route, cnt = _mix(diff, o, gp, gs, x2d, p['w_pool'], p['pool_scale'], p['w_sb_out'], p['w_out'],
                              p['norm_ffn'], p['w_r_hi'], p['w_r_lo'], p['b_r'], tm_mix)
    slots, block_expert, n_valid, n_slots = _routing_tables(route, cnt, tm_mix, bm)
    xs = _dispatch(slots, hp, jnp.zeros((n_slots, d // 2), jnp.uint32), tm_mix)
    y = _ffn(block_expert, n_valid, xs, p['w_g'], p['w_u'], p['w_d'], bm)
    out = _final(slots, x1, route, p['norm_final'], y, tm_fin)
    return out.reshape(b, n, d), u.reshape(b, n, dp), k, v


def kernel(x_prompt, x_sample, cache_sb_k, cache_sb_v, state_pool, norm_mix, w_in, w_pool, pool_scale, w_sb_out,
           w_out, norm_ffn, w_router_group, b_router_group, w_router_expert, b_router_expert, w_exp_gate,
           w_exp_up, w_exp_down, norm_final):
    depth = w_in.shape[0]
    assert depth == 1
    bp, sp, d = x_prompt.shape
    bs, ss, _ = x_sample.shape
    past = cache_sb_k.shape[2]
    heads, hd = cache_sb_k.shape[3], cache_sb_k.shape[4]
    assert hd == HEAD_DIM
    dp = state_pool.shape[3]
    n_state = state_pool.shape[2]

    w_r = jnp.concatenate([w_router_group[0], w_router_expert[0]], axis=1)
    w_r = jnp.pad(w_r, ((0, 0), (0, LANES - w_r.shape[1])))
    w_r_hi = w_r.astype(BF16)
    b_r = jnp.concatenate([b_router_group[0], b_router_expert[0]])
    p = dict(
        norm_mix=norm_mix[0][None, :], w_in=w_in[0].astype(BF16), w_pool=w_pool[0].astype(BF16),
        pool_scale=pool_scale[0][None, :], w_sb_out=w_sb_out[0].astype(BF16), w_out=w_out[0].astype(BF16),
        norm_ffn=norm_ffn[0][None, :], w_r_hi=w_r_hi, w_r_lo=(w_r - w_r_hi.astype(F32)).astype(BF16),
        b_r=jnp.pad(b_r, (0, LANES - b_r.shape[0]))[None, :].astype(F32),
        w_g=w_exp_gate[0].astype(BF16), w_u=w_exp_up[0].astype(BF16), w_d=w_exp_down[0].astype(BF16),
        norm_final=norm_final[None, :])

    hist_p = jnp.zeros((bp, POOL_HIST, dp), F32)
    yp, up, kp, vp = _stream(x_prompt, hist_p, None, None, p, tq=256, tk=256, group=2, tn=min(sp, 512),
                             tm_in=min(512, bp * sp), tm_mix=256, tm_fin=256, bm=256)
    hist_s = jnp.pad(state_pool[0], ((0, 0), (POOL_HIST - n_state, 0), (0, 0)))
    kh = cache_sb_k[0].reshape(bs, past, heads * hd)
    vh = cache_sb_v[0].reshape(bs, past, heads * hd)
    ys, us, ks, vs = _stream(x_sample, hist_s, kh, vh, p, tq=ss, tk=256, group=4, tn=ss,
                             tm_in=min(512, bs * ss), tm_mix=256, tm_fin=256, bm=256)

    def pool_state(hist, u):
        return jnp.concatenate([hist[:, POOL_HIST - n_state:], u], axis=1)[:, -n_state:][None]

    return (yp, ys,
            kp.reshape(1, bp, sp, heads, hd), vp.reshape(1, bp, sp, heads, hd), pool_state(hist_p, up),
            ks.reshape(1, bs, ss, heads, hd), vs.reshape(1, bs, ss, heads, hd), pool_state(hist_s, us))
```

```python
import functools

import jax
import jax.numpy as jnp
from jax import lax
from jax.experimental import pallas as pl
from jax.experimental.pallas import tpu as pltpu

F32 = jnp.float32
BF16 = jnp.bfloat16

EPS = 1e-6
HEAD_DIM = 64
LANES = 128
POOL_WINDOWS = (2, 4, 8, 16)
POOL_HIST = 16
N_GROUPS = 4
PER_GROUP = 8
N_EXPERTS = N_GROUPS * PER_GROUP
VMEM_LIMIT = 56 * 1024 * 1024
NEG = -1e30
LOG2E = 1.4426950408889634
SKIP_MASS = 160.0


def _cparams(sem, **kw):
    return pltpu.CompilerParams(dimension_semantics=sem, vmem_limit_bytes=VMEM_LIMIT, **kw)


def _inproj_body(x_ref, g_ref, w_ref, u_ref, q_ref, k_ref, v_ref, gp_ref, gs_ref, h_scr, *, d_half):
    j = pl.program_id(1)

    @pl.when(j == 0)
    def _():
        x = x_ref[...]
        r = lax.rsqrt(jnp.mean(x * x, axis=-1, keepdims=True) + EPS)
        h_scr[...] = ((x * r) * g_ref[...]).astype(BF16)

    def proj():
        return jnp.dot(h_scr[...], w_ref[...], preferred_element_type=F32)

    @pl.when(j == 0)
    def _():
        u_ref[...] = proj()

    @pl.when(j == 1)
    def _():
        q_ref[...] = (proj() * (HEAD_DIM ** -0.5)).astype(BF16)

    @pl.when(j == 2)
    def _():
        k_ref[...] = proj()

    @pl.when(j == 3)
    def _():
        v_ref[...] = proj()

    for jj, ref in ((4, gp_ref), (6, gs_ref)):
        for half in range(2):
            @pl.when(j == jj + half)
            def _(ref=ref, half=half):
                ref[:, half * d_half:(half + 1) * d_half] = (0.5 * jnp.tanh(0.5 * proj()) + 0.5).astype(BF16)


def _inproj(x, g, w_bf, tm):
    m, d = x.shape
    d_half = d // 2
    assert w_bf.shape == (d, 8 * d_half) and m % tm == 0
    row = lambda i, j: (i, 0)
    outs = [jax.ShapeDtypeStruct((m, d_half), F32), jax.ShapeDtypeStruct((m, d_half), BF16),
            jax.ShapeDtypeStruct((m, d_half), F32), jax.ShapeDtypeStruct((m, d_half), F32),
            jax.ShapeDtypeStruct((m, d), BF16), jax.ShapeDtypeStruct((m, d), BF16)]
    return pl.pallas_call(
        functools.partial(_inproj_body, d_half=d_half),
        out_shape=outs,
        grid=(m // tm, 8),
        in_specs=[pl.BlockSpec((tm, d), row),
                  pl.BlockSpec((1, d), lambda i, j: (0, 0)),
                  pl.BlockSpec((d, d_half), lambda i, j: (0, j))],
        out_specs=[pl.BlockSpec((tm, d_half), row)] * 4 + [pl.BlockSpec((tm, d), row)] * 2,
        scratch_shapes=[pltpu.VMEM((tm, d), BF16)],
        compiler_params=_cparams(("parallel", "arbitrary")),
        name="inproj",
    )(x, g, w_bf)


def _pool_body(hist_ref, u_ref, o_ref, ext_scr, *, pos0, tn):
    s = pl.program_id(1)

    @pl.when(s == 0)
    def _():
        ext_scr[0:POOL_HIST, :] = hist_ref[0]

    ext_scr[POOL_HIST:POOL_HIST + tn, :] = u_ref[0]
    pos = pos0 + s * tn + lax.broadcasted_iota(jnp.int32, (tn, 1), 0)
    group = u_ref.shape[2] // len(POOL_WINDOWS)
    for g, w in enumerate(POOL_WINDOWS):
        lo, hi = g * group, (g + 1) * group
        cur = ext_scr[POOL_HIST:POOL_HIST + tn, lo:hi]
        tot = cur
        for dlt in range(1, w):
            tot = tot + ext_scr[POOL_HIST - dlt:POOL_HIST - dlt + tn, lo:hi]
        cnt = jnp.minimum(pos + 1, w).astype(F32)
        o_ref[0, :, lo:hi] = (tot / cnt - cur).astype(BF16)
    ext_scr[0:POOL_HIST, :] = ext_scr[tn:tn + POOL_HIST, :]


def _pool_diff(hist, u, pos0, tn):
    b, n, dp = u.shape
    assert n % tn == 0 and hist.shape == (b, POOL_HIST, dp)
    return pl.pallas_call(
        functools.partial(_pool_body, pos0=pos0, tn=tn),
        out_shape=jax.ShapeDtypeStruct((b, n, dp), BF16),
        grid=(b, n // tn),
        in_specs=[pl.BlockSpec((1, POOL_HIST, dp), lambda i, s: (i, 0, 0)),
                  pl.BlockSpec((1, tn, dp), lambda i, s: (i, s, 0))],
        out_specs=pl.BlockSpec((1, tn, dp), lambda i, s: (i, s, 0)),
        scratch_shapes=[pltpu.VMEM((POOL_HIST + tn, dp), F32)],
        compiler_params=_cparams(("parallel", "arbitrary")),
        name="pool_diff",
    )(hist, u)


def _softplus2(z2):
    neg_abs = pltpu.bitcast(pltpu.bitcast(z2, jnp.uint32) | jnp.uint32(0x80000000), F32)
    return jnp.maximum(z2, 0.0) + jnp.log2(1.0 + jnp.exp2(neg_abs))


def _suffix_matrix(n):
    r = lax.broadcasted_iota(jnp.int32, (n, n), 0)
    c = lax.broadcasted_iota(jnp.int32, (n, n), 1)
    return jnp.where(r >= c, 1.0, 0.0).astype(BF16)


def _stack_heads(q2):
    lane = lax.broadcasted_iota(jnp.int32, q2.shape, 1)
    zero = jnp.zeros_like(q2)
    return jnp.concatenate([jnp.where(lane < HEAD_DIM, q2, zero), jnp.where(lane >= HEAD_DIM, q2, zero)], axis=0)


def _unstack_heads(acc, t):
    lane = lax.broadcasted_iota(jnp.int32, (t, LANES), 1)
    return jnp.where(lane < HEAD_DIM, acc[:t], acc[t:])


def _causal_mask(t):
    r = lax.broadcasted_iota(jnp.int32, (t, t), 0)
    c = lax.broadcasted_iota(jnp.int32, (t, t), 1)
    m = c < r
    return jnp.concatenate([m, m], axis=0)


def _sb_group(q_st, k_blocks, v_blocks, suffix, carry, masks, transposed_keys):
    dn = (((1,), (0,)), ((), ())) if transposed_keys else (((1,), (1,)), ((), ()))
    zs = [lax.dot_general(q_st, kb, dn, preferred_element_type=F32) * LOG2E for kb in k_blocks]
    his, los = [], []
    for z, mask in zip(zs, masks):
        sp = _softplus2(z)
        if mask is not None:
            sp = jnp.where(mask, sp, 0.0)
        hi = pltpu.bitcast(pltpu.bitcast(sp, jnp.uint32) & jnp.uint32(0xFFFF0000), F32)
        his.append(hi.astype(BF16))
        los.append((sp - hi).astype(BF16))
    cs = [jnp.dot(hi, suffix, preferred_element_type=F32) + jnp.dot(lo, suffix, preferred_element_type=F32)
          for hi, lo in zip(his, los)]
    out = None
    for z, c, vb, mask in zip(zs, cs, v_blocks, masks):
        arg = z - c - carry
        if mask is not None:
            arg = jnp.where(mask, arg, NEG)
        o = jnp.dot(jnp.exp2(arg).astype(BF16), vb, preferred_element_type=F32)
        out = o if out is None else out + o
        carry = carry + c[:, 0:1]
    return out, carry


def _attn_self_body(q_ref, k_ref, v_ref, o_ref, kt_scr, v_scr, acc_scr, car_scr, min_scr, *, t):
    qi = pl.program_id(2)

    @pl.when(qi == 0)
    def _fill():
        def body(c, _):
            row0 = pl.multiple_of(c * t, t)
            kt_scr[c] = k_ref[0, pl.ds(row0, t), :].T.astype(BF16)
            v_scr[c] = v_ref[0, pl.ds(row0, t), :].astype(BF16)
            return 0
        lax.fori_loop(0, kt_scr.shape[0], body, 0)

    q_st = _stack_heads(q_ref[0])
    suffix = _suffix_matrix(t)
    mask = _causal_mask(t)

    def run(blocks, masks, first):
        carry = jnp.zeros((2 * t, 1), F32) if first else car_scr[:, 0:1]
        out, carry = _sb_group(q_st, [kt_scr[b] for b in blocks], [v_scr[b] for b in blocks], suffix, carry,
                               masks, True)
        acc_scr[...] = out if first else acc_scr[...] + out
        car_scr[...] = jnp.broadcast_to(carry, car_scr.shape)
        min_scr[0] = jnp.min(carry)

    @pl.when(qi == 0)
    def _():
        run([0], [mask], True)

    @pl.when(qi > 0)
    def _():
        run([qi, qi - 1], [mask, None], True)

    rest = jnp.maximum(qi - 1, 0)

    def more(it):
        return (it < rest // 2) & (min_scr[0] < SKIP_MASS)

    def pair(it):
        b0 = qi - 2 - 2 * it
        run([b0, b0 - 1], [None, None], False)
        return it + 1

    lax.while_loop(more, pair, 0)

    @pl.when((rest % 2 == 1) & (min_scr[0] < SKIP_MASS))
    def _():
        run([0], [None], False)

    o_ref[0] = _unstack_heads(acc_scr[...], t).astype(o_ref.dtype)


def _attn_hist_body(q_ref, kn_ref, vn_ref, kh_ref, vh_ref, o_ref, acc_scr, car_scr, min_scr, *, tk, group):
    t = q_ref.shape[1]
    past = kh_ref.shape[1]
    q_st = _stack_heads(q_ref[0])

    def keep(out, carry, first):
        acc_scr[...] = out if first else acc_scr[...] + out
        car_scr[...] = jnp.broadcast_to(carry, car_scr.shape)
        min_scr[0] = jnp.min(carry)

    keep(*_sb_group(q_st, [kn_ref[0].astype(BF16)], [vn_ref[0].astype(BF16)], _suffix_matrix(t),
                    jnp.zeros((2 * t, 1), F32), [_causal_mask(t)], False), True)
    suffix = _suffix_matrix(tk)
    for top in range(past // tk, 0, -group):
        @pl.when(min_scr[0] < SKIP_MASS)
        def _(top=top):
            blocks = range(top - 1, top - 1 - group, -1)
            keep(*_sb_group(q_st, [kh_ref[0, b * tk:(b + 1) * tk, :].astype(BF16) for b in blocks],
                            [vh_ref[0, b * tk:(b + 1) * tk, :].astype(BF16) for b in blocks],
                            suffix, car_scr[:, 0:1], [None] * group, False), False)
    o_ref[0] = _unstack_heads(acc_scr[...], t).astype(o_ref.dtype)


def _attention(q, k_new, v_new, k_hist, v_hist, t, tk, group):
    b, n, dm = q.shape
    assert n % t == 0 and dm % LANES == 0
    tile = pl.BlockSpec((1, t, LANES), lambda i, p, s: (i, s, p))
    seq = lambda rows: pl.BlockSpec((1, rows, LANES), lambda i, p, s: (i, 0, p))
    state = [pltpu.VMEM((2 * t, LANES), F32), pltpu.VMEM((2 * t, LANES), F32), pltpu.SMEM((1,), F32)]
    if k_hist is None:
        body = functools.partial(_attn_self_body, t=t)
        in_specs, args = [tile, seq(n), seq(n)], [q, k_new, v_new]
        scratch = [pltpu.VMEM((n // t, LANES, t), BF16), pltpu.VMEM((n // t, t, LANES), BF16)] + state
    else:
        past = k_hist.shape[1]
        assert n == t and past % (tk * group) == 0
        body = functools.partial(_attn_hist_body, tk=tk, group=group)
        in_specs, args = [tile, seq(n), seq(n), seq(past), seq(past)], [q, k_new, v_new, k_hist, v_hist]
        scratch = state
    return pl.pallas_call(
        body,
        out_shape=jax.ShapeDtypeStruct((b, n, dm), BF16),
        grid=(b, dm // LANES, n // t),
        in_specs=in_specs,
        out_specs=tile,
        scratch_shapes=scratch,
        compiler_params=_cparams(("parallel", "parallel", "arbitrary")),
        name="sb_attention",
    )(*args)


def _mix_body(diff_ref, o_ref, gp_ref, gs_ref, x_ref, wp_ref, ps_ref, wsb_ref, wo_ref, nf_ref,
              wrh_ref, wrl_ref, br_ref, x1_ref, hp_ref, route_ref, cnt_ref, *, tm):
    n_pool = wp_ref.shape[0]
    group = wp_ref.shape[1]
    diff = diff_ref[...]
    pool = jnp.concatenate(
        [jnp.dot(diff[:, g * group:(g + 1) * group], wp_ref[g], preferred_element_type=F32)
         for g in range(n_pool)], axis=-1) * ps_ref[...]
    sb = jnp.dot(o_ref[...], wsb_ref[...], preferred_element_type=F32)
    mixed = gp_ref[...].astype(F32) * pool + gs_ref[...].astype(F32) * sb
    x1 = x_ref[...] + jnp.dot(mixed.astype(BF16), wo_ref[...], preferred_element_type=F32)
    x1_ref[...] = x1
    h = (x1 * lax.rsqrt(jnp.mean(x1 * x1, axis=-1, keepdims=True) + EPS)) * nf_ref[...]

    d_half = h.shape[1] // 2
    lo_bits = pltpu.bitcast(h[:, :d_half].astype(BF16).astype(F32), jnp.uint32)
    hi_bits = pltpu.bitcast(h[:, d_half:].astype(BF16).astype(F32), jnp.uint32)
    hp_ref[...] = (lo_bits >> 16) | (hi_bits & jnp.uint32(0xFFFF0000))

    hh = h.astype(BF16)
    hl = (h - hh.astype(F32)).astype(BF16)
    logits = (jnp.dot(hh, wrh_ref[...], preferred_element_type=F32)
              + jnp.dot(hl, wrh_ref[...], preferred_element_type=F32)
              + jnp.dot(hh, wrl_ref[...], preferred_element_type=F32)) + br_ref[...]
    lane = lax.broadcasted_iota(jnp.int32, (tm, LANES), 1)
    big = jnp.int32(LANES)

    def first_max(vals):
        m = jnp.max(vals, axis=-1, keepdims=True)
        idx = jnp.min(jnp.where(vals == m, lane, big), axis=-1, keepdims=True)
        return m, idx

    gl = jnp.where(lane < N_GROUPS, logits, NEG)
    gmax, grp = first_max(gl)
    p_grp = 1.0 / jnp.sum(jnp.exp(gl - gmax), axis=-1, keepdims=True)
    e_lo = N_GROUPS + grp * PER_GROUP
    el = jnp.where((lane >= e_lo) & (lane < e_lo + PER_GROUP), logits, NEG)
    m1, i1 = first_max(el)
    m2, i2 = first_max(jnp.where(lane == i1, NEG, el))
    t2 = jnp.exp(m2 - m1)
    w1 = p_grp / (1.0 + t2)
    w2 = w1 * t2
    e1 = i1 - N_GROUPS
    e2 = i2 - N_GROUPS

    oh1 = jnp.where(lane == e1, 1.0, 0.0).astype(BF16)
    oh2 = jnp.where(lane == e2, 1.0, 0.0).astype(BF16)
    rr = lax.broadcasted_iota(jnp.int32, (tm, tm), 0)
    cc = lax.broadcasted_iota(jnp.int32, (tm, tm), 1)
    before = jnp.where(cc < rr, 1.0, 0.0).astype(BF16)
    ones = jnp.ones((8, tm), BF16)
    pre1 = jnp.dot(before, oh1, preferred_element_type=F32)
    pre2 = jnp.dot(before, oh2, preferred_element_type=F32)
    c1 = jnp.dot(ones, oh1, preferred_element_type=F32)
    c2 = jnp.dot(ones, oh2, preferred_element_type=F32)
    rank1 = jnp.sum(jnp.where(lane == e1, pre1, 0.0), axis=-1, keepdims=True)
    rank2 = jnp.sum(jnp.where(lane == e2, pre2 + c1[0:1, :], 0.0), axis=-1, keepdims=True)
    cnt_ref[0] = c1 + c2

    route = jnp.where(lane == 0, e1.astype(F32), 0.0)
    route = jnp.where(lane == 1, e2.astype(F32), route)
    route = jnp.where(lane == 2, w1, route)
    route = jnp.where(lane == 3, w2, route)
    route = jnp.where(lane == 4, rank1, route)
    route = jnp.where(lane == 5, rank2, route)
    route_ref[...] = route


def _mix(diff, o, gp, gs, x, wp, ps, wsb, wo, nf, wrh, wrl, br, tm):
    m, d = x.shape
    assert m % tm == 0
    row = lambda i: (i, 0)
    full = lambda a: pl.BlockSpec(a.shape, lambda i: (0,) * a.ndim, pipeline_mode=pl.Buffered(1))
    acts = [diff, o, gp, gs, x]
    consts = [wp, ps, wsb, wo, nf, wrh, wrl, br]
    return pl.pallas_call(
        functools.partial(_mix_body, tm=tm),
        out_shape=[jax.ShapeDtypeStruct((m, d), F32), jax.ShapeDtypeStruct((m, d // 2), jnp.uint32),
                   jax.ShapeDtypeStruct((m, LANES), F32), jax.ShapeDtypeStruct((m // tm, 8, LANES), F32)],
        grid=(m // tm,),
        in_specs=[pl.BlockSpec((tm, a.shape[1]), row) for a in acts] + [full(a) for a in consts],
        out_specs=[pl.BlockSpec((tm, d), row), pl.BlockSpec((tm, d // 2), row),
                   pl.BlockSpec((tm, LANES), row), pl.BlockSpec((1, 8, LANES), lambda i: (i, 0, 0))],
        compiler_params=_cparams(("parallel",)),
        name="mix_outproj_router",
    )(*acts, *consts)


def _row_wait(src_ref, dst_ref, sem, n_rows):
    pltpu.make_async_copy(src_ref.at[pl.ds(0, n_rows)], dst_ref.at[pl.ds(0, n_rows)], sem).wait()


def _dispatch_body(slot_ref, hp_ref, xs_in_ref, xs_ref, buf, sem, *, tm):
    del xs_in_ref
    i = pl.program_id(0)
    last = pl.num_programs(0) - 1
    par = i % 2

    def drain(slot):
        for _ in range(2):
            _row_wait(buf.at[slot], xs_ref, sem.at[slot], tm)

    @pl.when(i >= 2)
    def _():
        drain(par)

    buf[par] = hp_ref[...]
    base = i * tm

    def body(t, _):
        src = buf.at[par, pl.ds(t, 1)]
        for kk in range(2):
            dst = xs_ref.at[pl.ds(slot_ref[2 * (base + t) + kk], 1)]
            pltpu.make_async_copy(src, dst, sem.at[par]).start()
        return 0

    lax.fori_loop(0, tm, body, 0, unroll=8)

    @pl.when(i == last)
    def _():
        drain(par)

        @pl.when(i >= 1)
        def _():
            drain(1 - par)


def _dispatch(slots_flat, hp, xs_zero, tm):
    m, dh = hp.shape
    assert m % tm == 0
    any_spec = pl.BlockSpec(memory_space=pl.ANY)
    return pl.pallas_call(
        functools.partial(_dispatch_body, tm=tm),
        out_shape=jax.ShapeDtypeStruct(xs_zero.shape, xs_zero.dtype),
        grid_spec=pltpu.PrefetchScalarGridSpec(
            num_scalar_prefetch=1, grid=(m // tm,),
            in_specs=[pl.BlockSpec((tm, dh), lambda i, s: (i, 0)), any_spec], out_specs=any_spec,
            scratch_shapes=[pltpu.VMEM((2, tm, dh), hp.dtype), pltpu.SemaphoreType.DMA((2,))]),
        input_output_aliases={2: 0},
        compiler_params=_cparams(("arbitrary",), disable_bounds_checks=True, has_side_effects=True),
        name="moe_dispatch",
    )(slots_flat, hp, xs_zero)


def _ffn_body(be_ref, nv_ref, xs_ref, wg_ref, wu_ref, wd_ref, y_ref):
    del be_ref

    @pl.when(pl.program_id(0) >= nv_ref[0])
    def _():
        y_ref[...] = jnp.zeros_like(y_ref)

    @pl.when(pl.program_id(0) < nv_ref[0])
    def _():
        words = xs_ref[...]
        d_half = words.shape[1]
        x_lo = pltpu.bitcast(words << 16, F32).astype(BF16)
        x_hi = pltpu.bitcast(words & jnp.uint32(0xFFFF0000), F32).astype(BF16)

        def proj(w_ref):
            return (jnp.dot(x_lo, w_ref[0, :d_half, :], preferred_element_type=F32)
                    + jnp.dot(x_hi, w_ref[0, d_half:, :], preferred_element_type=F32))

        hid = jax.nn.silu(proj(wg_ref)) * proj(wu_ref)
        y_ref[...] = jnp.dot(hid.astype(BF16), wd_ref[0], preferred_element_type=F32)


def _ffn(block_expert, n_valid, xs, wg, wu, wd, bm):
    n_slots, d_half = xs.shape
    d = 2 * d_half
    de = wg.shape[2]
    live = lambda b, be, nv: (jnp.minimum(b, nv[0] - 1), 0)
    wsel = lambda b, be, nv: (be[b], 0, 0)
    return pl.pallas_call(
        _ffn_body,
        out_shape=jax.ShapeDtypeStruct((n_slots, d), F32),
        grid_spec=pltpu.PrefetchScalarGridSpec(
            num_scalar_prefetch=2, grid=(n_slots // bm,),
            in_specs=[pl.BlockSpec((bm, d_half), live), pl.BlockSpec((1, d, de), wsel),
                      pl.BlockSpec((1, d, de), wsel), pl.BlockSpec((1, de, d), wsel)],
            out_specs=pl.BlockSpec((bm, d), lambda b, be, nv: (b, 0))),
        compiler_params=_cparams(("arbitrary",)),
        name="moe_ffn",
    )(block_expert, n_valid, xs, wg, wu, wd)


def _final_body(slot_ref, x1_ref, route_ref, g_ref, y_hbm, o_ref, buf, sem, *, tm):
    i = pl.program_id(0)
    n_steps = pl.num_programs(0)

    def issue(step, par):
        def body(t, _):
            for kk in range(2):
                src = y_hbm.at[pl.ds(slot_ref[2 * (step * tm + t) + kk], 1)]
                pltpu.make_async_copy(src, buf.at[par, kk, pl.ds(t, 1)], sem.at[par]).start()
            return 0
        lax.fori_loop(0, tm, body, 0, unroll=8)

    @pl.when(i == 0)
    def _():
        issue(0, 0)

    @pl.when(i + 1 < n_steps)
    def _():
        issue(i + 1, (i + 1) % 2)

    par = i % 2
    for kk in range(2):
        _row_wait(y_hbm, buf.at[par, kk], sem.at[par], tm)
    route = route_ref[...]
    x2 = x1_ref[...] + route[:, 2:3] * buf[par, 0] + route[:, 3:4] * buf[par, 1]
    o_ref[...] = (x2 * lax.rsqrt(jnp.mean(x2 * x2, axis=-1, keepdims=True) + EPS)) * g_ref[...]


def _final(slots_flat, x1, route, g, y, tm):
    m, d = x1.shape
    assert m % tm == 0
    row = lambda i, s: (i, 0)
    return pl.pallas_call(
        functools.partial(_final_body, tm=tm),
        out_shape=jax.ShapeDtypeStruct((m, d), F32),
        grid_spec=pltpu.PrefetchScalarGridSpec(
            num_scalar_prefetch=1, grid=(m // tm,),
            in_specs=[pl.BlockSpec((tm, d), row), pl.BlockSpec((tm, LANES), row),
                      pl.BlockSpec((1, d), lambda i, s: (0, 0)), pl.BlockSpec(memory_space=pl.ANY)],
            out_specs=pl.BlockSpec((tm, d), row),
            scratch_shapes=[pltpu.VMEM((2, 2, tm, d), F32), pltpu.SemaphoreType.DMA((2,))]),
        compiler_params=_cparams(("arbitrary",), disable_bounds_checks=True),
        name="moe_combine_final_norm",
    )(slots_flat, x1, route, g, y)


def _routing_tables(route, cnt, tm, bm):
    m = route.shape[0]
    n_blocks = (2 * m) // bm + N_EXPERTS
    counts = cnt[:, 0, :N_EXPERTS].astype(jnp.int32)
    sizes = jnp.sum(counts, axis=0)
    padded = (sizes + bm - 1) // bm * bm
    pad_end = jnp.cumsum(padded)
    base = (pad_end - padded)[None, :] + jnp.cumsum(counts, axis=0) - counts
    base_tok = jnp.repeat(base, tm, axis=0)
    e = route[:, 0:2].astype(jnp.int32)
    rank = route[:, 4:6].astype(jnp.int32)
    sel = e[:, :, None] == jnp.arange(N_EXPERTS, dtype=jnp.int32)[None, None, :]
    slots = jnp.sum(jnp.where(sel, base_tok[:, None, :], 0), axis=-1) + rank
    n_valid = (pad_end[-1] // bm).astype(jnp.int32)
    blk = jnp.minimum(jnp.arange(n_blocks, dtype=jnp.int32), n_valid - 1)
    block_expert = jnp.minimum(jnp.sum(pad_end[None, :] <= (blk * bm)[:, None], axis=1), N_EXPERTS - 1)
    return slots.reshape(-1), block_expert.astype(jnp.int32), n_valid.reshape(1), n_blocks * bm


def _stream(x, pool_hist, k_hist, v_hist, p, *, tq, tk, group, tn, tm_in, tm_mix, tm_fin, bm):
    b, n, d = x.shape
    m = b * n
    past = 0 if k_hist is None else k_hist.shape[1]
    x2d = x.reshape(m, d)
    u, q, k, v, gp, gs = _inproj(x2d, p['norm_mix'], p['w_in'], tm_in)
    dp = u.shape[1]
    diff = _pool_diff(pool_hist, u.reshape(b, n, dp), past, tn).reshape(m, dp)
    shp = (b, n, k.shape[1])
    o = _attention(q.reshape(shp), k.reshape(shp), v.reshape(shp), k_hist, v_hist, tq, tk, group).reshape(m, -1)
    x1, hp, route, cnt = _mix(diff, o, gp, gs, x2d, p['w_pool'], p['pool_scale'], p['w_sb_out'], p['w_out'],
                              p['norm_ffn'], p['w_r_hi'], p['w_r_lo'], p['b_r'], tm_mix)
    slots, block_expert, n_valid, n_slots = _routing_tables(route, cnt, tm_mix, bm)
    xs = _dispatch(slots, hp, jnp.zeros((n_slots, d // 2), jnp.uint32), tm_mix)
    y = _ffn(block_expert, n_valid, xs, p['w_g'], p['w_u'], p['w_d'], bm)
    out = _final(slots, x1, route, p['norm_final'], y, tm_fin)
    return out.reshape(b, n, d), u.reshape(b, n, dp), k, v


def kernel(x_prompt, x_sample, cache_sb_k, cache_sb_v, state_pool, norm_mix, w_in, w_pool, pool_scale, w_sb_out,
           w_out, norm_ffn, w_router_group, b_router_group, w_router_expert, b_router_expert, w_exp_gate,
           w_exp_up, w_exp_down, norm_final):
    depth = w_in.shape[0]
    assert depth == 1
    bp, sp, d = x_prompt.shape
    bs, ss, _ = x_sample.shape
    past = cache_sb_k.shape[2]
    heads, hd = cache_sb_k.shape[3], cache_sb_k.shape[4]
    assert hd == HEAD_DIM
    dp = state_pool.shape[3]
    n_state = state_pool.shape[2]

    w_r = jnp.concatenate([w_router_group[0], w_router_expert[0]], axis=1)
    w_r = jnp.pad(w_r, ((0, 0), (0, LANES - w_r.shape[1])))
    w_r_hi = w_r.astype(BF16)
    b_r = jnp.concatenate([b_router_group[0], b_router_expert[0]])
    p = dict(
        norm_mix=norm_mix[0][None, :], w_in=w_in[0].astype(BF16), w_pool=w_pool[0].astype(BF16),
        pool_scale=pool_scale[0][None, :], w_sb_out=w_sb_out[0].astype(BF16), w_out=w_out[0].astype(BF16),
        norm_ffn=norm_ffn[0][None, :], w_r_hi=w_r_hi, w_r_lo=(w_r - w_r_hi.astype(F32)).astype(BF16),
        b_r=jnp.pad(b_r, (0, LANES - b_r.shape[0]))[None, :].astype(F32),
        w_g=w_exp_gate[0].astype(BF16), w_u=w_exp_up[0].astype(BF16), w_d=w_exp_down[0].astype(BF16),
        norm_final=norm_final[None, :])

    hist_p = jnp.zeros((bp, POOL_HIST, dp), F32)
    yp, up, kp, vp = _stream(x_prompt, hist_p, None, None, p, tq=256, tk=256, group=2, tn=min(sp, 512),
                             tm_in=min(512, bp * sp), tm_mix=256, tm_fin=256, bm=256)
    hist_s = jnp.pad(state_pool[0], ((0, 0), (POOL_HIST - n_state, 0), (0, 0)))
    kh = cache_sb_k[0].reshape(bs, past, heads * hd)
    vh = cache_sb_v[0].reshape(bs, past, heads * hd)
    ys, us, ks, vs = _stream(x_sample, hist_s, kh, vh, p, tq=ss, tk=256, group=2, tn=ss,
                             tm_in=min(512, bs * ss), tm_mix=256, tm_fin=256, bm=256)

    def pool_state(hist, u):
        return jnp.concatenate([hist[:, POOL_HIST - n_state:], u], axis=1)[:, -n_state:][None]

    return (yp, ys,
            kp.reshape(1, bp, sp, heads, hd), vp.reshape(1, bp, sp, heads, hd), pool_state(hist_p, up),
            ks.reshape(1, bs, ss, heads, hd), vs.reshape(1, bs, ss, heads, hd), pool_state(hist_s, us))
```

```python
import functools

import jax
import jax.numpy as jnp
from jax import lax
from jax.experimental import pallas as pl
from jax.experimental.pallas import tpu as pltpu

F32 = jnp.float32
BF16 = jnp.bfloat16

EPS = 1e-6
HEAD_DIM = 64
LANES = 128
POOL_WINDOWS = (2, 4, 8, 16)
POOL_HIST = 16
N_GROUPS = 4
PER_GROUP = 8
N_EXPERTS = N_GROUPS * PER_GROUP
VMEM_LIMIT = 56 * 1024 * 1024
NEG = -1e30
LOG2E = 1.4426950408889634
SKIP_MASS = 160.0


def _fit(m, tile):
    while m % tile:
        tile //= 2
    return tile


def _cparams(sem, **kw):
    return pltpu.CompilerParams(dimension_semantics=sem, vmem_limit_bytes=VMEM_LIMIT, **kw)


def _prenorm_body(x_ref, g_ref, h_ref):
    x = x_ref[...]
    r = lax.rsqrt(jnp.mean(x * x, axis=-1, keepdims=True) + EPS)
    h_ref[...] = ((x * r) * g_ref[...]).astype(BF16)


def _prenorm(x, g, tm):
    m, d = x.shape
    assert m % tm == 0
    return pl.pallas_call(
        _prenorm_body,
        out_shape=jax.ShapeDtypeStruct((m, d), BF16),
        grid=(m // tm,),
        in_specs=[pl.BlockSpec((tm, d), lambda i: (i, 0)), pl.BlockSpec((1, d), lambda i: (0, 0))],
        out_specs=pl.BlockSpec((tm, d), lambda i: (i, 0)),
        compiler_params=_cparams(("parallel",)),
        name="prenorm",
    )(x, g)


def _proj_main_body(h_ref, w_ref, u_ref, q_ref, k_ref, v_ref):
    j = pl.program_id(1)

    def proj():
        return jnp.dot(h_ref[...], w_ref[...], preferred_element_type=F32)

    @pl.when(j == 0)
    def _():
        u_ref[...] = proj()

    @pl.when(j == 1)
    def _():
        q_ref[...] = (proj() * (HEAD_DIM ** -0.5)).astype(BF16)

    @pl.when(j == 2)
    def _():
        k_ref[...] = proj()

    @pl.when(j == 3)
    def _():
        v_ref[...] = proj()


def _proj_gates_body(h_ref, w_ref, gp_ref, gs_ref, *, tn):
    j = pl.program_id(1)
    for jj, ref in ((0, gp_ref), (2, gs_ref)):
        for half in range(2):
            @pl.when(j == jj + half)
            def _(ref=ref, half=half):
                a = jnp.dot(h_ref[...], w_ref[...], preferred_element_type=F32)
                ref[:, half * tn:(half + 1) * tn] = (0.5 * jnp.tanh(0.5 * a) + 0.5).astype(BF16)


def _inproj(h, w_bf, tm):
    m, d = h.shape
    tn = d // 2
    assert w_bf.shape == (d, 8 * tn) and m % tm == 0
    row = lambda i, j: (i, 0)
    h_spec = pl.BlockSpec((tm, d), row)
    half = jax.ShapeDtypeStruct((m, tn), F32)
    u, q, k, v = pl.pallas_call(
        _proj_main_body,
        out_shape=[half, jax.ShapeDtypeStruct((m, tn), BF16), half, half],
        grid=(m // tm, 4),
        in_specs=[h_spec, pl.BlockSpec((d, tn), lambda i, j: (0, j))],
        out_specs=[pl.BlockSpec((tm, tn), row)] * 4,
        compiler_params=_cparams(("parallel", "arbitrary")),
        name="inproj_main",
    )(h, w_bf)
    gp, gs = pl.pallas_call(
        functools.partial(_proj_gates_body, tn=tn),
        out_shape=[jax.ShapeDtypeStruct((m, d), BF16)] * 2,
        grid=(m // tm, 4),
        in_specs=[h_spec, pl.BlockSpec((d, tn), lambda i, j: (0, j + 4))],
        out_specs=[pl.BlockSpec((tm, d), row)] * 2,
        compiler_params=_cparams(("parallel", "arbitrary")),
        name="inproj_gates",
    )(h, w_bf)
    return u, q, k, v, gp, gs


def _pool_body(hist_ref, u_ref, o_ref, ext_scr, *, pos0, tn):
    s = pl.program_id(1)

    @pl.when(s == 0)
    def _():
        ext_scr[0:POOL_HIST, :] = hist_ref[0]

    ext_scr[POOL_HIST:POOL_HIST + tn, :] = u_ref[0]
    pos = pos0 + s * tn + lax.broadcasted_iota(jnp.int32, (tn, 1), 0)
    group = u_ref.shape[2] // len(POOL_WINDOWS)
    for g, w in enumerate(POOL_WINDOWS):
        lo, hi = g * group, (g + 1) * group
        cur = ext_scr[POOL_HIST:POOL_HIST + tn, lo:hi]
        tot = cur
        for dlt in range(1, w):
            tot = tot + ext_scr[POOL_HIST - dlt:POOL_HIST - dlt + tn, lo:hi]
        cnt = jnp.minimum(pos + 1, w).astype(F32)
        o_ref[0, :, lo:hi] = (tot / cnt - cur).astype(BF16)
    ext_scr[0:POOL_HIST, :] = ext_scr[tn:tn + POOL_HIST, :]


def _pool_diff(hist, u, pos0, tn):
    b, n, dp = u.shape
    assert n % tn == 0 and hist.shape == (b, POOL_HIST, dp)
    return pl.pallas_call(
        functools.partial(_pool_body, pos0=pos0, tn=tn),
        out_shape=jax.ShapeDtypeStruct((b, n, dp), BF16),
        grid=(b, n // tn),
        in_specs=[pl.BlockSpec((1, POOL_HIST, dp), lambda i, s: (i, 0, 0)),
                  pl.BlockSpec((1, tn, dp), lambda i, s: (i, s, 0))],
        out_specs=pl.BlockSpec((1, tn, dp), lambda i, s: (i, s, 0)),
        scratch_shapes=[pltpu.VMEM((POOL_HIST + tn, dp), F32)],
        compiler_params=_cparams(("parallel", "arbitrary")),
        name="pool_diff",
    )(hist, u)


def _softplus2(z2):
    neg_abs = pltpu.bitcast(pltpu.bitcast(z2, jnp.uint32) | jnp.uint32(0x80000000), F32)
    return jnp.maximum(z2, 0.0) + jnp.log2(1.0 + jnp.exp2(neg_abs))


def _suffix_matrix(n):
    r = lax.broadcasted_iota(jnp.int32, (n, n), 0)
    c = lax.broadcasted_iota(jnp.int32, (n, n), 1)
    return jnp.where(r >= c, 1.0, 0.0).astype(BF16)


def _stack_heads(q2):
    lane = lax.broadcasted_iota(jnp.int32, q2.shape, 1)
    zero = jnp.zeros_like(q2)
    return jnp.concatenate([jnp.where(lane < HEAD_DIM, q2, zero), jnp.where(lane >= HEAD_DIM, q2, zero)], axis=0)


def _unstack_heads(acc, t):
    lane = lax.broadcasted_iota(jnp.int32, (t, LANES), 1)
    return jnp.where(lane < HEAD_DIM, acc[:t], acc[t:])


def _causal_mask(t):
    r = lax.broadcasted_iota(jnp.int32, (t, t), 0)
    c = lax.broadcasted_iota(jnp.int32, (t, t), 1)
    m = c < r
    return jnp.concatenate([m, m], axis=0)


def _sb_group(q_st, k_blocks, v_blocks, suffix, carry, masks, transposed_keys):
    dn = (((1,), (0,)), ((), ())) if transposed_keys else (((1,), (1,)), ((), ()))
    zs = [lax.dot_general(q_st, kb, dn, preferred_element_type=F32) * LOG2E for kb in k_blocks]
    his, los = [], []
    for z, mask in zip(zs, masks):
        sp = _softplus2(z)
        if mask is not None:
            sp = jnp.where(mask, sp, 0.0)
        hi = pltpu.bitcast(pltpu.bitcast(sp, jnp.uint32) & jnp.uint32(0xFFFF0000), F32)
        his.append(hi.astype(BF16))
        los.append((sp - hi).astype(BF16))
    cs = [jnp.dot(hi, suffix, preferred_element_type=F32) + jnp.dot(lo, suffix, preferred_element_type=F32)
          for hi, lo in zip(his, los)]
    out = None
    for z, c, vb, mask in zip(zs, cs, v_blocks, masks):
        arg = z - c - carry
        if mask is not None:
            arg = jnp.where(mask, arg, NEG)
        o = jnp.dot(jnp.exp2(arg).astype(BF16), vb, preferred_element_type=F32)
        out = o if out is None else out + o
        carry = carry + c[:, 0:1]
    return out, carry


def _attn_self_body(q_ref, k_ref, v_ref, o_ref, kt_scr, v_scr, acc_scr, car_scr, min_scr, *, t):
    qi = pl.program_id(2)

    @pl.when(qi == 0)
    def _fill():
        def body(c, _):
            row0 = pl.multiple_of(c * t, t)
            kt_scr[c] = k_ref[0, pl.ds(row0, t), :].T.astype(BF16)
            v_scr[c] = v_ref[0, pl.ds(row0, t), :].astype(BF16)
            return 0
        lax.fori_loop(0, kt_scr.shape[0], body, 0)

    q_st = _stack_heads(q_ref[0])
    suffix = _suffix_matrix(t)
    mask = _causal_mask(t)

    def run(blocks, masks, first):
        carry = jnp.zeros((2 * t, 1), F32) if first else car_scr[:, 0:1]
        out, carry = _sb_group(q_st, [kt_scr[b] for b in blocks], [v_scr[b] for b in blocks], suffix, carry,
                               masks, True)
        acc_scr[...] = out if first else acc_scr[...] + out
        car_scr[...] = jnp.broadcast_to(carry, car_scr.shape)
        min_scr[0] = jnp.min(carry)

    @pl.when(qi == 0)
    def _():
        run([0], [mask], True)

    @pl.when(qi > 0)
    def _():
        run([qi, qi - 1], [mask, None], True)

    rest = jnp.maximum(qi - 1, 0)

    def more(it):
        return (it < rest // 2) & (min_scr[0] < SKIP_MASS)

    def pair(it):
        b0 = qi - 2 - 2 * it
        run([b0, b0 - 1], [None, None], False)
        return it + 1

    lax.while_loop(more, pair, 0)

    @pl.when((rest % 2 == 1) & (min_scr[0] < SKIP_MASS))
    def _():
        run([0], [None], False)

    o_ref[0] = _unstack_heads(acc_scr[...], t).astype(o_ref.dtype)


def _attn_hist_body(q_ref, kn_ref, vn_ref, kh_ref, vh_ref, o_ref, acc_scr, car_scr, min_scr, *, tk, group):
    t = q_ref.shape[1]
    past = kh_ref.shape[1]
    q_st = _stack_heads(q_ref[0])

    def keep(out, carry, first):
        acc_scr[...] = out if first else acc_scr[...] + out
        car_scr[...] = jnp.broadcast_to(carry, car_scr.shape)
        min_scr[0] = jnp.min(carry)

    keep(*_sb_group(q_st, [kn_ref[0].astype(BF16)], [vn_ref[0].astype(BF16)], _suffix_matrix(t),
                    jnp.zeros((2 * t, 1), F32), [_causal_mask(t)], False), True)
    suffix = _suffix_matrix(tk)
    for top in range(past // tk, 0, -group):
        @pl.when(min_scr[0] < SKIP_MASS)
        def _(top=top):
            blocks = range(top - 1, top - 1 - group, -1)
            keep(*_sb_group(q_st, [kh_ref[0, b * tk:(b + 1) * tk, :].astype(BF16) for b in blocks],
                            [vh_ref[0, b * tk:(b + 1) * tk, :].astype(BF16) for b in blocks],
                            suffix, car_scr[:, 0:1], [None] * group, False), False)
    o_ref[0] = _unstack_heads(acc_scr[...], t).astype(o_ref.dtype)


def _attention(q, k_new, v_new, k_hist, v_hist, t, tk, group):
    b, n, dm = q.shape
    assert n % t == 0 and dm % LANES == 0
    tile = pl.BlockSpec((1, t, LANES), lambda i, p, s: (i, s, p))
    seq = lambda rows: pl.BlockSpec((1, rows, LANES), lambda i, p, s: (i, 0, p))
    state = [pltpu.VMEM((2 * t, LANES), F32), pltpu.VMEM((2 * t, LANES), F32), pltpu.SMEM((1,), F32)]
    if k_hist is None:
        body = functools.partial(_attn_self_body, t=t)
        in_specs, args = [tile, seq(n), seq(n)], [q, k_new, v_new]
        scratch = [pltpu.VMEM((n // t, LANES, t), BF16), pltpu.VMEM((n // t, t, LANES), BF16)] + state
    else:
        past = k_hist.shape[1]
        assert n == t and past % (tk * group) == 0
        body = functools.partial(_attn_hist_body, tk=tk, group=group)
        in_specs, args = [tile, seq(n), seq(n), seq(past), seq(past)], [q, k_new, v_new, k_hist, v_hist]
        scratch = state
    return pl.pallas_call(
        body,
        out_shape=jax.ShapeDtypeStruct((b, n, dm), BF16),
        grid=(b, dm // LANES, n // t),
        in_specs=in_specs,
        out_specs=tile,
        scratch_shapes=scratch,
        compiler_params=_cparams(("parallel", "parallel", "arbitrary")),
        name="sb_attention",
    )(*args)


def _mix_body(diff_ref, o_ref, gp_ref, gs_ref, x_ref, wp_ref, ps_ref, wsb_ref, wo_ref, nf_ref,
              wr_ref, br_ref, x1_ref, hp_ref, route_ref, cnt_ref, *, tm):
    n_pool = wp_ref.shape[0]
    group = wp_ref.shape[1]
    diff = diff_ref[...]
    pool = jnp.concatenate(
        [jnp.dot(diff[:, g * group:(g + 1) * group], wp_ref[g], preferred_element_type=F32)
         for g in range(n_pool)], axis=-1) * ps_ref[...]
    sb = jnp.dot(o_ref[...], wsb_ref[...], preferred_element_type=F32)
    mixed = gp_ref[...].astype(F32) * pool + gs_ref[...].astype(F32) * sb
    x1 = x_ref[...] + jnp.dot(mixed.astype(BF16), wo_ref[...], preferred_element_type=F32)
    x1_ref[...] = x1
    h = (x1 * lax.rsqrt(jnp.mean(x1 * x1, axis=-1, keepdims=True) + EPS)) * nf_ref[...]

    d_half = h.shape[1] // 2
    lo_bits = pltpu.bitcast(h[:, :d_half].astype(BF16).astype(F32), jnp.uint32)
    hi_bits = pltpu.bitcast(h[:, d_half:].astype(BF16).astype(F32), jnp.uint32)
    hp_ref[...] = (lo_bits >> 16) | (hi_bits & jnp.uint32(0xFFFF0000))

    hh = h.astype(BF16)
    hl = (h - hh.astype(F32)).astype(BF16)
    r = jnp.dot(jnp.concatenate([hh, hl], axis=0), wr_ref[...], preferred_element_type=F32)
    logits = (r[:tm, :LANES] + r[:tm, LANES:]) + (r[tm:, :LANES] + r[tm:, LANES:]) + br_ref[...]
    lane = lax.broadcasted_iota(jnp.int32, (tm, LANES), 1)
    big = jnp.int32(LANES)

    def first_max(vals):
        m = jnp.max(vals, axis=-1, keepdims=True)
        idx = jnp.min(jnp.where(vals == m, lane, big), axis=-1, keepdims=True)
        return m, idx

    gl = jnp.where(lane < N_GROUPS, logits, NEG)
    gmax, grp = first_max(gl)
    p_grp = 1.0 / jnp.sum(jnp.exp(gl - gmax), axis=-1, keepdims=True)
    e_lo = N_GROUPS + grp * PER_GROUP
    el = jnp.where((lane >= e_lo) & (lane < e_lo + PER_GROUP), logits, NEG)
    m1, i1 = first_max(el)
    m2, i2 = first_max(jnp.where(lane == i1, NEG, el))
    t2 = jnp.exp(m2 - m1)
    w1 = p_grp / (1.0 + t2)
    w2 = w1 * t2
    e1 = i1 - N_GROUPS
    e2 = i2 - N_GROUPS

    oh1 = jnp.where(lane == e1, 1.0, 0.0).astype(BF16)
    oh2 = jnp.where(lane == e2, 1.0, 0.0).astype(BF16)
    rr = lax.broadcasted_iota(jnp.int32, (tm, tm), 0)
    cc = lax.broadcasted_iota(jnp.int32, (tm, tm), 1)
    before = jnp.where(cc < rr, 1.0, 0.0).astype(BF16)
    ones = jnp.ones((8, tm), BF16)
    pre1 = jnp.dot(before, oh1, preferred_element_type=F32)
    pre2 = jnp.dot(before, oh2, preferred_element_type=F32)
    c1 = jnp.dot(ones, oh1, preferred_element_type=F32)
    c2 = jnp.dot(ones, oh2, preferred_element_type=F32)
    rank1 = jnp.sum(jnp.where(lane == e1, pre1, 0.0), axis=-1, keepdims=True)
    rank2 = jnp.sum(jnp.where(lane == e2, pre2 + c1[0:1, :], 0.0), axis=-1, keepdims=True)
    cnt_ref[0] = c1 + c2

    route = jnp.where(lane == 0, e1.astype(F32), 0.0)
    route = jnp.where(lane == 1, e2.astype(F32), route)
    route = jnp.where(lane == 2, w1, route)
    route = jnp.where(lane == 3, w2, route)
    route = jnp.where(lane == 4, rank1, route)
    route = jnp.where(lane == 5, rank2, route)
    route_ref[...] = route


def _mix(diff, o, gp, gs, x, wp, ps, wsb, wo, nf, wr, br, tm):
    m, d = x.shape
    assert m % tm == 0
    row = lambda i: (i, 0)
    full = lambda a: pl.BlockSpec(a.shape, lambda i: (0,) * a.ndim, pipeline_mode=pl.Buffered(1))
    acts = [diff, o, gp, gs, x]
    consts = [wp, ps, wsb, wo, nf, wr, br]
    return pl.pallas_call(
        functools.partial(_mix_body, tm=tm),
        out_shape=[jax.ShapeDtypeStruct((m, d), F32), jax.ShapeDtypeStruct((m, d // 2), jnp.uint32),
                   jax.ShapeDtypeStruct((m, LANES), F32), jax.ShapeDtypeStruct((m // tm, 8, LANES), F32)],
        grid=(m // tm,),
        in_specs=[pl.BlockSpec((tm, a.shape[1]), row) for a in acts] + [full(a) for a in consts],
        out_specs=[pl.BlockSpec((tm, d), row), pl.BlockSpec((tm, d // 2), row),
                   pl.BlockSpec((tm, LANES), row), pl.BlockSpec((1, 8, LANES), lambda i: (i, 0, 0))],
        compiler_params=_cparams(("parallel",)),
        name="mix_outproj_router",
    )(*acts, *consts)


def _row_wait(src_ref, dst_ref, sem, n_rows):
    pltpu.make_async_copy(src_ref.at[pl.ds(0, n_rows)], dst_ref.at[pl.ds(0, n_rows)], sem).wait()


def _dispatch_body(slot_ref, zero_ref, hp_ref, xs_ref, buf, zbuf, sem, zsem, *, tm):
    i = pl.program_id(0)
    last = pl.num_programs(0) - 1
    par = i % 2

    @pl.when(i == 0)
    def _():
        zbuf[...] = jnp.zeros_like(zbuf)
        bm = zbuf.shape[0]
        for wait in (False, True):
            for z in range(zero_ref.shape[0]):
                @pl.when(zero_ref[z] >= 0)
                def _(z=z, wait=wait):
                    start = pl.multiple_of(jnp.maximum(zero_ref[z], 0), bm)
                    copy = pltpu.make_async_copy(zbuf, xs_ref.at[pl.ds(start, bm)], zsem)
                    copy.wait() if wait else copy.start()

    def drain(slot):
        for _ in range(2):
            _row_wait(buf.at[slot], xs_ref, sem.at[slot], tm)

    @pl.when(i >= 2)
    def _():
        drain(par)

    buf[par] = hp_ref[...]
    base = i * tm

    def body(t, _):
        src = buf.at[par, pl.ds(t, 1)]
        for kk in range(2):
            dst = xs_ref.at[pl.ds(slot_ref[2 * (base + t) + kk], 1)]
            pltpu.make_async_copy(src, dst, sem.at[par]).start()
        return 0

    lax.fori_loop(0, tm, body, 0, unroll=8)

    @pl.when(i == last)
    def _():
        drain(par)

        @pl.when(i >= 1)
        def _():
            drain(1 - par)


def _dispatch(slots_flat, zero_starts, hp, n_slots, tm, bm):
    m, dh = hp.shape
    assert m % tm == 0
    return pl.pallas_call(
        functools.partial(_dispatch_body, tm=tm),
        out_shape=jax.ShapeDtypeStruct((n_slots, dh), hp.dtype),
        grid_spec=pltpu.PrefetchScalarGridSpec(
            num_scalar_prefetch=2, grid=(m // tm,),
            in_specs=[pl.BlockSpec((tm, dh), lambda i, s, z: (i, 0))],
            out_specs=pl.BlockSpec(memory_space=pl.ANY),
            scratch_shapes=[pltpu.VMEM((2, tm, dh), hp.dtype), pltpu.VMEM((bm, dh), hp.dtype),
                            pltpu.SemaphoreType.DMA((2,)), pltpu.SemaphoreType.DMA]),
        compiler_params=_cparams(("arbitrary",), disable_bounds_checks=True, has_side_effects=True),
        name="moe_dispatch",
    )(slots_flat, zero_starts, hp)


def _ffn_body(be_ref, nv_ref, xs_ref, wg_ref, wu_ref, wd_ref, y_ref):
    del be_ref

    @pl.when(pl.program_id(0) >= nv_ref[0])
    def _():
        y_ref[...] = jnp.zeros_like(y_ref)

    @pl.when(pl.program_id(0) < nv_ref[0])
    def _():
        words = xs_ref[...]
        d_half = words.shape[1]
        x_lo = pltpu.bitcast(words << 16, F32).astype(BF16)
        x_hi = pltpu.bitcast(words & jnp.uint32(0xFFFF0000), F32).astype(BF16)

        def proj(w_ref):
            return (jnp.dot(x_lo, w_ref[0, :d_half, :], preferred_element_type=F32)
                    + jnp.dot(x_hi, w_ref[0, d_half:, :], preferred_element_type=F32))

        hid = jax.nn.silu(proj(wg_ref)) * proj(wu_ref)
        y_ref[...] = jnp.dot(hid.astype(BF16), wd_ref[0], preferred_element_type=F32)


def _ffn(block_expert, n_valid, xs, wg, wu, wd, bm):
    n_slots, d_half = xs.shape
    d = 2 * d_half
    de = wg.shape[2]
    live = lambda b, be, nv: (jnp.minimum(b, nv[0] - 1), 0)
    wsel = lambda b, be, nv: (be[b], 0, 0)
    return pl.pallas_call(
        _ffn_body,
        out_shape=jax.ShapeDtypeStruct((n_slots, d), F32),
        grid_spec=pltpu.PrefetchScalarGridSpec(
            num_scalar_prefetch=2, grid=(n_slots // bm,),
            in_specs=[pl.BlockSpec((bm, d_half), live), pl.BlockSpec((1, d, de), wsel),
                      pl.BlockSpec((1, d, de), wsel), pl.BlockSpec((1, de, d), wsel)],
            out_specs=pl.BlockSpec((bm, d), lambda b, be, nv: (b, 0))),
        compiler_params=_cparams(("arbitrary",)),
        name="moe_ffn",
    )(block_expert, n_valid, xs, wg, wu, wd)


def _final_body(slot_ref, x1_ref, route_ref, g_ref, y_hbm, o_ref, buf, sem, *, tm):
    i = pl.program_id(0)
    n_steps = pl.num_programs(0)

    def issue(step, par):
        def body(t, _):
            for kk in range(2):
                src = y_hbm.at[pl.ds(slot_ref[2 * (step * tm + t) + kk], 1)]
                pltpu.make_async_copy(src, buf.at[par, kk, pl.ds(t, 1)], sem.at[par]).start()
            return 0
        lax.fori_loop(0, tm, body, 0, unroll=8)

    @pl.when(i == 0)
    def _():
        issue(0, 0)

    @pl.when(i + 1 < n_steps)
    def _():
        issue(i + 1, (i + 1) % 2)

    par = i % 2
    for kk in range(2):
        _row_wait(y_hbm, buf.at[par, kk], sem.at[par], tm)
    route = route_ref[...]
    x2 = x1_ref[...] + route[:, 2:3] * buf[par, 0] + route[:, 3:4] * buf[par, 1]
    o_ref[...] = (x2 * lax.rsqrt(jnp.mean(x2 * x2, axis=-1, keepdims=True) + EPS)) * g_ref[...]


def _final(slots_flat, x1, route, g, y, tm):
    m, d = x1.shape
    assert m % tm == 0
    row = lambda i, s: (i, 0)
    return pl.pallas_call(
        functools.partial(_final_body, tm=tm),
        out_shape=jax.ShapeDtypeStruct((m, d), F32),
        grid_spec=pltpu.PrefetchScalarGridSpec(
            num_scalar_prefetch=1, grid=(m // tm,),
            in_specs=[pl.BlockSpec((tm, d), row), pl.BlockSpec((tm, LANES), row),
                      pl.BlockSpec((1, d), lambda i, s: (0, 0)), pl.BlockSpec(memory_space=pl.ANY)],
            out_specs=pl.BlockSpec((tm, d), row),
            scratch_shapes=[pltpu.VMEM((2, 2, tm, d), F32), pltpu.SemaphoreType.DMA((2,))]),
        compiler_params=_cparams(("arbitrary",), disable_bounds_checks=True),
        name="moe_combine_final_norm",
    )(slots_flat, x1, route, g, y)


def _routing_tables(route, cnt, tm, bm):
    m = route.shape[0]
    n_blocks = (2 * m) // bm + N_EXPERTS
    counts = cnt[:, 0, :N_EXPERTS].astype(jnp.int32)
    sizes = jnp.sum(counts, axis=0)
    padded = (sizes + bm - 1) // bm * bm
    pad_end = jnp.cumsum(padded)
    base = (pad_end - padded)[None, :] + jnp.cumsum(counts, axis=0) - counts
    base_tok = jnp.repeat(base, tm, axis=0)
    e = route[:, 0:2].astype(jnp.int32)
    rank = route[:, 4:6].astype(jnp.int32)
    sel = e[:, :, None] == jnp.arange(N_EXPERTS, dtype=jnp.int32)[None, None, :]
    slots = jnp.sum(jnp.where(sel, base_tok[:, None, :], 0), axis=-1) + rank
    n_valid = (pad_end[-1] // bm).astype(jnp.int32)
    blk = jnp.minimum(jnp.arange(n_blocks, dtype=jnp.int32), n_valid - 1)
    block_expert = jnp.minimum(jnp.sum(pad_end[None, :] <= (blk * bm)[:, None], axis=1), N_EXPERTS - 1)
    last_blk = jnp.where(padded > 0, pad_end - bm, -1)
    tail = n_valid + jnp.arange(N_EXPERTS, dtype=jnp.int32)
    tail = jnp.where(tail < n_blocks, tail * bm, -1)
    zero_starts = jnp.concatenate([last_blk, tail]).astype(jnp.int32)
    return slots.reshape(-1), block_expert.astype(jnp.int32), n_valid.reshape(1), zero_starts, n_blocks * bm


def _stream(x, pool_hist, k_hist, v_hist, p, *, tq, tk, group, tn, tm_in, tm_mix, tm_fin, bm):
    b, n, d = x.shape
    m = b * n
    past = 0 if k_hist is None else k_hist.shape[1]
    x2d = x.reshape(m, d)
    u, q, k, v, gp, gs = _inproj(_prenorm(x2d, p['norm_mix'], _fit(m, 512)), p['w_in'], _fit(m, tm_in))
    dp = u.shape[1]
    diff = _pool_diff(pool_hist, u.reshape(b, n, dp), past, tn).reshape(m, dp)
    shp = (b, n, k.shape[1])
    o = _attention(q.reshape(shp), k.reshape(shp), v.reshape(shp), k_hist, v_hist, tq, tk, group).reshape(m, -1)
    x1, hp, route, cnt = _mix(diff, o, gp, gs, x2d, p['w_pool'], p['pool_scale'], p['w_sb_out'], p['w_out'],
                              p['norm_ffn'], p['w_r'], p['b_r'], tm_mix)
    slots, block_expert, n_valid, zero_starts, n_slots = _routing_tables(route, cnt, tm_mix, bm)
    xs = _dispatch(slots, zero_starts, hp, n_slots, tm_mix, bm)
    y = _ffn(block_expert, n_valid, xs, p['w_g'], p['w_u'], p['w_d'], bm)
    out = _final(slots, x1, route, p['norm_final'], y, tm_fin)
    return out.reshape(b, n, d), u.reshape(b, n, dp), k, v


def kernel(x_prompt, x_sample, cache_sb_k, cache_sb_v, state_pool, norm_mix, w_in, w_pool, pool_scale, w_sb_out,
           w_out, norm_ffn, w_router_group, b_router_group, w_router_expert, b_router_expert, w_exp_gate,
           w_exp_up, w_exp_down, norm_final):
    depth = w_in.shape[0]
    assert depth == 1
    bp, sp, d = x_prompt.shape
    bs, ss, _ = x_sample.shape
    past = cache_sb_k.shape[2]
    heads, hd = cache_sb_k.shape[3], cache_sb_k.shape[4]
    assert hd == HEAD_DIM
    dp = state_pool.shape[3]
    n_state = state_pool.shape[2]

    w_r = jnp.concatenate([w_router_group[0], w_router_expert[0]], axis=1)
    w_r = jnp.pad(w_r, ((0, 0), (0, LANES - w_r.shape[1])))
    w_r_hi = w_r.astype(BF16)
    b_r = jnp.concatenate([b_router_group[0], b_router_expert[0]])
    p = dict(
        norm_mix=norm_mix[0][None, :], w_in=w_in[0].astype(BF16), w_pool=w_pool[0].astype(BF16),
        pool_scale=pool_scale[0][None, :], w_sb_out=w_sb_out[0].astype(BF16), w_out=w_out[0].astype(BF16),
        norm_ffn=norm_ffn[0][None, :],
        w_r=jnp.concatenate([w_r_hi, (w_r - w_r_hi.astype(F32)).astype(BF16)], axis=1),
        b_r=jnp.pad(b_r, (0, LANES - b_r.shape[0]))[None, :].astype(F32),
        w_g=w_exp_gate[0].astype(BF16), w_u=w_exp_up[0].astype(BF16), w_d=w_exp_down[0].astype(BF16),
        norm_final=norm_final[None, :])

    hist_p = jnp.zeros((bp, POOL_HIST, dp), F32)
    yp, up, kp, vp = _stream(x_prompt, hist_p, None, None, p, tq=256, tk=256, group=2, tn=min(sp, 512),
                             tm_in=1024, tm_mix=256, tm_fin=256, bm=256)
    hist_s = jnp.pad(state_pool[0], ((0, 0), (POOL_HIST - n_state, 0), (0, 0)))
    kh = cache_sb_k[0].reshape(bs, past, heads * hd)
    vh = cache_sb_v[0].reshape(bs, past, heads * hd)
    ys, us, ks, vs = _stream(x_sample, hist_s, kh, vh, p, tq=ss, tk=256, group=2, tn=ss,
                             tm_in=1024, tm_mix=256, tm_fin=256, bm=256)

    def pool_state(hist, u):
        return jnp.concatenate([hist[:, POOL_HIST - n_state:], u], axis=1)[:, -n_state:][None]

    return (yp, ys,
            kp.reshape(1, bp, sp, heads, hd), vp.reshape(1, bp, sp, heads, hd), pool_state(hist_p, up),
            ks.reshape(1, bs, ss, heads, hd), vs.reshape(1, bs, ss, heads, hd), pool_state(hist_s, us))
```

```python
import functools

import jax
import jax.numpy as jnp
from jax import lax
from jax.experimental import pallas as pl
from jax.experimental.pallas import tpu as pltpu

F32 = jnp.float32
BF16 = jnp.bfloat16

EPS = 1e-6
HEAD_DIM = 64
LANES = 128
POOL_WINDOWS = (2, 4, 8, 16)
POOL_HIST = 16
N_GROUPS = 4
PER_GROUP = 8
N_EXPERTS = N_GROUPS * PER_GROUP
VMEM_LIMIT = 58 * 1024 * 1024
NEG = -1e30
LOG2E = 1.4426950408889634
SKIP_MASS = 160.0


def _fit(m, tile):
    while m % tile:
        tile //= 2
    return tile


def _cparams(sem, **kw):
    return pltpu.CompilerParams(dimension_semantics=sem, vmem_limit_bytes=VMEM_LIMIT, **kw)


def _prenorm_body(x_ref, g_ref, h_ref):
    x = x_ref[...]
    r = lax.rsqrt(jnp.mean(x * x, axis=-1, keepdims=True) + EPS)
    h_ref[...] = ((x * r) * g_ref[...]).astype(BF16)


def _prenorm(x, g, tm):
    m, d = x.shape
    assert m % tm == 0
    return pl.pallas_call(
        _prenorm_body,
        out_shape=jax.ShapeDtypeStruct((m, d), BF16),
        grid=(m // tm,),
        in_specs=[pl.BlockSpec((tm, d), lambda i: (i, 0)), pl.BlockSpec((1, d), lambda i: (0, 0))],
        out_specs=pl.BlockSpec((tm, d), lambda i: (i, 0)),
        compiler_params=_cparams(("parallel",)),
        name="prenorm",
    )(x, g)


def _proj_main_body(h_ref, w_ref, u_ref, q_ref, kb_ref, vb_ref, k_hbm, v_hbm, kv_buf, sem):
    i = pl.program_id(0)
    j = pl.program_id(1)
    last = pl.num_programs(0) - 1
    tm = h_ref.shape[0]
    heads = k_hbm.shape[1]

    def proj():
        return jnp.dot(h_ref[...], w_ref[...], preferred_element_type=F32)

    def head_copies(slot, dst_hbm, step):
        row0 = pl.multiple_of(step * tm, tm)
        return [pltpu.make_async_copy(kv_buf.at[slot, hh], dst_hbm.at[pl.ds(row0, tm), hh, :], sem.at[slot])
                for hh in range(heads)]

    @pl.when(j == 0)
    def _():
        u_ref[...] = proj()

    @pl.when(j == 1)
    def _():
        q_ref[...] = (proj() * (HEAD_DIM ** -0.5)).astype(BF16)

    for jj, slot, dense_ref, dst_hbm in ((2, 0, kb_ref, k_hbm), (3, 1, vb_ref, v_hbm)):
        @pl.when(j == jj)
        def _(slot=slot, dense_ref=dense_ref, dst_hbm=dst_hbm):
            @pl.when(i > 0)
            def _():
                for c in head_copies(slot, dst_hbm, i - 1):
                    c.wait()
            acc = proj()
            dense_ref[...] = acc.astype(BF16)
            for hh in range(heads):
                kv_buf[slot, hh] = acc[:, hh * HEAD_DIM:(hh + 1) * HEAD_DIM]
            for c in head_copies(slot, dst_hbm, i):
                c.start()

    @pl.when((i == last) & (j == 3))
    def _():
        for slot, dst_hbm in ((0, k_hbm), (1, v_hbm)):
            for c in head_copies(slot, dst_hbm, i):
                c.wait()


def _proj_gates_body(h_ref, w_ref, gp_ref, gs_ref, *, tn):
    j = pl.program_id(1)
    for jj, ref in ((0, gp_ref), (2, gs_ref)):
        for half in range(2):
            @pl.when(j == jj + half)
            def _(ref=ref, half=half):
                a = jnp.dot(h_ref[...], w_ref[...], preferred_element_type=F32)
                ref[:, half * tn:(half + 1) * tn] = (0.5 * jnp.tanh(0.5 * a) + 0.5).astype(BF16)


def _inproj(h, w_bf, tm):
    m, d = h.shape
    tn = d // 2
    assert w_bf.shape == (d, 8 * tn) and m % tm == 0
    row = lambda i, j: (i, 0)
    h_spec = pl.BlockSpec((tm, d), row)
    dense = lambda dt: jax.ShapeDtypeStruct((m, tn), dt)
    cache = jax.ShapeDtypeStruct((m, tn // HEAD_DIM, HEAD_DIM), F32)
    any_spec = pl.BlockSpec(memory_space=pl.ANY)
    u, q, kb, vb, k, v = pl.pallas_call(
        _proj_main_body,
        out_shape=[dense(F32), dense(BF16), dense(BF16), dense(BF16), cache, cache],
        grid=(m // tm, 4),
        in_specs=[h_spec, pl.BlockSpec((d, tn), lambda i, j: (0, j))],
        out_specs=[pl.BlockSpec((tm, tn), row)] * 4 + [any_spec] * 2,
        scratch_shapes=[pltpu.VMEM((2, tn // HEAD_DIM, tm, HEAD_DIM), F32), pltpu.SemaphoreType.DMA((2,))],
        compiler_params=_cparams(("arbitrary", "arbitrary"), has_side_effects=True),
        name="inproj_main",
    )(h, w_bf)
    gp, gs = pl.pallas_call(
        functools.partial(_proj_gates_body, tn=tn),
        out_shape=[jax.ShapeDtypeStruct((m, d), BF16)] * 2,
        grid=(m // tm, 4),
        in_specs=[h_spec, pl.BlockSpec((d, tn), lambda i, j: (0, j + 4))],
        out_specs=[pl.BlockSpec((tm, d), row)] * 2,
        compiler_params=_cparams(("parallel", "arbitrary")),
        name="inproj_gates",
    )(h, w_bf)
    return u, q, kb, vb, k, v, gp, gs


def _pool_body(hist_ref, u_ref, o_ref, ext_scr, *, pos0, tn):
    s = pl.program_id(1)

    @pl.when(s == 0)
    def _():
        ext_scr[0:POOL_HIST, :] = hist_ref[0]

    ext_scr[POOL_HIST:POOL_HIST + tn, :] = u_ref[0]
    pos = pos0 + s * tn + lax.broadcasted_iota(jnp.int32, (tn, 1), 0)
    group = u_ref.shape[2] // len(POOL_WINDOWS)
    for g, w in enumerate(POOL_WINDOWS):
        lo, hi = g * group, (g + 1) * group
        cur = ext_scr[POOL_HIST:POOL_HIST + tn, lo:hi]
        tot = cur
        for dlt in range(1, w):
            tot = tot + ext_scr[POOL_HIST - dlt:POOL_HIST - dlt + tn, lo:hi]
        cnt = jnp.minimum(pos + 1, w).astype(F32)
        o_ref[0, :, lo:hi] = (tot / cnt - cur).astype(BF16)
    ext_scr[0:POOL_HIST, :] = ext_scr[tn:tn + POOL_HIST, :]


def _pool_diff(hist, u, pos0, tn):
    b, n, dp = u.shape
    assert n % tn == 0 and hist.shape == (b, POOL_HIST, dp)
    return pl.pallas_call(
        functools.partial(_pool_body, pos0=pos0, tn=tn),
        out_shape=jax.ShapeDtypeStruct((b, n, dp), BF16),
        grid=(b, n // tn),
        in_specs=[pl.BlockSpec((1, POOL_HIST, dp), lambda i, s: (i, 0, 0)),
                  pl.BlockSpec((1, tn, dp), lambda i, s: (i, s, 0))],
        out_specs=pl.BlockSpec((1, tn, dp), lambda i, s: (i, s, 0)),
        scratch_shapes=[pltpu.VMEM((POOL_HIST + tn, dp), F32)],
        compiler_params=_cparams(("parallel", "arbitrary")),
        name="pool_diff",
    )(hist, u)


def _softplus2(z2):
    neg_abs = pltpu.bitcast(pltpu.bitcast(z2, jnp.uint32) | jnp.uint32(0x80000000), F32)
    return jnp.maximum(z2, 0.0) + jnp.log2(1.0 + jnp.exp2(neg_abs))


def _suffix_matrix(n):
    r = lax.broadcasted_iota(jnp.int32, (n, n), 0)
    c = lax.broadcasted_iota(jnp.int32, (n, n), 1)
    return jnp.where(r >= c, 1.0, 0.0).astype(BF16)


def _stack_heads(q2):
    lane = lax.broadcasted_iota(jnp.int32, q2.shape, 1)
    zero = jnp.zeros_like(q2)
    return jnp.concatenate([jnp.where(lane < HEAD_DIM, q2, zero), jnp.where(lane >= HEAD_DIM, q2, zero)], axis=0)


def _unstack_heads(acc, t):
    lane = lax.broadcasted_iota(jnp.int32, (t, LANES), 1)
    return jnp.where(lane < HEAD_DIM, acc[:t], acc[t:])


def _causal_mask(t):
    r = lax.broadcasted_iota(jnp.int32, (t, t), 0)
    c = lax.broadcasted_iota(jnp.int32, (t, t), 1)
    m = c < r
    return jnp.concatenate([m, m], axis=0)


def _sb_group(q_st, k_blocks, v_blocks, suffix, carry, masks, transposed_keys):
    dn = (((1,), (0,)), ((), ())) if transposed_keys else (((1,), (1,)), ((), ()))
    zs = [lax.dot_general(q_st, kb, dn, preferred_element_type=F32) * LOG2E for kb in k_blocks]
    his, los = [], []
    for z, mask in zip(zs, masks):
        sp = _softplus2(z)
        if mask is not None:
            sp = jnp.where(mask, sp, 0.0)
        hi = pltpu.bitcast(pltpu.bitcast(sp, jnp.uint32) & jnp.uint32(0xFFFF0000), F32)
        his.append(hi.astype(BF16))
        los.append((sp - hi).astype(BF16))
    cs = [jnp.dot(hi, suffix, preferred_element_type=F32) + jnp.dot(lo, suffix, preferred_element_type=F32)
          for hi, lo in zip(his, los)]
    out = None
    for z, c, vb, mask in zip(zs, cs, v_blocks, masks):
        arg = z - c - carry
        if mask is not None:
            arg = jnp.where(mask, arg, NEG)
        o = jnp.dot(jnp.exp2(arg).astype(BF16), vb, preferred_element_type=F32)
        out = o if out is None else out + o
        carry = carry + c[:, 0:1]
    return out, carry


def _attn_self_body(q_ref, k_ref, v_ref, o_ref, kt_scr, v_scr, acc_scr, car_scr, min_scr, *, t):
    qi = pl.program_id(2)

    @pl.when(qi == 0)
    def _fill():
        def body(c, _):
            row0 = pl.multiple_of(c * t, t)
            kt_scr[c] = k_ref[0, pl.ds(row0, t), :].astype(F32).T.astype(BF16)
            v_scr[c] = v_ref[0, pl.ds(row0, t), :].astype(BF16)
            return 0
        lax.fori_loop(0, kt_scr.shape[0], body, 0)

    q_st = _stack_heads(q_ref[0])
    suffix = _suffix_matrix(t)
    mask = _causal_mask(t)

    def run(blocks, masks, first):
        carry = jnp.zeros((2 * t, 1), F32) if first else car_scr[:, 0:1]
        out, carry = _sb_group(q_st, [kt_scr[b] for b in blocks], [v_scr[b] for b in blocks], suffix, carry,
                               masks, True)
        acc_scr[...] = out if first else acc_scr[...] + out
        car_scr[...] = jnp.broadcast_to(carry, car_scr.shape)
        min_scr[0] = jnp.min(carry)

    @pl.when(qi == 0)
    def _():
        run([0], [mask], True)

    @pl.when(qi > 0)
    def _():
        run([qi, qi - 1], [mask, None], True)

    rest = jnp.maximum(qi - 1, 0)

    def more(it):
        return (it < rest // 2) & (min_scr[0] < SKIP_MASS)

    def pair(it):
        b0 = qi - 2 - 2 * it
        run([b0, b0 - 1], [None, None], False)
        return it + 1

    lax.while_loop(more, pair, 0)

    @pl.when((rest % 2 == 1) & (min_scr[0] < SKIP_MASS))
    def _():
        run([0], [None], False)

    o_ref[0] = _unstack_heads(acc_scr[...], t).astype(o_ref.dtype)


def _attn_hist_body(q_ref, kn_ref, vn_ref, kh_ref, vh_ref, o_ref, acc_scr, car_scr, min_scr, *, tk, group):
    t = q_ref.shape[1]
    past = kh_ref.shape[1]
    q_st = _stack_heads(q_ref[0])

    def keep(out, carry, first):
        acc_scr[...] = out if first else acc_scr[...] + out
        car_scr[...] = jnp.broadcast_to(carry, car_scr.shape)
        min_scr[0] = jnp.min(carry)

    keep(*_sb_group(q_st, [kn_ref[0].astype(BF16)], [vn_ref[0].astype(BF16)], _suffix_matrix(t),
                    jnp.zeros((2 * t, 1), F32), [_causal_mask(t)], False), True)
    suffix = _suffix_matrix(tk)
    for top in range(past // tk, 0, -group):
        @pl.when(min_scr[0] < SKIP_MASS)
        def _(top=top):
            blocks = range(top - 1, top - 1 - group, -1)
            keep(*_sb_group(q_st, [kh_ref[0, b * tk:(b + 1) * tk, :].astype(BF16) for b in blocks],
                            [vh_ref[0, b * tk:(b + 1) * tk, :].astype(BF16) for b in blocks],
                            suffix, car_scr[:, 0:1], [None] * group, False), False)
    o_ref[0] = _unstack_heads(acc_scr[...], t).astype(o_ref.dtype)


def _attention(q, k_new, v_new, k_hist, v_hist, t, tk, group):
    b, n, dm = q.shape
    assert n % t == 0 and dm % LANES == 0
    tile = pl.BlockSpec((1, t, LANES), lambda i, p, s: (i, s, p))
    seq = lambda rows: pl.BlockSpec((1, rows, LANES), lambda i, p, s: (i, 0, p))
    state = [pltpu.VMEM((2 * t, LANES), F32), pltpu.VMEM((2 * t, LANES), F32), pltpu.SMEM((1,), F32)]
    if k_hist is None:
        body = functools.partial(_attn_self_body, t=t)
        in_specs, args = [tile, seq(n), seq(n)], [q, k_new, v_new]
        scratch = [pltpu.VMEM((n // t, LANES, t), BF16), pltpu.VMEM((n // t, t, LANES), BF16)] + state
    else:
        past = k_hist.shape[1]
        assert n == t and past % (tk * group) == 0
        body = functools.partial(_attn_hist_body, tk=tk, group=group)
        in_specs, args = [tile, seq(n), seq(n), seq(past), seq(past)], [q, k_new, v_new, k_hist, v_hist]
        scratch = state
    return pl.pallas_call(
        body,
        out_shape=jax.ShapeDtypeStruct((b, n, dm), BF16),
        grid=(b, dm // LANES, n // t),
        in_specs=in_specs,
        out_specs=tile,
        scratch_shapes=scratch,
        compiler_params=_cparams(("parallel", "parallel", "arbitrary")),
        name="sb_attention",
    )(*args)


def _mix_body(diff_ref, o_ref, gp_ref, gs_ref, x_ref, wp_ref, ps_ref, wsb_ref, wo_ref, nf_ref,
              wr_ref, br_ref, x1_ref, hp_ref, route_ref, cnt_ref, *, tm):
    n_pool = wp_ref.shape[0]
    group = wp_ref.shape[1]
    diff = diff_ref[...]
    pool = jnp.concatenate(
        [jnp.dot(diff[:, g * group:(g + 1) * group], wp_ref[g], preferred_element_type=F32)
         for g in range(n_pool)], axis=-1) * ps_ref[...]
    sb = jnp.dot(o_ref[...], wsb_ref[...], preferred_element_type=F32)
    mixed = gp_ref[...].astype(F32) * pool + gs_ref[...].astype(F32) * sb
    x1 = x_ref[...] + jnp.dot(mixed.astype(BF16), wo_ref[...], preferred_element_type=F32)
    x1_ref[...] = x1
    h = (x1 * lax.rsqrt(jnp.mean(x1 * x1, axis=-1, keepdims=True) + EPS)) * nf_ref[...]

    d_half = h.shape[1] // 2
    lo_bits = pltpu.bitcast(h[:, :d_half].astype(BF16).astype(F32), jnp.uint32)
    hi_bits = pltpu.bitcast(h[:, d_half:].astype(BF16).astype(F32), jnp.uint32)
    hp_ref[...] = (lo_bits >> 16) | (hi_bits & jnp.uint32(0xFFFF0000))

    hh = h.astype(BF16)
    hl = (h - hh.astype(F32)).astype(BF16)
    r = jnp.dot(jnp.concatenate([hh, hl], axis=0), wr_ref[...], preferred_element_type=F32)
    logits = (r[:tm, :LANES] + r[:tm, LANES:]) + (r[tm:, :LANES] + r[tm:, LANES:]) + br_ref[...]
    lane = lax.broadcasted_iota(jnp.int32, (tm, LANES), 1)
    big = jnp.int32(LANES)

    def first_max(vals):
        m = jnp.max(vals, axis=-1, keepdims=True)
        idx = jnp.min(jnp.where(vals == m, lane, big), axis=-1, keepdims=True)
        return m, idx

    gl = jnp.where(lane < N_GROUPS, logits, NEG)
    gmax, grp = first_max(gl)
    p_grp = 1.0 / jnp.sum(jnp.exp(gl - gmax), axis=-1, keepdims=True)
    e_lo = N_GROUPS + grp * PER_GROUP
    el = jnp.where((lane >= e_lo) & (lane < e_lo + PER_GROUP), logits, NEG)
    m1, i1 = first_max(el)
    m2, i2 = first_max(jnp.where(lane == i1, NEG, el))
    t2 = jnp.exp(m2 - m1)
    w1 = p_grp / (1.0 + t2)
    w2 = w1 * t2
    e1 = i1 - N_GROUPS
    e2 = i2 - N_GROUPS

    oh1 = jnp.where(lane == e1, 1.0, 0.0).astype(BF16)
    oh2 = jnp.where(lane == e2, 1.0, 0.0).astype(BF16)
    rr = lax.broadcasted_iota(jnp.int32, (tm, tm), 0)
    cc = lax.broadcasted_iota(jnp.int32, (tm, tm), 1)
    before = jnp.where(cc < rr, 1.0, 0.0).astype(BF16)
    ones = jnp.ones((8, tm), BF16)
    pre1 = jnp.dot(before, oh1, preferred_element_type=F32)
    pre2 = jnp.dot(before, oh2, preferred_element_type=F32)
    c1 = jnp.dot(ones, oh1, preferred_element_type=F32)
    c2 = jnp.dot(ones, oh2, preferred_element_type=F32)
    rank1 = jnp.sum(jnp.where(lane == e1, pre1, 0.0), axis=-1, keepdims=True)
    rank2 = jnp.sum(jnp.where(lane == e2, pre2 + c1[0:1, :], 0.0), axis=-1, keepdims=True)
    cnt_ref[0] = c1 + c2

    route = jnp.where(lane == 0, e1.astype(F32), 0.0)
    route = jnp.where(lane == 1, e2.astype(F32), route)
    route = jnp.where(lane == 2, w1, route)
    route = jnp.where(lane == 3, w2, route)
    route = jnp.where(lane == 4, rank1, route)
    route = jnp.where(lane == 5, rank2, route)
    route_ref[...] = route


def _mix(diff, o, gp, gs, x, wp, ps, wsb, wo, nf, wr, br, tm):
    m, d = x.shape
    assert m % tm == 0
    row = lambda i: (i, 0)
    full = lambda a: pl.BlockSpec(a.shape, lambda i: (0,) * a.ndim, pipeline_mode=pl.Buffered(1))
    acts = [diff, o, gp, gs, x]
    consts = [wp, ps, wsb, wo, nf, wr, br]
    return pl.pallas_call(
        functools.partial(_mix_body, tm=tm),
        out_shape=[jax.ShapeDtypeStruct((m, d), F32), jax.ShapeDtypeStruct((m, d // 2), jnp.uint32),
                   jax.ShapeDtypeStruct((m, LANES), F32), jax.ShapeDtypeStruct((m // tm, 8, LANES), F32)],
        grid=(m // tm,),
        in_specs=[pl.BlockSpec((tm, a.shape[1]), row) for a in acts] + [full(a) for a in consts],
        out_specs=[pl.BlockSpec((tm, d), row), pl.BlockSpec((tm, d // 2), row),
                   pl.BlockSpec((tm, LANES), row), pl.BlockSpec((1, 8, LANES), lambda i: (i, 0, 0))],
        compiler_params=_cparams(("parallel",)),
        name="mix_outproj_router",
    )(*acts, *consts)


def _row_wait(src_ref, dst_ref, sem, n_rows):
    pltpu.make_async_copy(src_ref.at[pl.ds(0, n_rows)], dst_ref.at[pl.ds(0, n_rows)], sem).wait()


def _dispatch_body(slot_ref, zero_ref, hp_ref, xs_ref, buf, zbuf, sem, zsem, *, tm):
    i = pl.program_id(0)
    last = pl.num_programs(0) - 1
    par = i % 2

    @pl.when(i == 0)
    def _():
        zbuf[...] = jnp.zeros_like(zbuf)
        bm = zbuf.shape[0]
        for wait in (False, True):
            for z in range(zero_ref.shape[0]):
                @pl.when(zero_ref[z] >= 0)
                def _(z=z, wait=wait):
                    start = pl.multiple_of(jnp.maximum(zero_ref[z], 0), bm)
                    copy = pltpu.make_async_copy(zbuf, xs_ref.at[pl.ds(start, bm)], zsem)
                    copy.wait() if wait else copy.start()

    def drain(slot):
        for _ in range(2):
            _row_wait(buf.at[slot], xs_ref, sem.at[slot], tm)

    @pl.when(i >= 2)
    def _():
        drain(par)

    buf[par] = hp_ref[...]
    base = i * tm

    def body(t, _):
        src = buf.at[par, pl.ds(t, 1)]
        for kk in range(2):
            dst = xs_ref.at[pl.ds(slot_ref[2 * (base + t) + kk], 1)]
            pltpu.make_async_copy(src, dst, sem.at[par]).start()
        return 0

    lax.fori_loop(0, tm, body, 0, unroll=8)

    @pl.when(i == last)
    def _():
        drain(par)

        @pl.when(i >= 1)
        def _():
            drain(1 - par)


def _dispatch(slots_flat, zero_starts, hp, n_slots, tm, bm):
    m, dh = hp.shape
    assert m % tm == 0
    return pl.pallas_call(
        functools.partial(_dispatch_body, tm=tm),
        out_shape=jax.ShapeDtypeStruct((n_slots, dh), hp.dtype),
        grid_spec=pltpu.PrefetchScalarGridSpec(
            num_scalar_prefetch=2, grid=(m // tm,),
            in_specs=[pl.BlockSpec((tm, dh), lambda i, s, z: (i, 0))],
            out_specs=pl.BlockSpec(memory_space=pl.ANY),
            scratch_shapes=[pltpu.VMEM((2, tm, dh), hp.dtype), pltpu.VMEM((bm, dh), hp.dtype),
                            pltpu.SemaphoreType.DMA((2,)), pltpu.SemaphoreType.DMA]),
        compiler_params=_cparams(("arbitrary",), disable_bounds_checks=True, has_side_effects=True),
        name="moe_dispatch",
    )(slots_flat, zero_starts, hp)


def _ffn_body(be_ref, nv_ref, xs_ref, wg_ref, wu_ref, wd_ref, y_ref):
    del be_ref

    @pl.when(pl.program_id(0) >= nv_ref[0])
    def _():
        y_ref[...] = jnp.zeros_like(y_ref)

    @pl.when(pl.program_id(0) < nv_ref[0])
    def _():
        words = xs_ref[...]
        d_half = words.shape[1]
        x_lo = pltpu.bitcast(words << 16, F32).astype(BF16)
        x_hi = pltpu.bitcast(words & jnp.uint32(0xFFFF0000), F32).astype(BF16)

        def proj(w_ref):
            return (jnp.dot(x_lo, w_ref[0, :d_half, :], preferred_element_type=F32)
                    + jnp.dot(x_hi, w_ref[0, d_half:, :], preferred_element_type=F32))

        hid = jax.nn.silu(proj(wg_ref)) * proj(wu_ref)
        y_ref[...] = jnp.dot(hid.astype(BF16), wd_ref[0], preferred_element_type=F32)


def _ffn(block_expert, n_valid, xs, wg, wu, wd, bm):
    n_slots, d_half = xs.shape
    d = 2 * d_half
    de = wg.shape[2]
    live = lambda b, be, nv: (jnp.minimum(b, nv[0] - 1), 0)
    wsel = lambda b, be, nv: (be[b], 0, 0)
    return pl.pallas_call(
        _ffn_body,
        out_shape=jax.ShapeDtypeStruct((n_slots, d), F32),
        grid_spec=pltpu.PrefetchScalarGridSpec(
            num_scalar_prefetch=2, grid=(n_slots // bm,),
            in_specs=[pl.BlockSpec((bm, d_half), live), pl.BlockSpec((1, d, de), wsel),
                      pl.BlockSpec((1, d, de), wsel), pl.BlockSpec((1, de, d), wsel)],
            out_specs=pl.BlockSpec((bm, d), lambda b, be, nv: (b, 0))),
        compiler_params=_cparams(("arbitrary",)),
        name="moe_ffn",
    )(block_expert, n_valid, xs, wg, wu, wd)


def _final_body(slot_ref, x1_ref, route_ref, g_ref, y_hbm, o_ref, buf, sem, *, tm):
    i = pl.program_id(0)
    n_steps = pl.num_programs(0)

    def issue(step, par):
        def body(t, _):
            for kk in range(2):
                src = y_hbm.at[pl.ds(slot_ref[2 * (step * tm + t) + kk], 1)]
                pltpu.make_async_copy(src, buf.at[par, kk, pl.ds(t, 1)], sem.at[par]).start()
            return 0
        lax.fori_loop(0, tm, body, 0, unroll=8)

    @pl.when(i == 0)
    def _():
        issue(0, 0)

    @pl.when(i + 1 < n_steps)
    def _():
        issue(i + 1, (i + 1) % 2)

    par = i % 2
    for kk in range(2):
        _row_wait(y_hbm, buf.at[par, kk], sem.at[par], tm)
    route = route_ref[...]
    x2 = x1_ref[...] + route[:, 2:3] * buf[par, 0] + route[:, 3:4] * buf[par, 1]
    o_ref[...] = (x2 * lax.rsqrt(jnp.mean(x2 * x2, axis=-1, keepdims=True) + EPS)) * g_ref[...]


def _final(slots_flat, x1, route, g, y, tm):
    m, d = x1.shape
    assert m % tm == 0
    row = lambda i, s: (i, 0)
    return pl.pallas_call(
        functools.partial(_final_body, tm=tm),
        out_shape=jax.ShapeDtypeStruct((m, d), F32),
        grid_spec=pltpu.PrefetchScalarGridSpec(
            num_scalar_prefetch=1, grid=(m // tm,),
            in_specs=[pl.BlockSpec((tm, d), row), pl.BlockSpec((tm, LANES), row),
                      pl.BlockSpec((1, d), lambda i, s: (0, 0)), pl.BlockSpec(memory_space=pl.ANY)],
            out_specs=pl.BlockSpec((tm, d), row),
            scratch_shapes=[pltpu.VMEM((2, 2, tm, d), F32), pltpu.SemaphoreType.DMA((2,))]),
        compiler_params=_cparams(("arbitrary",), disable_bounds_checks=True),
        name="moe_combine_final_norm",
    )(slots_flat, x1, route, g, y)


def _routing_tables(route, cnt, tm, bm):
    m = route.shape[0]
    n_blocks = (2 * m) // bm + N_EXPERTS
    counts = cnt[:, 0, :N_EXPERTS].astype(jnp.int32)
    sizes = jnp.sum(counts, axis=0)
    padded = (sizes + bm - 1) // bm * bm
    pad_end = jnp.cumsum(padded)
    base = (pad_end - padded)[None, :] + jnp.cumsum(counts, axis=0) - counts
    base_tok = jnp.repeat(base, tm, axis=0)
    e = route[:, 0:2].astype(jnp.int32)
    rank = route[:, 4:6].astype(jnp.int32)
    sel = e[:, :, None] == jnp.arange(N_EXPERTS, dtype=jnp.int32)[None, None, :]
    slots = jnp.sum(jnp.where(sel, base_tok[:, None, :], 0), axis=-1) + rank
    n_valid = (pad_end[-1] // bm).astype(jnp.int32)
    blk = jnp.minimum(jnp.arange(n_blocks, dtype=jnp.int32), n_valid - 1)
    block_expert = jnp.minimum(jnp.sum(pad_end[None, :] <= (blk * bm)[:, None], axis=1), N_EXPERTS - 1)
    last_blk = jnp.where(padded > 0, pad_end - bm, -1)
    tail = n_valid + jnp.arange(N_EXPERTS, dtype=jnp.int32)
    tail = jnp.where(tail < n_blocks, tail * bm, -1)
    zero_starts = jnp.concatenate([last_blk, tail]).astype(jnp.int32)
    return slots.reshape(-1), block_expert.astype(jnp.int32), n_valid.reshape(1), zero_starts, n_blocks * bm


def _stream(x, pool_hist, k_hist, v_hist, p, *, tq, tk, group, tn, tm_in, tm_mix, tm_fin, bm):
    b, n, d = x.shape
    m = b * n
    past = 0 if k_hist is None else k_hist.shape[1]
    x2d = x.reshape(m, d)
    u, q, kb, vb, k, v, gp, gs = _inproj(_prenorm(x2d, p['norm_mix'], _fit(m, 512)), p['w_in'], _fit(m, tm_in))
    dp = u.shape[1]
    diff = _pool_diff(pool_hist, u.reshape(b, n, dp), past, tn).reshape(m, dp)
    shp = (b, n, kb.shape[1])
    o = _attention(q.reshape(shp), kb.reshape(shp), vb.reshape(shp), k_hist, v_hist, tq, tk, group).reshape(m, -1)
    x1, hp, route, cnt = _mix(diff, o, gp, gs, x2d, p['w_pool'], p['pool_scale'], p['w_sb_out'], p['w_out'],
                              p['norm_ffn'], p['w_r'], p['b_r'], tm_mix)
    slots, block_expert, n_valid, zero_starts, n_slots = _routing_tables(route, cnt, tm_mix, bm)
    xs = _dispatch(slots, zero_starts, hp, n_slots, tm_mix, bm)
    y = _ffn(block_expert, n_valid, xs, p['w_g'], p['w_u'], p['w_d'], bm)
    out = _final(slots, x1, route, p['norm_final'], y, tm_fin)
    return out.reshape(b, n, d), u.reshape(b, n, dp), k, v


def kernel(x_prompt, x_sample, cache_sb_k, cache_sb_v, state_pool, norm_mix, w_in, w_pool, pool_scale, w_sb_out,
           w_out, norm_ffn, w_router_group, b_router_group, w_router_expert, b_router_expert, w_exp_gate,
           w_exp_up, w_exp_down, norm_final):
    depth = w_in.shape[0]
    assert depth == 1
    bp, sp, d = x_prompt.shape
    bs, ss, _ = x_sample.shape
    past = cache_sb_k.shape[2]
    heads, hd = cache_sb_k.shape[3], cache_sb_k.shape[4]
    assert hd == HEAD_DIM
    dp = state_pool.shape[3]
    n_state = state_pool.shape[2]

    w_r = jnp.concatenate([w_router_group[0], w_router_expert[0]], axis=1)
    w_r = jnp.pad(w_r, ((0, 0), (0, LANES - w_r.shape[1])))
    w_r_hi = w_r.astype(BF16)
    b_r = jnp.concatenate([b_router_group[0], b_router_expert[0]])
    p = dict(
        norm_mix=norm_mix[0][None, :], w_in=w_in[0].astype(BF16), w_pool=w_pool[0].astype(BF16),
        pool_scale=pool_scale[0][None, :], w_sb_out=w_sb_out[0].astype(BF16), w_out=w_out[0].astype(BF16),
        norm_ffn=norm_ffn[0][None, :],
        w_r=jnp.concatenate([w_r_hi, (w_r - w_r_hi.astype(F32)).astype(BF16)], axis=1),
        b_r=jnp.pad(b_r, (0, LANES - b_r.shape[0]))[None, :].astype(F32),
        w_g=w_exp_gate[0].astype(BF16), w_u=w_exp_up[0].astype(BF16), w_d=w_exp_down[0].astype(BF16),
        norm_final=norm_final[None, :])

    hist_p = jnp.zeros((bp, POOL_HIST, dp), F32)
    yp, up, kp, vp = _stream(x_prompt, hist_p, None, None, p, tq=256, tk=256, group=2, tn=min(sp, 512),
                             tm_in=1024, tm_mix=256, tm_fin=256, bm=256)
    hist_s = jnp.pad(state_pool[0], ((0, 0), (POOL_HIST - n_state, 0), (0, 0)))
    kh = cache_sb_k[0].reshape(bs, past, heads * hd)
    vh = cache_sb_v[0].reshape(bs, past, heads * hd)
    ys, us, ks, vs = _stream(x_sample, hist_s, kh, vh, p, tq=ss, tk=256, group=2, tn=ss,
                             tm_in=1024, tm_mix=256, tm_fin=256, bm=256)

    def pool_state(hist, u):
        return jnp.concatenate([hist[:, POOL_HIST - n_state:], u], axis=1)[:, -n_state:][None]

    return (yp, ys,
            kp.reshape(1, bp, sp, heads, hd), vp.reshape(1, bp, sp, heads, hd), pool_state(hist_p, up),
            ks.reshape(1, bs, ss, heads, hd), vs.reshape(1, bs, ss, heads, hd), pool_state(hist_s, us))
```

```python
import functools

import jax
import jax.numpy as jnp
from jax import lax
from jax.experimental import pallas as pl
from jax.experimental.pallas import tpu as pltpu

F32 = jnp.float32
BF16 = jnp.bfloat16

EPS = 1e-6
HEAD_DIM = 64
LANES = 128
POOL_WINDOWS = (2, 4, 8, 16)
POOL_HIST = 16
N_GROUPS = 4
PER_GROUP = 8
N_EXPERTS = N_GROUPS * PER_GROUP
VMEM_LIMIT = 58 * 1024 * 1024
NEG = -1e30
LOG2E = 1.4426950408889634
SKIP_MASS = 160.0


def _fit(m, tile):
    while m % tile:
        tile //= 2
    return tile


def _cparams(sem, **kw):
    return pltpu.CompilerParams(dimension_semantics=sem, vmem_limit_bytes=VMEM_LIMIT, **kw)


def _proj_main_body(h_ref, w_ref, u_ref, q_ref, kb_ref, vb_ref, k_hbm, v_hbm, kv_buf, sem):
    i = pl.program_id(0)
    j = pl.program_id(1)
    last = pl.num_programs(0) - 1
    tm = h_ref.shape[0]
    heads = k_hbm.shape[1]

    def proj():
        return jnp.dot(h_ref[...], w_ref[...], preferred_element_type=F32)

    def head_copies(slot, dst_hbm, step):
        row0 = pl.multiple_of(step * tm, tm)
        return [pltpu.make_async_copy(kv_buf.at[slot, hh], dst_hbm.at[pl.ds(row0, tm), hh, :], sem.at[slot])
                for hh in range(heads)]

    @pl.when(j == 0)
    def _():
        u_ref[...] = proj()

    @pl.when(j == 1)
    def _():
        q_ref[...] = (proj() * (HEAD_DIM ** -0.5)).astype(BF16)

    for jj, slot, dense_ref, dst_hbm in ((2, 0, kb_ref, k_hbm), (3, 1, vb_ref, v_hbm)):
        @pl.when(j == jj)
        def _(slot=slot, dense_ref=dense_ref, dst_hbm=dst_hbm):
            @pl.when(i > 0)
            def _():
                for c in head_copies(slot, dst_hbm, i - 1):
                    c.wait()
            acc = proj()
            dense_ref[...] = acc.astype(BF16)
            for hh in range(heads):
                kv_buf[slot, hh] = acc[:, hh * HEAD_DIM:(hh + 1) * HEAD_DIM]
            for c in head_copies(slot, dst_hbm, i):
                c.start()

    @pl.when((i == last) & (j == 3))
    def _():
        for slot, dst_hbm in ((0, k_hbm), (1, v_hbm)):
            for c in head_copies(slot, dst_hbm, i):
                c.wait()


def _proj_gates_body(x_ref, g_ref, w_ref, h_ref, gp_ref, gs_ref, *, tn):
    j = pl.program_id(1)

    @pl.when(j == 0)
    def _():
        x = x_ref[...]
        r = lax.rsqrt(jnp.mean(x * x, axis=-1, keepdims=True) + EPS)
        h_ref[...] = ((x * r) * g_ref[...]).astype(BF16)

    for jj, ref in ((0, gp_ref), (2, gs_ref)):
        for half in range(2):
            @pl.when(j == jj + half)
            def _(ref=ref, half=half):
                a = jnp.dot(h_ref[...], w_ref[...], preferred_element_type=F32)
                ref[:, half * tn:(half + 1) * tn] = (0.5 * jnp.tanh(0.5 * a) + 0.5).astype(BF16)


def _inproj(x, g, w_bf, tm):
    m, d = x.shape
    tn = d // 2
    assert w_bf.shape == (d, 8 * tn) and m % tm == 0
    row = lambda i, j: (i, 0)
    h_spec = pl.BlockSpec((tm, d), row)
    h, gp, gs = pl.pallas_call(
        functools.partial(_proj_gates_body, tn=tn),
        out_shape=[jax.ShapeDtypeStruct((m, d), BF16)] * 3,
        grid=(m // tm, 4),
        in_specs=[h_spec, pl.BlockSpec((1, d), lambda i, j: (0, 0)),
                  pl.BlockSpec((d, tn), lambda i, j: (0, j + 4))],
        out_specs=[h_spec] * 3,
        compiler_params=_cparams(("parallel", "arbitrary")),
        name="inproj_gates",
    )(x, g, w_bf)
    dense = lambda dt: jax.ShapeDtypeStruct((m, tn), dt)
    cache = jax.ShapeDtypeStruct((m, tn // HEAD_DIM, HEAD_DIM), F32)
    any_spec = pl.BlockSpec(memory_space=pl.ANY)
    u, q, kb, vb, k, v = pl.pallas_call(
        _proj_main_body,
        out_shape=[dense(F32), dense(BF16), dense(BF16), dense(BF16), cache, cache],
        grid=(m // tm, 4),
        in_specs=[h_spec, pl.BlockSpec((d, tn), lambda i, j: (0, j))],
        out_specs=[pl.BlockSpec((tm, tn), row)] * 4 + [any_spec] * 2,
        scratch_shapes=[pltpu.VMEM((2, tn // HEAD_DIM, tm, HEAD_DIM), F32), pltpu.SemaphoreType.DMA((2,))],
        compiler_params=_cparams(("arbitrary", "arbitrary"), has_side_effects=True),
        name="inproj_main",
    )(h, w_bf)
    return u, q, kb, vb, k, v, gp, gs


def _pool_body(hist_ref, u_ref, o_ref, ext_scr, *, pos0, tn):
    s = pl.program_id(1)

    @pl.when(s == 0)
    def _():
        ext_scr[0:POOL_HIST, :] = hist_ref[0]

    ext_scr[POOL_HIST:POOL_HIST + tn, :] = u_ref[0]
    pos = pos0 + s * tn + lax.broadcasted_iota(jnp.int32, (tn, 1), 0)
    group = u_ref.shape[2] // len(POOL_WINDOWS)
    for g, w in enumerate(POOL_WINDOWS):
        lo, hi = g * group, (g + 1) * group
        cur = ext_scr[POOL_HIST:POOL_HIST + tn, lo:hi]
        tot = cur
        for dlt in range(1, w):
            tot = tot + ext_scr[POOL_HIST - dlt:POOL_HIST - dlt + tn, lo:hi]
        cnt = jnp.minimum(pos + 1, w).astype(F32)
        o_ref[0, :, lo:hi] = (tot / cnt - cur).astype(BF16)
    ext_scr[0:POOL_HIST, :] = ext_scr[tn:tn + POOL_HIST, :]


def _pool_diff(hist, u, pos0, tn):
    b, n, dp = u.shape
    assert n % tn == 0 and hist.shape == (b, POOL_HIST, dp)
    return pl.pallas_call(
        functools.partial(_pool_body, pos0=pos0, tn=tn),
        out_shape=jax.ShapeDtypeStruct((b, n, dp), BF16),
        grid=(b, n // tn),
        in_specs=[pl.BlockSpec((1, POOL_HIST, dp), lambda i, s: (i, 0, 0)),
                  pl.BlockSpec((1, tn, dp), lambda i, s: (i, s, 0))],
        out_specs=pl.BlockSpec((1, tn, dp), lambda i, s: (i, s, 0)),
        scratch_shapes=[pltpu.VMEM((POOL_HIST + tn, dp), F32)],
        compiler_params=_cparams(("parallel", "arbitrary")),
        name="pool_diff",
    )(hist, u)


def _softplus2(z2):
    neg_abs = pltpu.bitcast(pltpu.bitcast(z2, jnp.uint32) | jnp.uint32(0x80000000), F32)
    return jnp.maximum(z2, 0.0) + jnp.log2(1.0 + jnp.exp2(neg_abs))


def _suffix_matrix(n):
    r = lax.broadcasted_iota(jnp.int32, (n, n), 0)
    c = lax.broadcasted_iota(jnp.int32, (n, n), 1)
    return jnp.where(r >= c, 1.0, 0.0).astype(BF16)


def _stack_heads(q2):
    lane = lax.broadcasted_iota(jnp.int32, q2.shape, 1)
    zero = jnp.zeros_like(q2)
    return jnp.concatenate([jnp.where(lane < HEAD_DIM, q2, zero), jnp.where(lane >= HEAD_DIM, q2, zero)], axis=0)


def _unstack_heads(acc, t):
    lane = lax.broadcasted_iota(jnp.int32, (t, LANES), 1)
    return jnp.where(lane < HEAD_DIM, acc[:t], acc[t:])


def _causal_mask(t):
    r = lax.broadcasted_iota(jnp.int32, (t, t), 0)
    c = lax.broadcasted_iota(jnp.int32, (t, t), 1)
    m = c < r
    return jnp.concatenate([m, m], axis=0)


def _sb_group(q_st, k_blocks, v_blocks, suffix, carry, masks, transposed_keys):
    dn = (((1,), (0,)), ((), ())) if transposed_keys else (((1,), (1,)), ((), ()))
    zs = [lax.dot_general(q_st, kb, dn, preferred_element_type=F32) * LOG2E for kb in k_blocks]
    his, los = [], []
    for z, mask in zip(zs, masks):
        sp = _softplus2(z)
        if mask is not None:
            sp = jnp.where(mask, sp, 0.0)
        hi = pltpu.bitcast(pltpu.bitcast(sp, jnp.uint32) & jnp.uint32(0xFFFF0000), F32)
        his.append(hi.astype(BF16))
        los.append((sp - hi).astype(BF16))
    cs = [jnp.dot(hi, suffix, preferred_element_type=F32) + jnp.dot(lo, suffix, preferred_element_type=F32)
          for hi, lo in zip(his, los)]
    out = None
    for z, c, vb, mask in zip(zs, cs, v_blocks, masks):
        arg = z - c - carry
        if mask is not None:
            arg = jnp.where(mask, arg, NEG)
        o = jnp.dot(jnp.exp2(arg).astype(BF16), vb, preferred_element_type=F32)
        out = o if out is None else out + o
        carry = carry + c[:, 0:1]
    return out, carry


def _attn_self_body(q_ref, k_ref, v_ref, o_ref, kt_scr, v_scr, acc_scr, car_scr, min_scr, *, t):
    qi = pl.program_id(2)

    @pl.when(qi == 0)
    def _fill():
        def body(c, _):
            row0 = pl.multiple_of(c * t, t)
            kt_scr[c] = k_ref[0, pl.ds(row0, t), :].astype(F32).T.astype(BF16)
            v_scr[c] = v_ref[0, pl.ds(row0, t), :].astype(BF16)
            return 0
        lax.fori_loop(0, kt_scr.shape[0], body, 0)

    q_st = _stack_heads(q_ref[0])
    suffix = _suffix_matrix(t)
    mask = _causal_mask(t)

    def run(blocks, masks, first):
        carry = jnp.zeros((2 * t, 1), F32) if first else car_scr[:, 0:1]
        out, carry = _sb_group(q_st, [kt_scr[b] for b in blocks], [v_scr[b] for b in blocks], suffix, carry,
                               masks, True)
        acc_scr[...] = out if first else acc_scr[...] + out
        car_scr[...] = jnp.broadcast_to(carry, car_scr.shape)
        min_scr[0] = jnp.min(carry)

    @pl.when(qi == 0)
    def _():
        run([0], [mask], True)

    @pl.when(qi > 0)
    def _():
        run([qi, qi - 1], [mask, None], True)

    rest = jnp.maximum(qi - 1, 0)

    def more(it):
        return (it < rest // 2) & (min_scr[0] < SKIP_MASS)

    def pair(it):
        b0 = qi - 2 - 2 * it
        run([b0, b0 - 1], [None, None], False)
        return it + 1

    lax.while_loop(more, pair, 0)

    @pl.when((rest % 2 == 1) & (min_scr[0] < SKIP_MASS))
    def _():
        run([0], [None], False)

    o_ref[0] = _unstack_heads(acc_scr[...], t).astype(o_ref.dtype)


def _attn_hist_body(q_ref, kn_ref, vn_ref, kh_ref, vh_ref, o_ref, left_ref, acc_scr, car_scr, min_scr, *, tk,
                    group):
    t = q_ref.shape[1]
    past = kh_ref.shape[1]
    q_st = _stack_heads(q_ref[0])

    def keep(out, carry, first):
        acc_scr[...] = out if first else acc_scr[...] + out
        car_scr[...] = jnp.broadcast_to(carry, car_scr.shape)
        min_scr[0] = jnp.min(carry)

    keep(*_sb_group(q_st, [kn_ref[0].astype(BF16)], [vn_ref[0].astype(BF16)], _suffix_matrix(t),
                    jnp.zeros((2 * t, 1), F32), [_causal_mask(t)], False), True)
    suffix = _suffix_matrix(tk)
    for top in range(past // tk, 0, -group):
        @pl.when(min_scr[0] < SKIP_MASS)
        def _(top=top):
            blocks = range(top - 1, top - 1 - group, -1)
            keep(*_sb_group(q_st, [kh_ref[0, b * tk:(b + 1) * tk, :].astype(BF16) for b in blocks],
                            [vh_ref[0, b * tk:(b + 1) * tk, :].astype(BF16) for b in blocks],
                            suffix, car_scr[:, 0:1], [None] * group, False), False)
    o_ref[0] = _unstack_heads(acc_scr[...], t).astype(o_ref.dtype)
    left_ref[0, 0] = jnp.full(left_ref.shape[2:], min_scr[0], F32)


def _attn_specs(t):
    tile = pl.BlockSpec((1, t, LANES), lambda i, p, s: (i, s, p))
    seq = lambda rows: pl.BlockSpec((1, rows, LANES), lambda i, p, s: (i, 0, p))
    state = [pltpu.VMEM((2 * t, LANES), F32), pltpu.VMEM((2 * t, LANES), F32), pltpu.SMEM((1,), F32)]
    return tile, seq, state


def _attention_self(q, k, v, t):
    b, n, dm = q.shape
    assert n % t == 0 and dm % LANES == 0
    tile, seq, state = _attn_specs(t)
    return pl.pallas_call(
        functools.partial(_attn_self_body, t=t),
        out_shape=jax.ShapeDtypeStruct((b, n, dm), BF16),
        grid=(b, dm // LANES, n // t),
        in_specs=[tile, seq(n), seq(n)],
        out_specs=tile,
        scratch_shapes=[pltpu.VMEM((n // t, LANES, t), BF16), pltpu.VMEM((n // t, t, LANES), BF16)] + state,
        compiler_params=_cparams(("parallel", "parallel", "arbitrary")),
        name="sb_attention",
    )(q, k, v)


def _attention_cached(q, k_new, v_new, k_hist, v_hist, tk, group):
    b, t, dm = q.shape
    past = k_hist.shape[1]
    assert dm % LANES == 0 and past % (tk * group) == 0
    tile, seq, state = _attn_specs(t)
    return pl.pallas_call(
        functools.partial(_attn_hist_body, tk=tk, group=group),
        out_shape=[jax.ShapeDtypeStruct((b, t, dm), BF16), jax.ShapeDtypeStruct((b, dm // LANES, 8, LANES), F32)],
        grid=(b, dm // LANES, 1),
        in_specs=[tile, seq(t), seq(t), seq(past), seq(past)],
        out_specs=[tile, pl.BlockSpec((1, 1, 8, LANES), lambda i, p, s: (i, p, 0, 0))],
        scratch_shapes=state,
        compiler_params=_cparams(("parallel", "parallel", "arbitrary")),
        name="sb_attention_cached",
    )(q, k_new, v_new, k_hist, v_hist)


def _attention_recent_first(q, k_new, v_new, cache_k, cache_v, tk, group):
    b, past = cache_k.shape[:2]
    flat = lambda c: c.reshape(b, c.shape[1], -1)
    walk = lambda ck, cv: _attention_cached(q, k_new, v_new, flat(ck), flat(cv), tk, group)
    recent = tk * group
    if past <= recent:
        return walk(cache_k, cache_v)[0]
    o, left = walk(cache_k[:, past - recent:], cache_v[:, past - recent:])
    return lax.cond(jnp.min(left) < SKIP_MASS, lambda: walk(cache_k, cache_v)[0], lambda: o)


def _mix_body(diff_ref, o_ref, gp_ref, gs_ref, x_ref, wp_ref, ps_ref, wsb_ref, wo_ref, nf_ref,
              wr_ref, br_ref, x1_ref, hp_ref, route_ref, cnt_ref, *, tm):
    n_pool = wp_ref.shape[0]
    group = wp_ref.shape[1]
    diff = diff_ref[...]
    pool = jnp.concatenate(
        [jnp.dot(diff[:, g * group:(g + 1) * group], wp_ref[g], preferred_element_type=F32)
         for g in range(n_pool)], axis=-1) * ps_ref[...]
    sb = jnp.dot(o_ref[...], wsb_ref[...], preferred_element_type=F32)
    mixed = gp_ref[...].astype(F32) * pool + gs_ref[...].astype(F32) * sb
    x1 = x_ref[...] + jnp.dot(mixed.astype(BF16), wo_ref[...], preferred_element_type=F32)
    x1_ref[...] = x1
    h = (x1 * lax.rsqrt(jnp.mean(x1 * x1, axis=-1, keepdims=True) + EPS)) * nf_ref[...]

    d_half = h.shape[1] // 2
    lo_bits = pltpu.bitcast(h[:, :d_half].astype(BF16).astype(F32), jnp.uint32)
    hi_bits = pltpu.bitcast(h[:, d_half:].astype(BF16).astype(F32), jnp.uint32)
    hp_ref[...] = (lo_bits >> 16) | (hi_bits & jnp.uint32(0xFFFF0000))

    hh = h.astype(BF16)
    hl = (h - hh.astype(F32)).astype(BF16)
    r = jnp.dot(jnp.concatenate([hh, hl], axis=0), wr_ref[...], preferred_element_type=F32)
    logits = (r[:tm, :LANES] + r[:tm, LANES:]) + (r[tm:, :LANES] + r[tm:, LANES:]) + br_ref[...]
    lane = lax.broadcasted_iota(jnp.int32, (tm, LANES), 1)
    big = jnp.int32(LANES)

    def first_max(vals):
        m = jnp.max(vals, axis=-1, keepdims=True)
        idx = jnp.min(jnp.where(vals == m, lane, big), axis=-1, keepdims=True)
        return m, idx

    gl = jnp.where(lane < N_GROUPS, logits, NEG)
    gmax, grp = first_max(gl)
    p_grp = 1.0 / jnp.sum(jnp.exp(gl - gmax), axis=-1, keepdims=True)
    e_lo = N_GROUPS + grp * PER_GROUP
    el = jnp.where((lane >= e_lo) & (lane < e_lo + PER_GROUP), logits, NEG)
    m1, i1 = first_max(el)
    m2, i2 = first_max(jnp.where(lane == i1, NEG, el))
    t2 = jnp.exp(m2 - m1)
    w1 = p_grp / (1.0 + t2)
    w2 = w1 * t2
    e1 = i1 - N_GROUPS
    e2 = i2 - N_GROUPS

    oh1 = jnp.where(lane == e1, 1.0, 0.0).astype(BF16)
    oh2 = jnp.where(lane == e2, 1.0, 0.0).astype(BF16)
    rr = lax.broadcasted_iota(jnp.int32, (tm, tm), 0)
    cc = lax.broadcasted_iota(jnp.int32, (tm, tm), 1)
    before = jnp.where(cc < rr, 1.0, 0.0).astype(BF16)
    ones = jnp.ones((8, tm), BF16)
    pre1 = jnp.dot(before, oh1, preferred_element_type=F32)
    pre2 = jnp.dot(before, oh2, preferred_element_type=F32)
    c1 = jnp.dot(ones, oh1, preferred_element_type=F32)
    c2 = jnp.dot(ones, oh2, preferred_element_type=F32)
    rank1 = jnp.sum(jnp.where(lane == e1, pre1, 0.0), axis=-1, keepdims=True)
    rank2 = jnp.sum(jnp.where(lane == e2, pre2 + c1[0:1, :], 0.0), axis=-1, keepdims=True)
    cnt_ref[0] = c1 + c2

    route = jnp.where(lane == 0, e1.astype(F32), 0.0)
    route = jnp.where(lane == 1, e2.astype(F32), route)
    route = jnp.where(lane == 2, w1, route)
    route = jnp.where(lane == 3, w2, route)
    route = jnp.where(lane == 4, rank1, route)
    route = jnp.where(lane == 5, rank2, route)
    route_ref[...] = route


def _mix(diff, o, gp, gs, x, wp, ps, wsb, wo, nf, wr, br, tm):
    m, d = x.shape
    assert m % tm == 0
    row = lambda i: (i, 0)
    full = lambda a: pl.BlockSpec(a.shape, lambda i: (0,) * a.ndim, pipeline_mode=pl.Buffered(1))
    acts = [diff, o, gp, gs, x]
    consts = [wp, ps, wsb, wo, nf, wr, br]
    return pl.pallas_call(
        functools.partial(_mix_body, tm=tm),
        out_shape=[jax.ShapeDtypeStruct((m, d), F32), jax.ShapeDtypeStruct((m, d // 2), jnp.uint32),
                   jax.ShapeDtypeStruct((m, LANES), F32), jax.ShapeDtypeStruct((m // tm, 8, LANES), F32)],
        grid=(m // tm,),
        in_specs=[pl.BlockSpec((tm, a.shape[1]), row) for a in acts] + [full(a) for a in consts],
        out_specs=[pl.BlockSpec((tm, d), row), pl.BlockSpec((tm, d // 2), row),
                   pl.BlockSpec((tm, LANES), row), pl.BlockSpec((1, 8, LANES), lambda i: (i, 0, 0))],
        compiler_params=_cparams(("parallel",)),
        name="mix_outproj_router",
    )(*acts, *consts)


def _row_wait(src_ref, dst_ref, sem, n_rows):
    pltpu.make_async_copy(src_ref.at[pl.ds(0, n_rows)], dst_ref.at[pl.ds(0, n_rows)], sem).wait()


def _dispatch_body(slot_ref, zero_ref, hp_ref, xs_ref, buf, zbuf, sem, zsem, *, tm):
    i = pl.program_id(0)
    last = pl.num_programs(0) - 1
    par = i % 2

    @pl.when(i == 0)
    def _():
        zbuf[...] = jnp.zeros_like(zbuf)
        bm = zbuf.shape[0]
        for wait in (False, True):
            for z in range(zero_ref.shape[0]):
                @pl.when(zero_ref[z] >= 0)
                def _(z=z, wait=wait):
                    start = pl.multiple_of(jnp.maximum(zero_ref[z], 0), bm)
                    copy = pltpu.make_async_copy(zbuf, xs_ref.at[pl.ds(start, bm)], zsem)
                    copy.wait() if wait else copy.start()

    def drain(slot):
        for _ in range(2):
            _row_wait(buf.at[slot], xs_ref, sem.at[slot], tm)

    @pl.when(i >= 2)
    def _():
        drain(par)

    buf[par] = hp_ref[...]
    base = i * tm

    def body(t, _):
        src = buf.at[par, pl.ds(t, 1)]
        for kk in range(2):
            dst = xs_ref.at[pl.ds(slot_ref[2 * (base + t) + kk], 1)]
            pltpu.make_async_copy(src, dst, sem.at[par]).start()
        return 0

    lax.fori_loop(0, tm, body, 0, unroll=8)

    @pl.when(i == last)
    def _():
        drain(par)

        @pl.when(i >= 1)
        def _():
            drain(1 - par)


def _dispatch(slots_flat, zero_starts, hp, n_slots, tm, bm):
    m, dh = hp.shape
    assert m % tm == 0
    return pl.pallas_call(
        functools.partial(_dispatch_body, tm=tm),
        out_shape=jax.ShapeDtypeStruct((n_slots, dh), hp.dtype),
        grid_spec=pltpu.PrefetchScalarGridSpec(
            num_scalar_prefetch=2, grid=(m // tm,),
            in_specs=[pl.BlockSpec((tm, dh), lambda i, s, z: (i, 0))],
            out_specs=pl.BlockSpec(memory_space=pl.ANY),
            scratch_shapes=[pltpu.VMEM((2, tm, dh), hp.dtype), pltpu.VMEM((bm, dh), hp.dtype),
                            pltpu.SemaphoreType.DMA((2,)), pltpu.SemaphoreType.DMA]),
        compiler_params=_cparams(("arbitrary",), disable_bounds_checks=True, has_side_effects=True),
        name="moe_dispatch",
    )(slots_flat, zero_starts, hp)


def _ffn_body(be_ref, nv_ref, xs_ref, wg_ref, wu_ref, wd_ref, y_ref):
    del be_ref

    @pl.when(pl.program_id(0) >= nv_ref[0])
    def _():
        y_ref[...] = jnp.zeros_like(y_ref)

    @pl.when(pl.program_id(0) < nv_ref[0])
    def _():
        words = xs_ref[...]
        d_half = words.shape[1]
        x_lo = pltpu.bitcast(words << 16, F32).astype(BF16)
        x_hi = pltpu.bitcast(words & jnp.uint32(0xFFFF0000), F32).astype(BF16)

        def proj(w_ref):
            return (jnp.dot(x_lo, w_ref[0, :d_half, :], preferred_element_type=F32)
                    + jnp.dot(x_hi, w_ref[0, d_half:, :], preferred_element_type=F32))

        hid = jax.nn.silu(proj(wg_ref)) * proj(wu_ref)
        y_ref[...] = jnp.dot(hid.astype(BF16), wd_ref[0], preferred_element_type=F32)


def _ffn(block_expert, n_valid, xs, wg, wu, wd, bm):
    n_slots, d_half = xs.shape
    d = 2 * d_half
    de = wg.shape[2]
    live = lambda b, be, nv: (jnp.minimum(b, nv[0] - 1), 0)
    wsel = lambda b, be, nv: (be[b], 0, 0)
    return pl.pallas_call(
        _ffn_body,
        out_shape=jax.ShapeDtypeStruct((n_slots, d), F32),
        grid_spec=pltpu.PrefetchScalarGridSpec(
            num_scalar_prefetch=2, grid=(n_slots // bm,),
            in_specs=[pl.BlockSpec((bm, d_half), live), pl.BlockSpec((1, d, de), wsel),
                      pl.BlockSpec((1, d, de), wsel), pl.BlockSpec((1, de, d), wsel)],
            out_specs=pl.BlockSpec((bm, d), lambda b, be, nv: (b, 0))),
        compiler_params=_cparams(("arbitrary",)),
        name="moe_ffn",
    )(block_expert, n_valid, xs, wg, wu, wd)


def _final_body(slot_ref, x1_ref, route_ref, g_ref, y_hbm, o_ref, buf, sem, *, tm):
    i = pl.program_id(0)
    n_steps = pl.num_programs(0)

    def issue(step, par):
        def body(t, _):
            for kk in range(2):
                src = y_hbm.at[pl.ds(slot_ref[2 * (step * tm + t) + kk], 1)]
                pltpu.make_async_copy(src, buf.at[par, kk, pl.ds(t, 1)], sem.at[par]).start()
            return 0
        lax.fori_loop(0, tm, body, 0, unroll=8)

    @pl.when(i == 0)
    def _():
        issue(0, 0)

    @pl.when(i + 1 < n_steps)
    def _():
        issue(i + 1, (i + 1) % 2)

    par = i % 2
    for kk in range(2):
        _row_wait(y_hbm, buf.at[par, kk], sem.at[par], tm)
    route = route_ref[...]
    x2 = x1_ref[...] + route[:, 2:3] * buf[par, 0] + route[:, 3:4] * buf[par, 1]
    o_ref[...] = (x2 * lax.rsqrt(jnp.mean(x2 * x2, axis=-1, keepdims=True) + EPS)) * g_ref[...]


def _final(slots_flat, x1, route, g, y, tm):
    m, d = x1.shape
    assert m % tm == 0
    row = lambda i, s: (i, 0)
    return pl.pallas_call(
        functools.partial(_final_body, tm=tm),
        out_shape=jax.ShapeDtypeStruct((m, d), F32),
        grid_spec=pltpu.PrefetchScalarGridSpec(
            num_scalar_prefetch=1, grid=(m // tm,),
            in_specs=[pl.BlockSpec((tm, d), row), pl.BlockSpec((tm, LANES), row),
                      pl.BlockSpec((1, d), lambda i, s: (0, 0)), pl.BlockSpec(memory_space=pl.ANY)],
            out_specs=pl.BlockSpec((tm, d), row),
            scratch_shapes=[pltpu.VMEM((2, 2, tm, d), F32), pltpu.SemaphoreType.DMA((2,))]),
        compiler_params=_cparams(("arbitrary",), disable_bounds_checks=True),
        name="moe_combine_final_norm",
    )(slots_flat, x1, route, g, y)


def _routing_tables(route, cnt, tm, bm):
    m = route.shape[0]
    n_blocks = (2 * m) // bm + N_EXPERTS
    counts = cnt[:, 0, :N_EXPERTS].astype(jnp.int32)
    sizes = jnp.sum(counts, axis=0)
    padded = (sizes + bm - 1) // bm * bm
    pad_end = jnp.cumsum(padded)
    base = (pad_end - padded)[None, :] + jnp.cumsum(counts, axis=0) - counts
    base_tok = jnp.repeat(base, tm, axis=0)
    e = route[:, 0:2].astype(jnp.int32)
    rank = route[:, 4:6].astype(jnp.int32)
    sel = e[:, :, None] == jnp.arange(N_EXPERTS, dtype=jnp.int32)[None, None, :]
    slots = jnp.sum(jnp.where(sel, base_tok[:, None, :], 0), axis=-1) + rank
    n_valid = (pad_end[-1] // bm).astype(jnp.int32)
    blk = jnp.minimum(jnp.arange(n_blocks, dtype=jnp.int32), n_valid - 1)
    block_expert = jnp.minimum(jnp.sum(pad_end[None, :] <= (blk * bm)[:, None], axis=1), N_EXPERTS - 1)
    last_blk = jnp.where(padded > 0, pad_end - bm, -1)
    tail = n_valid + jnp.arange(N_EXPERTS, dtype=jnp.int32)
    tail = jnp.where(tail < n_blocks, tail * bm, -1)
    zero_starts = jnp.concatenate([last_blk, tail]).astype(jnp.int32)
    return slots.reshape(-1), block_expert.astype(jnp.int32), n_valid.reshape(1), zero_starts, n_blocks * bm


def _stream(x, pool_hist, k_hist, v_hist, p, *, tq, tk, group, tn, tm_in, tm_mix, tm_fin, bm):
    b, n, d = x.shape
    m = b * n
    past = 0 if k_hist is None else k_hist.shape[1]
    x2d = x.reshape(m, d)
    u, q, kb, vb, k, v, gp, gs = _inproj(x2d, p['norm_mix'], p['w_in'], _fit(m, tm_in))
    dp = u.shape[1]
    diff = _pool_diff(pool_hist, u.reshape(b, n, dp), past, tn).reshape(m, dp)
    shp = (b, n, kb.shape[1])
    if k_hist is None:
        o = _attention_self(q.reshape(shp), kb.reshape(shp), vb.reshape(shp), tq)
    else:
        o = _attention_recent_first(q.reshape(shp), kb.reshape(shp), vb.reshape(shp), k_hist, v_hist, tk, group)
    o = o.reshape(m, -1)
    x1, hp, route, cnt = _mix(diff, o, gp, gs, x2d, p['w_pool'], p['pool_scale'], p['w_sb_out'], p['w_out'],
                              p['norm_ffn'], p['w_r'], p['b_r'], tm_mix)
    slots, block_expert, n_valid, zero_starts, n_slots = _routing_tables(route, cnt, tm_mix, bm)
    xs = _dispatch(slots, zero_starts, hp, n_slots, tm_mix, bm)
    y = _ffn(block_expert, n_valid, xs, p['w_g'], p['w_u'], p['w_d'], bm)
    out = _final(slots, x1, route, p['norm_final'], y, tm_fin)
    return out.reshape(b, n, d), u.reshape(b, n, dp), k, v


def kernel(x_prompt, x_sample, cache_sb_k, cache_sb_v, state_pool, norm_mix, w_in, w_pool, pool_scale, w_sb_out,
           w_out, norm_ffn, w_router_group, b_router_group, w_router_expert, b_router_expert, w_exp_gate,
           w_exp_up, w_exp_down, norm_final):
    depth = w_in.shape[0]
    assert depth == 1
    bp, sp, d = x_prompt.shape
    bs, ss, _ = x_sample.shape
    past = cache_sb_k.shape[2]
    heads, hd = cache_sb_k.shape[3], cache_sb_k.shape[4]
    assert hd == HEAD_DIM
    dp = state_pool.shape[3]
    n_state = state_pool.shape[2]

    w_r = jnp.concatenate([w_router_group[0], w_router_expert[0]], axis=1)
    w_r = jnp.pad(w_r, ((0, 0), (0, LANES - w_r.shape[1])))
    w_r_hi = w_r.astype(BF16)
    b_r = jnp.concatenate([b_router_group[0], b_router_expert[0]])
    p = dict(
        norm_mix=norm_mix[0][None, :], w_in=w_in[0].astype(BF16), w_pool=w_pool[0].astype(BF16),
        pool_scale=pool_scale[0][None, :], w_sb_out=w_sb_out[0].astype(BF16), w_out=w_out[0].astype(BF16),
        norm_ffn=norm_ffn[0][None, :],
        w_r=jnp.concatenate([w_r_hi, (w_r - w_r_hi.astype(F32)).astype(BF16)], axis=1),
        b_r=jnp.pad(b_r, (0, LANES - b_r.shape[0]))[None, :].astype(F32),
        w_g=w_exp_gate[0].astype(BF16), w_u=w_exp_up[0].astype(BF16), w_d=w_exp_down[0].astype(BF16),
        norm_final=norm_final[None, :])

    hist_p = jnp.zeros((bp, POOL_HIST, dp), F32)
    yp, up, kp, vp = _stream(x_prompt, hist_p, None, None, p, tq=256, tk=256, group=2, tn=min(sp, 512),
                             tm_in=1024, tm_mix=256, tm_fin=256, bm=256)
    hist_s = jnp.pad(state_pool[0], ((0, 0), (POOL_HIST - n_state, 0), (0, 0)))
    ys, us, ks, vs = _stream(x_sample, hist_s, cache_sb_k[0], cache_sb_v[0], p, tq=ss, tk=256, group=2, tn=ss,
                             tm_in=1024, tm_mix=256, tm_fin=256, bm=256)

    def pool_state(hist, u):
        return jnp.concatenate([hist[:, POOL_HIST - n_state:], u], axis=1)[:, -n_state:][None]

    return (yp, ys,
            kp.reshape(1, bp, sp, heads, hd), vp.reshape(1, bp, sp, heads, hd), pool_state(hist_p, up),
            ks.reshape(1, bs, ss, heads, hd), vs.reshape(1, bs, ss, heads, hd), pool_state(hist_s, us))
```

```python
import functools

import jax
import jax.numpy as jnp
from jax import lax
from jax.experimental import pallas as pl
from jax.experimental.pallas import tpu as pltpu

F32 = jnp.float32
BF16 = jnp.bfloat16

EPS = 1e-6
HEAD_DIM = 64
LANES = 128
POOL_WINDOWS = (2, 4, 8, 16)
POOL_HIST = 16
N_GROUPS = 4
PER_GROUP = 8
N_EXPERTS = N_GROUPS * PER_GROUP
VMEM_LIMIT = 58 * 1024 * 1024
NEG = -1e30
LOG2E = 1.4426950408889634
MIX_PARTS = 2
FFN_PARTS = 2
SKIP_MASS = 160.0


def _fit(m, tile):
    while m % tile:
        tile //= 2
    return tile


def _cparams(sem, **kw):
    return pltpu.CompilerParams(dimension_semantics=sem, vmem_limit_bytes=VMEM_LIMIT, **kw)


def _proj_main_body(h_ref, w_ref, u_ref, q_ref, kb_ref, vb_ref, k_hbm, v_hbm, kv_buf, sem):
    i = pl.program_id(0)
    j = pl.program_id(1)
    last = pl.num_programs(0) - 1
    tm = h_ref.shape[0]
    heads = k_hbm.shape[1]

    def proj():
        return jnp.dot(h_ref[...], w_ref[...], preferred_element_type=F32)

    def head_copies(slot, dst_hbm, step):
        row0 = pl.multiple_of(step * tm, tm)
        return [pltpu.make_async_copy(kv_buf.at[slot, hh], dst_hbm.at[pl.ds(row0, tm), hh, :], sem.at[slot])
                for hh in range(heads)]

    @pl.when(j == 0)
    def _():
        u_ref[...] = proj()

    @pl.when(j == 1)
    def _():
        q_ref[...] = (proj() * (HEAD_DIM ** -0.5)).astype(BF16)

    for jj, slot, dense_ref, dst_hbm in ((2, 0, kb_ref, k_hbm), (3, 1, vb_ref, v_hbm)):
        @pl.when(j == jj)
        def _(slot=slot, dense_ref=dense_ref, dst_hbm=dst_hbm):
            @pl.when(i > 0)
            def _():
                for c in head_copies(slot, dst_hbm, i - 1):
                    c.wait()
            acc = proj()
            dense_ref[...] = acc.astype(BF16)
            for hh in range(heads):
                kv_buf[slot, hh] = acc[:, hh * HEAD_DIM:(hh + 1) * HEAD_DIM]
            for c in head_copies(slot, dst_hbm, i):
                c.start()

    @pl.when((i == last) & (j == 3))
    def _():
        for slot, dst_hbm in ((0, k_hbm), (1, v_hbm)):
            for c in head_copies(slot, dst_hbm, i):
                c.wait()


def _proj_gates_body(x_ref, g_ref, w_ref, h_ref, gp_ref, gs_ref, *, tn):
    j = pl.program_id(1)

    @pl.when(j == 0)
    def _():
        x = x_ref[...]
        r = lax.rsqrt(jnp.mean(x * x, axis=-1, keepdims=True) + EPS)
        h_ref[...] = ((x * r) * g_ref[...]).astype(BF16)

    for jj, ref in ((0, gp_ref), (2, gs_ref)):
        for half in range(2):
            @pl.when(j == jj + half)
            def _(ref=ref, half=half):
                a = jnp.dot(h_ref[...], w_ref[...], preferred_element_type=F32)
                ref[:, half * tn:(half + 1) * tn] = (0.5 * jnp.tanh(0.5 * a) + 0.5).astype(BF16)


def _inproj(x, g, w_bf, tm):
    m, d = x.shape
    tn = d // 2
    assert w_bf.shape == (d, 8 * tn) and m % tm == 0
    row = lambda i, j: (i, 0)
    h_spec = pl.BlockSpec((tm, d), row)
    h, gp, gs = pl.pallas_call(
        functools.partial(_proj_gates_body, tn=tn),
        out_shape=[jax.ShapeDtypeStruct((m, d), BF16)] * 3,
        grid=(m // tm, 4),
        in_specs=[h_spec, pl.BlockSpec((1, d), lambda i, j: (0, 0)),
                  pl.BlockSpec((d, tn), lambda i, j: (0, j + 4))],
        out_specs=[h_spec] * 3,
        compiler_params=_cparams(("parallel", "arbitrary")),
        name="inproj_gates",
    )(x, g, w_bf)
    dense = lambda dt: jax.ShapeDtypeStruct((m, tn), dt)
    cache = jax.ShapeDtypeStruct((m, tn // HEAD_DIM, HEAD_DIM), F32)
    any_spec = pl.BlockSpec(memory_space=pl.ANY)
    u, q, kb, vb, k, v = pl.pallas_call(
        _proj_main_body,
        out_shape=[dense(F32), dense(BF16), dense(BF16), dense(BF16), cache, cache],
        grid=(m // tm, 4),
        in_specs=[h_spec, pl.BlockSpec((d, tn), lambda i, j: (0, j))],
        out_specs=[pl.BlockSpec((tm, tn), row)] * 4 + [any_spec] * 2,
        scratch_shapes=[pltpu.VMEM((2, tn // HEAD_DIM, tm, HEAD_DIM), F32), pltpu.SemaphoreType.DMA((2,))],
        compiler_params=_cparams(("arbitrary", "arbitrary"), has_side_effects=True),
        name="inproj_main",
    )(h, w_bf)
    return u, q, kb, vb, k, v, gp, gs


def _pool_body(hist_ref, u_ref, o_ref, ext_scr, *, pos0, tn):
    s = pl.program_id(1)

    @pl.when(s == 0)
    def _():
        ext_scr[0:POOL_HIST, :] = hist_ref[0]

    ext_scr[POOL_HIST:POOL_HIST + tn, :] = u_ref[0]
    pos = pos0 + s * tn + lax.broadcasted_iota(jnp.int32, (tn, 1), 0)
    group = u_ref.shape[2] // len(POOL_WINDOWS)
    for g, w in enumerate(POOL_WINDOWS):
        lo, hi = g * group, (g + 1) * group
        cur = ext_scr[POOL_HIST:POOL_HIST + tn, lo:hi]
        tot = cur
        for dlt in range(1, w):
            tot = tot + ext_scr[POOL_HIST - dlt:POOL_HIST - dlt + tn, lo:hi]
        cnt = jnp.minimum(pos + 1, w).astype(F32)
        o_ref[0, :, lo:hi] = (tot / cnt - cur).astype(BF16)
    ext_scr[0:POOL_HIST, :] = ext_scr[tn:tn + POOL_HIST, :]


def _pool_diff(hist, u, pos0, tn):
    b, n, dp = u.shape
    assert n % tn == 0 and hist.shape == (b, POOL_HIST, dp)
    return pl.pallas_call(
        functools.partial(_pool_body, pos0=pos0, tn=tn),
        out_shape=jax.ShapeDtypeStruct((b, n, dp), BF16),
        grid=(b, n // tn),
        in_specs=[pl.BlockSpec((1, POOL_HIST, dp), lambda i, s: (i, 0, 0)),
                  pl.BlockSpec((1, tn, dp), lambda i, s: (i, s, 0))],
        out_specs=pl.BlockSpec((1, tn, dp), lambda i, s: (i, s, 0)),
        scratch_shapes=[pltpu.VMEM((POOL_HIST + tn, dp), F32)],
        compiler_params=_cparams(("parallel", "arbitrary")),
        name="pool_diff",
    )(hist, u)


def _softplus2(z2):
    neg_abs = pltpu.bitcast(pltpu.bitcast(z2, jnp.uint32) | jnp.uint32(0x80000000), F32)
    return jnp.maximum(z2, 0.0) + jnp.log2(1.0 + jnp.exp2(neg_abs))


def _suffix_matrix(n):
    r = lax.broadcasted_iota(jnp.int32, (n, n), 0)
    c = lax.broadcasted_iota(jnp.int32, (n, n), 1)
    return jnp.where(r >= c, 1.0, 0.0).astype(BF16)


def _stack_heads(q2):
    lane = lax.broadcasted_iota(jnp.int32, q2.shape, 1)
    zero = jnp.zeros_like(q2)
    return jnp.concatenate([jnp.where(lane < HEAD_DIM, q2, zero), jnp.where(lane >= HEAD_DIM, q2, zero)], axis=0)


def _unstack_heads(acc, t):
    lane = lax.broadcasted_iota(jnp.int32, (t, LANES), 1)
    return jnp.where(lane < HEAD_DIM, acc[:t], acc[t:])


def _causal_mask(t):
    r = lax.broadcasted_iota(jnp.int32, (t, t), 0)
    c = lax.broadcasted_iota(jnp.int32, (t, t), 1)
    m = c < r
    return jnp.concatenate([m, m], axis=0)


def _sb_group(q_st, k_blocks, v_blocks, suffix, carry, masks, transposed_keys):
    dn = (((1,), (0,)), ((), ())) if transposed_keys else (((1,), (1,)), ((), ()))
    zs = [lax.dot_general(q_st, kb, dn, preferred_element_type=F32) * LOG2E for kb in k_blocks]
    his, los = [], []
    for z, mask in zip(zs, masks):
        sp = _softplus2(z)
        if mask is not None:
            sp = jnp.where(mask, sp, 0.0)
        hi = pltpu.bitcast(pltpu.bitcast(sp, jnp.uint32) & jnp.uint32(0xFFFF0000), F32)
        his.append(hi.astype(BF16))
        los.append((sp - hi).astype(BF16))
    cs = [jnp.dot(hi, suffix, preferred_element_type=F32) + jnp.dot(lo, suffix, preferred_element_type=F32)
          for hi, lo in zip(his, los)]
    out = None
    for z, c, vb, mask in zip(zs, cs, v_blocks, masks):
        arg = z - c - carry
        if mask is not None:
            arg = jnp.where(mask, arg, NEG)
        o = jnp.dot(jnp.exp2(arg).astype(BF16), vb, preferred_element_type=F32)
        out = o if out is None else out + o
        carry = carry + c[:, 0:1]
    return out, carry


def _attn_self_body(q_ref, k_ref, v_ref, o_ref, kt_scr, v_scr, acc_scr, car_scr, min_scr, *, t):
    qi = pl.program_id(2)

    @pl.when(qi == 0)
    def _fill():
        def body(c, _):
            row0 = pl.multiple_of(c * t, t)
            kt_scr[c] = k_ref[0, pl.ds(row0, t), :].astype(F32).T.astype(BF16)
            v_scr[c] = v_ref[0, pl.ds(row0, t), :].astype(BF16)
            return 0
        lax.fori_loop(0, kt_scr.shape[0], body, 0)

    q_st = _stack_heads(q_ref[0])
    suffix = _suffix_matrix(t)
    mask = _causal_mask(t)

    def run(blocks, masks, first):
        carry = jnp.zeros((2 * t, 1), F32) if first else car_scr[:, 0:1]
        out, carry = _sb_group(q_st, [kt_scr[b] for b in blocks], [v_scr[b] for b in blocks], suffix, carry,
                               masks, True)
        acc_scr[...] = out if first else acc_scr[...] + out
        car_scr[...] = jnp.broadcast_to(carry, car_scr.shape)
        min_scr[0] = jnp.min(carry)

    @pl.when(qi == 0)
    def _():
        run([0], [mask], True)

    @pl.when(qi > 0)
    def _():
        run([qi, qi - 1], [mask, None], True)

    rest = jnp.maximum(qi - 1, 0)

    def more(it):
        return (it < rest // 2) & (min_scr[0] < SKIP_MASS)

    def pair(it):
        b0 = qi - 2 - 2 * it
        run([b0, b0 - 1], [None, None], False)
        return it + 1

    lax.while_loop(more, pair, 0)

    @pl.when((rest % 2 == 1) & (min_scr[0] < SKIP_MASS))
    def _():
        run([0], [None], False)

    o_ref[0] = _unstack_heads(acc_scr[...], t).astype(o_ref.dtype)


def _attn_hist_body(q_ref, kn_ref, vn_ref, kh_ref, vh_ref, o_ref, left_ref, acc_scr, car_scr, min_scr, *, tk,
                    group):
    t = q_ref.shape[1]
    past = kh_ref.shape[1]
    q_st = _stack_heads(q_ref[0])

    def keep(out, carry, first):
        acc_scr[...] = out if first else acc_scr[...] + out
        car_scr[...] = jnp.broadcast_to(carry, car_scr.shape)
        min_scr[0] = jnp.min(carry)

    keep(*_sb_group(q_st, [kn_ref[0].astype(BF16)], [vn_ref[0].astype(BF16)], _suffix_matrix(t),
                    jnp.zeros((2 * t, 1), F32), [_causal_mask(t)], False), True)
    suffix = _suffix_matrix(tk)
    for top in range(past // tk, 0, -group):
        @pl.when(min_scr[0] < SKIP_MASS)
        def _(top=top):
            blocks = range(top - 1, top - 1 - group, -1)
            keep(*_sb_group(q_st, [kh_ref[0, b * tk:(b + 1) * tk, :].astype(BF16) for b in blocks],
                            [vh_ref[0, b * tk:(b + 1) * tk, :].astype(BF16) for b in blocks],
                            suffix, car_scr[:, 0:1], [None] * group, False), False)
    o_ref[0] = _unstack_heads(acc_scr[...], t).astype(o_ref.dtype)
    left_ref[0, 0] = jnp.full(left_ref.shape[2:], min_scr[0], F32)


def _attn_specs(t):
    tile = pl.BlockSpec((1, t, LANES), lambda i, p, s: (i, s, p))
    seq = lambda rows: pl.BlockSpec((1, rows, LANES), lambda i, p, s: (i, 0, p))
    state = [pltpu.VMEM((2 * t, LANES), F32), pltpu.VMEM((2 * t, LANES), F32), pltpu.SMEM((1,), F32)]
    return tile, seq, state


def _attention_self(q, k, v, t):
    b, n, dm = q.shape
    assert n % t == 0 and dm % LANES == 0
    tile, seq, state = _attn_specs(t)
    return pl.pallas_call(
        functools.partial(_attn_self_body, t=t),
        out_shape=jax.ShapeDtypeStruct((b, n, dm), BF16),
        grid=(b, dm // LANES, n // t),
        in_specs=[tile, seq(n), seq(n)],
        out_specs=tile,
        scratch_shapes=[pltpu.VMEM((n // t, LANES, t), BF16), pltpu.VMEM((n // t, t, LANES), BF16)] + state,
        compiler_params=_cparams(("parallel", "parallel", "arbitrary")),
        name="sb_attention",
    )(q, k, v)


def _attention_cached(q, k_new, v_new, k_hist, v_hist, tk, group):
    b, t, dm = q.shape
    past = k_hist.shape[1]
    assert dm % LANES == 0 and past % (tk * group) == 0
    tile, seq, state = _attn_specs(t)
    return pl.pallas_call(
        functools.partial(_attn_hist_body, tk=tk, group=group),
        out_shape=[jax.ShapeDtypeStruct((b, t, dm), BF16), jax.ShapeDtypeStruct((b, dm // LANES, 8, LANES), F32)],
        grid=(b, dm // LANES, 1),
        in_specs=[tile, seq(t), seq(t), seq(past), seq(past)],
        out_specs=[tile, pl.BlockSpec((1, 1, 8, LANES), lambda i, p, s: (i, p, 0, 0))],
        scratch_shapes=state,
        compiler_params=_cparams(("parallel", "parallel", "arbitrary")),
        name="sb_attention_cached",
    )(q, k_new, v_new, k_hist, v_hist)


def _attention_recent_first(q, k_new, v_new, cache_k, cache_v, tk, group):
    b, past = cache_k.shape[:2]
    flat = lambda c: c.reshape(b, c.shape[1], -1)
    walk = lambda ck, cv: _attention_cached(q, k_new, v_new, flat(ck), flat(cv), tk, group)
    recent = tk * group
    if past <= recent:
        return walk(cache_k, cache_v)[0]
    o, left = walk(cache_k[:, past - recent:], cache_v[:, past - recent:])
    return lax.cond(jnp.min(left) < SKIP_MASS, lambda: walk(cache_k, cache_v)[0], lambda: o)


def _mix_body(diff_ref, o_ref, gp_ref, gs_ref, x_ref, wp_ref, ps_ref, wsb_ref, wo_ref, nf_ref,
              wr_ref, br_ref, x1_ref, hp_ref, route_ref, cnt_ref, logit_scr, *, tm):
    @pl.when(pl.program_id(0) == 0)
    def _():
        logit_scr[...] = jnp.zeros_like(logit_scr)

    late_logits = logit_scr[...]

    n_pool = wp_ref.shape[0]
    group = wp_ref.shape[1]
    rows = tm // MIX_PARTS
    parts = [pl.ds(i * rows, rows) for i in range(MIX_PARTS)]
    pools = [jnp.concatenate(
        [jnp.dot(diff_ref[s, g * group:(g + 1) * group], wp_ref[g], preferred_element_type=F32)
         for g in range(n_pool)], axis=-1) * ps_ref[...] for s in parts]
    sbs = [jnp.dot(o_ref[s, :], wsb_ref[...], preferred_element_type=F32) for s in parts]
    mixed = [(gp_ref[s, :].astype(F32) * pool + gs_ref[s, :].astype(F32) * sb).astype(BF16)
             for s, pool, sb in zip(parts, pools, sbs)]
    x1s = [x_ref[s, :] + jnp.dot(mx, wo_ref[...], preferred_element_type=F32) for s, mx in zip(parts, mixed)]
    splits = []
    for s, x1 in zip(parts, x1s):
        x1_ref[s, :] = x1
        h = (x1 * lax.rsqrt(jnp.mean(x1 * x1, axis=-1, keepdims=True) + EPS)) * nf_ref[...]
        d_half = h.shape[1] // 2
        lo_bits = pltpu.bitcast(h[:, :d_half].astype(BF16).astype(F32), jnp.uint32)
        hi_bits = pltpu.bitcast(h[:, d_half:].astype(BF16).astype(F32), jnp.uint32)
        hp_ref[s, :] = (lo_bits >> 16) | (hi_bits & jnp.uint32(0xFFFF0000))
        hh = h.astype(BF16)
        splits.append(jnp.concatenate([hh, (h - hh.astype(F32)).astype(BF16)], axis=0))

    rs = [jnp.dot(sp, wr_ref[...], preferred_element_type=F32) for sp in splits]
    logit_scr[...] = jnp.concatenate(
        [(r[:rows, :LANES] + r[:rows, LANES:]) + (r[rows:, :LANES] + r[rows:, LANES:]) for r in rs],
        axis=0) + br_ref[...]
    _route_tile(late_logits, route_ref, cnt_ref, tm)


def _route_tile(logits, route_ref, cnt_ref, tm):
    lane = lax.broadcasted_iota(jnp.int32, (tm, LANES), 1)
    big = jnp.int32(LANES)

    def first_max(vals):
        m = jnp.max(vals, axis=-1, keepdims=True)
        idx = jnp.min(jnp.where(vals == m, lane, big), axis=-1, keepdims=True)
        return m, idx

    gl = jnp.where(lane < N_GROUPS, logits, NEG)
    gmax, grp = first_max(gl)
    p_grp = 1.0 / jnp.sum(jnp.exp(gl - gmax), axis=-1, keepdims=True)
    e_lo = N_GROUPS + grp * PER_GROUP
    el = jnp.where((lane >= e_lo) & (lane < e_lo + PER_GROUP), logits, NEG)
    m1, i1 = first_max(el)
    m2, i2 = first_max(jnp.where(lane == i1, NEG, el))
    t2 = jnp.exp(m2 - m1)
    w1 = p_grp / (1.0 + t2)
    w2 = w1 * t2
    e1 = i1 - N_GROUPS
    e2 = i2 - N_GROUPS

    oh1 = jnp.where(lane == e1, 1.0, 0.0).astype(BF16)
    oh2 = jnp.where(lane == e2, 1.0, 0.0).astype(BF16)
    rr = lax.broadcasted_iota(jnp.int32, (tm, tm), 0)
    cc = lax.broadcasted_iota(jnp.int32, (tm, tm), 1)
    before = jnp.where(cc < rr, 1.0, 0.0).astype(BF16)
    ones = jnp.ones((8, tm), BF16)
    pre1 = jnp.dot(before, oh1, preferred_element_type=F32)
    pre2 = jnp.dot(before, oh2, preferred_element_type=F32)
    c1 = jnp.dot(ones, oh1, preferred_element_type=F32)
    c2 = jnp.dot(ones, oh2, preferred_element_type=F32)
    rank1 = jnp.sum(jnp.where(lane == e1, pre1, 0.0), axis=-1, keepdims=True)
    rank2 = jnp.sum(jnp.where(lane == e2, pre2 + c1[0:1, :], 0.0), axis=-1, keepdims=True)
    cnt_ref[0] = c1 + c2

    route = jnp.where(lane == 0, e1.astype(F32), 0.0)
    route = jnp.where(lane == 1, e2.astype(F32), route)
    route = jnp.where(lane == 2, w1, route)
    route = jnp.where(lane == 3, w2, route)
    route = jnp.where(lane == 4, rank1, route)
    route = jnp.where(lane == 5, rank2, route)
    route_ref[...] = route


def _mix(diff, o, gp, gs, x, wp, ps, wsb, wo, nf, wr, br, tm):
    m, d = x.shape
    assert m % tm == 0
    n = m // tm
    row = lambda i: (jnp.minimum(i, n - 1), 0)
    late = lambda i: (jnp.maximum(i - 1, 0), 0)
    full = lambda a: pl.BlockSpec(a.shape, lambda i: (0,) * a.ndim, pipeline_mode=pl.Buffered(1))
    acts = [diff, o, gp, gs, x]
    consts = [wp, ps, wsb, wo, nf, wr, br]
    return pl.pallas_call(
        functools.partial(_mix_body, tm=tm),
        out_shape=[jax.ShapeDtypeStruct((m, d), F32), jax.ShapeDtypeStruct((m, d // 2), jnp.uint32),
                   jax.ShapeDtypeStruct((m, LANES), F32), jax.ShapeDtypeStruct((n, 8, LANES), F32)],
        grid=(n + 1,),
        in_specs=[pl.BlockSpec((tm, a.shape[1]), row) for a in acts] + [full(a) for a in consts],
        out_specs=[pl.BlockSpec((tm, d), row), pl.BlockSpec((tm, d // 2), row),
                   pl.BlockSpec((tm, LANES), late), pl.BlockSpec((1, 8, LANES), lambda i: late(i) + (0,))],
        scratch_shapes=[pltpu.VMEM((tm, LANES), F32)],
        compiler_params=_cparams(("arbitrary",)),
        name="mix_outproj_router",
    )(*acts, *consts)


def _row_wait(src_ref, dst_ref, sem, n_rows):
    pltpu.make_async_copy(src_ref.at[pl.ds(0, n_rows)], dst_ref.at[pl.ds(0, n_rows)], sem).wait()


def _dispatch_body(slot_ref, zero_ref, hp_ref, xs_ref, buf, zbuf, sem, zsem, *, tm):
    i = pl.program_id(0)
    last = pl.num_programs(0) - 1
    par = i % 2

    @pl.when(i == 0)
    def _():
        zbuf[...] = jnp.zeros_like(zbuf)
        bm = zbuf.shape[0]
        for wait in (False, True):
            for z in range(zero_ref.shape[0]):
                @pl.when(zero_ref[z] >= 0)
                def _(z=z, wait=wait):
                    start = pl.multiple_of(jnp.maximum(zero_ref[z], 0), bm)
                    copy = pltpu.make_async_copy(zbuf, xs_ref.at[pl.ds(start, bm)], zsem)
                    copy.wait() if wait else copy.start()

    def drain(slot):
        for _ in range(2):
            _row_wait(buf.at[slot], xs_ref, sem.at[slot], tm)

    @pl.when(i >= 2)
    def _():
        drain(par)

    buf[par] = hp_ref[...]
    base = i * tm

    def body(t, _):
        src = buf.at[par, pl.ds(t, 1)]
        for kk in range(2):
            dst = xs_ref.at[pl.ds(slot_ref[2 * (base + t) + kk], 1)]
            pltpu.make_async_copy(src, dst, sem.at[par]).start()
        return 0

    lax.fori_loop(0, tm, body, 0, unroll=8)

    @pl.when(i == last)
    def _():
        drain(par)

        @pl.when(i >= 1)
        def _():
            drain(1 - par)


def _dispatch(slots_flat, zero_starts, hp, n_slots, tm, bm):
    m, dh = hp.shape
    assert m % tm == 0
    return pl.pallas_call(
        functools.partial(_dispatch_body, tm=tm),
        out_shape=jax.ShapeDtypeStruct((n_slots, dh), hp.dtype),
        grid_spec=pltpu.PrefetchScalarGridSpec(
            num_scalar_prefetch=2, grid=(m // tm,),
            in_specs=[pl.BlockSpec((tm, dh), lambda i, s, z: (i, 0))],
            out_specs=pl.BlockSpec(memory_space=pl.ANY),
            scratch_shapes=[pltpu.VMEM((2, tm, dh), hp.dtype), pltpu.VMEM((bm, dh), hp.dtype),
                            pltpu.SemaphoreType.DMA((2,)), pltpu.SemaphoreType.DMA]),
        compiler_params=_cparams(("arbitrary",), disable_bounds_checks=True, has_side_effects=True),
        name="moe_dispatch",
    )(slots_flat, zero_starts, hp)


def _ffn_body(be_ref, nv_ref, xs_ref, wg_ref, wu_ref, wd_ref, y_ref):
    del be_ref

    @pl.when(pl.program_id(0) >= nv_ref[0])
    def _():
        y_ref[...] = jnp.zeros_like(y_ref)

    @pl.when(pl.program_id(0) < nv_ref[0])
    def _():
        d_half = xs_ref.shape[1]
        rows = xs_ref.shape[0] // FFN_PARTS
        hids = []
        for i in range(FFN_PARTS):
            words = xs_ref[pl.ds(i * rows, rows), :]
            x_lo = pltpu.bitcast(words << 16, F32).astype(BF16)
            x_hi = pltpu.bitcast(words & jnp.uint32(0xFFFF0000), F32).astype(BF16)

            def proj(w_ref):
                return (jnp.dot(x_lo, w_ref[0, :d_half, :], preferred_element_type=F32)
                        + jnp.dot(x_hi, w_ref[0, d_half:, :], preferred_element_type=F32))

            hids.append((jax.nn.silu(proj(wg_ref)) * proj(wu_ref)).astype(BF16))
        for i, hid in enumerate(hids):
            y_ref[pl.ds(i * rows, rows), :] = jnp.dot(hid, wd_ref[0], preferred_element_type=F32)


def _ffn(block_expert, n_valid, xs, wg, wu, wd, bm):
    n_slots, d_half = xs.shape
    d = 2 * d_half
    de = wg.shape[2]
    live = lambda b, be, nv: (jnp.minimum(b, nv[0] - 1), 0)
    wsel = lambda b, be, nv: (be[b], 0, 0)
    return pl.pallas_call(
        _ffn_body,
        out_shape=jax.ShapeDtypeStruct((n_slots, d), F32),
        grid_spec=pltpu.PrefetchScalarGridSpec(
            num_scalar_prefetch=2, grid=(n_slots // bm,),
            in_specs=[pl.BlockSpec((bm, d_half), live), pl.BlockSpec((1, d, de), wsel),
                      pl.BlockSpec((1, d, de), wsel), pl.BlockSpec((1, de, d), wsel)],
            out_specs=pl.BlockSpec((bm, d), lambda b, be, nv: (b, 0))),
        compiler_params=_cparams(("arbitrary",)),
        name="moe_ffn",
    )(block_expert, n_valid, xs, wg, wu, wd)


def _final_body(slot_ref, x1_ref, route_ref, g_ref, y_hbm, o_ref, buf, sem, *, tm):
    i = pl.program_id(0)
    n_steps = pl.num_programs(0)

    def issue(step, par):
        def body(t, _):
            for kk in range(2):
                src = y_hbm.at[pl.ds(slot_ref[2 * (step * tm + t) + kk], 1)]
                pltpu.make_async_copy(src, buf.at[par, kk, pl.ds(t, 1)], sem.at[par]).start()
            return 0
        lax.fori_loop(0, tm, body, 0, unroll=8)

    @pl.when(i == 0)
    def _():
        issue(0, 0)

    @pl.when(i + 1 < n_steps)
    def _():
        issue(i + 1, (i + 1) % 2)

    par = i % 2
    for kk in range(2):
        _row_wait(y_hbm, buf.at[par, kk], sem.at[par], tm)
    route = route_ref[...]
    x2 = x1_ref[...] + route[:, 2:3] * buf[par, 0] + route[:, 3:4] * buf[par, 1]
    o_ref[...] = (x2 * lax.rsqrt(jnp.mean(x2 * x2, axis=-1, keepdims=True) + EPS)) * g_ref[...]


def _final(slots_flat, x1, route, g, y, tm):
    m, d = x1.shape
    assert m % tm == 0
    row = lambda i, s: (i, 0)
    return pl.pallas_call(
        functools.partial(_final_body, tm=tm),
        out_shape=jax.ShapeDtypeStruct((m, d), F32),
        grid_spec=pltpu.PrefetchScalarGridSpec(
            num_scalar_prefetch=1, grid=(m // tm,),
            in_specs=[pl.BlockSpec((tm, d), row), pl.BlockSpec((tm, LANES), row),
                      pl.BlockSpec((1, d), lambda i, s: (0, 0)), pl.BlockSpec(memory_space=pl.ANY)],
            out_specs=pl.BlockSpec((tm, d), row),
            scratch_shapes=[pltpu.VMEM((2, 2, tm, d), F32), pltpu.SemaphoreType.DMA((2,))]),
        compiler_params=_cparams(("arbitrary",), disable_bounds_checks=True),
        name="moe_combine_final_norm",
    )(slots_flat, x1, route, g, y)


def _routing_tables(route, cnt, tm, bm):
    m = route.shape[0]
    n_blocks = (2 * m) // bm + N_EXPERTS
    counts = cnt[:, 0, :N_EXPERTS].astype(jnp.int32)
    sizes = jnp.sum(counts, axis=0)
    padded = (sizes + bm - 1) // bm * bm
    pad_end = jnp.cumsum(padded)
    base = (pad_end - padded)[None, :] + jnp.cumsum(counts, axis=0) - counts
    base_tok = jnp.repeat(base, tm, axis=0)
    e = route[:, 0:2].astype(jnp.int32)
    rank = route[:, 4:6].astype(jnp.int32)
    sel = e[:, :, None] == jnp.arange(N_EXPERTS, dtype=jnp.int32)[None, None, :]
    slots = jnp.sum(jnp.where(sel, base_tok[:, None, :], 0), axis=-1) + rank
    n_valid = (pad_end[-1] // bm).astype(jnp.int32)
    blk = jnp.minimum(jnp.arange(n_blocks, dtype=jnp.int32), n_valid - 1)
    block_expert = jnp.minimum(jnp.sum(pad_end[None, :] <= (blk * bm)[:, None], axis=1), N_EXPERTS - 1)
    last_blk = jnp.where(padded > 0, pad_end - bm, -1)
    tail = n_valid + jnp.arange(N_EXPERTS, dtype=jnp.int32)
    tail = jnp.where(tail < n_blocks, tail * bm, -1)
    zero_starts = jnp.concatenate([last_blk, tail]).astype(jnp.int32)
    return slots.reshape(-1), block_expert.astype(jnp.int32), n_valid.reshape(1), zero_starts, n_blocks * bm


def _stream(x, pool_hist, k_hist, v_hist, p, *, tq, tk, group, tn, tm_in, tm_mix, tm_fin, bm):
    b, n, d = x.shape
    m = b * n
    past = 0 if k_hist is None else k_hist.shape[1]
    x2d = x.reshape(m, d)
    u, q, kb, vb, k, v, gp, gs = _inproj(x2d, p['norm_mix'], p['w_in'], _fit(m, tm_in))
    dp = u.shape[1]
    diff = _pool_diff(pool_hist, u.reshape(b, n, dp), past, tn).reshape(m, dp)
    shp = (b, n, kb.shape[1])
    if k_hist is None:
        o = _attention_self(q.reshape(shp), kb.reshape(shp), vb.reshape(shp), tq)
    else:
        o = _attention_recent_first(q.reshape(shp), kb.reshape(shp), vb.reshape(shp), k_hist, v_hist, tk, group)
    o = o.reshape(m, -1)
    x1, hp, route, cnt = _mix(diff, o, gp, gs, x2d, p['w_pool'], p['pool_scale'], p['w_sb_out'], p['w_out'],
                              p['norm_ffn'], p['w_r'], p['b_r'], tm_mix)
    slots, block_expert, n_valid, zero_starts, n_slots = _routing_tables(route, cnt, tm_mix, bm)
    xs = _dispatch(slots, zero_starts, hp, n_slots, tm_mix, bm)
    y = _ffn(block_expert, n_valid, xs, p['w_g'], p['w_u'], p['w_d'], bm)
    out = _final(slots, x1, route, p['norm_final'], y, tm_fin)
    return out.reshape(b, n, d), u.reshape(b, n, dp), k, v


def kernel(x_prompt, x_sample, cache_sb_k, cache_sb_v, state_pool, norm_mix, w_in, w_pool, pool_scale, w_sb_out,
           w_out, norm_ffn, w_router_group, b_router_group, w_router_expert, b_router_expert, w_exp_gate,
           w_exp_up, w_exp_down, norm_final):
    depth = w_in.shape[0]
    assert depth == 1
    bp, sp, d = x_prompt.shape
    bs, ss, _ = x_sample.shape
    past = cache_sb_k.shape[2]
    heads, hd = cache_sb_k.shape[3], cache_sb_k.shape[4]
    assert hd == HEAD_DIM
    dp = state_pool.shape[3]
    n_state = state_pool.shape[2]

    w_r = jnp.concatenate([w_router_group[0], w_router_expert[0]], axis=1)
    w_r = jnp.pad(w_r, ((0, 0), (0, LANES - w_r.shape[1])))
    w_r_hi = w_r.astype(BF16)
    b_r = jnp.concatenate([b_router_group[0], b_router_expert[0]])
    p = dict(
        norm_mix=norm_mix[0][None, :], w_in=w_in[0].astype(BF16), w_pool=w_pool[0].astype(BF16),
        pool_scale=pool_scale[0][None, :], w_sb_out=w_sb_out[0].astype(BF16), w_out=w_out[0].astype(BF16),
        norm_ffn=norm_ffn[0][None, :],
        w_r=jnp.concatenate([w_r_hi, (w_r - w_r_hi.astype(F32)).astype(BF16)], axis=1),
        b_r=jnp.pad(b_r, (0, LANES - b_r.shape[0]))[None, :].astype(F32),
        w_g=w_exp_gate[0].astype(BF16), w_u=w_exp_up[0].astype(BF16), w_d=w_exp_down[0].astype(BF16),
        norm_final=norm_final[None, :])

    hist_p = jnp.zeros((bp, POOL_HIST, dp), F32)
    yp, up, kp, vp = _stream(x_prompt, hist_p, None, None, p, tq=256, tk=256, group=2, tn=min(sp, 512),
                             tm_in=1024, tm_mix=256, tm_fin=256, bm=256)
    hist_s = jnp.pad(state_pool[0], ((0, 0), (POOL_HIST - n_state, 0), (0, 0)))
    ys, us, ks, vs = _stream(x_sample, hist_s, cache_sb_k[0], cache_sb_v[0], p, tq=ss, tk=256, group=2, tn=ss,
                             tm_in=1024, tm_mix=256, tm_fin=256, bm=256)

    def pool_state(hist, u):
        return jnp.concatenate([hist[:, POOL_HIST - n_state:], u], axis=1)[:, -n_state:][None]

    return (yp, ys,
            kp.reshape(1, bp, sp, heads, hd), vp.reshape(1, bp, sp, heads, hd), pool_state(hist_p, up),
            ks.reshape(1, bs, ss, heads, hd), vs.reshape(1, bs, ss, heads, hd), pool_state(hist_s, us))
```

```python
import functools

import jax
import jax.numpy as jnp
from jax import lax
from jax.experimental import pallas as pl
from jax.experimental.pallas import tpu as pltpu

F32 = jnp.float32
BF16 = jnp.bfloat16

EPS = 1e-6
HEAD_DIM = 64
LANES = 128
POOL_WINDOWS = (2, 4, 8, 16)
POOL_HIST = 16
N_GROUPS = 4
PER_GROUP = 8
N_EXPERTS = N_GROUPS * PER_GROUP
VMEM_LIMIT = 58 * 1024 * 1024
NEG = -1e30
LOG2E = 1.4426950408889634
MIX_PARTS = 2
FFN_PARTS = 2
SKIP_MASS = 160.0


def _fit(m, tile):
    while m % tile:
        tile //= 2
    return tile


def _cparams(sem, **kw):
    return pltpu.CompilerParams(dimension_semantics=sem, vmem_limit_bytes=VMEM_LIMIT, **kw)


def _proj_main_body(h_ref, w_ref, u_ref, q_ref, kb_ref, vb_ref, k_hbm, v_hbm, kv_buf, sem):
    i = pl.program_id(0)
    j = pl.program_id(1)
    last = pl.num_programs(0) - 1
    tm = h_ref.shape[0]
    heads = k_hbm.shape[1]

    def proj():
        return jnp.dot(h_ref[...], w_ref[...], preferred_element_type=F32)

    def head_copies(slot, dst_hbm, step):
        row0 = pl.multiple_of(step * tm, tm)
        return [pltpu.make_async_copy(kv_buf.at[slot, hh], dst_hbm.at[pl.ds(row0, tm), hh, :], sem.at[slot])
                for hh in range(heads)]

    @pl.when(j == 0)
    def _():
        u_ref[...] = proj()

    @pl.when(j == 1)
    def _():
        q_ref[...] = (proj() * (HEAD_DIM ** -0.5)).astype(BF16)

    for jj, slot, dense_ref, dst_hbm in ((2, 0, kb_ref, k_hbm), (3, 1, vb_ref, v_hbm)):
        @pl.when(j == jj)
        def _(slot=slot, dense_ref=dense_ref, dst_hbm=dst_hbm):
            @pl.when(i > 0)
            def _():
                for c in head_copies(slot, dst_hbm, i - 1):
                    c.wait()
            acc = proj()
            dense_ref[...] = acc.astype(BF16)
            for hh in range(heads):
                kv_buf[slot, hh] = acc[:, hh * HEAD_DIM:(hh + 1) * HEAD_DIM]
            for c in head_copies(slot, dst_hbm, i):
                c.start()

    @pl.when((i == last) & (j == 3))
    def _():
        for slot, dst_hbm in ((0, k_hbm), (1, v_hbm)):
            for c in head_copies(slot, dst_hbm, i):
                c.wait()


def _proj_gates_body(x_ref, g_ref, w_ref, h_ref, gp_ref, gs_ref, *, tn):
    j = pl.program_id(1)

    @pl.when(j == 0)
    def _():
        x = x_ref[...]
        r = lax.rsqrt(jnp.mean(x * x, axis=-1, keepdims=True) + EPS)
        h_ref[...] = ((x * r) * g_ref[...]).astype(BF16)

    for jj, ref in ((0, gp_ref), (2, gs_ref)):
        for half in range(2):
            @pl.when(j == jj + half)
            def _(ref=ref, half=half):
                a = jnp.dot(h_ref[...], w_ref[...], preferred_element_type=F32)
                ref[:, half * tn:(half + 1) * tn] = (0.5 * jnp.tanh(0.5 * a) + 0.5).astype(BF16)


def _inproj(x, g, w_bf, tm):
    m, d = x.shape
    tn = d // 2
    assert w_bf.shape == (d, 8 * tn) and m % tm == 0
    row = lambda i, j: (i, 0)
    h_spec = pl.BlockSpec((tm, d), row)
    h, gp, gs = pl.pallas_call(
        functools.partial(_proj_gates_body, tn=tn),
        out_shape=[jax.ShapeDtypeStruct((m, d), BF16)] * 3,
        grid=(m // tm, 4),
        in_specs=[h_spec, pl.BlockSpec((1, d), lambda i, j: (0, 0)),
                  pl.BlockSpec((d, tn), lambda i, j: (0, j + 4))],
        out_specs=[h_spec] * 3,
        compiler_params=_cparams(("parallel", "arbitrary")),
        name="inproj_gates",
    )(x, g, w_bf)
    dense = lambda dt: jax.ShapeDtypeStruct((m, tn), dt)
    cache = jax.ShapeDtypeStruct((m, tn // HEAD_DIM, HEAD_DIM), F32)
    any_spec = pl.BlockSpec(memory_space=pl.ANY)
    u, q, kb, vb, k, v = pl.pallas_call(
        _proj_main_body,
        out_shape=[dense(F32), dense(BF16), dense(BF16), dense(BF16), cache, cache],
        grid=(m // tm, 4),
        in_specs=[h_spec, pl.BlockSpec((d, tn), lambda i, j: (0, j))],
        out_specs=[pl.BlockSpec((tm, tn), row)] * 4 + [any_spec] * 2,
        scratch_shapes=[pltpu.VMEM((2, tn // HEAD_DIM, tm, HEAD_DIM), F32), pltpu.SemaphoreType.DMA((2,))],
        compiler_params=_cparams(("arbitrary", "arbitrary"), has_side_effects=True),
        name="inproj_main",
    )(h, w_bf)
    return u, q, kb, vb, k, v, gp, gs


def _pool_body(hist_ref, u_ref, o_ref, ext_scr, *, pos0, tn):
    s = pl.program_id(1)

    @pl.when(s == 0)
    def _():
        ext_scr[0:POOL_HIST, :] = hist_ref[0]

    ext_scr[POOL_HIST:POOL_HIST + tn, :] = u_ref[0]
    pos = pos0 + s * tn + lax.broadcasted_iota(jnp.int32, (tn, 1), 0)
    group = u_ref.shape[2] // len(POOL_WINDOWS)
    for g, w in enumerate(POOL_WINDOWS):
        lo, hi = g * group, (g + 1) * group
        cur = ext_scr[POOL_HIST:POOL_HIST + tn, lo:hi]
        tot = cur
        for dlt in range(1, w):
            tot = tot + ext_scr[POOL_HIST - dlt:POOL_HIST - dlt + tn, lo:hi]
        cnt = jnp.minimum(pos + 1, w).astype(F32)
        o_ref[0, :, lo:hi] = (tot / cnt - cur).astype(BF16)
    ext_scr[0:POOL_HIST, :] = ext_scr[tn:tn + POOL_HIST, :]


def _pool_diff(hist, u, pos0, tn):
    b, n, dp = u.shape
    assert n % tn == 0 and hist.shape == (b, POOL_HIST, dp)
    return pl.pallas_call(
        functools.partial(_pool_body, pos0=pos0, tn=tn),
        out_shape=jax.ShapeDtypeStruct((b, n, dp), BF16),
        grid=(b, n // tn),
        in_specs=[pl.BlockSpec((1, POOL_HIST, dp), lambda i, s: (i, 0, 0)),
                  pl.BlockSpec((1, tn, dp), lambda i, s: (i, s, 0))],
        out_specs=pl.BlockSpec((1, tn, dp), lambda i, s: (i, s, 0)),
        scratch_shapes=[pltpu.VMEM((POOL_HIST + tn, dp), F32)],
        compiler_params=_cparams(("parallel", "arbitrary")),
        name="pool_diff",
    )(hist, u)


def _softplus2(z2):
    neg_abs = pltpu.bitcast(pltpu.bitcast(z2, jnp.uint32) | jnp.uint32(0x80000000), F32)
    return jnp.maximum(z2, 0.0) + jnp.log2(1.0 + jnp.exp2(neg_abs))


def _suffix_matrix(n):
    r = lax.broadcasted_iota(jnp.int32, (n, n), 0)
    c = lax.broadcasted_iota(jnp.int32, (n, n), 1)
    return jnp.where(r >= c, 1.0, 0.0).astype(BF16)


def _stack_heads(q2):
    lane = lax.broadcasted_iota(jnp.int32, q2.shape, 1)
    zero = jnp.zeros_like(q2)
    return jnp.concatenate([jnp.where(lane < HEAD_DIM, q2, zero), jnp.where(lane >= HEAD_DIM, q2, zero)], axis=0)


def _unstack_heads(acc, t):
    lane = lax.broadcasted_iota(jnp.int32, (t, LANES), 1)
    return jnp.where(lane < HEAD_DIM, acc[:t], acc[t:])


def _causal_mask(t):
    r = lax.broadcasted_iota(jnp.int32, (t, t), 0)
    c = lax.broadcasted_iota(jnp.int32, (t, t), 1)
    m = c < r
    return jnp.concatenate([m, m], axis=0)


def _sb_group(q_st, k_blocks, v_blocks, suffix, carry, masks, transposed_keys):
    dn = (((1,), (0,)), ((), ())) if transposed_keys else (((1,), (1,)), ((), ()))
    zs = [lax.dot_general(q_st, kb, dn, preferred_element_type=F32) * LOG2E for kb in k_blocks]
    cs = _sb_masses(zs, masks, [suffix] * len(zs))
    out = None
    for z, c, vb, mask in zip(zs, cs, v_blocks, masks):
        a, carry = _sb_weights(z, c, carry, mask)
        o = jnp.dot(a, vb, preferred_element_type=F32)
        out = o if out is None else out + o
    return out, carry


def _sb_masses(zs, masks, suffixes):
    splits = []
    for z, mask in zip(zs, masks):
        sp = _softplus2(z)
        if mask is not None:
            sp = jnp.where(mask, sp, 0.0)
        hi = pltpu.bitcast(pltpu.bitcast(sp, jnp.uint32) & jnp.uint32(0xFFFF0000), F32)
        splits.append((hi.astype(BF16), (sp - hi).astype(BF16)))
    return [jnp.dot(hi, sfx, preferred_element_type=F32) + jnp.dot(lo, sfx, preferred_element_type=F32)
            for (hi, lo), sfx in zip(splits, suffixes)]


def _sb_weights(z, c, carry, mask):
    arg = z - c - carry
    if mask is not None:
        arg = jnp.where(mask, arg, NEG)
    return jnp.exp2(arg).astype(BF16), carry + c[:, 0:1]


def _attn_self_body(q_ref, k_ref, v_ref, o_ref, kt_scr, v_scr, acc_scr, car_scr, min_scr, *, t):
    qi = pl.program_id(2)

    @pl.when(qi == 0)
    def _fill():
        def body(c, _):
            row0 = pl.multiple_of(c * t, t)
            kt_scr[c] = k_ref[0, pl.ds(row0, t), :].astype(F32).T.astype(BF16)
            v_scr[c] = v_ref[0, pl.ds(row0, t), :].astype(BF16)
            return 0
        lax.fori_loop(0, kt_scr.shape[0], body, 0)

    q_st = _stack_heads(q_ref[0])
    suffix = _suffix_matrix(t)
    mask = _causal_mask(t)

    def run(blocks, masks, first):
        carry = jnp.zeros((2 * t, 1), F32) if first else car_scr[:, 0:1]
        out, carry = _sb_group(q_st, [kt_scr[b] for b in blocks], [v_scr[b] for b in blocks], suffix, carry,
                               masks, True)
        acc_scr[...] = out if first else acc_scr[...] + out
        car_scr[...] = jnp.broadcast_to(carry, car_scr.shape)
        min_scr[0] = jnp.min(carry)

    @pl.when(qi == 0)
    def _():
        run([0], [mask], True)

    @pl.when(qi > 0)
    def _():
        run([qi, qi - 1], [mask, None], True)

    rest = jnp.maximum(qi - 1, 0)

    def more(it):
        return (it < rest // 2) & (min_scr[0] < SKIP_MASS)

    def pair(it):
        b0 = qi - 2 - 2 * it
        run([b0, b0 - 1], [None, None], False)
        return it + 1

    lax.while_loop(more, pair, 0)

    @pl.when((rest % 2 == 1) & (min_scr[0] < SKIP_MASS))
    def _():
        run([0], [None], False)

    o_ref[0] = _unstack_heads(acc_scr[...], t).astype(o_ref.dtype)


def _attn_hist_body(q_ref, kn_ref, vn_ref, kh_ref, vh_ref, o_ref, left_ref, acc_scr, car_scr, min_scr, *, tk,
                    group):
    t = q_ref.shape[1]
    past = kh_ref.shape[1]
    q_st = _stack_heads(q_ref[0])

    def keep(out, carry, first):
        acc_scr[...] = out if first else acc_scr[...] + out
        car_scr[...] = jnp.broadcast_to(carry, car_scr.shape)
        min_scr[0] = jnp.min(carry)

    keep(*_sb_group(q_st, [kn_ref[0].astype(BF16)], [vn_ref[0].astype(BF16)], _suffix_matrix(t),
                    jnp.zeros((2 * t, 1), F32), [_causal_mask(t)], False), True)
    suffix = _suffix_matrix(tk)
    for top in range(past // tk, 0, -group):
        @pl.when(min_scr[0] < SKIP_MASS)
        def _(top=top):
            blocks = range(top - 1, top - 1 - group, -1)
            keep(*_sb_group(q_st, [kh_ref[0, b * tk:(b + 1) * tk, :].astype(BF16) for b in blocks],
                            [vh_ref[0, b * tk:(b + 1) * tk, :].astype(BF16) for b in blocks],
                            suffix, car_scr[:, 0:1], [None] * group, False), False)
    o_ref[0] = _unstack_heads(acc_scr[...], t).astype(o_ref.dtype)
    left_ref[0, 0] = jnp.full(left_ref.shape[2:], min_scr[0], F32)


def _attn_recent_body(q_ref, kn_ref, vn_ref, kc_ref, vc_ref, o_ref, left_ref, *, heads, tk):
    t = q_ref.shape[1]
    n_blk = kc_ref.shape[1] // (tk * heads)
    q, kn, vn = q_ref[0], kn_ref[0], vn_ref[0]
    head = lambda a, h: a[:, h * HEAD_DIM:(h + 1) * HEAD_DIM]
    cached = lambda ref, blk, h: ref[0, pl.ds(blk * tk * heads + h, tk, stride=heads), :].astype(BF16)
    score = lambda qh, kh: lax.dot_general(qh, kh, (((1,), (1,)), ((), ())), preferred_element_type=F32)
    blocks = list(range(n_blk - 1, -1, -1))
    mask, sfx_new, sfx_old = _causal_mask(t), _suffix_matrix(t), _suffix_matrix(tk)
    zs, masks, sfx = [], [], []
    for p in range(heads // 2):
        pair = (2 * p, 2 * p + 1)
        zs.append(jnp.concatenate([score(head(q, h), head(kn, h)) for h in pair], axis=0) * LOG2E)
        masks.append(mask)
        sfx.append(sfx_new)
        for j in blocks:
            zs.append(jnp.concatenate([score(head(q, h), cached(kc_ref, j, h)) for h in pair], axis=0) * LOG2E)
            masks.append(None)
            sfx.append(sfx_old)
    cs = _sb_masses(zs, masks, sfx)
    outs, left = [], None
    chain = 1 + n_blk
    for p in range(heads // 2):
        carry = jnp.zeros((2 * t, 1), F32)
        acc = [None, None]
        for i in range(chain):
            a, carry = _sb_weights(zs[p * chain + i], cs[p * chain + i], carry, masks[p * chain + i])
            for r in range(2):
                h = 2 * p + r
                vb = head(vn, h) if i == 0 else cached(vc_ref, blocks[i - 1], h)
                o = jnp.dot(a[r * t:(r + 1) * t], vb, preferred_element_type=F32)
                acc[r] = o if acc[r] is None else acc[r] + o
        outs += acc
        low = jnp.min(carry)
        left = low if left is None else jnp.minimum(left, low)
    o_ref[0] = jnp.concatenate(outs, axis=1).astype(o_ref.dtype)
    left_ref[0] = jnp.full(left_ref.shape[1:], left, F32)


def _attn_specs(t):
    tile = pl.BlockSpec((1, t, LANES), lambda i, p, s: (i, s, p))
    seq = lambda rows: pl.BlockSpec((1, rows, LANES), lambda i, p, s: (i, 0, p))
    state = [pltpu.VMEM((2 * t, LANES), F32), pltpu.VMEM((2 * t, LANES), F32), pltpu.SMEM((1,), F32)]
    return tile, seq, state


def _attention_self(q, k, v, t):
    b, n, dm = q.shape
    assert n % t == 0 and dm % LANES == 0
    tile, seq, state = _attn_specs(t)
    return pl.pallas_call(
        functools.partial(_attn_self_body, t=t),
        out_shape=jax.ShapeDtypeStruct((b, n, dm), BF16),
        grid=(b, dm // LANES, n // t),
        in_specs=[tile, seq(n), seq(n)],
        out_specs=tile,
        scratch_shapes=[pltpu.VMEM((n // t, LANES, t), BF16), pltpu.VMEM((n // t, t, LANES), BF16)] + state,
        compiler_params=_cparams(("parallel", "parallel", "arbitrary")),
        name="sb_attention",
    )(q, k, v)


def _attention_cached(q, k_new, v_new, k_hist, v_hist, tk, group):
    b, t, dm = q.shape
    past = k_hist.shape[1]
    assert dm % LANES == 0 and past % (tk * group) == 0
    tile, seq, state = _attn_specs(t)
    return pl.pallas_call(
        functools.partial(_attn_hist_body, tk=tk, group=group),
        out_shape=[jax.ShapeDtypeStruct((b, t, dm), BF16), jax.ShapeDtypeStruct((b, dm // LANES, 8, LANES), F32)],
        grid=(b, dm // LANES, 1),
        in_specs=[tile, seq(t), seq(t), seq(past), seq(past)],
        out_specs=[tile, pl.BlockSpec((1, 1, 8, LANES), lambda i, p, s: (i, p, 0, 0))],
        scratch_shapes=state,
        compiler_params=_cparams(("parallel", "parallel", "arbitrary")),
        name="sb_attention_cached",
    )(q, k_new, v_new, k_hist, v_hist)


def _attention_recent_first(q, k_new, v_new, cache_k, cache_v, tk, group):
    b, past, heads, hd = cache_k.shape
    t, dm = q.shape[1:]
    flat = lambda c: c.reshape(b, past, heads * hd)
    walk = lambda: _attention_cached(q, k_new, v_new, flat(cache_k), flat(cache_v), tk, group)[0]
    recent = tk * group
    if past % recent:
        return walk()
    rows = lambda c: c.reshape(b, past * heads, hd)
    seq = pl.BlockSpec((1, t, dm), lambda i: (i, 0, 0))
    newest = pl.BlockSpec((1, recent * heads, hd), lambda i: (i, past // recent - 1, 0))
    o, left = pl.pallas_call(
        functools.partial(_attn_recent_body, heads=heads, tk=tk),
        out_shape=[jax.ShapeDtypeStruct((b, t, dm), BF16), jax.ShapeDtypeStruct((b, 8, LANES), F32)],
        grid=(b,),
        in_specs=[seq, seq, seq, newest, newest],
        out_specs=[seq, pl.BlockSpec((1, 8, LANES), lambda i: (i, 0, 0))],
        compiler_params=_cparams(("parallel",)),
        name="sb_attention_recent",
    )(q, k_new, v_new, rows(cache_k), rows(cache_v))
    if past == recent:
        return o
    return lax.cond(jnp.min(left) < SKIP_MASS, walk, lambda: o)


def _mix_body(diff_ref, o_ref, gp_ref, gs_ref, x_ref, wp_ref, ps_ref, wsb_ref, wo_ref, nf_ref,
              wr_ref, br_ref, x1_ref, hp_ref, route_ref, cnt_ref, logit_scr, *, tm):
    @pl.when(pl.program_id(0) == 0)
    def _():
        logit_scr[...] = jnp.zeros_like(logit_scr)

    late_logits = logit_scr[...]

    n_pool = wp_ref.shape[0]
    group = wp_ref.shape[1]
    rows = tm // MIX_PARTS
    parts = [pl.ds(i * rows, rows) for i in range(MIX_PARTS)]
    pools = [jnp.concatenate(
        [jnp.dot(diff_ref[s, g * group:(g + 1) * group], wp_ref[g], preferred_element_type=F32)
         for g in range(n_pool)], axis=-1) * ps_ref[...] for s in parts]
    sbs = [jnp.dot(o_ref[s, :], wsb_ref[...], preferred_element_type=F32) for s in parts]
    mixed = [(gp_ref[s, :].astype(F32) * pool + gs_ref[s, :].astype(F32) * sb).astype(BF16)
             for s, pool, sb in zip(parts, pools, sbs)]
    x1s = [x_ref[s, :] + jnp.dot(mx, wo_ref[...], preferred_element_type=F32) for s, mx in zip(parts, mixed)]
    splits = []
    for s, x1 in zip(parts, x1s):
        x1_ref[s, :] = x1
        h = (x1 * lax.rsqrt(jnp.mean(x1 * x1, axis=-1, keepdims=True) + EPS)) * nf_ref[...]
        d_half = h.shape[1] // 2
        lo_bits = pltpu.bitcast(h[:, :d_half].astype(BF16).astype(F32), jnp.uint32)
        hi_bits = pltpu.bitcast(h[:, d_half:].astype(BF16).astype(F32), jnp.uint32)
        hp_ref[s, :] = (lo_bits >> 16) | (hi_bits & jnp.uint32(0xFFFF0000))
        hh = h.astype(BF16)
        splits.append(jnp.concatenate([hh, (h - hh.astype(F32)).astype(BF16)], axis=0))

    rs = [jnp.dot(sp, wr_ref[...], preferred_element_type=F32) for sp in splits]
    logit_scr[...] = jnp.concatenate(
        [(r[:rows, :LANES] + r[:rows, LANES:]) + (r[rows:, :LANES] + r[rows:, LANES:]) for r in rs],
        axis=0) + br_ref[...]
    _route_tile(late_logits, route_ref, cnt_ref, tm)


def _route_tile(logits, route_ref, cnt_ref, tm):
    lane = lax.broadcasted_iota(jnp.int32, (tm, LANES), 1)
    big = jnp.int32(LANES)

    def first_max(vals):
        m = jnp.max(vals, axis=-1, keepdims=True)
        idx = jnp.min(jnp.where(vals == m, lane, big), axis=-1, keepdims=True)
        return m, idx

    gl = jnp.where(lane < N_GROUPS, logits, NEG)
    gmax, grp = first_max(gl)
    p_grp = 1.0 / jnp.sum(jnp.exp(gl - gmax), axis=-1, keepdims=True)
    e_lo = N_GROUPS + grp * PER_GROUP
    el = jnp.where((lane >= e_lo) & (lane < e_lo + PER_GROUP), logits, NEG)
    m1, i1 = first_max(el)
    m2, i2 = first_max(jnp.where(lane == i1, NEG, el))
    t2 = jnp.exp(m2 - m1)
    w1 = p_grp / (1.0 + t2)
    w2 = w1 * t2
    e1 = i1 - N_GROUPS
    e2 = i2 - N_GROUPS

    oh1 = jnp.where(lane == e1, 1.0, 0.0).astype(BF16)
    oh2 = jnp.where(lane == e2, 1.0, 0.0).astype(BF16)
    rr = lax.broadcasted_iota(jnp.int32, (tm, tm), 0)
    cc = lax.broadcasted_iota(jnp.int32, (tm, tm), 1)
    before = jnp.where(cc < rr, 1.0, 0.0).astype(BF16)
    ones = jnp.ones((8, tm), BF16)
    pre1 = jnp.dot(before, oh1, preferred_element_type=F32)
    pre2 = jnp.dot(before, oh2, preferred_element_type=F32)
    c1 = jnp.dot(ones, oh1, preferred_element_type=F32)
    c2 = jnp.dot(ones, oh2, preferred_element_type=F32)
    rank1 = jnp.sum(jnp.where(lane == e1, pre1, 0.0), axis=-1, keepdims=True)
    rank2 = jnp.sum(jnp.where(lane == e2, pre2 + c1[0:1, :], 0.0), axis=-1, keepdims=True)
    cnt_ref[0] = c1 + c2

    route = jnp.where(lane == 0, e1.astype(F32), 0.0)
    route = jnp.where(lane == 1, e2.astype(F32), route)
    route = jnp.where(lane == 2, w1, route)
    route = jnp.where(lane == 3, w2, route)
    route = jnp.where(lane == 4, rank1, route)
    route = jnp.where(lane == 5, rank2, route)
    route_ref[...] = route


def _mix(diff, o, gp, gs, x, wp, ps, wsb, wo, nf, wr, br, tm):
    m, d = x.shape
    assert m % tm == 0
    n = m // tm
    row = lambda i: (jnp.minimum(i, n - 1), 0)
    late = lambda i: (jnp.maximum(i - 1, 0), 0)
    full = lambda a: pl.BlockSpec(a.shape, lambda i: (0,) * a.ndim, pipeline_mode=pl.Buffered(1))
    acts = [diff, o, gp, gs, x]
    consts = [wp, ps, wsb, wo, nf, wr, br]
    return pl.pallas_call(
        functools.partial(_mix_body, tm=tm),
        out_shape=[jax.ShapeDtypeStruct((m, d), F32), jax.ShapeDtypeStruct((m, d // 2), jnp.uint32),
                   jax.ShapeDtypeStruct((m, LANES), F32), jax.ShapeDtypeStruct((n, 8, LANES), F32)],
        grid=(n + 1,),
        in_specs=[pl.BlockSpec((tm, a.shape[1]), row) for a in acts] + [full(a) for a in consts],
        out_specs=[pl.BlockSpec((tm, d), row), pl.BlockSpec((tm, d // 2), row),
                   pl.BlockSpec((tm, LANES), late), pl.BlockSpec((1, 8, LANES), lambda i: late(i) + (0,))],
        scratch_shapes=[pltpu.VMEM((tm, LANES), F32)],
        compiler_params=_cparams(("arbitrary",)),
        name="mix_outproj_router",
    )(*acts, *consts)


def _row_wait(src_ref, dst_ref, sem, n_rows):
    pltpu.make_async_copy(src_ref.at[pl.ds(0, n_rows)], dst_ref.at[pl.ds(0, n_rows)], sem).wait()


def _dispatch_body(slot_ref, zero_ref, hp_ref, xs_ref, buf, zbuf, sem, zsem, *, tm):
    i = pl.program_id(0)
    last = pl.num_programs(0) - 1
    par = i % 2

    @pl.when(i == 0)
    def _():
        zbuf[...] = jnp.zeros_like(zbuf)
        bm = zbuf.shape[0]
        for wait in (False, True):
            for z in range(zero_ref.shape[0]):
                @pl.when(zero_ref[z] >= 0)
                def _(z=z, wait=wait):
                    start = pl.multiple_of(jnp.maximum(zero_ref[z], 0), bm)
                    copy = pltpu.make_async_copy(zbuf, xs_ref.at[pl.ds(start, bm)], zsem)
                    copy.wait() if wait else copy.start()

    def drain(slot):
        for _ in range(2):
            _row_wait(buf.at[slot], xs_ref, sem.at[slot], tm)

    @pl.when(i >= 2)
    def _():
        drain(par)

    buf[par] = hp_ref[...]
    base = i * tm

    def body(t, _):
        src = buf.at[par, pl.ds(t, 1)]
        for kk in range(2):
            dst = xs_ref.at[pl.ds(slot_ref[2 * (base + t) + kk], 1)]
            pltpu.make_async_copy(src, dst, sem.at[par]).start()
        return 0

    lax.fori_loop(0, tm, body, 0, unroll=8)

    @pl.when(i == last)
    def _():
        drain(par)

        @pl.when(i >= 1)
        def _():
            drain(1 - par)


def _dispatch(slots_flat, zero_starts, hp, n_slots, tm, bm):
    m, dh = hp.shape
    assert m % tm == 0
    return pl.pallas_call(
        functools.partial(_dispatch_body, tm=tm),
        out_shape=jax.ShapeDtypeStruct((n_slots, dh), hp.dtype),
        grid_spec=pltpu.PrefetchScalarGridSpec(
            num_scalar_prefetch=2, grid=(m // tm,),
            in_specs=[pl.BlockSpec((tm, dh), lambda i, s, z: (i, 0))],
            out_specs=pl.BlockSpec(memory_space=pl.ANY),
            scratch_shapes=[pltpu.VMEM((2, tm, dh), hp.dtype), pltpu.VMEM((bm, dh), hp.dtype),
                            pltpu.SemaphoreType.DMA((2,)), pltpu.SemaphoreType.DMA]),
        compiler_params=_cparams(("arbitrary",), disable_bounds_checks=True, has_side_effects=True),
        name="moe_dispatch",
    )(slots_flat, zero_starts, hp)


def _ffn_body(be_ref, nv_ref, xs_ref, wg_ref, wu_ref, wd_ref, y_ref):
    del be_ref

    @pl.when(pl.program_id(0) >= nv_ref[0])
    def _():
        y_ref[...] = jnp.zeros_like(y_ref)

    @pl.when(pl.program_id(0) < nv_ref[0])
    def _():
        d_half = xs_ref.shape[1]
        rows = xs_ref.shape[0] // FFN_PARTS
        hids = []
        for i in range(FFN_PARTS):
            words = xs_ref[pl.ds(i * rows, rows), :]
            x_lo = pltpu.bitcast(words << 16, F32).astype(BF16)
            x_hi = pltpu.bitcast(words & jnp.uint32(0xFFFF0000), F32).astype(BF16)

            def proj(w_ref):
                return (jnp.dot(x_lo, w_ref[0, :d_half, :], preferred_element_type=F32)
                        + jnp.dot(x_hi, w_ref[0, d_half:, :], preferred_element_type=F32))

            hids.append((jax.nn.silu(proj(wg_ref)) * proj(wu_ref)).astype(BF16))
        for i, hid in enumerate(hids):
            y_ref[pl.ds(i * rows, rows), :] = jnp.dot(hid, wd_ref[0], preferred_element_type=F32)


def _ffn(block_expert, n_valid, xs, wg, wu, wd, bm):
    n_slots, d_half = xs.shape
    d = 2 * d_half
    de = wg.shape[2]
    live = lambda b, be, nv: (jnp.minimum(b, nv[0] - 1), 0)
    wsel = lambda b, be, nv: (be[b], 0, 0)
    return pl.pallas_call(
        _ffn_body,
        out_shape=jax.ShapeDtypeStruct((n_slots, d), F32),
        grid_spec=pltpu.PrefetchScalarGridSpec(
            num_scalar_prefetch=2, grid=(n_slots // bm,),
            in_specs=[pl.BlockSpec((bm, d_half), live), pl.BlockSpec((1, d, de), wsel),
                      pl.BlockSpec((1, d, de), wsel), pl.BlockSpec((1, de, d), wsel)],
            out_specs=pl.BlockSpec((bm, d), lambda b, be, nv: (b, 0))),
        compiler_params=_cparams(("arbitrary",)),
        name="moe_ffn",
    )(block_expert, n_valid, xs, wg, wu, wd)


def _final_body(slot_ref, x1_ref, route_ref, g_ref, y_hbm, o_ref, buf, sem, *, tm):
    i = pl.program_id(0)
    n_steps = pl.num_programs(0)

    def issue(step, par):
        def body(t, _):
            for kk in range(2):
                src = y_hbm.at[pl.ds(slot_ref[2 * (step * tm + t) + kk], 1)]
                pltpu.make_async_copy(src, buf.at[par, kk, pl.ds(t, 1)], sem.at[par]).start()
            return 0
        lax.fori_loop(0, tm, body, 0, unroll=8)

    @pl.when(i == 0)
    def _():
        issue(0, 0)

    @pl.when(i + 1 < n_steps)
    def _():
        issue(i + 1, (i + 1) % 2)

    par = i % 2
    for kk in range(2):
        _row_wait(y_hbm, buf.at[par, kk], sem.at[par], tm)
    route = route_ref[...]
    x2 = x1_ref[...] + route[:, 2:3] * buf[par, 0] + route[:, 3:4] * buf[par, 1]
    o_ref[...] = (x2 * lax.rsqrt(jnp.mean(x2 * x2, axis=-1, keepdims=True) + EPS)) * g_ref[...]


def _final(slots_flat, x1, route, g, y, tm):
    m, d = x1.shape
    assert m % tm == 0
    row = lambda i, s: (i, 0)
    return pl.pallas_call(
        functools.partial(_final_body, tm=tm),
        out_shape=jax.ShapeDtypeStruct((m, d), F32),
        grid_spec=pltpu.PrefetchScalarGridSpec(
            num_scalar_prefetch=1, grid=(m // tm,),
            in_specs=[pl.BlockSpec((tm, d), row), pl.BlockSpec((tm, LANES), row),
                      pl.BlockSpec((1, d), lambda i, s: (0, 0)), pl.BlockSpec(memory_space=pl.ANY)],
            out_specs=pl.BlockSpec((tm, d), row),
            scratch_shapes=[pltpu.VMEM((2, 2, tm, d), F32), pltpu.SemaphoreType.DMA((2,))]),
        compiler_params=_cparams(("arbitrary",), disable_bounds_checks=True),
        name="moe_combine_final_norm",
    )(slots_flat, x1, route, g, y)


def _routing_tables(route, cnt, tm, bm):
    m = route.shape[0]
    n_blocks = (2 * m) // bm + N_EXPERTS
    counts = cnt[:, 0, :N_EXPERTS].astype(jnp.int32)
    sizes = jnp.sum(counts, axis=0)
    padded = (sizes + bm - 1) // bm * bm
    pad_end = jnp.cumsum(padded)
    base = (pad_end - padded)[None, :] + jnp.cumsum(counts, axis=0) - counts
    base_tok = jnp.repeat(base, tm, axis=0)
    e = route[:, 0:2].astype(jnp.int32)
    rank = route[:, 4:6].astype(jnp.int32)
    sel = e[:, :, None] == jnp.arange(N_EXPERTS, dtype=jnp.int32)[None, None, :]
    slots = jnp.sum(jnp.where(sel, base_tok[:, None, :], 0), axis=-1) + rank
    n_valid = (pad_end[-1] // bm).astype(jnp.int32)
    blk = jnp.minimum(jnp.arange(n_blocks, dtype=jnp.int32), n_valid - 1)
    block_expert = jnp.minimum(jnp.sum(pad_end[None, :] <= (blk * bm)[:, None], axis=1), N_EXPERTS - 1)
    last_blk = jnp.where(padded > 0, pad_end - bm, -1)
    tail = n_valid + jnp.arange(N_EXPERTS, dtype=jnp.int32)
    tail = jnp.where(tail < n_blocks, tail * bm, -1)
    zero_starts = jnp.concatenate([last_blk, tail]).astype(jnp.int32)
    return slots.reshape(-1), block_expert.astype(jnp.int32), n_valid.reshape(1), zero_starts, n_blocks * bm


def _stream(x, pool_hist, k_hist, v_hist, p, *, tq, tk, group, tn, tm_in, tm_mix, tm_fin, bm):
    b, n, d = x.shape
    m = b * n
    past = 0 if k_hist is None else k_hist.shape[1]
    x2d = x.reshape(m, d)
    u, q, kb, vb, k, v, gp, gs = _inproj(x2d, p['norm_mix'], p['w_in'], _fit(m, tm_in))
    dp = u.shape[1]
    diff = _pool_diff(pool_hist, u.reshape(b, n, dp), past, tn).reshape(m, dp)
    shp = (b, n, kb.shape[1])
    if k_hist is None:
        o = _attention_self(q.reshape(shp), kb.reshape(shp), vb.reshape(shp), tq)
    else:
        o = _attention_recent_first(q.reshape(shp), kb.reshape(shp), vb.reshape(shp), k_hist, v_hist, tk, group)
    o = o.reshape(m, -1)
    x1, hp, route, cnt = _mix(diff, o, gp, gs, x2d, p['w_pool'], p['pool_scale'], p['w_sb_out'], p['w_out'],
                              p['norm_ffn'], p['w_r'], p['b_r'], tm_mix)
    slots, block_expert, n_valid, zero_starts, n_slots = _routing_tables(route, cnt, tm_mix, bm)
    xs = _dispatch(slots, zero_starts, hp, n_slots, tm_mix, bm)
    y = _ffn(block_expert, n_valid, xs, p['w_g'], p['w_u'], p['w_d'], bm)
    out = _final(slots, x1, route, p['norm_final'], y, tm_fin)
    return out.reshape(b, n, d), u.reshape(b, n, dp), k, v


def kernel(x_prompt, x_sample, cache_sb_k, cache_sb_v, state_pool, norm_mix, w_in, w_pool, pool_scale, w_sb_out,
           w_out, norm_ffn, w_router_group, b_router_group, w_router_expert, b_router_expert, w_exp_gate,
           w_exp_up, w_exp_down, norm_final):
    depth = w_in.shape[0]
    assert depth == 1
    bp, sp, d = x_prompt.shape
    bs, ss, _ = x_sample.shape
    past = cache_sb_k.shape[2]
    heads, hd = cache_sb_k.shape[3], cache_sb_k.shape[4]
    assert hd == HEAD_DIM
    dp = state_pool.shape[3]
    n_state = state_pool.shape[2]

    w_r = jnp.concatenate([w_router_group[0], w_router_expert[0]], axis=1)
    w_r = jnp.pad(w_r, ((0, 0), (0, LANES - w_r.shape[1])))
    w_r_hi = w_r.astype(BF16)
    b_r = jnp.concatenate([b_router_group[0], b_router_expert[0]])
    p = dict(
        norm_mix=norm_mix[0][None, :], w_in=w_in[0].astype(BF16), w_pool=w_pool[0].astype(BF16),
        pool_scale=pool_scale[0][None, :], w_sb_out=w_sb_out[0].astype(BF16), w_out=w_out[0].astype(BF16),
        norm_ffn=norm_ffn[0][None, :],
        w_r=jnp.concatenate([w_r_hi, (w_r - w_r_hi.astype(F32)).astype(BF16)], axis=1),
        b_r=jnp.pad(b_r, (0, LANES - b_r.shape[0]))[None, :].astype(F32),
        w_g=w_exp_gate[0].astype(BF16), w_u=w_exp_up[0].astype(BF16), w_d=w_exp_down[0].astype(BF16),
        norm_final=norm_final[None, :])

    hist_p = jnp.zeros((bp, POOL_HIST, dp), F32)
    yp, up, kp, vp = _stream(x_prompt, hist_p, None, None, p, tq=256, tk=256, group=2, tn=min(sp, 512),
                             tm_in=1024, tm_mix=256, tm_fin=256, bm=256)
    hist_s = jnp.pad(state_pool[0], ((0, 0), (POOL_HIST - n_state, 0), (0, 0)))
    ys, us, ks, vs = _stream(x_sample, hist_s, cache_sb_k[0], cache_sb_v[0], p, tq=ss, tk=256, group=2, tn=ss,
                             tm_in=1024, tm_mix=256, tm_fin=256, bm=256)

    def pool_state(hist, u):
        return jnp.concatenate([hist[:, POOL_HIST - n_state:], u], axis=1)[:, -n_state:][None]

    return (yp, ys,
            kp.reshape(1, bp, sp, heads, hd), vp.reshape(1, bp, sp, heads, hd), pool_state(hist_p, up),
            ks.reshape(1, bs, ss, heads, hd), vs.reshape(1, bs, ss, heads, hd), pool_state(hist_s, us))
```

```python
import functools

import jax
import jax.numpy as jnp
from jax import lax
from jax.experimental import pallas as pl
from jax.experimental.pallas import tpu as pltpu

F32 = jnp.float32
BF16 = jnp.bfloat16

EPS = 1e-6
HEAD_DIM = 64
LANES = 128
POOL_WINDOWS = (2, 4, 8, 16)
POOL_HIST = 16
N_GROUPS = 4
PER_GROUP = 8
N_EXPERTS = N_GROUPS * PER_GROUP
VMEM_LIMIT = 58 * 1024 * 1024
NEG = -1e30
LOG2E = 1.4426950408889634
MIX_PARTS = 2
FFN_PARTS = 2
SKIP_MASS = 160.0


def _fit(m, tile):
    while m % tile:
        tile //= 2
    return tile


def _cparams(sem, **kw):
    return pltpu.CompilerParams(dimension_semantics=sem, vmem_limit_bytes=VMEM_LIMIT, **kw)


def _proj_main_body(h_ref, w_ref, u_ref, q_ref, kb_ref, vb_ref, k_hbm, v_hbm, kv_buf, sem):
    i = pl.program_id(0)
    j = pl.program_id(1)
    last = pl.num_programs(0) - 1
    tm = h_ref.shape[0]
    heads = k_hbm.shape[1]

    def proj():
        return jnp.dot(h_ref[...], w_ref[...], preferred_element_type=F32)

    def head_copies(slot, dst_hbm, step):
        row0 = pl.multiple_of(step * tm, tm)
        return [pltpu.make_async_copy(kv_buf.at[slot, hh], dst_hbm.at[pl.ds(row0, tm), hh, :], sem.at[slot])
                for hh in range(heads)]

    @pl.when(j == 0)
    def _():
        u_ref[...] = proj()

    @pl.when(j == 1)
    def _():
        q_ref[...] = (proj() * (HEAD_DIM ** -0.5)).astype(BF16)

    for jj, slot, dense_ref, dst_hbm in ((2, 0, kb_ref, k_hbm), (3, 1, vb_ref, v_hbm)):
        @pl.when(j == jj)
        def _(slot=slot, dense_ref=dense_ref, dst_hbm=dst_hbm):
            @pl.when(i > 0)
            def _():
                for c in head_copies(slot, dst_hbm, i - 1):
                    c.wait()
            acc = proj()
            dense_ref[...] = acc.astype(BF16)
            for hh in range(heads):
                kv_buf[slot, hh] = acc[:, hh * HEAD_DIM:(hh + 1) * HEAD_DIM]
            for c in head_copies(slot, dst_hbm, i):
                c.start()

    @pl.when((i == last) & (j == 3))
    def _():
        for slot, dst_hbm in ((0, k_hbm), (1, v_hbm)):
            for c in head_copies(slot, dst_hbm, i):
                c.wait()


def _proj_gates_body(x_ref, g_ref, w_ref, h_ref, gp_ref, gs_ref, *, tn):
    j = pl.program_id(1)

    @pl.when(j == 0)
    def _():
        x = x_ref[...]
        r = lax.rsqrt(jnp.mean(x * x, axis=-1, keepdims=True) + EPS)
        h_ref[...] = ((x * r) * g_ref[...]).astype(BF16)

    for jj, ref in ((0, gp_ref), (2, gs_ref)):
        for half in range(2):
            @pl.when(j == jj + half)
            def _(ref=ref, half=half):
                a = jnp.dot(h_ref[...], w_ref[...], preferred_element_type=F32)
                ref[:, half * tn:(half + 1) * tn] = (0.5 * jnp.tanh(0.5 * a) + 0.5).astype(BF16)


def _inproj(x, g, w_bf, tm):
    m, d = x.shape
    tn = d // 2
    assert w_bf.shape == (d, 8 * tn) and m % tm == 0
    row = lambda i, j: (i, 0)
    h_spec = pl.BlockSpec((tm, d), row)
    h, gp, gs = pl.pallas_call(
        functools.partial(_proj_gates_body, tn=tn),
        out_shape=[jax.ShapeDtypeStruct((m, d), BF16)] * 3,
        grid=(m // tm, 4),
        in_specs=[h_spec, pl.BlockSpec((1, d), lambda i, j: (0, 0)),
                  pl.BlockSpec((d, tn), lambda i, j: (0, j + 4))],
        out_specs=[h_spec] * 3,
        compiler_params=_cparams(("parallel", "arbitrary")),
        name="inproj_gates",
    )(x, g, w_bf)
    dense = lambda dt: jax.ShapeDtypeStruct((m, tn), dt)
    cache = jax.ShapeDtypeStruct((m, tn // HEAD_DIM, HEAD_DIM), F32)
    any_spec = pl.BlockSpec(memory_space=pl.ANY)
    u, q, kb, vb, k, v = pl.pallas_call(
        _proj_main_body,
        out_shape=[dense(F32), dense(BF16), dense(BF16), dense(BF16), cache, cache],
        grid=(m // tm, 4),
        in_specs=[h_spec, pl.BlockSpec((d, tn), lambda i, j: (0, j))],
        out_specs=[pl.BlockSpec((tm, tn), row)] * 4 + [any_spec] * 2,
        scratch_shapes=[pltpu.VMEM((2, tn // HEAD_DIM, tm, HEAD_DIM), F32), pltpu.SemaphoreType.DMA((2,))],
        compiler_params=_cparams(("arbitrary", "arbitrary"), has_side_effects=True),
        name="inproj_main",
    )(h, w_bf)
    return u, q, kb, vb, k, v, gp, gs


def _pool_body(hist_ref, u_ref, o_ref, ext_scr, *, pos0, tn):
    s = pl.program_id(1)

    @pl.when(s == 0)
    def _():
        ext_scr[0:POOL_HIST, :] = hist_ref[0]

    ext_scr[POOL_HIST:POOL_HIST + tn, :] = u_ref[0]
    pos = pos0 + s * tn + lax.broadcasted_iota(jnp.int32, (tn, 1), 0)
    group = u_ref.shape[2] // len(POOL_WINDOWS)
    for g, w in enumerate(POOL_WINDOWS):
        lo, hi = g * group, (g + 1) * group
        cur = ext_scr[POOL_HIST:POOL_HIST + tn, lo:hi]
        tot = cur
        for dlt in range(1, w):
            tot = tot + ext_scr[POOL_HIST - dlt:POOL_HIST - dlt + tn, lo:hi]
        cnt = jnp.minimum(pos + 1, w).astype(F32)
        o_ref[0, :, lo:hi] = (tot / cnt - cur).astype(BF16)
    ext_scr[0:POOL_HIST, :] = ext_scr[tn:tn + POOL_HIST, :]


def _pool_diff(hist, u, pos0, tn):
    b, n, dp = u.shape
    assert n % tn == 0 and hist.shape == (b, POOL_HIST, dp)
    return pl.pallas_call(
        functools.partial(_pool_body, pos0=pos0, tn=tn),
        out_shape=jax.ShapeDtypeStruct((b, n, dp), BF16),
        grid=(b, n // tn),
        in_specs=[pl.BlockSpec((1, POOL_HIST, dp), lambda i, s: (i, 0, 0)),
                  pl.BlockSpec((1, tn, dp), lambda i, s: (i, s, 0))],
        out_specs=pl.BlockSpec((1, tn, dp), lambda i, s: (i, s, 0)),
        scratch_shapes=[pltpu.VMEM((POOL_HIST + tn, dp), F32)],
        compiler_params=_cparams(("parallel", "arbitrary")),
        name="pool_diff",
    )(hist, u)


def _softplus2(z2):
    neg_abs = pltpu.bitcast(pltpu.bitcast(z2, jnp.uint32) | jnp.uint32(0x80000000), F32)
    return jnp.maximum(z2, 0.0) + jnp.log2(1.0 + jnp.exp2(neg_abs))


def _suffix_matrix(n):
    r = lax.broadcasted_iota(jnp.int32, (n, n), 0)
    c = lax.broadcasted_iota(jnp.int32, (n, n), 1)
    return jnp.where(r >= c, 1.0, 0.0).astype(BF16)


def _stack_heads(q2):
    lane = lax.broadcasted_iota(jnp.int32, q2.shape, 1)
    zero = jnp.zeros_like(q2)
    return jnp.concatenate([jnp.where(lane < HEAD_DIM, q2, zero), jnp.where(lane >= HEAD_DIM, q2, zero)], axis=0)


def _unstack_heads(acc, t):
    lane = lax.broadcasted_iota(jnp.int32, (t, LANES), 1)
    return jnp.where(lane < HEAD_DIM, acc[:t], acc[t:])


def _causal_mask(t):
    r = lax.broadcasted_iota(jnp.int32, (t, t), 0)
    c = lax.broadcasted_iota(jnp.int32, (t, t), 1)
    m = c < r
    return jnp.concatenate([m, m], axis=0)


def _sb_group(q_st, k_blocks, v_blocks, suffix, carry, masks, transposed_keys):
    dn = (((1,), (0,)), ((), ())) if transposed_keys else (((1,), (1,)), ((), ()))
    zs = [lax.dot_general(q_st, kb, dn, preferred_element_type=F32) * LOG2E for kb in k_blocks]
    cs = _sb_masses(zs, masks, [suffix] * len(zs))
    out = None
    for z, c, vb, mask in zip(zs, cs, v_blocks, masks):
        a, carry = _sb_weights(z, c, carry, mask)
        o = jnp.dot(a, vb, preferred_element_type=F32)
        out = o if out is None else out + o
    return out, carry


def _sb_masses(zs, masks, suffixes):
    splits = []
    for z, mask in zip(zs, masks):
        sp = _softplus2(z)
        if mask is not None:
            sp = jnp.where(mask, sp, 0.0)
        hi = pltpu.bitcast(pltpu.bitcast(sp, jnp.uint32) & jnp.uint32(0xFFFF0000), F32)
        splits.append((hi.astype(BF16), (sp - hi).astype(BF16)))
    return [jnp.dot(hi, sfx, preferred_element_type=F32) + jnp.dot(lo, sfx, preferred_element_type=F32)
            for (hi, lo), sfx in zip(splits, suffixes)]


def _sb_weights(z, c, carry, mask):
    arg = z - c - carry
    if mask is not None:
        arg = jnp.where(mask, arg, NEG)
    return jnp.exp2(arg).astype(BF16), carry + c[:, 0:1]


def _attn_self_body(q_ref, k_ref, v_ref, o_ref, kt_scr, v_scr, acc_scr, car_scr, min_scr, *, t):
    qi = pl.program_id(2)

    @pl.when(qi == 0)
    def _fill():
        def body(c, _):
            row0 = pl.multiple_of(c * t, t)
            kt_scr[c] = k_ref[0, pl.ds(row0, t), :].astype(F32).T.astype(BF16)
            v_scr[c] = v_ref[0, pl.ds(row0, t), :].astype(BF16)
            return 0
        lax.fori_loop(0, kt_scr.shape[0], body, 0)

    q_st = _stack_heads(q_ref[0])
    suffix = _suffix_matrix(t)
    mask = _causal_mask(t)

    def run(blocks, masks, first):
        carry = jnp.zeros((2 * t, 1), F32) if first else car_scr[:, 0:1]
        out, carry = _sb_group(q_st, [kt_scr[b] for b in blocks], [v_scr[b] for b in blocks], suffix, carry,
                               masks, True)
        acc_scr[...] = out if first else acc_scr[...] + out
        car_scr[...] = jnp.broadcast_to(carry, car_scr.shape)
        min_scr[0] = jnp.min(carry)

    @pl.when(qi == 0)
    def _():
        run([0], [mask], True)

    @pl.when(qi > 0)
    def _():
        run([qi, qi - 1], [mask, None], True)

    rest = jnp.maximum(qi - 1, 0)

    def more(it):
        return (it < rest // 2) & (min_scr[0] < SKIP_MASS)

    def pair(it):
        b0 = qi - 2 - 2 * it
        run([b0, b0 - 1], [None, None], False)
        return it + 1

    lax.while_loop(more, pair, 0)

    @pl.when((rest % 2 == 1) & (min_scr[0] < SKIP_MASS))
    def _():
        run([0], [None], False)

    o_ref[0] = _unstack_heads(acc_scr[...], t).astype(o_ref.dtype)


def _attn_hist_body(q_ref, kn_ref, vn_ref, kh_ref, vh_ref, o_ref, left_ref, acc_scr, car_scr, min_scr, *, tk,
                    group):
    t = q_ref.shape[1]
    past = kh_ref.shape[1]
    q_st = _stack_heads(q_ref[0])

    def keep(out, carry, first):
        acc_scr[...] = out if first else acc_scr[...] + out
        car_scr[...] = jnp.broadcast_to(carry, car_scr.shape)
        min_scr[0] = jnp.min(carry)

    keep(*_sb_group(q_st, [kn_ref[0].astype(BF16)], [vn_ref[0].astype(BF16)], _suffix_matrix(t),
                    jnp.zeros((2 * t, 1), F32), [_causal_mask(t)], False), True)
    suffix = _suffix_matrix(tk)
    for top in range(past // tk, 0, -group):
        @pl.when(min_scr[0] < SKIP_MASS)
        def _(top=top):
            blocks = range(top - 1, top - 1 - group, -1)
            keep(*_sb_group(q_st, [kh_ref[0, b * tk:(b + 1) * tk, :].astype(BF16) for b in blocks],
                            [vh_ref[0, b * tk:(b + 1) * tk, :].astype(BF16) for b in blocks],
                            suffix, car_scr[:, 0:1], [None] * group, False), False)
    o_ref[0] = _unstack_heads(acc_scr[...], t).astype(o_ref.dtype)
    left_ref[0, 0] = jnp.full(left_ref.shape[2:], min_scr[0], F32)


def _attn_recent_body(q_ref, kn_ref, vn_ref, kc_ref, vc_ref, o_ref, left_ref, *, heads, tk):
    t = q_ref.shape[1]
    n_blk = kc_ref.shape[1] // tk
    q, kn, vn = q_ref[0], kn_ref[0], vn_ref[0]
    head = lambda a, h: a[:, h * HEAD_DIM:(h + 1) * HEAD_DIM]
    cached = lambda ref, blk, h: ref[0, pl.ds(blk * tk, tk), h, :].astype(BF16)
    score = lambda qh, kh: lax.dot_general(qh, kh, (((1,), (1,)), ((), ())), preferred_element_type=F32)
    blocks = list(range(n_blk - 1, -1, -1))
    mask, sfx_new, sfx_old = _causal_mask(t), _suffix_matrix(t), _suffix_matrix(tk)
    zs, masks, sfx = [], [], []
    for p in range(heads // 2):
        pair = (2 * p, 2 * p + 1)
        zs.append(jnp.concatenate([score(head(q, h), head(kn, h)) for h in pair], axis=0) * LOG2E)
        masks.append(mask)
        sfx.append(sfx_new)
        for j in blocks:
            zs.append(jnp.concatenate([score(head(q, h), cached(kc_ref, j, h)) for h in pair], axis=0) * LOG2E)
            masks.append(None)
            sfx.append(sfx_old)
    cs = _sb_masses(zs, masks, sfx)
    outs, left = [], None
    chain = 1 + n_blk
    for p in range(heads // 2):
        carry = jnp.zeros((2 * t, 1), F32)
        acc = [None, None]
        for i in range(chain):
            a, carry = _sb_weights(zs[p * chain + i], cs[p * chain + i], carry, masks[p * chain + i])
            for r in range(2):
                h = 2 * p + r
                vb = head(vn, h) if i == 0 else cached(vc_ref, blocks[i - 1], h)
                o = jnp.dot(a[r * t:(r + 1) * t], vb, preferred_element_type=F32)
                acc[r] = o if acc[r] is None else acc[r] + o
        outs += acc
        low = jnp.min(carry)
        left = low if left is None else jnp.minimum(left, low)
    o_ref[0] = jnp.concatenate(outs, axis=1).astype(o_ref.dtype)
    left_ref[0] = jnp.full(left_ref.shape[1:], left, F32)


def _attn_specs(t):
    tile = pl.BlockSpec((1, t, LANES), lambda i, p, s: (i, s, p))
    seq = lambda rows: pl.BlockSpec((1, rows, LANES), lambda i, p, s: (i, 0, p))
    state = [pltpu.VMEM((2 * t, LANES), F32), pltpu.VMEM((2 * t, LANES), F32), pltpu.SMEM((1,), F32)]
    return tile, seq, state


def _attention_self(q, k, v, t):
    b, n, dm = q.shape
    assert n % t == 0 and dm % LANES == 0
    tile, seq, state = _attn_specs(t)
    return pl.pallas_call(
        functools.partial(_attn_self_body, t=t),
        out_shape=jax.ShapeDtypeStruct((b, n, dm), BF16),
        grid=(b, dm // LANES, n // t),
        in_specs=[tile, seq(n), seq(n)],
        out_specs=tile,
        scratch_shapes=[pltpu.VMEM((n // t, LANES, t), BF16), pltpu.VMEM((n // t, t, LANES), BF16)] + state,
        compiler_params=_cparams(("parallel", "parallel", "arbitrary")),
        name="sb_attention",
    )(q, k, v)


def _attention_cached(q, k_new, v_new, k_hist, v_hist, tk, group):
    b, t, dm = q.shape
    past = k_hist.shape[1]
    assert dm % LANES == 0 and past % (tk * group) == 0
    tile, seq, state = _attn_specs(t)
    return pl.pallas_call(
        functools.partial(_attn_hist_body, tk=tk, group=group),
        out_shape=[jax.ShapeDtypeStruct((b, t, dm), BF16), jax.ShapeDtypeStruct((b, dm // LANES, 8, LANES), F32)],
        grid=(b, dm // LANES, 1),
        in_specs=[tile, seq(t), seq(t), seq(past), seq(past)],
        out_specs=[tile, pl.BlockSpec((1, 1, 8, LANES), lambda i, p, s: (i, p, 0, 0))],
        scratch_shapes=state,
        compiler_params=_cparams(("parallel", "parallel", "arbitrary")),
        name="sb_attention_cached",
    )(q, k_new, v_new, k_hist, v_hist)


def _attention_recent_first(q, k_new, v_new, cache_k, cache_v, tk, group):
    b, past, heads, hd = cache_k.shape
    t, dm = q.shape[1:]
    flat = lambda c: c.reshape(b, past, heads * hd)
    walk = lambda: _attention_cached(q, k_new, v_new, flat(cache_k), flat(cache_v), tk, group)[0]
    recent = tk * group
    if past % recent:
        return walk()
    seq = pl.BlockSpec((1, t, dm), lambda i: (i, 0, 0))
    newest = pl.BlockSpec((1, recent, heads, hd), lambda i: (i, past // recent - 1, 0, 0))
    o, left = pl.pallas_call(
        functools.partial(_attn_recent_body, heads=heads, tk=tk),
        out_shape=[jax.ShapeDtypeStruct((b, t, dm), BF16), jax.ShapeDtypeStruct((b, 8, LANES), F32)],
        grid=(b,),
        in_specs=[seq, seq, seq, newest, newest],
        out_specs=[seq, pl.BlockSpec((1, 8, LANES), lambda i: (i, 0, 0))],
        compiler_params=_cparams(("parallel",)),
        name="sb_attention_recent",
    )(q, k_new, v_new, cache_k, cache_v)
    if past == recent:
        return o
    return lax.cond(jnp.min(left) < SKIP_MASS, walk, lambda: o)


def _mix_body(diff_ref, o_ref, gp_ref, gs_ref, x_ref, wp_ref, ps_ref, wsb_ref, wo_ref, nf_ref,
              wr_ref, br_ref, x1_ref, hp_ref, route_ref, cnt_ref, logit_scr, *, tm):
    @pl.when(pl.program_id(0) == 0)
    def _():
        logit_scr[...] = jnp.zeros_like(logit_scr)

    late_logits = logit_scr[...]

    n_pool = wp_ref.shape[0]
    group = wp_ref.shape[1]
    rows = tm // MIX_PARTS
    parts = [pl.ds(i * rows, rows) for i in range(MIX_PARTS)]
    pools = [jnp.concatenate(
        [jnp.dot(diff_ref[s, g * group:(g + 1) * group], wp_ref[g], preferred_element_type=F32)
         for g in range(n_pool)], axis=-1) * ps_ref[...] for s in parts]
    sbs = [jnp.dot(o_ref[s, :], wsb_ref[...], preferred_element_type=F32) for s in parts]
    mixed = [(gp_ref[s, :].astype(F32) * pool + gs_ref[s, :].astype(F32) * sb).astype(BF16)
             for s, pool, sb in zip(parts, pools, sbs)]
    x1s = [x_ref[s, :] + jnp.dot(mx, wo_ref[...], preferred_element_type=F32) for s, mx in zip(parts, mixed)]
    splits = []
    for s, x1 in zip(parts, x1s):
        x1_ref[s, :] = x1
        h = (x1 * lax.rsqrt(jnp.mean(x1 * x1, axis=-1, keepdims=True) + EPS)) * nf_ref[...]
        d_half = h.shape[1] // 2
        lo_bits = pltpu.bitcast(h[:, :d_half].astype(BF16).astype(F32), jnp.uint32)
        hi_bits = pltpu.bitcast(h[:, d_half:].astype(BF16).astype(F32), jnp.uint32)
        hp_ref[s, :] = (lo_bits >> 16) | (hi_bits & jnp.uint32(0xFFFF0000))
        hh = h.astype(BF16)
        splits.append(jnp.concatenate([hh, (h - hh.astype(F32)).astype(BF16)], axis=0))

    rs = [jnp.dot(sp, wr_ref[...], preferred_element_type=F32) for sp in splits]
    logit_scr[...] = jnp.concatenate(
        [(r[:rows, :LANES] + r[:rows, LANES:]) + (r[rows:, :LANES] + r[rows:, LANES:]) for r in rs],
        axis=0) + br_ref[...]
    _route_tile(late_logits, route_ref, cnt_ref, tm)


def _route_tile(logits, route_ref, cnt_ref, tm):
    lane = lax.broadcasted_iota(jnp.int32, (tm, LANES), 1)
    big = jnp.int32(LANES)

    def first_max(vals):
        m = jnp.max(vals, axis=-1, keepdims=True)
        idx = jnp.min(jnp.where(vals == m, lane, big), axis=-1, keepdims=True)
        return m, idx

    gl = jnp.where(lane < N_GROUPS, logits, NEG)
    gmax, grp = first_max(gl)
    p_grp = 1.0 / jnp.sum(jnp.exp(gl - gmax), axis=-1, keepdims=True)
    e_lo = N_GROUPS + grp * PER_GROUP
    el = jnp.where((lane >= e_lo) & (lane < e_lo + PER_GROUP), logits, NEG)
    m1, i1 = first_max(el)
    m2, i2 = first_max(jnp.where(lane == i1, NEG, el))
    t2 = jnp.exp(m2 - m1)
    w1 = p_grp / (1.0 + t2)
    w2 = w1 * t2
    e1 = i1 - N_GROUPS
    e2 = i2 - N_GROUPS

    oh1 = jnp.where(lane == e1, 1.0, 0.0).astype(BF16)
    oh2 = jnp.where(lane == e2, 1.0, 0.0).astype(BF16)
    rr = lax.broadcasted_iota(jnp.int32, (tm, tm), 0)
    cc = lax.broadcasted_iota(jnp.int32, (tm, tm), 1)
    before = jnp.where(cc < rr, 1.0, 0.0).astype(BF16)
    ones = jnp.ones((8, tm), BF16)
    pre1 = jnp.dot(before, oh1, preferred_element_type=F32)
    pre2 = jnp.dot(before, oh2, preferred_element_type=F32)
    c1 = jnp.dot(ones, oh1, preferred_element_type=F32)
    c2 = jnp.dot(ones, oh2, preferred_element_type=F32)
    rank1 = jnp.sum(jnp.where(lane == e1, pre1, 0.0), axis=-1, keepdims=True)
    rank2 = jnp.sum(jnp.where(lane == e2, pre2 + c1[0:1, :], 0.0), axis=-1, keepdims=True)
    cnt_ref[0] = c1 + c2

    route = jnp.where(lane == 0, e1.astype(F32), 0.0)
    route = jnp.where(lane == 1, e2.astype(F32), route)
    route = jnp.where(lane == 2, w1, route)
    route = jnp.where(lane == 3, w2, route)
    route = jnp.where(lane == 4, rank1, route)
    route = jnp.where(lane == 5, rank2, route)
    route_ref[...] = route


def _mix(diff, o, gp, gs, x, wp, ps, wsb, wo, nf, wr, br, tm):
    m, d = x.shape
    assert m % tm == 0
    n = m // tm
    row = lambda i: (jnp.minimum(i, n - 1), 0)
    late = lambda i: (jnp.maximum(i - 1, 0), 0)
    full = lambda a: pl.BlockSpec(a.shape, lambda i: (0,) * a.ndim, pipeline_mode=pl.Buffered(1))
    acts = [diff, o, gp, gs, x]
    consts = [wp, ps, wsb, wo, nf, wr, br]
    return pl.pallas_call(
        functools.partial(_mix_body, tm=tm),
        out_shape=[jax.ShapeDtypeStruct((m, d), F32), jax.ShapeDtypeStruct((m, d // 2), jnp.uint32),
                   jax.ShapeDtypeStruct((m, LANES), F32), jax.ShapeDtypeStruct((n, 8, LANES), F32)],
        grid=(n + 1,),
        in_specs=[pl.BlockSpec((tm, a.shape[1]), row) for a in acts] + [full(a) for a in consts],
        out_specs=[pl.BlockSpec((tm, d), row), pl.BlockSpec((tm, d // 2), row),
                   pl.BlockSpec((tm, LANES), late), pl.BlockSpec((1, 8, LANES), lambda i: late(i) + (0,))],
        scratch_shapes=[pltpu.VMEM((tm, LANES), F32)],
        compiler_params=_cparams(("arbitrary",)),
        name="mix_outproj_router",
    )(*acts, *consts)


def _row_wait(src_ref, dst_ref, sem, n_rows):
    pltpu.make_async_copy(src_ref.at[pl.ds(0, n_rows)], dst_ref.at[pl.ds(0, n_rows)], sem).wait()


def _dispatch_body(slot_ref, zero_ref, hp_ref, xs_ref, buf, zbuf, sem, zsem, *, tm):
    i = pl.program_id(0)
    last = pl.num_programs(0) - 1
    par = i % 2

    @pl.when(i == 0)
    def _():
        zbuf[...] = jnp.zeros_like(zbuf)
        bm = zbuf.shape[0]
        for wait in (False, True):
            for z in range(zero_ref.shape[0]):
                @pl.when(zero_ref[z] >= 0)
                def _(z=z, wait=wait):
                    start = pl.multiple_of(jnp.maximum(zero_ref[z], 0), bm)
                    copy = pltpu.make_async_copy(zbuf, xs_ref.at[pl.ds(start, bm)], zsem)
                    copy.wait() if wait else copy.start()

    def drain(slot):
        for _ in range(2):
            _row_wait(buf.at[slot], xs_ref, sem.at[slot], tm)

    @pl.when(i >= 2)
    def _():
        drain(par)

    buf[par] = hp_ref[...]
    base = i * tm

    def body(t, _):
        src = buf.at[par, pl.ds(t, 1)]
        for kk in range(2):
            dst = xs_ref.at[pl.ds(slot_ref[2 * (base + t) + kk], 1)]
            pltpu.make_async_copy(src, dst, sem.at[par]).start()
        return 0

    lax.fori_loop(0, tm, body, 0, unroll=8)

    @pl.when(i == last)
    def _():
        drain(par)

        @pl.when(i >= 1)
        def _():
            drain(1 - par)


def _dispatch(slots_flat, zero_starts, hp, n_slots, tm, bm):
    m, dh = hp.shape
    assert m % tm == 0
    return pl.pallas_call(
        functools.partial(_dispatch_body, tm=tm),
        out_shape=jax.ShapeDtypeStruct((n_slots, dh), hp.dtype),
        grid_spec=pltpu.PrefetchScalarGridSpec(
            num_scalar_prefetch=2, grid=(m // tm,),
            in_specs=[pl.BlockSpec((tm, dh), lambda i, s, z: (i, 0))],
            out_specs=pl.BlockSpec(memory_space=pl.ANY),
            scratch_shapes=[pltpu.VMEM((2, tm, dh), hp.dtype), pltpu.VMEM((bm, dh), hp.dtype),
                            pltpu.SemaphoreType.DMA((2,)), pltpu.SemaphoreType.DMA]),
        compiler_params=_cparams(("arbitrary",), disable_bounds_checks=True, has_side_effects=True),
        name="moe_dispatch",
    )(slots_flat, zero_starts, hp)


def _ffn_body(be_ref, nv_ref, xs_ref, wg_ref, wu_ref, wd_ref, y_ref):
    del be_ref

    @pl.when(pl.program_id(0) >= nv_ref[0])
    def _():
        y_ref[...] = jnp.zeros_like(y_ref)

    @pl.when(pl.program_id(0) < nv_ref[0])
    def _():
        d_half = xs_ref.shape[1]
        rows = xs_ref.shape[0] // FFN_PARTS
        hids = []
        for i in range(FFN_PARTS):
            words = xs_ref[pl.ds(i * rows, rows), :]
            x_lo = pltpu.bitcast(words << 16, F32).astype(BF16)
            x_hi = pltpu.bitcast(words & jnp.uint32(0xFFFF0000), F32).astype(BF16)

            def proj(w_ref):
                return (jnp.dot(x_lo, w_ref[0, :d_half, :], preferred_element_type=F32)
                        + jnp.dot(x_hi, w_ref[0, d_half:, :], preferred_element_type=F32))

            hids.append((jax.nn.silu(proj(wg_ref)) * proj(wu_ref)).astype(BF16))
        for i, hid in enumerate(hids):
            y_ref[pl.ds(i * rows, rows), :] = jnp.dot(hid, wd_ref[0], preferred_element_type=F32)


def _ffn(block_expert, n_valid, xs, wg, wu, wd, bm):
    n_slots, d_half = xs.shape
    d = 2 * d_half
    de = wg.shape[2]
    live = lambda b, be, nv: (jnp.minimum(b, nv[0] - 1), 0)
    wsel = lambda b, be, nv: (be[b], 0, 0)
    return pl.pallas_call(
        _ffn_body,
        out_shape=jax.ShapeDtypeStruct((n_slots, d), F32),
        grid_spec=pltpu.PrefetchScalarGridSpec(
            num_scalar_prefetch=2, grid=(n_slots // bm,),
            in_specs=[pl.BlockSpec((bm, d_half), live), pl.BlockSpec((1, d, de), wsel),
                      pl.BlockSpec((1, d, de), wsel), pl.BlockSpec((1, de, d), wsel)],
            out_specs=pl.BlockSpec((bm, d), lambda b, be, nv: (b, 0))),
        compiler_params=_cparams(("arbitrary",)),
        name="moe_ffn",
    )(block_expert, n_valid, xs, wg, wu, wd)


def _final_body(slot_ref, x1_ref, route_ref, g_ref, y_hbm, o_ref, buf, sem, *, tm):
    i = pl.program_id(0)
    n_steps = pl.num_programs(0)

    def issue(step, par):
        def body(t, _):
            for kk in range(2):
                src = y_hbm.at[pl.ds(slot_ref[2 * (step * tm + t) + kk], 1)]
                pltpu.make_async_copy(src, buf.at[par, kk, pl.ds(t, 1)], sem.at[par]).start()
            return 0
        lax.fori_loop(0, tm, body, 0, unroll=8)

    @pl.when(i == 0)
    def _():
        issue(0, 0)

    @pl.when(i + 1 < n_steps)
    def _():
        issue(i + 1, (i + 1) % 2)

    par = i % 2
    for kk in range(2):
        _row_wait(y_hbm, buf.at[par, kk], sem.at[par], tm)
    route = route_ref[...]
    x2 = x1_ref[...] + route[:, 2:3] * buf[par, 0] + route[:, 3:4] * buf[par, 1]
    o_ref[...] = (x2 * lax.rsqrt(jnp.mean(x2 * x2, axis=-1, keepdims=True) + EPS)) * g_ref[...]


def _final(slots_flat, x1, route, g, y, tm):
    m, d = x1.shape
    assert m % tm == 0
    row = lambda i, s: (i, 0)
    return pl.pallas_call(
        functools.partial(_final_body, tm=tm),
        out_shape=jax.ShapeDtypeStruct((m, d), F32),
        grid_spec=pltpu.PrefetchScalarGridSpec(
            num_scalar_prefetch=1, grid=(m // tm,),
            in_specs=[pl.BlockSpec((tm, d), row), pl.BlockSpec((tm, LANES), row),
                      pl.BlockSpec((1, d), lambda i, s: (0, 0)), pl.BlockSpec(memory_space=pl.ANY)],
            out_specs=pl.BlockSpec((tm, d), row),
            scratch_shapes=[pltpu.VMEM((2, 2, tm, d), F32), pltpu.SemaphoreType.DMA((2,))]),
        compiler_params=_cparams(("arbitrary",), disable_bounds_checks=True),
        name="moe_combine_final_norm",
    )(slots_flat, x1, route, g, y)


def _routing_tables(route, cnt, tm, bm):
    m = route.shape[0]
    n_blocks = (2 * m) // bm + N_EXPERTS
    counts = cnt[:, 0, :N_EXPERTS].astype(jnp.int32)
    sizes = jnp.sum(counts, axis=0)
    padded = (sizes + bm - 1) // bm * bm
    pad_end = jnp.cumsum(padded)
    base = (pad_end - padded)[None, :] + jnp.cumsum(counts, axis=0) - counts
    base_tok = jnp.repeat(base, tm, axis=0)
    e = route[:, 0:2].astype(jnp.int32)
    rank = route[:, 4:6].astype(jnp.int32)
    sel = e[:, :, None] == jnp.arange(N_EXPERTS, dtype=jnp.int32)[None, None, :]
    slots = jnp.sum(jnp.where(sel, base_tok[:, None, :], 0), axis=-1) + rank
    n_valid = (pad_end[-1] // bm).astype(jnp.int32)
    blk = jnp.minimum(jnp.arange(n_blocks, dtype=jnp.int32), n_valid - 1)
    block_expert = jnp.minimum(jnp.sum(pad_end[None, :] <= (blk * bm)[:, None], axis=1), N_EXPERTS - 1)
    last_blk = jnp.where(padded > 0, pad_end - bm, -1)
    tail = n_valid + jnp.arange(N_EXPERTS, dtype=jnp.int32)
    tail = jnp.where(tail < n_blocks, tail * bm, -1)
    zero_starts = jnp.concatenate([last_blk, tail]).astype(jnp.int32)
    return slots.reshape(-1), block_expert.astype(jnp.int32), n_valid.reshape(1), zero_starts, n_blocks * bm


def _stream(x, pool_hist, k_hist, v_hist, p, *, tq, tk, group, tn, tm_in, tm_mix, tm_fin, bm):
    b, n, d = x.shape
    m = b * n
    past = 0 if k_hist is None else k_hist.shape[1]
    x2d = x.reshape(m, d)
    u, q, kb, vb, k, v, gp, gs = _inproj(x2d, p['norm_mix'], p['w_in'], _fit(m, tm_in))
    dp = u.shape[1]
    diff = _pool_diff(pool_hist, u.reshape(b, n, dp), past, tn).reshape(m, dp)
    shp = (b, n, kb.shape[1])
    if k_hist is None:
        o = _attention_self(q.reshape(shp), kb.reshape(shp), vb.reshape(shp), tq)
    else:
        o = _attention_recent_first(q.reshape(shp), kb.reshape(shp), vb.reshape(shp), k_hist, v_hist, tk, group)
    o = o.reshape(m, -1)
    x1, hp, route, cnt = _mix(diff, o, gp, gs, x2d, p['w_pool'], p['pool_scale'], p['w_sb_out'], p['w_out'],
                              p['norm_ffn'], p['w_r'], p['b_r'], tm_mix)
    slots, block_expert, n_valid, zero_starts, n_slots = _routing_tables(route, cnt, tm_mix, bm)
    xs = _dispatch(slots, zero_starts, hp, n_slots, tm_mix, bm)
    y = _ffn(block_expert, n_valid, xs, p['w_g'], p['w_u'], p['w_d'], bm)
    out = _final(slots, x1, route, p['norm_final'], y, tm_fin)
    return out.reshape(b, n, d), u.reshape(b, n, dp), k, v


def kernel(x_prompt, x_sample, cache_sb_k, cache_sb_v, state_pool, norm_mix, w_in, w_pool, pool_scale, w_sb_out,
           w_out, norm_ffn, w_router_group, b_router_group, w_router_expert, b_router_expert, w_exp_gate,
           w_exp_up, w_exp_down, norm_final):
    depth = w_in.shape[0]
    assert depth == 1
    bp, sp, d = x_prompt.shape
    bs, ss, _ = x_sample.shape
    past = cache_sb_k.shape[2]
    heads, hd = cache_sb_k.shape[3], cache_sb_k.shape[4]
    assert hd == HEAD_DIM
    dp = state_pool.shape[3]
    n_state = state_pool.shape[2]

    w_r = jnp.concatenate([w_router_group[0], w_router_expert[0]], axis=1)
    w_r = jnp.pad(w_r, ((0, 0), (0, LANES - w_r.shape[1])))
    w_r_hi = w_r.astype(BF16)
    b_r = jnp.concatenate([b_router_group[0], b_router_expert[0]])
    p = dict(
        norm_mix=norm_mix[0][None, :], w_in=w_in[0].astype(BF16), w_pool=w_pool[0].astype(BF16),
        pool_scale=pool_scale[0][None, :], w_sb_out=w_sb_out[0].astype(BF16), w_out=w_out[0].astype(BF16),
        norm_ffn=norm_ffn[0][None, :],
        w_r=jnp.concatenate([w_r_hi, (w_r - w_r_hi.astype(F32)).astype(BF16)], axis=1),
        b_r=jnp.pad(b_r, (0, LANES - b_r.shape[0]))[None, :].astype(F32),
        w_g=w_exp_gate[0].astype(BF16), w_u=w_exp_up[0].astype(BF16), w_d=w_exp_down[0].astype(BF16),
        norm_final=norm_final[None, :])

    hist_p = jnp.zeros((bp, POOL_HIST, dp), F32)
    yp, up, kp, vp = _stream(x_prompt, hist_p, None, None, p, tq=256, tk=256, group=2, tn=min(sp, 512),
                             tm_in=1024, tm_mix=256, tm_fin=256, bm=256)
    hist_s = jnp.pad(state_pool[0], ((0, 0), (POOL_HIST - n_state, 0), (0, 0)))
    ys, us, ks, vs = _stream(x_sample, hist_s, cache_sb_k[0], cache_sb_v[0], p, tq=ss, tk=256, group=2, tn=ss,
                             tm_in=1024, tm_mix=256, tm_fin=256, bm=256)

    def pool_state(hist, u):
        return jnp.concatenate([hist[:, POOL_HIST - n_state:], u], axis=1)[:, -n_state:][None]

    return (yp, ys,
            kp.reshape(1, bp, sp, heads, hd), vp.reshape(1, bp, sp, heads, hd), pool_state(hist_p, up),
            ks.reshape(1, bs, ss, heads, hd), vs.reshape(1, bs, ss, heads, hd), pool_state(hist_s, us))
```

```python
import functools

import jax
import jax.numpy as jnp
from jax import lax
from jax.experimental import pallas as pl
from jax.experimental.pallas import tpu as pltpu

F32 = jnp.float32
BF16 = jnp.bfloat16

EPS = 1e-6
HEAD_DIM = 64
LANES = 128
POOL_WINDOWS = (2, 4, 8, 16)
POOL_HIST = 16
N_GROUPS = 4
PER_GROUP = 8
N_EXPERTS = N_GROUPS * PER_GROUP
VMEM_LIMIT = 58 * 1024 * 1024
NEG = -1e30
LOG2E = 1.4426950408889634
MIX_PARTS = 2
FFN_PARTS = 2
SKIP_MASS = 160.0


def _fit(m, tile):
    while m % tile:
        tile //= 2
    return tile


def _cparams(sem, **kw):
    return pltpu.CompilerParams(dimension_semantics=sem, vmem_limit_bytes=VMEM_LIMIT, **kw)


def _proj_main_body(h_ref, w_ref, u_ref, q_ref, kb_ref, vb_ref, k_hbm, v_hbm, kv_buf, sem):
    i = pl.program_id(0)
    j = pl.program_id(1)
    last = pl.num_programs(0) - 1
    tm = h_ref.shape[0]
    heads = k_hbm.shape[1]

    def proj():
        return jnp.dot(h_ref[...], w_ref[...], preferred_element_type=F32)

    def head_copies(slot, dst_hbm, step):
        row0 = pl.multiple_of(step * tm, tm)
        return [pltpu.make_async_copy(kv_buf.at[slot, hh], dst_hbm.at[pl.ds(row0, tm), hh, :], sem.at[slot])
                for hh in range(heads)]

    @pl.when(j == 0)
    def _():
        u_ref[...] = proj()

    @pl.when(j == 1)
    def _():
        q_ref[...] = (proj() * (HEAD_DIM ** -0.5)).astype(BF16)

    for jj, slot, dense_ref, dst_hbm in ((2, 0, kb_ref, k_hbm), (3, 1, vb_ref, v_hbm)):
        @pl.when(j == jj)
        def _(slot=slot, dense_ref=dense_ref, dst_hbm=dst_hbm):
            @pl.when(i > 0)
            def _():
                for c in head_copies(slot, dst_hbm, i - 1):
                    c.wait()
            acc = proj()
            dense_ref[...] = acc.astype(BF16)
            for hh in range(heads):
                kv_buf[slot, hh] = acc[:, hh * HEAD_DIM:(hh + 1) * HEAD_DIM]
            for c in head_copies(slot, dst_hbm, i):
                c.start()

    @pl.when((i == last) & (j == 3))
    def _():
        for slot, dst_hbm in ((0, k_hbm), (1, v_hbm)):
            for c in head_copies(slot, dst_hbm, i):
                c.wait()


def _proj_gates_body(x_ref, g_ref, w_ref, h_ref, gp_ref, gs_ref, *, tn):
    j = pl.program_id(1)

    @pl.when(j == 0)
    def _():
        x = x_ref[...]
        r = lax.rsqrt(jnp.mean(x * x, axis=-1, keepdims=True) + EPS)
        h_ref[...] = ((x * r) * g_ref[...]).astype(BF16)

    for jj, ref in ((0, gp_ref), (2, gs_ref)):
        for half in range(2):
            @pl.when(j == jj + half)
            def _(ref=ref, half=half):
                a = jnp.dot(h_ref[...], w_ref[...], preferred_element_type=F32)
                ref[:, half * tn:(half + 1) * tn] = (0.5 * jnp.tanh(0.5 * a) + 0.5).astype(BF16)


def _inproj(x, g, w_bf, tm):
    m, d = x.shape
    tn = d // 2
    assert w_bf.shape == (d, 8 * tn) and m % tm == 0
    row = lambda i, j: (i, 0)
    h_spec = pl.BlockSpec((tm, d), row)
    h, gp, gs = pl.pallas_call(
        functools.partial(_proj_gates_body, tn=tn),
        out_shape=[jax.ShapeDtypeStruct((m, d), BF16)] * 3,
        grid=(m // tm, 4),
        in_specs=[h_spec, pl.BlockSpec((1, d), lambda i, j: (0, 0)),
                  pl.BlockSpec((d, tn), lambda i, j: (0, j + 4))],
        out_specs=[h_spec] * 3,
        compiler_params=_cparams(("parallel", "arbitrary")),
        name="inproj_gates",
    )(x, g, w_bf)
    dense = lambda dt: jax.ShapeDtypeStruct((m, tn), dt)
    cache = jax.ShapeDtypeStruct((m, tn // HEAD_DIM, HEAD_DIM), F32)
    any_spec = pl.BlockSpec(memory_space=pl.ANY)
    u, q, kb, vb, k, v = pl.pallas_call(
        _proj_main_body,
        out_shape=[dense(F32), dense(BF16), dense(BF16), dense(BF16), cache, cache],
        grid=(m // tm, 4),
        in_specs=[h_spec, pl.BlockSpec((d, tn), lambda i, j: (0, j))],
        out_specs=[pl.BlockSpec((tm, tn), row)] * 4 + [any_spec] * 2,
        scratch_shapes=[pltpu.VMEM((2, tn // HEAD_DIM, tm, HEAD_DIM), F32), pltpu.SemaphoreType.DMA((2,))],
        compiler_params=_cparams(("arbitrary", "arbitrary"), has_side_effects=True),
        name="inproj_main",
    )(h, w_bf)
    return u, q, kb, vb, k, v, gp, gs


def _pool_body(hist_ref, u_ref, o_ref, ext_scr, *, pos0, tn):
    s = pl.program_id(1)

    @pl.when(s == 0)
    def _():
        ext_scr[0:POOL_HIST, :] = hist_ref[0]

    ext_scr[POOL_HIST:POOL_HIST + tn, :] = u_ref[0]
    pos = pos0 + s * tn + lax.broadcasted_iota(jnp.int32, (tn, 1), 0)
    group = u_ref.shape[2] // len(POOL_WINDOWS)
    for g, w in enumerate(POOL_WINDOWS):
        lo, hi = g * group, (g + 1) * group
        cur = ext_scr[POOL_HIST:POOL_HIST + tn, lo:hi]
        tot = cur
        for dlt in range(1, w):
            tot = tot + ext_scr[POOL_HIST - dlt:POOL_HIST - dlt + tn, lo:hi]
        cnt = jnp.minimum(pos + 1, w).astype(F32)
        o_ref[0, :, lo:hi] = (tot / cnt - cur).astype(BF16)
    ext_scr[0:POOL_HIST, :] = ext_scr[tn:tn + POOL_HIST, :]


def _pool_diff(hist, u, pos0, tn):
    b, n, dp = u.shape
    assert n % tn == 0 and hist.shape == (b, POOL_HIST, dp)
    return pl.pallas_call(
        functools.partial(_pool_body, pos0=pos0, tn=tn),
        out_shape=jax.ShapeDtypeStruct((b, n, dp), BF16),
        grid=(b, n // tn),
        in_specs=[pl.BlockSpec((1, POOL_HIST, dp), lambda i, s: (i, 0, 0)),
                  pl.BlockSpec((1, tn, dp), lambda i, s: (i, s, 0))],
        out_specs=pl.BlockSpec((1, tn, dp), lambda i, s: (i, s, 0)),
        scratch_shapes=[pltpu.VMEM((POOL_HIST + tn, dp), F32)],
        compiler_params=_cparams(("parallel", "arbitrary")),
        name="pool_diff",
    )(hist, u)


def _softplus2(z2):
    neg_abs = pltpu.bitcast(pltpu.bitcast(z2, jnp.uint32) | jnp.uint32(0x80000000), F32)
    return jnp.maximum(z2, 0.0) + jnp.log2(1.0 + jnp.exp2(neg_abs))


def _suffix_matrix(n):
    r = lax.broadcasted_iota(jnp.int32, (n, n), 0)
    c = lax.broadcasted_iota(jnp.int32, (n, n), 1)
    return jnp.where(r >= c, 1.0, 0.0).astype(BF16)


def _stack_heads(q2):
    lane = lax.broadcasted_iota(jnp.int32, q2.shape, 1)
    zero = jnp.zeros_like(q2)
    return jnp.concatenate([jnp.where(lane < HEAD_DIM, q2, zero), jnp.where(lane >= HEAD_DIM, q2, zero)], axis=0)


def _unstack_heads(acc, t):
    lane = lax.broadcasted_iota(jnp.int32, (t, LANES), 1)
    return jnp.where(lane < HEAD_DIM, acc[:t], acc[t:])


def _causal_mask(t):
    r = lax.broadcasted_iota(jnp.int32, (t, t), 0)
    c = lax.broadcasted_iota(jnp.int32, (t, t), 1)
    m = c < r
    return jnp.concatenate([m, m], axis=0)


def _sb_group(q_st, k_blocks, v_blocks, suffix, carry, masks, transposed_keys):
    dn = (((1,), (0,)), ((), ())) if transposed_keys else (((1,), (1,)), ((), ()))
    zs = [lax.dot_general(q_st, kb, dn, preferred_element_type=F32) * LOG2E for kb in k_blocks]
    cs = _sb_masses(zs, masks, [suffix] * len(zs))
    out = None
    for z, c, vb, mask in zip(zs, cs, v_blocks, masks):
        a, carry = _sb_weights(z, c, carry, mask)
        o = jnp.dot(a, vb, preferred_element_type=F32)
        out = o if out is None else out + o
    return out, carry


def _sb_masses(zs, masks, suffixes):
    splits = []
    for z, mask in zip(zs, masks):
        sp = _softplus2(z)
        if mask is not None:
            sp = jnp.where(mask, sp, 0.0)
        hi = pltpu.bitcast(pltpu.bitcast(sp, jnp.uint32) & jnp.uint32(0xFFFF0000), F32)
        splits.append((hi.astype(BF16), (sp - hi).astype(BF16)))
    return [jnp.dot(hi, sfx, preferred_element_type=F32) + jnp.dot(lo, sfx, preferred_element_type=F32)
            for (hi, lo), sfx in zip(splits, suffixes)]


def _sb_weights(z, c, carry, mask):
    arg = z - c - carry
    if mask is not None:
        arg = jnp.where(mask, arg, NEG)
    return jnp.exp2(arg).astype(BF16), carry + c[:, 0:1]


def _attn_self_body(q_ref, k_ref, v_ref, o_ref, kt_scr, v_scr, acc_scr, car_scr, min_scr, *, t):
    qi = pl.program_id(2)

    @pl.when(qi == 0)
    def _fill():
        def body(c, _):
            row0 = pl.multiple_of(c * t, t)
            kt_scr[c] = k_ref[0, pl.ds(row0, t), :].astype(F32).T.astype(BF16)
            v_scr[c] = v_ref[0, pl.ds(row0, t), :].astype(BF16)
            return 0
        lax.fori_loop(0, kt_scr.shape[0], body, 0)

    q_st = _stack_heads(q_ref[0])
    suffix = _suffix_matrix(t)
    mask = _causal_mask(t)

    def run(blocks, masks, first):
        carry = jnp.zeros((2 * t, 1), F32) if first else car_scr[:, 0:1]
        out, carry = _sb_group(q_st, [kt_scr[b] for b in blocks], [v_scr[b] for b in blocks], suffix, carry,
                               masks, True)
        acc_scr[...] = out if first else acc_scr[...] + out
        car_scr[...] = jnp.broadcast_to(carry, car_scr.shape)
        min_scr[0] = jnp.min(carry)

    @pl.when(qi == 0)
    def _():
        run([0], [mask], True)

    @pl.when(qi > 0)
    def _():
        run([qi, qi - 1], [mask, None], True)

    rest = jnp.maximum(qi - 1, 0)

    def more(it):
        return (it < rest // 2) & (min_scr[0] < SKIP_MASS)

    def pair(it):
        b0 = qi - 2 - 2 * it
        run([b0, b0 - 1], [None, None], False)
        return it + 1

    lax.while_loop(more, pair, 0)

    @pl.when((rest % 2 == 1) & (min_scr[0] < SKIP_MASS))
    def _():
        run([0], [None], False)

    o_ref[0] = _unstack_heads(acc_scr[...], t).astype(o_ref.dtype)


def _attn_hist_body(q_ref, kn_ref, vn_ref, kh_ref, vh_ref, o_ref, left_ref, acc_scr, car_scr, min_scr, *, tk,
                    group):
    t = q_ref.shape[1]
    past = kh_ref.shape[1]
    q_st = _stack_heads(q_ref[0])

    def keep(out, carry, first):
        acc_scr[...] = out if first else acc_scr[...] + out
        car_scr[...] = jnp.broadcast_to(carry, car_scr.shape)
        min_scr[0] = jnp.min(carry)

    keep(*_sb_group(q_st, [kn_ref[0].astype(BF16)], [vn_ref[0].astype(BF16)], _suffix_matrix(t),
                    jnp.zeros((2 * t, 1), F32), [_causal_mask(t)], False), True)
    suffix = _suffix_matrix(tk)
    for top in range(past // tk, 0, -group):
        @pl.when(min_scr[0] < SKIP_MASS)
        def _(top=top):
            blocks = range(top - 1, top - 1 - group, -1)
            keep(*_sb_group(q_st, [kh_ref[0, b * tk:(b + 1) * tk, :].astype(BF16) for b in blocks],
                            [vh_ref[0, b * tk:(b + 1) * tk, :].astype(BF16) for b in blocks],
                            suffix, car_scr[:, 0:1], [None] * group, False), False)
    o_ref[0] = _unstack_heads(acc_scr[...], t).astype(o_ref.dtype)
    left_ref[0, 0] = jnp.full(left_ref.shape[2:], min_scr[0], F32)


def _attn_recent_body(q_ref, kn_ref, vn_ref, kc_ref, vc_ref, o_ref, left_ref, *, tk):
    t, dm = q_ref.shape[1:]
    n_blk = kc_ref.shape[1] // tk
    pair = lambda ref, rows, p: ref[0, rows, p * LANES:(p + 1) * LANES].astype(BF16)
    score = lambda qs, kb: lax.dot_general(qs, kb, (((1,), (1,)), ((), ())), preferred_element_type=F32)
    own = slice(None)
    blocks = [pl.ds(j * tk, tk) for j in range(n_blk - 1, -1, -1)]
    mask, sfx_new, sfx_old = _causal_mask(t), _suffix_matrix(t), _suffix_matrix(tk)
    zs, masks, sfx = [], [], []
    for p in range(dm // LANES):
        q_st = _stack_heads(pair(q_ref, own, p))
        zs.append(score(q_st, pair(kn_ref, own, p)) * LOG2E)
        masks.append(mask)
        sfx.append(sfx_new)
        for rows in blocks:
            zs.append(score(q_st, pair(kc_ref, rows, p)) * LOG2E)
            masks.append(None)
            sfx.append(sfx_old)
    cs = _sb_masses(zs, masks, sfx)
    outs, left = [], None
    chain = 1 + n_blk
    for p in range(dm // LANES):
        carry = jnp.zeros((2 * t, 1), F32)
        out = None
        for i in range(chain):
            a, carry = _sb_weights(zs[p * chain + i], cs[p * chain + i], carry, masks[p * chain + i])
            vb = pair(vn_ref, own, p) if i == 0 else pair(vc_ref, blocks[i - 1], p)
            o = jnp.dot(a, vb, preferred_element_type=F32)
            out = o if out is None else out + o
        outs.append(_unstack_heads(out, t))
        low = jnp.min(carry)
        left = low if left is None else jnp.minimum(left, low)
    o_ref[0] = jnp.concatenate(outs, axis=1).astype(o_ref.dtype)
    left_ref[0] = jnp.full(left_ref.shape[1:], left, F32)


def _attn_specs(t):
    tile = pl.BlockSpec((1, t, LANES), lambda i, p, s: (i, s, p))
    seq = lambda rows: pl.BlockSpec((1, rows, LANES), lambda i, p, s: (i, 0, p))
    state = [pltpu.VMEM((2 * t, LANES), F32), pltpu.VMEM((2 * t, LANES), F32), pltpu.SMEM((1,), F32)]
    return tile, seq, state


def _attention_self(q, k, v, t):
    b, n, dm = q.shape
    assert n % t == 0 and dm % LANES == 0
    tile, seq, state = _attn_specs(t)
    return pl.pallas_call(
        functools.partial(_attn_self_body, t=t),
        out_shape=jax.ShapeDtypeStruct((b, n, dm), BF16),
        grid=(b, dm // LANES, n // t),
        in_specs=[tile, seq(n), seq(n)],
        out_specs=tile,
        scratch_shapes=[pltpu.VMEM((n // t, LANES, t), BF16), pltpu.VMEM((n // t, t, LANES), BF16)] + state,
        compiler_params=_cparams(("parallel", "parallel", "arbitrary")),
        name="sb_attention",
    )(q, k, v)


def _attention_cached(q, k_new, v_new, k_hist, v_hist, tk, group):
    b, t, dm = q.shape
    past = k_hist.shape[1]
    assert dm % LANES == 0 and past % (tk * group) == 0
    tile, seq, state = _attn_specs(t)
    return pl.pallas_call(
        functools.partial(_attn_hist_body, tk=tk, group=group),
        out_shape=[jax.ShapeDtypeStruct((b, t, dm), BF16), jax.ShapeDtypeStruct((b, dm // LANES, 8, LANES), F32)],
        grid=(b, dm // LANES, 1),
        in_specs=[tile, seq(t), seq(t), seq(past), seq(past)],
        out_specs=[tile, pl.BlockSpec((1, 1, 8, LANES), lambda i, p, s: (i, p, 0, 0))],
        scratch_shapes=state,
        compiler_params=_cparams(("parallel", "parallel", "arbitrary")),
        name="sb_attention_cached",
    )(q, k_new, v_new, k_hist, v_hist)


def _attention_recent_first(q, k_new, v_new, cache_k, cache_v, tk, group):
    b, past, heads, hd = cache_k.shape
    t, dm = q.shape[1:]
    flat = lambda c: c.reshape(b, c.shape[1], heads * hd)
    walk = lambda: _attention_cached(q, k_new, v_new, flat(cache_k), flat(cache_v), tk, group)[0]
    recent = tk * group
    if past <= recent:
        return walk()
    seq = lambda rows: pl.BlockSpec((1, rows, dm), lambda i: (i, 0, 0))
    o, left = pl.pallas_call(
        functools.partial(_attn_recent_body, tk=tk),
        out_shape=[jax.ShapeDtypeStruct((b, t, dm), BF16), jax.ShapeDtypeStruct((b, 8, LANES), F32)],
        grid=(b,),
        in_specs=[seq(t), seq(t), seq(t), seq(recent), seq(recent)],
        out_specs=[seq(t), pl.BlockSpec((1, 8, LANES), lambda i: (i, 0, 0))],
        compiler_params=_cparams(("parallel",)),
        name="sb_attention_recent",
    )(q, k_new, v_new, flat(cache_k[:, past - recent:]), flat(cache_v[:, past - recent:]))
    return lax.cond(jnp.min(left) < SKIP_MASS, walk, lambda: o)


def _mix_body(diff_ref, o_ref, gp_ref, gs_ref, x_ref, wp_ref, ps_ref, wsb_ref, wo_ref, nf_ref,
              wr_ref, br_ref, x1_ref, hp_ref, route_ref, cnt_ref, logit_scr, *, tm):
    @pl.when(pl.program_id(0) == 0)
    def _():
        logit_scr[...] = jnp.zeros_like(logit_scr)

    late_logits = logit_scr[...]

    n_pool = wp_ref.shape[0]
    group = wp_ref.shape[1]
    rows = tm // MIX_PARTS
    parts = [pl.ds(i * rows, rows) for i in range(MIX_PARTS)]
    pools = [jnp.concatenate(
        [jnp.dot(diff_ref[s, g * group:(g + 1) * group], wp_ref[g], preferred_element_type=F32)
         for g in range(n_pool)], axis=-1) * ps_ref[...] for s in parts]
    sbs = [jnp.dot(o_ref[s, :], wsb_ref[...], preferred_element_type=F32) for s in parts]
    mixed = [(gp_ref[s, :].astype(F32) * pool + gs_ref[s, :].astype(F32) * sb).astype(BF16)
             for s, pool, sb in zip(parts, pools, sbs)]
    x1s = [x_ref[s, :] + jnp.dot(mx, wo_ref[...], preferred_element_type=F32) for s, mx in zip(parts, mixed)]
    splits = []
    for s, x1 in zip(parts, x1s):
        x1_ref[s, :] = x1
        h = (x1 * lax.rsqrt(jnp.mean(x1 * x1, axis=-1, keepdims=True) + EPS)) * nf_ref[...]
        d_half = h.shape[1] // 2
        lo_bits = pltpu.bitcast(h[:, :d_half].astype(BF16).astype(F32), jnp.uint32)
        hi_bits = pltpu.bitcast(h[:, d_half:].astype(BF16).astype(F32), jnp.uint32)
        hp_ref[s, :] = (lo_bits >> 16) | (hi_bits & jnp.uint32(0xFFFF0000))
        hh = h.astype(BF16)
        splits.append(jnp.concatenate([hh, (h - hh.astype(F32)).astype(BF16)], axis=0))

    rs = [jnp.dot(sp, wr_ref[...], preferred_element_type=F32) for sp in splits]
    logit_scr[...] = jnp.concatenate(
        [(r[:rows, :LANES] + r[:rows, LANES:]) + (r[rows:, :LANES] + r[rows:, LANES:]) for r in rs],
        axis=0) + br_ref[...]
    _route_tile(late_logits, route_ref, cnt_ref, tm)


def _route_tile(logits, route_ref, cnt_ref, tm):
    lane = lax.broadcasted_iota(jnp.int32, (tm, LANES), 1)
    big = jnp.int32(LANES)

    def first_max(vals):
        m = jnp.max(vals, axis=-1, keepdims=True)
        idx = jnp.min(jnp.where(vals == m, lane, big), axis=-1, keepdims=True)
        return m, idx

    gl = jnp.where(lane < N_GROUPS, logits, NEG)
    gmax, grp = first_max(gl)
    p_grp = 1.0 / jnp.sum(jnp.exp(gl - gmax), axis=-1, keepdims=True)
    e_lo = N_GROUPS + grp * PER_GROUP
    el = jnp.where((lane >= e_lo) & (lane < e_lo + PER_GROUP), logits, NEG)
    m1, i1 = first_max(el)
    m2, i2 = first_max(jnp.where(lane == i1, NEG, el))
    t2 = jnp.exp(m2 - m1)
    w1 = p_grp / (1.0 + t2)
    w2 = w1 * t2
    e1 = i1 - N_GROUPS
    e2 = i2 - N_GROUPS

    oh1 = jnp.where(lane == e1, 1.0, 0.0).astype(BF16)
    oh2 = jnp.where(lane == e2, 1.0, 0.0).astype(BF16)
    rr = lax.broadcasted_iota(jnp.int32, (tm, tm), 0)
    cc = lax.broadcasted_iota(jnp.int32, (tm, tm), 1)
    before = jnp.where(cc < rr, 1.0, 0.0).astype(BF16)
    ones = jnp.ones((8, tm), BF16)
    pre1 = jnp.dot(before, oh1, preferred_element_type=F32)
    pre2 = jnp.dot(before, oh2, preferred_element_type=F32)
    c1 = jnp.dot(ones, oh1, preferred_element_type=F32)
    c2 = jnp.dot(ones, oh2, preferred_element_type=F32)
    rank1 = jnp.sum(jnp.where(lane == e1, pre1, 0.0), axis=-1, keepdims=True)
    rank2 = jnp.sum(jnp.where(lane == e2, pre2 + c1[0:1, :], 0.0), axis=-1, keepdims=True)
    cnt_ref[0] = c1 + c2

    route = jnp.where(lane == 0, e1.astype(F32), 0.0)
    route = jnp.where(lane == 1, e2.astype(F32), route)
    route = jnp.where(lane == 2, w1, route)
    route = jnp.where(lane == 3, w2, route)
    route = jnp.where(lane == 4, rank1, route)
    route = jnp.where(lane == 5, rank2, route)
    route_ref[...] = route


def _mix(diff, o, gp, gs, x, wp, ps, wsb, wo, nf, wr, br, tm):
    m, d = x.shape
    assert m % tm == 0
    n = m // tm
    row = lambda i: (jnp.minimum(i, n - 1), 0)
    late = lambda i: (jnp.maximum(i - 1, 0), 0)
    full = lambda a: pl.BlockSpec(a.shape, lambda i: (0,) * a.ndim, pipeline_mode=pl.Buffered(1))
    acts = [diff, o, gp, gs, x]
    consts = [wp, ps, wsb, wo, nf, wr, br]
    return pl.pallas_call(
        functools.partial(_mix_body, tm=tm),
        out_shape=[jax.ShapeDtypeStruct((m, d), F32), jax.ShapeDtypeStruct((m, d // 2), jnp.uint32),
                   jax.ShapeDtypeStruct((m, LANES), F32), jax.ShapeDtypeStruct((n, 8, LANES), F32)],
        grid=(n + 1,),
        in_specs=[pl.BlockSpec((tm, a.shape[1]), row) for a in acts] + [full(a) for a in consts],
        out_specs=[pl.BlockSpec((tm, d), row), pl.BlockSpec((tm, d // 2), row),
                   pl.BlockSpec((tm, LANES), late), pl.BlockSpec((1, 8, LANES), lambda i: late(i) + (0,))],
        scratch_shapes=[pltpu.VMEM((tm, LANES), F32)],
        compiler_params=_cparams(("arbitrary",)),
        name="mix_outproj_router",
    )(*acts, *consts)


def _row_wait(src_ref, dst_ref, sem, n_rows):
    pltpu.make_async_copy(src_ref.at[pl.ds(0, n_rows)], dst_ref.at[pl.ds(0, n_rows)], sem).wait()


def _dispatch_body(slot_ref, zero_ref, hp_ref, xs_ref, buf, zbuf, sem, zsem, *, tm):
    i = pl.program_id(0)
    last = pl.num_programs(0) - 1
    par = i % 2

    @pl.when(i == 0)
    def _():
        zbuf[...] = jnp.zeros_like(zbuf)
        bm = zbuf.shape[0]
        for wait in (False, True):
            for z in range(zero_ref.shape[0]):
                @pl.when(zero_ref[z] >= 0)
                def _(z=z, wait=wait):
                    start = pl.multiple_of(jnp.maximum(zero_ref[z], 0), bm)
                    copy = pltpu.make_async_copy(zbuf, xs_ref.at[pl.ds(start, bm)], zsem)
                    copy.wait() if wait else copy.start()

    def drain(slot):
        for _ in range(2):
            _row_wait(buf.at[slot], xs_ref, sem.at[slot], tm)

    @pl.when(i >= 2)
    def _():
        drain(par)

    buf[par] = hp_ref[...]
    base = i * tm

    def body(t, _):
        src = buf.at[par, pl.ds(t, 1)]
        for kk in range(2):
            dst = xs_ref.at[pl.ds(slot_ref[2 * (base + t) + kk], 1)]
            pltpu.make_async_copy(src, dst, sem.at[par]).start()
        return 0

    lax.fori_loop(0, tm, body, 0, unroll=8)

    @pl.when(i == last)
    def _():
        drain(par)

        @pl.when(i >= 1)
        def _():
            drain(1 - par)


def _dispatch(slots_flat, zero_starts, hp, n_slots, tm, bm):
    m, dh = hp.shape
    assert m % tm == 0
    return pl.pallas_call(
        functools.partial(_dispatch_body, tm=tm),
        out_shape=jax.ShapeDtypeStruct((n_slots, dh), hp.dtype),
        grid_spec=pltpu.PrefetchScalarGridSpec(
            num_scalar_prefetch=2, grid=(m // tm,),
            in_specs=[pl.BlockSpec((tm, dh), lambda i, s, z: (i, 0))],
            out_specs=pl.BlockSpec(memory_space=pl.ANY),
            scratch_shapes=[pltpu.VMEM((2, tm, dh), hp.dtype), pltpu.VMEM((bm, dh), hp.dtype),
                            pltpu.SemaphoreType.DMA((2,)), pltpu.SemaphoreType.DMA]),
        compiler_params=_cparams(("arbitrary",), disable_bounds_checks=True, has_side_effects=True),
        name="moe_dispatch",
    )(slots_flat, zero_starts, hp)


def _ffn_body(be_ref, nv_ref, xs_ref, wg_ref, wu_ref, wd_ref, y_ref):
    del be_ref

    @pl.when(pl.program_id(0) >= nv_ref[0])
    def _():
        y_ref[...] = jnp.zeros_like(y_ref)

    @pl.when(pl.program_id(0) < nv_ref[0])
    def _():
        d_half = xs_ref.shape[1]
        rows = xs_ref.shape[0] // FFN_PARTS
        hids = []
        for i in range(FFN_PARTS):
            words = xs_ref[pl.ds(i * rows, rows), :]
            x_lo = pltpu.bitcast(words << 16, F32).astype(BF16)
            x_hi = pltpu.bitcast(words & jnp.uint32(0xFFFF0000), F32).astype(BF16)

            def proj(w_ref):
                return (jnp.dot(x_lo, w_ref[0, :d_half, :], preferred_element_type=F32)
                        + jnp.dot(x_hi, w_ref[0, d_half:, :], preferred_element_type=F32))

            hids.append((jax.nn.silu(proj(wg_ref)) * proj(wu_ref)).astype(BF16))
        for i, hid in enumerate(hids):
            y_ref[pl.ds(i * rows, rows), :] = jnp.dot(hid, wd_ref[0], preferred_element_type=F32)


def _ffn(block_expert, n_valid, xs, wg, wu, wd, bm):
    n_slots, d_half = xs.shape
    d = 2 * d_half
    de = wg.shape[2]
    live = lambda b, be, nv: (jnp.minimum(b, nv[0] - 1), 0)
    wsel = lambda b, be, nv: (be[b], 0, 0)
    return pl.pallas_call(
        _ffn_body,
        out_shape=jax.ShapeDtypeStruct((n_slots, d), F32),
        grid_spec=pltpu.PrefetchScalarGridSpec(
            num_scalar_prefetch=2, grid=(n_slots // bm,),
            in_specs=[pl.BlockSpec((bm, d_half), live), pl.BlockSpec((1, d, de), wsel),
                      pl.BlockSpec((1, d, de), wsel), pl.BlockSpec((1, de, d), wsel)],
            out_specs=pl.BlockSpec((bm, d), lambda b, be, nv: (b, 0))),
        compiler_params=_cparams(("arbitrary",)),
        name="moe_ffn",
    )(block_expert, n_valid, xs, wg, wu, wd)


def _final_body(slot_ref, x1_ref, route_ref, g_ref, y_hbm, o_ref, buf, sem, *, tm):
    i = pl.program_id(0)
    n_steps = pl.num_programs(0)

    def issue(step, par):
        def body(t, _):
            for kk in range(2):
                src = y_hbm.at[pl.ds(slot_ref[2 * (step * tm + t) + kk], 1)]
                pltpu.make_async_copy(src, buf.at[par, kk, pl.ds(t, 1)], sem.at[par]).start()
            return 0
        lax.fori_loop(0, tm, body, 0, unroll=8)

    @pl.when(i == 0)
    def _():
        issue(0, 0)

    @pl.when(i + 1 < n_steps)
    def _():
        issue(i + 1, (i + 1) % 2)

    par = i % 2
    for kk in range(2):
        _row_wait(y_hbm, buf.at[par, kk], sem.at[par], tm)
    route = route_ref[...]
    x2 = x1_ref[...] + route[:, 2:3] * buf[par, 0] + route[:, 3:4] * buf[par, 1]
    o_ref[...] = (x2 * lax.rsqrt(jnp.mean(x2 * x2, axis=-1, keepdims=True) + EPS)) * g_ref[...]


def _final(slots_flat, x1, route, g, y, tm):
    m, d = x1.shape
    assert m % tm == 0
    row = lambda i, s: (i, 0)
    return pl.pallas_call(
        functools.partial(_final_body, tm=tm),
        out_shape=jax.ShapeDtypeStruct((m, d), F32),
        grid_spec=pltpu.PrefetchScalarGridSpec(
            num_scalar_prefetch=1, grid=(m // tm,),
            in_specs=[pl.BlockSpec((tm, d), row), pl.BlockSpec((tm, LANES), row),
                      pl.BlockSpec((1, d), lambda i, s: (0, 0)), pl.BlockSpec(memory_space=pl.ANY)],
            out_specs=pl.BlockSpec((tm, d), row),
            scratch_shapes=[pltpu.VMEM((2, 2, tm, d), F32), pltpu.SemaphoreType.DMA((2,))]),
        compiler_params=_cparams(("arbitrary",), disable_bounds_checks=True),
        name="moe_combine_final_norm",
    )(slots_flat, x1, route, g, y)


def _routing_tables(route, cnt, tm, bm):
    m = route.shape[0]
    n_blocks = (2 * m) // bm + N_EXPERTS
    counts = cnt[:, 0, :N_EXPERTS].astype(jnp.int32)
    sizes = jnp.sum(counts, axis=0)
    padded = (sizes + bm - 1) // bm * bm
    pad_end = jnp.cumsum(padded)
    base = (pad_end - padded)[None, :] + jnp.cumsum(counts, axis=0) - counts
    base_tok = jnp.repeat(base, tm, axis=0)
    e = route[:, 0:2].astype(jnp.int32)
    rank = route[:, 4:6].astype(jnp.int32)
    sel = e[:, :, None] == jnp.arange(N_EXPERTS, dtype=jnp.int32)[None, None, :]
    slots = jnp.sum(jnp.where(sel, base_tok[:, None, :], 0), axis=-1) + rank
    n_valid = (pad_end[-1] // bm).astype(jnp.int32)
    blk = jnp.minimum(jnp.arange(n_blocks, dtype=jnp.int32), n_valid - 1)
    block_expert = jnp.minimum(jnp.sum(pad_end[None, :] <= (blk * bm)[:, None], axis=1), N_EXPERTS - 1)
    last_blk = jnp.where(padded > 0, pad_end - bm, -1)
    tail = n_valid + jnp.arange(N_EXPERTS, dtype=jnp.int32)
    tail = jnp.where(tail < n_blocks, tail * bm, -1)
    zero_starts = jnp.concatenate([last_blk, tail]).astype(jnp.int32)
    return slots.reshape(-1), block_expert.astype(jnp.int32), n_valid.reshape(1), zero_starts, n_blocks * bm


def _stream(x, pool_hist, k_hist, v_hist, p, *, tq, tk, group, tn, tm_in, tm_mix, tm_fin, bm):
    b, n, d = x.shape
    m = b * n
    past = 0 if k_hist is None else k_hist.shape[1]
    x2d = x.reshape(m, d)
    u, q, kb, vb, k, v, gp, gs = _inproj(x2d, p['norm_mix'], p['w_in'], _fit(m, tm_in))
    dp = u.shape[1]
    diff = _pool_diff(pool_hist, u.reshape(b, n, dp), past, tn).reshape(m, dp)
    shp = (b, n, kb.shape[1])
    if k_hist is None:
        o = _attention_self(q.reshape(shp), kb.reshape(shp), vb.reshape(shp), tq)
    else:
        o = _attention_recent_first(q.reshape(shp), kb.reshape(shp), vb.reshape(shp), k_hist, v_hist, tk, group)
    o = o.reshape(m, -1)
    x1, hp, route, cnt = _mix(diff, o, gp, gs, x2d, p['w_pool'], p['pool_scale'], p['w_sb_out'], p['w_out'],
                              p['norm_ffn'], p['w_r'], p['b_r'], tm_mix)
    slots, block_expert, n_valid, zero_starts, n_slots = _routing_tables(route, cnt, tm_mix, bm)
    xs = _dispatch(slots, zero_starts, hp, n_slots, tm_mix, bm)
    y = _ffn(block_expert, n_valid, xs, p['w_g'], p['w_u'], p['w_d'], bm)
    out = _final(slots, x1, route, p['norm_final'], y, tm_fin)
    return out.reshape(b, n, d), u.reshape(b, n, dp), k, v


def kernel(x_prompt, x_sample, cache_sb_k, cache_sb_v, state_pool, norm_mix, w_in, w_pool, pool_scale, w_sb_out,
           w_out, norm_ffn, w_router_group, b_router_group, w_router_expert, b_router_expert, w_exp_gate,
           w_exp_up, w_exp_down, norm_final):
    depth = w_in.shape[0]
    assert depth == 1
    bp, sp, d = x_prompt.shape
    bs, ss, _ = x_sample.shape
    past = cache_sb_k.shape[2]
    heads, hd = cache_sb_k.shape[3], cache_sb_k.shape[4]
    assert hd == HEAD_DIM
    dp = state_pool.shape[3]
    n_state = state_pool.shape[2]

    w_r = jnp.concatenate([w_router_group[0], w_router_expert[0]], axis=1)
    w_r = jnp.pad(w_r, ((0, 0), (0, LANES - w_r.shape[1])))
    w_r_hi = w_r.astype(BF16)
    b_r = jnp.concatenate([b_router_group[0], b_router_expert[0]])
    p = dict(
        norm_mix=norm_mix[0][None, :], w_in=w_in[0].astype(BF16), w_pool=w_pool[0].astype(BF16),
        pool_scale=pool_scale[0][None, :], w_sb_out=w_sb_out[0].astype(BF16), w_out=w_out[0].astype(BF16),
        norm_ffn=norm_ffn[0][None, :],
        w_r=jnp.concatenate([w_r_hi, (w_r - w_r_hi.astype(F32)).astype(BF16)], axis=1),
        b_r=jnp.pad(b_r, (0, LANES - b_r.shape[0]))[None, :].astype(F32),
        w_g=w_exp_gate[0].astype(BF16), w_u=w_exp_up[0].astype(BF16), w_d=w_exp_down[0].astype(BF16),
        norm_final=norm_final[None, :])

    hist_p = jnp.zeros((bp, POOL_HIST, dp), F32)
    yp, up, kp, vp = _stream(x_prompt, hist_p, None, None, p, tq=256, tk=256, group=2, tn=min(sp, 512),
                             tm_in=1024, tm_mix=256, tm_fin=256, bm=256)
    hist_s = jnp.pad(state_pool[0], ((0, 0), (POOL_HIST - n_state, 0), (0, 0)))
    ys, us, ks, vs = _stream(x_sample, hist_s, cache_sb_k[0], cache_sb_v[0], p, tq=ss, tk=256, group=2, tn=ss,
                             tm_in=1024, tm_mix=256, tm_fin=256, bm=256)

    def pool_state(hist, u):
        return jnp.concatenate([hist[:, POOL_HIST - n_state:], u], axis=1)[:, -n_state:][None]

    return (yp, ys,
            kp.reshape(1, bp, sp, heads, hd), vp.reshape(1, bp, sp, heads, hd), pool_state(hist_p, up),
            ks.reshape(1, bs, ss, heads, hd), vs.reshape(1, bs, ss, heads, hd), pool_state(hist_s, us))
```

```python
import functools

import jax
import jax.numpy as jnp
from jax import lax
from jax.experimental import pallas as pl
from jax.experimental.pallas import tpu as pltpu

F32 = jnp.float32
BF16 = jnp.bfloat16

EPS = 1e-6
HEAD_DIM = 64
LANES = 128
POOL_WINDOWS = (2, 4, 8, 16)
POOL_HIST = 16
N_GROUPS = 4
PER_GROUP = 8
N_EXPERTS = N_GROUPS * PER_GROUP
VMEM_LIMIT = 58 * 1024 * 1024
NEG = -1e30
LOG2E = 1.4426950408889634
ATTN_PAIRS = 4
MIX_PARTS = 2
FFN_PARTS = 2
SKIP_MASS = 160.0


def _fit(m, tile):
    while m % tile:
        tile //= 2
    return tile


def _cparams(sem, **kw):
    return pltpu.CompilerParams(dimension_semantics=sem, vmem_limit_bytes=VMEM_LIMIT, **kw)


def _proj_main_body(h_ref, w_ref, u_ref, q_ref, kb_ref, vb_ref, k_hbm, v_hbm, kv_buf, sem):
    i = pl.program_id(0)
    j = pl.program_id(1)
    last = pl.num_programs(0) - 1
    tm = h_ref.shape[0]
    heads = k_hbm.shape[1]

    def proj():
        return jnp.dot(h_ref[...], w_ref[...], preferred_element_type=F32)

    def head_copies(slot, dst_hbm, step):
        row0 = pl.multiple_of(step * tm, tm)
        return [pltpu.make_async_copy(kv_buf.at[slot, hh], dst_hbm.at[pl.ds(row0, tm), hh, :], sem.at[slot])
                for hh in range(heads)]

    @pl.when(j == 0)
    def _():
        u_ref[...] = proj()

    @pl.when(j == 1)
    def _():
        q_ref[...] = (proj() * (HEAD_DIM ** -0.5)).astype(BF16)

    for jj, slot, dense_ref, dst_hbm in ((2, 0, kb_ref, k_hbm), (3, 1, vb_ref, v_hbm)):
        @pl.when(j == jj)
        def _(slot=slot, dense_ref=dense_ref, dst_hbm=dst_hbm):
            @pl.when(i > 0)
            def _():
                for c in head_copies(slot, dst_hbm, i - 1):
                    c.wait()
            acc = proj()
            dense_ref[...] = acc.astype(BF16)
            for hh in range(heads):
                kv_buf[slot, hh] = acc[:, hh * HEAD_DIM:(hh + 1) * HEAD_DIM]
            for c in head_copies(slot, dst_hbm, i):
                c.start()

    @pl.when((i == last) & (j == 3))
    def _():
        for slot, dst_hbm in ((0, k_hbm), (1, v_hbm)):
            for c in head_copies(slot, dst_hbm, i):
                c.wait()


def _proj_gates_body(x_ref, g_ref, w_ref, h_ref, gp_ref, gs_ref, *, tn):
    j = pl.program_id(1)

    @pl.when(j == 0)
    def _():
        x = x_ref[...]
        r = lax.rsqrt(jnp.mean(x * x, axis=-1, keepdims=True) + EPS)
        h_ref[...] = ((x * r) * g_ref[...]).astype(BF16)

    for jj, ref in ((0, gp_ref), (2, gs_ref)):
        for half in range(2):
            @pl.when(j == jj + half)
            def _(ref=ref, half=half):
                a = jnp.dot(h_ref[...], w_ref[...], preferred_element_type=F32)
                ref[:, half * tn:(half + 1) * tn] = (0.5 * jnp.tanh(0.5 * a) + 0.5).astype(BF16)


def _inproj(x, g, w_bf, tm):
    m, d = x.shape
    tn = d // 2
    assert w_bf.shape == (d, 8 * tn) and m % tm == 0
    row = lambda i, j: (i, 0)
    h_spec = pl.BlockSpec((tm, d), row)
    h, gp, gs = pl.pallas_call(
        functools.partial(_proj_gates_body, tn=tn),
        out_shape=[jax.ShapeDtypeStruct((m, d), BF16)] * 3,
        grid=(m // tm, 4),
        in_specs=[h_spec, pl.BlockSpec((1, d), lambda i, j: (0, 0)),
                  pl.BlockSpec((d, tn), lambda i, j: (0, j + 4))],
        out_specs=[h_spec] * 3,
        compiler_params=_cparams(("parallel", "arbitrary")),
        name="inproj_gates",
    )(x, g, w_bf)
    dense = lambda dt: jax.ShapeDtypeStruct((m, tn), dt)
    cache = jax.ShapeDtypeStruct((m, tn // HEAD_DIM, HEAD_DIM), F32)
    any_spec = pl.BlockSpec(memory_space=pl.ANY)
    u, q, kb, vb, k, v = pl.pallas_call(
        _proj_main_body,
        out_shape=[dense(F32), dense(BF16), dense(BF16), dense(BF16), cache, cache],
        grid=(m // tm, 4),
        in_specs=[h_spec, pl.BlockSpec((d, tn), lambda i, j: (0, j))],
        out_specs=[pl.BlockSpec((tm, tn), row)] * 4 + [any_spec] * 2,
        scratch_shapes=[pltpu.VMEM((2, tn // HEAD_DIM, tm, HEAD_DIM), F32), pltpu.SemaphoreType.DMA((2,))],
        compiler_params=_cparams(("arbitrary", "arbitrary"), has_side_effects=True),
        name="inproj_main",
    )(h, w_bf)
    return u, q, kb, vb, k, v, gp, gs


def _pool_body(hist_ref, u_ref, o_ref, ext_scr, *, pos0, tn):
    s = pl.program_id(1)

    @pl.when(s == 0)
    def _():
        ext_scr[0:POOL_HIST, :] = hist_ref[0]

    ext_scr[POOL_HIST:POOL_HIST + tn, :] = u_ref[0]
    pos = pos0 + s * tn + lax.broadcasted_iota(jnp.int32, (tn, 1), 0)
    group = u_ref.shape[2] // len(POOL_WINDOWS)
    for g, w in enumerate(POOL_WINDOWS):
        lo, hi = g * group, (g + 1) * group
        cur = ext_scr[POOL_HIST:POOL_HIST + tn, lo:hi]
        tot = cur
        for dlt in range(1, w):
            tot = tot + ext_scr[POOL_HIST - dlt:POOL_HIST - dlt + tn, lo:hi]
        cnt = jnp.minimum(pos + 1, w).astype(F32)
        o_ref[0, :, lo:hi] = (tot / cnt - cur).astype(BF16)
    ext_scr[0:POOL_HIST, :] = ext_scr[tn:tn + POOL_HIST, :]


def _pool_diff(hist, u, pos0, tn):
    b, n, dp = u.shape
    assert n % tn == 0 and hist.shape == (b, POOL_HIST, dp)
    return pl.pallas_call(
        functools.partial(_pool_body, pos0=pos0, tn=tn),
        out_shape=jax.ShapeDtypeStruct((b, n, dp), BF16),
        grid=(b, n // tn),
        in_specs=[pl.BlockSpec((1, POOL_HIST, dp), lambda i, s: (i, 0, 0)),
                  pl.BlockSpec((1, tn, dp), lambda i, s: (i, s, 0))],
        out_specs=pl.BlockSpec((1, tn, dp), lambda i, s: (i, s, 0)),
        scratch_shapes=[pltpu.VMEM((POOL_HIST + tn, dp), F32)],
        compiler_params=_cparams(("parallel", "arbitrary")),
        name="pool_diff",
    )(hist, u)


def _softplus2(z2):
    neg_abs = pltpu.bitcast(pltpu.bitcast(z2, jnp.uint32) | jnp.uint32(0x80000000), F32)
    return jnp.maximum(z2, 0.0) + jnp.log2(1.0 + jnp.exp2(neg_abs))


def _suffix_matrix(n):
    r = lax.broadcasted_iota(jnp.int32, (n, n), 0)
    c = lax.broadcasted_iota(jnp.int32, (n, n), 1)
    return jnp.where(r >= c, 1.0, 0.0).astype(BF16)


def _stack_heads(q2):
    lane = lax.broadcasted_iota(jnp.int32, q2.shape, 1)
    zero = jnp.zeros_like(q2)
    return jnp.concatenate([jnp.where(lane < HEAD_DIM, q2, zero), jnp.where(lane >= HEAD_DIM, q2, zero)], axis=0)


def _unstack_heads(acc, t):
    lane = lax.broadcasted_iota(jnp.int32, (t, LANES), 1)
    return jnp.where(lane < HEAD_DIM, acc[:t], acc[t:])


def _causal_mask(t):
    r = lax.broadcasted_iota(jnp.int32, (t, t), 0)
    c = lax.broadcasted_iota(jnp.int32, (t, t), 1)
    m = c < r
    return jnp.concatenate([m, m], axis=0)


def _sb_group(q_st, k_blocks, v_blocks, suffix, carry, masks, transposed_keys):
    (out, carry), = _sb_groups([q_st], [k_blocks], [v_blocks], suffix, [carry], masks, transposed_keys)
    return out, carry


def _sb_groups(q_sts, k_blocks, v_blocks, suffix, carries, masks, transposed_keys):
    dn = (((1,), (0,)), ((), ())) if transposed_keys else (((1,), (1,)), ((), ()))
    zs = [[lax.dot_general(q_st, kb, dn, preferred_element_type=F32) * LOG2E for kb in kbs]
          for q_st, kbs in zip(q_sts, k_blocks)]
    n_blk = len(masks)
    cs = _sb_masses([z for zq in zs for z in zq], masks * len(q_sts), [suffix] * (n_blk * len(q_sts)))
    results = []
    for i, (zq, vbs, carry) in enumerate(zip(zs, v_blocks, carries)):
        out = None
        for j, (z, vb, mask) in enumerate(zip(zq, vbs, masks)):
            a, carry = _sb_weights(z, cs[i * n_blk + j], carry, mask)
            o = jnp.dot(a, vb, preferred_element_type=F32)
            out = o if out is None else out + o
        results.append((out, carry))
    return results


def _sb_masses(zs, masks, suffixes):
    splits = []
    for z, mask in zip(zs, masks):
        sp = _softplus2(z)
        if mask is not None:
            sp = jnp.where(mask, sp, 0.0)
        hi = pltpu.bitcast(pltpu.bitcast(sp, jnp.uint32) & jnp.uint32(0xFFFF0000), F32)
        splits.append((hi.astype(BF16), (sp - hi).astype(BF16)))
    return [jnp.dot(hi, sfx, preferred_element_type=F32) + jnp.dot(lo, sfx, preferred_element_type=F32)
            for (hi, lo), sfx in zip(splits, suffixes)]


def _sb_weights(z, c, carry, mask):
    arg = z - c - carry
    if mask is not None:
        arg = jnp.where(mask, arg, NEG)
    return jnp.exp2(arg).astype(BF16), carry + c[:, 0:1]


def _attn_self_body(q_ref, k_ref, v_ref, o_ref, kt_scr, v_scr, acc_scr, car_scr, min_scr, *, t):
    qi = pl.program_id(2)
    n_pairs = kt_scr.shape[0]
    lanes = lambda p: slice(p * LANES, (p + 1) * LANES)

    @pl.when(qi == 0)
    def _fill():
        def body(c, _):
            row0 = pl.multiple_of(c * t, t)
            for p in range(n_pairs):
                kt_scr[p, c] = k_ref[0, pl.ds(row0, t), lanes(p)].astype(F32).T.astype(BF16)
                v_scr[p, c] = v_ref[0, pl.ds(row0, t), lanes(p)].astype(BF16)
            return 0
        lax.fori_loop(0, kt_scr.shape[1], body, 0)

    q_sts = [_stack_heads(q_ref[0, :, lanes(p)]) for p in range(n_pairs)]
    suffix = _suffix_matrix(t)
    mask = _causal_mask(t)

    def run(blocks, masks, first):
        carries = [jnp.zeros((2 * t, 1), F32) if first else car_scr[p, :, 0:1] for p in range(n_pairs)]
        results = _sb_groups(q_sts, [[kt_scr[p, b] for b in blocks] for p in range(n_pairs)],
                             [[v_scr[p, b] for b in blocks] for p in range(n_pairs)], suffix, carries, masks, True)
        low = None
        for p, (out, carry) in enumerate(results):
            acc_scr[p] = out if first else acc_scr[p] + out
            car_scr[p] = jnp.broadcast_to(carry, car_scr.shape[1:])
            low = jnp.min(carry) if low is None else jnp.minimum(low, jnp.min(carry))
        min_scr[0] = low

    @pl.when(qi == 0)
    def _():
        run([0], [mask], True)

    @pl.when(qi > 0)
    def _():
        run([qi, qi - 1], [mask, None], True)

    rest = jnp.maximum(qi - 1, 0)

    def more(it):
        return (it < rest // 2) & (min_scr[0] < SKIP_MASS)

    def pair(it):
        b0 = qi - 2 - 2 * it
        run([b0, b0 - 1], [None, None], False)
        return it + 1

    lax.while_loop(more, pair, 0)

    @pl.when((rest % 2 == 1) & (min_scr[0] < SKIP_MASS))
    def _():
        run([0], [None], False)

    o_ref[0] = jnp.concatenate([_unstack_heads(acc_scr[p], t) for p in range(n_pairs)],
                               axis=1).astype(o_ref.dtype)


def _attn_hist_body(q_ref, kn_ref, vn_ref, kh_ref, vh_ref, o_ref, left_ref, acc_scr, car_scr, min_scr, *, tk,
                    group):
    t = q_ref.shape[1]
    past = kh_ref.shape[1]
    q_st = _stack_heads(q_ref[0])

    def keep(out, carry, first):
        acc_scr[...] = out if first else acc_scr[...] + out
        car_scr[...] = jnp.broadcast_to(carry, car_scr.shape)
        min_scr[0] = jnp.min(carry)

    keep(*_sb_group(q_st, [kn_ref[0].astype(BF16)], [vn_ref[0].astype(BF16)], _suffix_matrix(t),
                    jnp.zeros((2 * t, 1), F32), [_causal_mask(t)], False), True)
    suffix = _suffix_matrix(tk)
    for top in range(past // tk, 0, -group):
        @pl.when(min_scr[0] < SKIP_MASS)
        def _(top=top):
            blocks = range(top - 1, top - 1 - group, -1)
            keep(*_sb_group(q_st, [kh_ref[0, b * tk:(b + 1) * tk, :].astype(BF16) for b in blocks],
                            [vh_ref[0, b * tk:(b + 1) * tk, :].astype(BF16) for b in blocks],
                            suffix, car_scr[:, 0:1], [None] * group, False), False)
    o_ref[0] = _unstack_heads(acc_scr[...], t).astype(o_ref.dtype)
    left_ref[0, 0] = jnp.full(left_ref.shape[2:], min_scr[0], F32)


def _attn_recent_body(q_ref, kn_ref, vn_ref, kc_ref, vc_ref, o_ref, left_ref, *, tk):
    t, dm = q_ref.shape[1:]
    n_blk = kc_ref.shape[1] // tk
    pair = lambda ref, rows, p: ref[0, rows, p * LANES:(p + 1) * LANES].astype(BF16)
    score = lambda qs, kb: lax.dot_general(qs, kb, (((1,), (1,)), ((), ())), preferred_element_type=F32)
    own = slice(None)
    blocks = [pl.ds(j * tk, tk) for j in range(n_blk - 1, -1, -1)]
    mask, sfx_new, sfx_old = _causal_mask(t), _suffix_matrix(t), _suffix_matrix(tk)
    zs, masks, sfx = [], [], []
    for p in range(dm // LANES):
        q_st = _stack_heads(pair(q_ref, own, p))
        zs.append(score(q_st, pair(kn_ref, own, p)) * LOG2E)
        masks.append(mask)
        sfx.append(sfx_new)
        for rows in blocks:
            zs.append(score(q_st, pair(kc_ref, rows, p)) * LOG2E)
            masks.append(None)
            sfx.append(sfx_old)
    cs = _sb_masses(zs, masks, sfx)
    outs, left = [], None
    chain = 1 + n_blk
    for p in range(dm // LANES):
        carry = jnp.zeros((2 * t, 1), F32)
        out = None
        for i in range(chain):
            a, carry = _sb_weights(zs[p * chain + i], cs[p * chain + i], carry, masks[p * chain + i])
            vb = pair(vn_ref, own, p) if i == 0 else pair(vc_ref, blocks[i - 1], p)
            o = jnp.dot(a, vb, preferred_element_type=F32)
            out = o if out is None else out + o
        outs.append(_unstack_heads(out, t))
        low = jnp.min(carry)
        left = low if left is None else jnp.minimum(left, low)
    o_ref[0] = jnp.concatenate(outs, axis=1).astype(o_ref.dtype)
    left_ref[0] = jnp.full(left_ref.shape[1:], left, F32)


def _attn_specs(t):
    tile = pl.BlockSpec((1, t, LANES), lambda i, p, s: (i, s, p))
    seq = lambda rows: pl.BlockSpec((1, rows, LANES), lambda i, p, s: (i, 0, p))
    state = [pltpu.VMEM((2 * t, LANES), F32), pltpu.VMEM((2 * t, LANES), F32), pltpu.SMEM((1,), F32)]
    return tile, seq, state


def _attention_self(q, k, v, t, pairs):
    b, n, dm = q.shape
    width = pairs * LANES
    assert n % t == 0 and dm % width == 0
    tile = pl.BlockSpec((1, t, width), lambda i, p, s: (i, s, p))
    seq = pl.BlockSpec((1, n, width), lambda i, p, s: (i, 0, p))
    return pl.pallas_call(
        functools.partial(_attn_self_body, t=t),
        out_shape=jax.ShapeDtypeStruct((b, n, dm), BF16),
        grid=(b, dm // width, n // t),
        in_specs=[tile, seq, seq],
        out_specs=tile,
        scratch_shapes=[pltpu.VMEM((pairs, n // t, LANES, t), BF16), pltpu.VMEM((pairs, n // t, t, LANES), BF16),
                        pltpu.VMEM((pairs, 2 * t, LANES), F32), pltpu.VMEM((pairs, 2 * t, LANES), F32),
                        pltpu.SMEM((1,), F32)],
        compiler_params=_cparams(("parallel", "parallel", "arbitrary")),
        name="sb_attention",
    )(q, k, v)


def _attention_cached(q, k_new, v_new, k_hist, v_hist, tk, group):
    b, t, dm = q.shape
    past = k_hist.shape[1]
    assert dm % LANES == 0 and past % (tk * group) == 0
    tile, seq, state = _attn_specs(t)
    return pl.pallas_call(
        functools.partial(_attn_hist_body, tk=tk, group=group),
        out_shape=[jax.ShapeDtypeStruct((b, t, dm), BF16), jax.ShapeDtypeStruct((b, dm // LANES, 8, LANES), F32)],
        grid=(b, dm // LANES, 1),
        in_specs=[tile, seq(t), seq(t), seq(past), seq(past)],
        out_specs=[tile, pl.BlockSpec((1, 1, 8, LANES), lambda i, p, s: (i, p, 0, 0))],
        scratch_shapes=state,
        compiler_params=_cparams(("parallel", "parallel", "arbitrary")),
        name="sb_attention_cached",
    )(q, k_new, v_new, k_hist, v_hist)


def _attention_recent_first(q, k_new, v_new, cache_k, cache_v, tk, group):
    b, past, heads, hd = cache_k.shape
    t, dm = q.shape[1:]
    flat = lambda c: c.reshape(b, c.shape[1], heads * hd)
    walk = lambda: _attention_cached(q, k_new, v_new, flat(cache_k), flat(cache_v), tk, group)[0]
    recent = tk * group
    if past <= recent:
        return walk()
    seq = lambda rows: pl.BlockSpec((1, rows, dm), lambda i: (i, 0, 0))
    o, left = pl.pallas_call(
        functools.partial(_attn_recent_body, tk=tk),
        out_shape=[jax.ShapeDtypeStruct((b, t, dm), BF16), jax.ShapeDtypeStruct((b, 8, LANES), F32)],
        grid=(b,),
        in_specs=[seq(t), seq(t), seq(t), seq(recent), seq(recent)],
        out_specs=[seq(t), pl.BlockSpec((1, 8, LANES), lambda i: (i, 0, 0))],
        compiler_params=_cparams(("parallel",)),
        name="sb_attention_recent",
    )(q, k_new, v_new, flat(cache_k[:, past - recent:]), flat(cache_v[:, past - recent:]))
    return lax.cond(jnp.min(left) < SKIP_MASS, walk, lambda: o)


def _mix_body(diff_ref, o_ref, gp_ref, gs_ref, x_ref, wp_ref, ps_ref, wsb_ref, wo_ref, nf_ref,
              wr_ref, br_ref, x1_ref, hp_ref, route_ref, cnt_ref, logit_scr, *, tm):
    @pl.when(pl.program_id(0) == 0)
    def _():
        logit_scr[...] = jnp.zeros_like(logit_scr)

    late_logits = logit_scr[...]

    n_pool = wp_ref.shape[0]
    group = wp_ref.shape[1]
    rows = tm // MIX_PARTS
    parts = [pl.ds(i * rows, rows) for i in range(MIX_PARTS)]
    pools = [jnp.concatenate(
        [jnp.dot(diff_ref[s, g * group:(g + 1) * group], wp_ref[g], preferred_element_type=F32)
         for g in range(n_pool)], axis=-1) * ps_ref[...] for s in parts]
    sbs = [jnp.dot(o_ref[s, :], wsb_ref[...], preferred_element_type=F32) for s in parts]
    mixed = [(gp_ref[s, :].astype(F32) * pool + gs_ref[s, :].astype(F32) * sb).astype(BF16)
             for s, pool, sb in zip(parts, pools, sbs)]
    x1s = [x_ref[s, :] + jnp.dot(mx, wo_ref[...], preferred_element_type=F32) for s, mx in zip(parts, mixed)]
    splits = []
    for s, x1 in zip(parts, x1s):
        x1_ref[s, :] = x1
        h = (x1 * lax.rsqrt(jnp.mean(x1 * x1, axis=-1, keepdims=True) + EPS)) * nf_ref[...]
        d_half = h.shape[1] // 2
        lo_bits = pltpu.bitcast(h[:, :d_half].astype(BF16).astype(F32), jnp.uint32)
        hi_bits = pltpu.bitcast(h[:, d_half:].astype(BF16).astype(F32), jnp.uint32)
        hp_ref[s, :] = (lo_bits >> 16) | (hi_bits & jnp.uint32(0xFFFF0000))
        hh = h.astype(BF16)
        splits.append(jnp.concatenate([hh, (h - hh.astype(F32)).astype(BF16)], axis=0))

    rs = [jnp.dot(sp, wr_ref[...], preferred_element_type=F32) for sp in splits]
    logit_scr[...] = jnp.concatenate(
        [(r[:rows, :LANES] + r[:rows, LANES:]) + (r[rows:, :LANES] + r[rows:, LANES:]) for r in rs],
        axis=0) + br_ref[...]
    _route_tile(late_logits, route_ref, cnt_ref, tm)


def _route_tile(logits, route_ref, cnt_ref, tm):
    lane = lax.broadcasted_iota(jnp.int32, (tm, LANES), 1)
    big = jnp.int32(LANES)

    def first_max(vals):
        m = jnp.max(vals, axis=-1, keepdims=True)
        idx = jnp.min(jnp.where(vals == m, lane, big), axis=-1, keepdims=True)
        return m, idx

    gl = jnp.where(lane < N_GROUPS, logits, NEG)
    gmax, grp = first_max(gl)
    p_grp = 1.0 / jnp.sum(jnp.exp(gl - gmax), axis=-1, keepdims=True)
    e_lo = N_GROUPS + grp * PER_GROUP
    el = jnp.where((lane >= e_lo) & (lane < e_lo + PER_GROUP), logits, NEG)
    m1, i1 = first_max(el)
    m2, i2 = first_max(jnp.where(lane == i1, NEG, el))
    t2 = jnp.exp(m2 - m1)
    w1 = p_grp / (1.0 + t2)
    w2 = w1 * t2
    e1 = i1 - N_GROUPS
    e2 = i2 - N_GROUPS

    oh1 = jnp.where(lane == e1, 1.0, 0.0).astype(BF16)
    oh2 = jnp.where(lane == e2, 1.0, 0.0).astype(BF16)
    rr = lax.broadcasted_iota(jnp.int32, (tm, tm), 0)
    cc = lax.broadcasted_iota(jnp.int32, (tm, tm), 1)
    before = jnp.where(cc < rr, 1.0, 0.0).astype(BF16)
    ones = jnp.ones((8, tm), BF16)
    pre1 = jnp.dot(before, oh1, preferred_element_type=F32)
    pre2 = jnp.dot(before, oh2, preferred_element_type=F32)
    c1 = jnp.dot(ones, oh1, preferred_element_type=F32)
    c2 = jnp.dot(ones, oh2, preferred_element_type=F32)
    rank1 = jnp.sum(jnp.where(lane == e1, pre1, 0.0), axis=-1, keepdims=True)
    rank2 = jnp.sum(jnp.where(lane == e2, pre2 + c1[0:1, :], 0.0), axis=-1, keepdims=True)
    cnt_ref[0] = c1 + c2

    route = jnp.where(lane == 0, e1.astype(F32), 0.0)
    route = jnp.where(lane == 1, e2.astype(F32), route)
    route = jnp.where(lane == 2, w1, route)
    route = jnp.where(lane == 3, w2, route)
    route = jnp.where(lane == 4, rank1, route)
    route = jnp.where(lane == 5, rank2, route)
    route_ref[...] = route


def _mix(diff, o, gp, gs, x, wp, ps, wsb, wo, nf, wr, br, tm):
    m, d = x.shape
    assert m % tm == 0
    n = m // tm
    row = lambda i: (jnp.minimum(i, n - 1), 0)
    late = lambda i: (jnp.maximum(i - 1, 0), 0)
    full = lambda a: pl.BlockSpec(a.shape, lambda i: (0,) * a.ndim, pipeline_mode=pl.Buffered(1))
    acts = [diff, o, gp, gs, x]
    consts = [wp, ps, wsb, wo, nf, wr, br]
    return pl.pallas_call(
        functools.partial(_mix_body, tm=tm),
        out_shape=[jax.ShapeDtypeStruct((m, d), F32), jax.ShapeDtypeStruct((m, d // 2), jnp.uint32),
                   jax.ShapeDtypeStruct((m, LANES), F32), jax.ShapeDtypeStruct((n, 8, LANES), F32)],
        grid=(n + 1,),
        in_specs=[pl.BlockSpec((tm, a.shape[1]), row) for a in acts] + [full(a) for a in consts],
        out_specs=[pl.BlockSpec((tm, d), row), pl.BlockSpec((tm, d // 2), row),
                   pl.BlockSpec((tm, LANES), late), pl.BlockSpec((1, 8, LANES), lambda i: late(i) + (0,))],
        scratch_shapes=[pltpu.VMEM((tm, LANES), F32)],
        compiler_params=_cparams(("arbitrary",)),
        name="mix_outproj_router",
    )(*acts, *consts)


def _row_wait(src_ref, dst_ref, sem, n_rows):
    pltpu.make_async_copy(src_ref.at[pl.ds(0, n_rows)], dst_ref.at[pl.ds(0, n_rows)], sem).wait()


def _dispatch_body(slot_ref, zero_ref, hp_ref, xs_ref, buf, zbuf, sem, zsem, *, tm):
    i = pl.program_id(0)
    last = pl.num_programs(0) - 1
    par = i % 2

    @pl.when(i == 0)
    def _():
        zbuf[...] = jnp.zeros_like(zbuf)
        bm = zbuf.shape[0]
        for wait in (False, True):
            for z in range(zero_ref.shape[0]):
                @pl.when(zero_ref[z] >= 0)
                def _(z=z, wait=wait):
                    start = pl.multiple_of(jnp.maximum(zero_ref[z], 0), bm)
                    copy = pltpu.make_async_copy(zbuf, xs_ref.at[pl.ds(start, bm)], zsem)
                    copy.wait() if wait else copy.start()

    def drain(slot):
        for _ in range(2):
            _row_wait(buf.at[slot], xs_ref, sem.at[slot], tm)

    @pl.when(i >= 2)
    def _():
        drain(par)

    buf[par] = hp_ref[...]
    base = i * tm

    def body(t, _):
        src = buf.at[par, pl.ds(t, 1)]
        for kk in range(2):
            dst = xs_ref.at[pl.ds(slot_ref[2 * (base + t) + kk], 1)]
            pltpu.make_async_copy(src, dst, sem.at[par]).start()
        return 0

    lax.fori_loop(0, tm, body, 0, unroll=8)

    @pl.when(i == last)
    def _():
        drain(par)

        @pl.when(i >= 1)
        def _():
            drain(1 - par)


def _dispatch(slots_flat, zero_starts, hp, n_slots, tm, bm):
    m, dh = hp.shape
    assert m % tm == 0
    return pl.pallas_call(
        functools.partial(_dispatch_body, tm=tm),
        out_shape=jax.ShapeDtypeStruct((n_slots, dh), hp.dtype),
        grid_spec=pltpu.PrefetchScalarGridSpec(
            num_scalar_prefetch=2, grid=(m // tm,),
            in_specs=[pl.BlockSpec((tm, dh), lambda i, s, z: (i, 0))],
            out_specs=pl.BlockSpec(memory_space=pl.ANY),
            scratch_shapes=[pltpu.VMEM((2, tm, dh), hp.dtype), pltpu.VMEM((bm, dh), hp.dtype),
                            pltpu.SemaphoreType.DMA((2,)), pltpu.SemaphoreType.DMA]),
        compiler_params=_cparams(("arbitrary",), disable_bounds_checks=True, has_side_effects=True),
        name="moe_dispatch",
    )(slots_flat, zero_starts, hp)


def _ffn_body(be_ref, nv_ref, xs_ref, wg_ref, wu_ref, wd_ref, y_ref):
    del be_ref

    @pl.when(pl.program_id(0) >= nv_ref[0])
    def _():
        y_ref[...] = jnp.zeros_like(y_ref)

    @pl.when(pl.program_id(0) < nv_ref[0])
    def _():
        d_half = xs_ref.shape[1]
        rows = xs_ref.shape[0] // FFN_PARTS
        hids = []
        for i in range(FFN_PARTS):
            words = xs_ref[pl.ds(i * rows, rows), :]
            x_lo = pltpu.bitcast(words << 16, F32).astype(BF16)
            x_hi = pltpu.bitcast(words & jnp.uint32(0xFFFF0000), F32).astype(BF16)

            def proj(w_ref):
                return (jnp.dot(x_lo, w_ref[0, :d_half, :], preferred_element_type=F32)
                        + jnp.dot(x_hi, w_ref[0, d_half:, :], preferred_element_type=F32))

            hids.append((jax.nn.silu(proj(wg_ref)) * proj(wu_ref)).astype(BF16))
        for i, hid in enumerate(hids):
            y_ref[pl.ds(i * rows, rows), :] = jnp.dot(hid, wd_ref[0], preferred_element_type=F32)


def _ffn(block_expert, n_valid, xs, wg, wu, wd, bm):
    n_slots, d_half = xs.shape
    d = 2 * d_half
    de = wg.shape[2]
    live = lambda b, be, nv: (jnp.minimum(b, nv[0] - 1), 0)
    wsel = lambda b, be, nv: (be[b], 0, 0)
    return pl.pallas_call(
        _ffn_body,
        out_shape=jax.ShapeDtypeStruct((n_slots, d), F32),
        grid_spec=pltpu.PrefetchScalarGridSpec(
            num_scalar_prefetch=2, grid=(n_slots // bm,),
            in_specs=[pl.BlockSpec((bm, d_half), live), pl.BlockSpec((1, d, de), wsel),
                      pl.BlockSpec((1, d, de), wsel), pl.BlockSpec((1, de, d), wsel)],
            out_specs=pl.BlockSpec((bm, d), lambda b, be, nv: (b, 0))),
        compiler_params=_cparams(("arbitrary",)),
        name="moe_ffn",
    )(block_expert, n_valid, xs, wg, wu, wd)


def _final_body(slot_ref, x1_ref, route_ref, g_ref, y_hbm, o_ref, buf, sem, *, tm):
    i = pl.program_id(0)
    n_steps = pl.num_programs(0)

    def issue(step, par):
        def body(t, _):
            for kk in range(2):
                src = y_hbm.at[pl.ds(slot_ref[2 * (step * tm + t) + kk], 1)]
                pltpu.make_async_copy(src, buf.at[par, kk, pl.ds(t, 1)], sem.at[par]).start()
            return 0
        lax.fori_loop(0, tm, body, 0, unroll=8)

    @pl.when(i == 0)
    def _():
        issue(0, 0)

    @pl.when(i + 1 < n_steps)
    def _():
        issue(i + 1, (i + 1) % 2)

    par = i % 2
    for kk in range(2):
        _row_wait(y_hbm, buf.at[par, kk], sem.at[par], tm)
    route = route_ref[...]
    x2 = x1_ref[...] + route[:, 2:3] * buf[par, 0] + route[:, 3:4] * buf[par, 1]
    o_ref[...] = (x2 * lax.rsqrt(jnp.mean(x2 * x2, axis=-1, keepdims=True) + EPS)) * g_ref[...]


def _final(slots_flat, x1, route, g, y, tm):
    m, d = x1.shape
    assert m % tm == 0
    row = lambda i, s: (i, 0)
    return pl.pallas_call(
        functools.partial(_final_body, tm=tm),
        out_shape=jax.ShapeDtypeStruct((m, d), F32),
        grid_spec=pltpu.PrefetchScalarGridSpec(
            num_scalar_prefetch=1, grid=(m // tm,),
            in_specs=[pl.BlockSpec((tm, d), row), pl.BlockSpec((tm, LANES), row),
                      pl.BlockSpec((1, d), lambda i, s: (0, 0)), pl.BlockSpec(memory_space=pl.ANY)],
            out_specs=pl.BlockSpec((tm, d), row),
            scratch_shapes=[pltpu.VMEM((2, 2, tm, d), F32), pltpu.SemaphoreType.DMA((2,))]),
        compiler_params=_cparams(("arbitrary",), disable_bounds_checks=True),
        name="moe_combine_final_norm",
    )(slots_flat, x1, route, g, y)


def _routing_tables(route, cnt, tm, bm):
    m = route.shape[0]
    n_blocks = (2 * m) // bm + N_EXPERTS
    counts = cnt[:, 0, :N_EXPERTS].astype(jnp.int32)
    sizes = jnp.sum(counts, axis=0)
    padded = (sizes + bm - 1) // bm * bm
    pad_end = jnp.cumsum(padded)
    base = (pad_end - padded)[None, :] + jnp.cumsum(counts, axis=0) - counts
    base_tok = jnp.repeat(base, tm, axis=0)
    e = route[:, 0:2].astype(jnp.int32)
    rank = route[:, 4:6].astype(jnp.int32)
    sel = e[:, :, None] == jnp.arange(N_EXPERTS, dtype=jnp.int32)[None, None, :]
    slots = jnp.sum(jnp.where(sel, base_tok[:, None, :], 0), axis=-1) + rank
    n_valid = (pad_end[-1] // bm).astype(jnp.int32)
    blk = jnp.minimum(jnp.arange(n_blocks, dtype=jnp.int32), n_valid - 1)
    block_expert = jnp.minimum(jnp.sum(pad_end[None, :] <= (blk * bm)[:, None], axis=1), N_EXPERTS - 1)
    last_blk = jnp.where(padded > 0, pad_end - bm, -1)
    tail = n_valid + jnp.arange(N_EXPERTS, dtype=jnp.int32)
    tail = jnp.where(tail < n_blocks, tail * bm, -1)
    zero_starts = jnp.concatenate([last_blk, tail]).astype(jnp.int32)
    return slots.reshape(-1), block_expert.astype(jnp.int32), n_valid.reshape(1), zero_starts, n_blocks * bm


def _stream(x, pool_hist, k_hist, v_hist, p, *, tq, tk, group, tn, tm_in, tm_mix, tm_fin, bm):
    b, n, d = x.shape
    m = b * n
    past = 0 if k_hist is None else k_hist.shape[1]
    x2d = x.reshape(m, d)
    u, q, kb, vb, k, v, gp, gs = _inproj(x2d, p['norm_mix'], p['w_in'], _fit(m, tm_in))
    dp = u.shape[1]
    diff = _pool_diff(pool_hist, u.reshape(b, n, dp), past, tn).reshape(m, dp)
    shp = (b, n, kb.shape[1])
    if k_hist is None:
        o = _attention_self(q.reshape(shp), kb.reshape(shp), vb.reshape(shp), tq, ATTN_PAIRS)
    else:
        o = _attention_recent_first(q.reshape(shp), kb.reshape(shp), vb.reshape(shp), k_hist, v_hist, tk, group)
    o = o.reshape(m, -1)
    x1, hp, route, cnt = _mix(diff, o, gp, gs, x2d, p['w_pool'], p['pool_scale'], p['w_sb_out'], p['w_out'],
                              p['norm_ffn'], p['w_r'], p['b_r'], tm_mix)
    slots, block_expert, n_valid, zero_starts, n_slots = _routing_tables(route, cnt, tm_mix, bm)
    xs = _dispatch(slots, zero_starts, hp, n_slots, tm_mix, bm)
    y = _ffn(block_expert, n_valid, xs, p['w_g'], p['w_u'], p['w_d'], bm)
    out = _final(slots, x1, route, p['norm_final'], y, tm_fin)
    return out.reshape(b, n, d), u.reshape(b, n, dp), k, v


def kernel(x_prompt, x_sample, cache_sb_k, cache_sb_v, state_pool, norm_mix, w_in, w_pool, pool_scale, w_sb_out,
           w_out, norm_ffn, w_router_group, b_router_group, w_router_expert, b_router_expert, w_exp_gate,
           w_exp_up, w_exp_down, norm_final):
    depth = w_in.shape[0]
    assert depth == 1
    bp, sp, d = x_prompt.shape
    bs, ss, _ = x_sample.shape
    past = cache_sb_k.shape[2]
    heads, hd = cache_sb_k.shape[3], cache_sb_k.shape[4]
    assert hd == HEAD_DIM
    dp = state_pool.shape[3]
    n_state = state_pool.shape[2]

    w_r = jnp.concatenate([w_router_group[0], w_router_expert[0]], axis=1)
    w_r = jnp.pad(w_r, ((0, 0), (0, LANES - w_r.shape[1])))
    w_r_hi = w_r.astype(BF16)
    b_r = jnp.concatenate([b_router_group[0], b_router_expert[0]])
    p = dict(
        norm_mix=norm_mix[0][None, :], w_in=w_in[0].astype(BF16), w_pool=w_pool[0].astype(BF16),
        pool_scale=pool_scale[0][None, :], w_sb_out=w_sb_out[0].astype(BF16), w_out=w_out[0].astype(BF16),
        norm_ffn=norm_ffn[0][None, :],
        w_r=jnp.concatenate([w_r_hi, (w_r - w_r_hi.astype(F32)).astype(BF16)], axis=1),
        b_r=jnp.pad(b_r, (0, LANES - b_r.shape[0]))[None, :].astype(F32),
        w_g=w_exp_gate[0].astype(BF16), w_u=w_exp_up[0].astype(BF16), w_d=w_exp_down[0].astype(BF16),
        norm_final=norm_final[None, :])

    hist_p = jnp.zeros((bp, POOL_HIST, dp), F32)
    yp, up, kp, vp = _stream(x_prompt, hist_p, None, None, p, tq=256, tk=256, group=2, tn=min(sp, 512),
                             tm_in=1024, tm_mix=256, tm_fin=256, bm=256)
    hist_s = jnp.pad(state_pool[0], ((0, 0), (POOL_HIST - n_state, 0), (0, 0)))
    ys, us, ks, vs = _stream(x_sample, hist_s, cache_sb_k[0], cache_sb_v[0], p, tq=ss, tk=256, group=2, tn=ss,
                             tm_in=1024, tm_mix=256, tm_fin=256, bm=256)

    def pool_state(hist, u):
        return jnp.concatenate([hist[:, POOL_HIST - n_state:], u], axis=1)[:, -n_state:][None]

    return (yp, ys,
            kp.reshape(1, bp, sp, heads, hd), vp.reshape(1, bp, sp, heads, hd), pool_state(hist_p, up),
            ks.reshape(1, bs, ss, heads, hd), vs.reshape(1, bs, ss, heads, hd), pool_state(hist_s, us))
```

```python
import functools

import jax
import jax.numpy as jnp
from jax import lax
from jax.experimental import pallas as pl
from jax.experimental.pallas import tpu as pltpu

F32 = jnp.float32
BF16 = jnp.bfloat16

EPS = 1e-6
HEAD_DIM = 64
LANES = 128
POOL_WINDOWS = (2, 4, 8, 16)
POOL_HIST = 16
N_GROUPS = 4
PER_GROUP = 8
N_EXPERTS = N_GROUPS * PER_GROUP
VMEM_LIMIT = 58 * 1024 * 1024
NEG = -1e30
LOG2E = 1.4426950408889634
ATTN_PAIRS = 4
MIX_PARTS = 2
FFN_PARTS = 2
SKIP_MASS = 160.0


def _fit(m, tile):
    while m % tile:
        tile //= 2
    return tile


def _cparams(sem, **kw):
    return pltpu.CompilerParams(dimension_semantics=sem, vmem_limit_bytes=VMEM_LIMIT, **kw)


def _proj_main_body(h_ref, w_ref, u_ref, q_ref, kb_ref, vb_ref, k_hbm, v_hbm, kv_buf, sem):
    i = pl.program_id(0)
    j = pl.program_id(1)
    last = pl.num_programs(0) - 1
    tm = h_ref.shape[0]
    heads = k_hbm.shape[1]

    def proj():
        return jnp.dot(h_ref[...], w_ref[...], preferred_element_type=F32)

    def head_copies(slot, dst_hbm, step):
        row0 = pl.multiple_of(step * tm, tm)
        return [pltpu.make_async_copy(kv_buf.at[slot, hh], dst_hbm.at[pl.ds(row0, tm), hh, :], sem.at[slot])
                for hh in range(heads)]

    @pl.when(j == 0)
    def _():
        u_ref[...] = proj()

    @pl.when(j == 1)
    def _():
        q_ref[...] = (proj() * (HEAD_DIM ** -0.5)).astype(BF16)

    for jj, slot, dense_ref, dst_hbm in ((2, 0, kb_ref, k_hbm), (3, 1, vb_ref, v_hbm)):
        @pl.when(j == jj)
        def _(slot=slot, dense_ref=dense_ref, dst_hbm=dst_hbm):
            @pl.when(i > 0)
            def _():
                for c in head_copies(slot, dst_hbm, i - 1):
                    c.wait()
            acc = proj()
            dense_ref[...] = acc.astype(BF16)
            for hh in range(heads):
                kv_buf[slot, hh] = acc[:, hh * HEAD_DIM:(hh + 1) * HEAD_DIM]
            for c in head_copies(slot, dst_hbm, i):
                c.start()

    @pl.when((i == last) & (j == 3))
    def _():
        for slot, dst_hbm in ((0, k_hbm), (1, v_hbm)):
            for c in head_copies(slot, dst_hbm, i):
                c.wait()


def _proj_gates_body(x_ref, g_ref, w_ref, h_ref, gp_ref, gs_ref, *, tn):
    j = pl.program_id(1)

    @pl.when(j == 0)
    def _():
        x = x_ref[...]
        r = lax.rsqrt(jnp.mean(x * x, axis=-1, keepdims=True) + EPS)
        h_ref[...] = ((x * r) * g_ref[...]).astype(BF16)

    for jj, ref in ((0, gp_ref), (2, gs_ref)):
        for half in range(2):
            @pl.when(j == jj + half)
            def _(ref=ref, half=half):
                a = jnp.dot(h_ref[...], w_ref[...], preferred_element_type=F32)
                ref[:, half * tn:(half + 1) * tn] = (0.5 * jnp.tanh(0.5 * a) + 0.5).astype(BF16)


def _inproj(x, g, w_bf, tm):
    m, d = x.shape
    tn = d // 2
    assert w_bf.shape == (d, 8 * tn) and m % tm == 0
    row = lambda i, j: (i, 0)
    h_spec = pl.BlockSpec((tm, d), row)
    h, gp, gs = pl.pallas_call(
        functools.partial(_proj_gates_body, tn=tn),
        out_shape=[jax.ShapeDtypeStruct((m, d), BF16)] * 3,
        grid=(m // tm, 4),
        in_specs=[h_spec, pl.BlockSpec((1, d), lambda i, j: (0, 0)),
                  pl.BlockSpec((d, tn), lambda i, j: (0, j + 4))],
        out_specs=[h_spec] * 3,
        compiler_params=_cparams(("parallel", "arbitrary")),
        name="inproj_gates",
    )(x, g, w_bf)
    dense = lambda dt: jax.ShapeDtypeStruct((m, tn), dt)
    cache = jax.ShapeDtypeStruct((m, tn // HEAD_DIM, HEAD_DIM), F32)
    any_spec = pl.BlockSpec(memory_space=pl.ANY)
    u, q, kb, vb, k, v = pl.pallas_call(
        _proj_main_body,
        out_shape=[dense(F32), dense(BF16), dense(BF16), dense(BF16), cache, cache],
        grid=(m // tm, 4),
        in_specs=[h_spec, pl.BlockSpec((d, tn), lambda i, j: (0, j))],
        out_specs=[pl.BlockSpec((tm, tn), row)] * 4 + [any_spec] * 2,
        scratch_shapes=[pltpu.VMEM((2, tn // HEAD_DIM, tm, HEAD_DIM), F32), pltpu.SemaphoreType.DMA((2,))],
        compiler_params=_cparams(("arbitrary", "arbitrary"), has_side_effects=True),
        name="inproj_main",
    )(h, w_bf)
    return u, q, kb, vb, k, v, gp, gs


def _pool_body(hist_ref, u_ref, o_ref, ext_scr, *, pos0, tn):
    s = pl.program_id(1)

    @pl.when(s == 0)
    def _():
        ext_scr[0:POOL_HIST, :] = hist_ref[0]

    ext_scr[POOL_HIST:POOL_HIST + tn, :] = u_ref[0]
    pos = pos0 + s * tn + lax.broadcasted_iota(jnp.int32, (tn, 1), 0)
    group = u_ref.shape[2] // len(POOL_WINDOWS)
    for g, w in enumerate(POOL_WINDOWS):
        lo, hi = g * group, (g + 1) * group
        cur = ext_scr[POOL_HIST:POOL_HIST + tn, lo:hi]
        tot = cur
        for dlt in range(1, w):
            tot = tot + ext_scr[POOL_HIST - dlt:POOL_HIST - dlt + tn, lo:hi]
        cnt = jnp.minimum(pos + 1, w).astype(F32)
        o_ref[0, :, lo:hi] = (tot / cnt - cur).astype(BF16)
    ext_scr[0:POOL_HIST, :] = ext_scr[tn:tn + POOL_HIST, :]


def _pool_diff(hist, u, pos0, tn):
    b, n, dp = u.shape
    assert n % tn == 0 and hist.shape == (b, POOL_HIST, dp)
    return pl.pallas_call(
        functools.partial(_pool_body, pos0=pos0, tn=tn),
        out_shape=jax.ShapeDtypeStruct((b, n, dp), BF16),
        grid=(b, n // tn),
        in_specs=[pl.BlockSpec((1, POOL_HIST, dp), lambda i, s: (i, 0, 0)),
                  pl.BlockSpec((1, tn, dp), lambda i, s: (i, s, 0))],
        out_specs=pl.BlockSpec((1, tn, dp), lambda i, s: (i, s, 0)),
        scratch_shapes=[pltpu.VMEM((POOL_HIST + tn, dp), F32)],
        compiler_params=_cparams(("parallel", "arbitrary")),
        name="pool_diff",
    )(hist, u)


def _softplus2(z2):
    neg_abs = pltpu.bitcast(pltpu.bitcast(z2, jnp.uint32) | jnp.uint32(0x80000000), F32)
    return jnp.maximum(z2, 0.0) + jnp.log2(1.0 + jnp.exp2(neg_abs))


def _suffix_matrix(n):
    r = lax.broadcasted_iota(jnp.int32, (n, n), 0)
    c = lax.broadcasted_iota(jnp.int32, (n, n), 1)
    return jnp.where(r >= c, 1.0, 0.0).astype(BF16)


def _stack_heads(q2):
    lane = lax.broadcasted_iota(jnp.int32, q2.shape, 1)
    zero = jnp.zeros_like(q2)
    return jnp.concatenate([jnp.where(lane < HEAD_DIM, q2, zero), jnp.where(lane >= HEAD_DIM, q2, zero)], axis=0)


def _unstack_heads(acc, t):
    lane = lax.broadcasted_iota(jnp.int32, (t, LANES), 1)
    return jnp.where(lane < HEAD_DIM, acc[:t], acc[t:])


def _causal_mask(t):
    r = lax.broadcasted_iota(jnp.int32, (t, t), 0)
    c = lax.broadcasted_iota(jnp.int32, (t, t), 1)
    m = c < r
    return jnp.concatenate([m, m], axis=0)


def _sb_group(q_st, k_blocks, v_blocks, suffix, carry, masks, transposed_keys):
    (out, carry), = _sb_groups([q_st], [k_blocks], [v_blocks], suffix, [carry], masks, transposed_keys)
    return out, carry


def _sb_groups(q_sts, k_blocks, v_blocks, suffix, carries, masks, transposed_keys):
    dn = (((1,), (0,)), ((), ())) if transposed_keys else (((1,), (1,)), ((), ()))
    zs = [[lax.dot_general(q_st, kb, dn, preferred_element_type=F32) * LOG2E for kb in kbs]
          for q_st, kbs in zip(q_sts, k_blocks)]
    n_blk = len(masks)
    cs = _sb_masses([z for zq in zs for z in zq], masks * len(q_sts), [suffix] * (n_blk * len(q_sts)))
    results = []
    for i, (zq, vbs, carry) in enumerate(zip(zs, v_blocks, carries)):
        out = None
        for j, (z, vb, mask) in enumerate(zip(zq, vbs, masks)):
            a, carry = _sb_weights(z, cs[i * n_blk + j], carry, mask)
            o = jnp.dot(a, vb, preferred_element_type=F32)
            out = o if out is None else out + o
        results.append((out, carry))
    return results


def _sb_masses(zs, masks, suffixes):
    splits = []
    for z, mask in zip(zs, masks):
        sp = _softplus2(z)
        if mask is not None:
            sp = jnp.where(mask, sp, 0.0)
        hi = pltpu.bitcast(pltpu.bitcast(sp, jnp.uint32) & jnp.uint32(0xFFFF0000), F32)
        splits.append((hi.astype(BF16), (sp - hi).astype(BF16)))
    return [jnp.dot(hi, sfx, preferred_element_type=F32) + jnp.dot(lo, sfx, preferred_element_type=F32)
            for (hi, lo), sfx in zip(splits, suffixes)]


def _sb_weights(z, c, carry, mask):
    arg = z - c - carry
    if mask is not None:
        arg = jnp.where(mask, arg, NEG)
    return jnp.exp2(arg).astype(BF16), carry + c[:, 0:1]


def _attn_self_body(q_ref, k_ref, v_ref, o_ref, acc_scr, car_scr, min_scr, *, t):
    qi = pl.program_id(2)
    n_pairs = q_ref.shape[2] // LANES
    lanes = lambda p: slice(p * LANES, (p + 1) * LANES)
    block = lambda ref, p, b: ref[0, pl.ds(pl.multiple_of(b * t, t), t), lanes(p)]

    q_sts = [_stack_heads(q_ref[0, :, lanes(p)]) for p in range(n_pairs)]
    suffix = _suffix_matrix(t)
    mask = _causal_mask(t)

    def run(blocks, masks, first):
        carries = [jnp.zeros((2 * t, 1), F32) if first else car_scr[p, :, 0:1] for p in range(n_pairs)]
        results = _sb_groups(q_sts, [[block(k_ref, p, b) for b in blocks] for p in range(n_pairs)],
                             [[block(v_ref, p, b) for b in blocks] for p in range(n_pairs)], suffix, carries, masks,
                             False)
        low = None
        for p, (out, carry) in enumerate(results):
            acc_scr[p] = out if first else acc_scr[p] + out
            car_scr[p] = jnp.broadcast_to(carry, car_scr.shape[1:])
            low = jnp.min(carry) if low is None else jnp.minimum(low, jnp.min(carry))
        min_scr[0] = low

    @pl.when(qi == 0)
    def _():
        run([0], [mask], True)

    @pl.when(qi > 0)
    def _():
        run([qi, qi - 1], [mask, None], True)

    rest = jnp.maximum(qi - 1, 0)

    def more(it):
        return (it < rest // 2) & (min_scr[0] < SKIP_MASS)

    def pair(it):
        b0 = qi - 2 - 2 * it
        run([b0, b0 - 1], [None, None], False)
        return it + 1

    lax.while_loop(more, pair, 0)

    @pl.when((rest % 2 == 1) & (min_scr[0] < SKIP_MASS))
    def _():
        run([0], [None], False)

    o_ref[0] = jnp.concatenate([_unstack_heads(acc_scr[p], t) for p in range(n_pairs)],
                               axis=1).astype(o_ref.dtype)


def _attn_hist_body(q_ref, kn_ref, vn_ref, kh_ref, vh_ref, o_ref, left_ref, acc_scr, car_scr, min_scr, *, tk,
                    group):
    t = q_ref.shape[1]
    past = kh_ref.shape[1]
    q_st = _stack_heads(q_ref[0])

    def keep(out, carry, first):
        acc_scr[...] = out if first else acc_scr[...] + out
        car_scr[...] = jnp.broadcast_to(carry, car_scr.shape)
        min_scr[0] = jnp.min(carry)

    keep(*_sb_group(q_st, [kn_ref[0].astype(BF16)], [vn_ref[0].astype(BF16)], _suffix_matrix(t),
                    jnp.zeros((2 * t, 1), F32), [_causal_mask(t)], False), True)
    suffix = _suffix_matrix(tk)
    for top in range(past // tk, 0, -group):
        @pl.when(min_scr[0] < SKIP_MASS)
        def _(top=top):
            blocks = range(top - 1, top - 1 - group, -1)
            keep(*_sb_group(q_st, [kh_ref[0, b * tk:(b + 1) * tk, :].astype(BF16) for b in blocks],
                            [vh_ref[0, b * tk:(b + 1) * tk, :].astype(BF16) for b in blocks],
                            suffix, car_scr[:, 0:1], [None] * group, False), False)
    o_ref[0] = _unstack_heads(acc_scr[...], t).astype(o_ref.dtype)
    left_ref[0, 0] = jnp.full(left_ref.shape[2:], min_scr[0], F32)


def _attn_recent_body(q_ref, kn_ref, vn_ref, kc_ref, vc_ref, o_ref, left_ref, *, tk):
    t, dm = q_ref.shape[1:]
    n_blk = kc_ref.shape[1] // tk
    pair = lambda ref, rows, p: ref[0, rows, p * LANES:(p + 1) * LANES].astype(BF16)
    score = lambda qs, kb: lax.dot_general(qs, kb, (((1,), (1,)), ((), ())), preferred_element_type=F32)
    own = slice(None)
    blocks = [pl.ds(j * tk, tk) for j in range(n_blk - 1, -1, -1)]
    mask, sfx_new, sfx_old = _causal_mask(t), _suffix_matrix(t), _suffix_matrix(tk)
    zs, masks, sfx = [], [], []
    for p in range(dm // LANES):
        q_st = _stack_heads(pair(q_ref, own, p))
        zs.append(score(q_st, pair(kn_ref, own, p)) * LOG2E)
        masks.append(mask)
        sfx.append(sfx_new)
        for rows in blocks:
            zs.append(score(q_st, pair(kc_ref, rows, p)) * LOG2E)
            masks.append(None)
            sfx.append(sfx_old)
    cs = _sb_masses(zs, masks, sfx)
    outs, left = [], None
    chain = 1 + n_blk
    for p in range(dm // LANES):
        carry = jnp.zeros((2 * t, 1), F32)
        out = None
        for i in range(chain):
            a, carry = _sb_weights(zs[p * chain + i], cs[p * chain + i], carry, masks[p * chain + i])
            vb = pair(vn_ref, own, p) if i == 0 else pair(vc_ref, blocks[i - 1], p)
            o = jnp.dot(a, vb, preferred_element_type=F32)
            out = o if out is None else out + o
        outs.append(_unstack_heads(out, t))
        low = jnp.min(carry)
        left = low if left is None else jnp.minimum(left, low)
    o_ref[0] = jnp.concatenate(outs, axis=1).astype(o_ref.dtype)
    left_ref[0] = jnp.full(left_ref.shape[1:], left, F32)


def _attn_specs(t):
    tile = pl.BlockSpec((1, t, LANES), lambda i, p, s: (i, s, p))
    seq = lambda rows: pl.BlockSpec((1, rows, LANES), lambda i, p, s: (i, 0, p))
    state = [pltpu.VMEM((2 * t, LANES), F32), pltpu.VMEM((2 * t, LANES), F32), pltpu.SMEM((1,), F32)]
    return tile, seq, state


def _attention_self(q, k, v, t, pairs):
    b, n, dm = q.shape
    width = pairs * LANES
    assert n % t == 0 and dm % width == 0
    tile = pl.BlockSpec((1, t, width), lambda i, p, s: (i, s, p))
    seq = pl.BlockSpec((1, n, width), lambda i, p, s: (i, 0, p))
    return pl.pallas_call(
        functools.partial(_attn_self_body, t=t),
        out_shape=jax.ShapeDtypeStruct((b, n, dm), BF16),
        grid=(b, dm // width, n // t),
        in_specs=[tile, seq, seq],
        out_specs=tile,
        scratch_shapes=[pltpu.VMEM((pairs, 2 * t, LANES), F32), pltpu.VMEM((pairs, 2 * t, LANES), F32),
                        pltpu.SMEM((1,), F32)],
        compiler_params=_cparams(("parallel", "parallel", "arbitrary")),
        name="sb_attention",
    )(q, k, v)


def _attention_cached(q, k_new, v_new, k_hist, v_hist, tk, group):
    b, t, dm = q.shape
    past = k_hist.shape[1]
    assert dm % LANES == 0 and past % (tk * group) == 0
    tile, seq, state = _attn_specs(t)
    return pl.pallas_call(
        functools.partial(_attn_hist_body, tk=tk, group=group),
        out_shape=[jax.ShapeDtypeStruct((b, t, dm), BF16), jax.ShapeDtypeStruct((b, dm // LANES, 8, LANES), F32)],
        grid=(b, dm // LANES, 1),
        in_specs=[tile, seq(t), seq(t), seq(past), seq(past)],
        out_specs=[tile, pl.BlockSpec((1, 1, 8, LANES), lambda i, p, s: (i, p, 0, 0))],
        scratch_shapes=state,
        compiler_params=_cparams(("parallel", "parallel", "arbitrary")),
        name="sb_attention_cached",
    )(q, k_new, v_new, k_hist, v_hist)


def _attention_recent_first(q, k_new, v_new, cache_k, cache_v, tk, group):
    b, past, heads, hd = cache_k.shape
    t, dm = q.shape[1:]
    flat = lambda c: c.reshape(b, c.shape[1], heads * hd)
    walk = lambda: _attention_cached(q, k_new, v_new, flat(cache_k), flat(cache_v), tk, group)[0]
    recent = tk * group
    if past <= recent:
        return walk()
    seq = lambda rows: pl.BlockSpec((1, rows, dm), lambda i: (i, 0, 0))
    o, left = pl.pallas_call(
        functools.partial(_attn_recent_body, tk=tk),
        out_shape=[jax.ShapeDtypeStruct((b, t, dm), BF16), jax.ShapeDtypeStruct((b, 8, LANES), F32)],
        grid=(b,),
        in_specs=[seq(t), seq(t), seq(t), seq(recent), seq(recent)],
        out_specs=[seq(t), pl.BlockSpec((1, 8, LANES), lambda i: (i, 0, 0))],
        compiler_params=_cparams(("parallel",)),
        name="sb_attention_recent",
    )(q, k_new, v_new, flat(cache_k[:, past - recent:]), flat(cache_v[:, past - recent:]))
    return lax.cond(jnp.min(left) < SKIP_MASS, walk, lambda: o)


def _mix_body(diff_ref, o_ref, gp_ref, gs_ref, x_ref, wp_ref, ps_ref, wsb_ref, wo_ref, nf_ref,
              wr_ref, br_ref, x1_ref, hp_ref, route_ref, cnt_ref, logit_scr, *, tm):
    @pl.when(pl.program_id(0) == 0)
    def _():
        logit_scr[...] = jnp.zeros_like(logit_scr)

    late_logits = logit_scr[...]

    n_pool = wp_ref.shape[0]
    group = wp_ref.shape[1]
    rows = tm // MIX_PARTS
    parts = [pl.ds(i * rows, rows) for i in range(MIX_PARTS)]
    pools = [jnp.concatenate(
        [jnp.dot(diff_ref[s, g * group:(g + 1) * group], wp_ref[g], preferred_element_type=F32)
         for g in range(n_pool)], axis=-1) * ps_ref[...] for s in parts]
    sbs = [jnp.dot(o_ref[s, :], wsb_ref[...], preferred_element_type=F32) for s in parts]
    mixed = [(gp_ref[s, :].astype(F32) * pool + gs_ref[s, :].astype(F32) * sb).astype(BF16)
             for s, pool, sb in zip(parts, pools, sbs)]
    x1s = [x_ref[s, :] + jnp.dot(mx, wo_ref[...], preferred_element_type=F32) for s, mx in zip(parts, mixed)]
    splits = []
    for s, x1 in zip(parts, x1s):
        x1_ref[s, :] = x1
        h = (x1 * lax.rsqrt(jnp.mean(x1 * x1, axis=-1, keepdims=True) + EPS)) * nf_ref[...]
        d_half = h.shape[1] // 2
        lo_bits = pltpu.bitcast(h[:, :d_half].astype(BF16).astype(F32), jnp.uint32)
        hi_bits = pltpu.bitcast(h[:, d_half:].astype(BF16).astype(F32), jnp.uint32)
        hp_ref[s, :] = (lo_bits >> 16) | (hi_bits & jnp.uint32(0xFFFF0000))
        hh = h.astype(BF16)
        splits.append(jnp.concatenate([hh, (h - hh.astype(F32)).astype(BF16)], axis=0))

    rs = [jnp.dot(sp, wr_ref[...], preferred_element_type=F32) for sp in splits]
    logit_scr[...] = jnp.concatenate(
        [(r[:rows, :LANES] + r[:rows, LANES:]) + (r[rows:, :LANES] + r[rows:, LANES:]) for r in rs],
        axis=0) + br_ref[...]
    _route_tile(late_logits, route_ref, cnt_ref, tm)


def _route_tile(logits, route_ref, cnt_ref, tm):
    lane = lax.broadcasted_iota(jnp.int32, (tm, LANES), 1)
    big = jnp.int32(LANES)

    def first_max(vals):
        m = jnp.max(vals, axis=-1, keepdims=True)
        idx = jnp.min(jnp.where(vals == m, lane, big), axis=-1, keepdims=True)
        return m, idx

    gl = jnp.where(lane < N_GROUPS, logits, NEG)
    gmax, grp = first_max(gl)
    p_grp = 1.0 / jnp.sum(jnp.exp(gl - gmax), axis=-1, keepdims=True)
    e_lo = N_GROUPS + grp * PER_GROUP
    el = jnp.where((lane >= e_lo) & (lane < e_lo + PER_GROUP), logits, NEG)
    m1, i1 = first_max(el)
    m2, i2 = first_max(jnp.where(lane == i1, NEG, el))
    t2 = jnp.exp(m2 - m1)
    w1 = p_grp / (1.0 + t2)
    w2 = w1 * t2
    e1 = i1 - N_GROUPS
    e2 = i2 - N_GROUPS

    oh1 = jnp.where(lane == e1, 1.0, 0.0).astype(BF16)
    oh2 = jnp.where(lane == e2, 1.0, 0.0).astype(BF16)
    rr = lax.broadcasted_iota(jnp.int32, (tm, tm), 0)
    cc = lax.broadcasted_iota(jnp.int32, (tm, tm), 1)
    before = jnp.where(cc < rr, 1.0, 0.0).astype(BF16)
    ones = jnp.ones((8, tm), BF16)
    pre1 = jnp.dot(before, oh1, preferred_element_type=F32)
    pre2 = jnp.dot(before, oh2, preferred_element_type=F32)
    c1 = jnp.dot(ones, oh1, preferred_element_type=F32)
    c2 = jnp.dot(ones, oh2, preferred_element_type=F32)
    rank1 = jnp.sum(jnp.where(lane == e1, pre1, 0.0), axis=-1, keepdims=True)
    rank2 = jnp.sum(jnp.where(lane == e2, pre2 + c1[0:1, :], 0.0), axis=-1, keepdims=True)
    cnt_ref[0] = c1 + c2

    route = jnp.where(lane == 0, e1.astype(F32), 0.0)
    route = jnp.where(lane == 1, e2.astype(F32), route)
    route = jnp.where(lane == 2, w1, route)
    route = jnp.where(lane == 3, w2, route)
    route = jnp.where(lane == 4, rank1, route)
    route = jnp.where(lane == 5, rank2, route)
    route_ref[...] = route


def _mix(diff, o, gp, gs, x, wp, ps, wsb, wo, nf, wr, br, tm):
    m, d = x.shape
    assert m % tm == 0
    n = m // tm
    row = lambda i: (jnp.minimum(i, n - 1), 0)
    late = lambda i: (jnp.maximum(i - 1, 0), 0)
    full = lambda a: pl.BlockSpec(a.shape, lambda i: (0,) * a.ndim, pipeline_mode=pl.Buffered(1))
    acts = [diff, o, gp, gs, x]
    consts = [wp, ps, wsb, wo, nf, wr, br]
    return pl.pallas_call(
        functools.partial(_mix_body, tm=tm),
        out_shape=[jax.ShapeDtypeStruct((m, d), F32), jax.ShapeDtypeStruct((m, d // 2), jnp.uint32),
                   jax.ShapeDtypeStruct((m, LANES), F32), jax.ShapeDtypeStruct((n, 8, LANES), F32)],
        grid=(n + 1,),
        in_specs=[pl.BlockSpec((tm, a.shape[1]), row) for a in acts] + [full(a) for a in consts],
        out_specs=[pl.BlockSpec((tm, d), row), pl.BlockSpec((tm, d // 2), row),
                   pl.BlockSpec((tm, LANES), late), pl.BlockSpec((1, 8, LANES), lambda i: late(i) + (0,))],
        scratch_shapes=[pltpu.VMEM((tm, LANES), F32)],
        compiler_params=_cparams(("arbitrary",)),
        name="mix_outproj_router",
    )(*acts, *consts)


def _row_wait(src_ref, dst_ref, sem, n_rows):
    pltpu.make_async_copy(src_ref.at[pl.ds(0, n_rows)], dst_ref.at[pl.ds(0, n_rows)], sem).wait()


def _dispatch_body(slot_ref, zero_ref, hp_ref, xs_ref, buf, zbuf, sem, zsem, *, tm):
    i = pl.program_id(0)
    last = pl.num_programs(0) - 1
    par = i % 2

    @pl.when(i == 0)
    def _():
        zbuf[...] = jnp.zeros_like(zbuf)
        bm = zbuf.shape[0]
        for wait in (False, True):
            for z in range(zero_ref.shape[0]):
                @pl.when(zero_ref[z] >= 0)
                def _(z=z, wait=wait):
                    start = pl.multiple_of(jnp.maximum(zero_ref[z], 0), bm)
                    copy = pltpu.make_async_copy(zbuf, xs_ref.at[pl.ds(start, bm)], zsem)
                    copy.wait() if wait else copy.start()

    def drain(slot):
        for _ in range(2):
            _row_wait(buf.at[slot], xs_ref, sem.at[slot], tm)

    @pl.when(i >= 2)
    def _():
        drain(par)

    buf[par] = hp_ref[...]
    base = i * tm

    def body(t, _):
        src = buf.at[par, pl.ds(t, 1)]
        for kk in range(2):
            dst = xs_ref.at[pl.ds(slot_ref[2 * (base + t) + kk], 1)]
            pltpu.make_async_copy(src, dst, sem.at[par]).start()
        return 0

    lax.fori_loop(0, tm, body, 0, unroll=8)

    @pl.when(i == last)
    def _():
        drain(par)

        @pl.when(i >= 1)
        def _():
            drain(1 - par)


def _dispatch(slots_flat, zero_starts, hp, n_slots, tm, bm):
    m, dh = hp.shape
    assert m % tm == 0
    return pl.pallas_call(
        functools.partial(_dispatch_body, tm=tm),
        out_shape=jax.ShapeDtypeStruct((n_slots, dh), hp.dtype),
        grid_spec=pltpu.PrefetchScalarGridSpec(
            num_scalar_prefetch=2, grid=(m // tm,),
            in_specs=[pl.BlockSpec((tm, dh), lambda i, s, z: (i, 0))],
            out_specs=pl.BlockSpec(memory_space=pl.ANY),
            scratch_shapes=[pltpu.VMEM((2, tm, dh), hp.dtype), pltpu.VMEM((bm, dh), hp.dtype),
                            pltpu.SemaphoreType.DMA((2,)), pltpu.SemaphoreType.DMA]),
        compiler_params=_cparams(("arbitrary",), disable_bounds_checks=True, has_side_effects=True),
        name="moe_dispatch",
    )(slots_flat, zero_starts, hp)


def _ffn_body(be_ref, nv_ref, xs_ref, wg_ref, wu_ref, wd_ref, y_ref):
    del be_ref

    @pl.when(pl.program_id(0) >= nv_ref[0])
    def _():
        y_ref[...] = jnp.zeros_like(y_ref)

    @pl.when(pl.program_id(0) < nv_ref[0])
    def _():
        d_half = xs_ref.shape[1]
        rows = xs_ref.shape[0] // FFN_PARTS
        hids = []
        for i in range(FFN_PARTS):
            words = xs_ref[pl.ds(i * rows, rows), :]
            x_lo = pltpu.bitcast(words << 16, F32).astype(BF16)
            x_hi = pltpu.bitcast(words & jnp.uint32(0xFFFF0000), F32).astype(BF16)

            def proj(w_ref):
                return (jnp.dot(x_lo, w_ref[0, :d_half, :], preferred_element_type=F32)
                        + jnp.dot(x_hi, w_ref[0, d_half:, :], preferred_element_type=F32))

            hids.append((jax.nn.silu(proj(wg_ref)) * proj(wu_ref)).astype(BF16))
        for i, hid in enumerate(hids):
            y_ref[pl.ds(i * rows, rows), :] = jnp.dot(hid, wd_ref[0], preferred_element_type=F32)


def _ffn(block_expert, n_valid, xs, wg, wu, wd, bm):
    n_slots, d_half = xs.shape
    d = 2 * d_half
    de = wg.shape[2]
    live = lambda b, be, nv: (jnp.minimum(b, nv[0] - 1), 0)
    wsel = lambda b, be, nv: (be[b], 0, 0)
    return pl.pallas_call(
        _ffn_body,
        out_shape=jax.ShapeDtypeStruct((n_slots, d), F32),
        grid_spec=pltpu.PrefetchScalarGridSpec(
            num_scalar_prefetch=2, grid=(n_slots // bm,),
            in_specs=[pl.BlockSpec((bm, d_half), live), pl.BlockSpec((1, d, de), wsel),
                      pl.BlockSpec((1, d, de), wsel), pl.BlockSpec((1, de, d), wsel)],
            out_specs=pl.BlockSpec((bm, d), lambda b, be, nv: (b, 0))),
        compiler_params=_cparams(("arbitrary",)),
        name="moe_ffn",
    )(block_expert, n_valid, xs, wg, wu, wd)


def _final_body(slot_ref, x1_ref, route_ref, g_ref, y_hbm, o_ref, buf, sem, *, tm):
    i = pl.program_id(0)
    n_steps = pl.num_programs(0)

    def issue(step, par):
        def body(t, _):
            for kk in range(2):
                src = y_hbm.at[pl.ds(slot_ref[2 * (step * tm + t) + kk], 1)]
                pltpu.make_async_copy(src, buf.at[par, kk, pl.ds(t, 1)], sem.at[par]).start()
            return 0
        lax.fori_loop(0, tm, body, 0, unroll=8)

    @pl.when(i == 0)
    def _():
        issue(0, 0)

    @pl.when(i + 1 < n_steps)
    def _():
        issue(i + 1, (i + 1) % 2)

    par = i % 2
    for kk in range(2):
        _row_wait(y_hbm, buf.at[par, kk], sem.at[par], tm)
    route = route_ref[...]
    x2 = x1_ref[...] + route[:, 2:3] * buf[par, 0] + route[:, 3:4] * buf[par, 1]
    o_ref[...] = (x2 * lax.rsqrt(jnp.mean(x2 * x2, axis=-1, keepdims=True) + EPS)) * g_ref[...]


def _final(slots_flat, x1, route, g, y, tm):
    m, d = x1.shape
    assert m % tm == 0
    row = lambda i, s: (i, 0)
    return pl.pallas_call(
        functools.partial(_final_body, tm=tm),
        out_shape=jax.ShapeDtypeStruct((m, d), F32),
        grid_spec=pltpu.PrefetchScalarGridSpec(
            num_scalar_prefetch=1, grid=(m // tm,),
            in_specs=[pl.BlockSpec((tm, d), row), pl.BlockSpec((tm, LANES), row),
                      pl.BlockSpec((1, d), lambda i, s: (0, 0)), pl.BlockSpec(memory_space=pl.ANY)],
            out_specs=pl.BlockSpec((tm, d), row),
            scratch_shapes=[pltpu.VMEM((2, 2, tm, d), F32), pltpu.SemaphoreType.DMA((2,))]),
        compiler_params=_cparams(("arbitrary",), disable_bounds_checks=True),
        name="moe_combine_final_norm",
    )(slots_flat, x1, route, g, y)


def _routing_tables(route, cnt, tm, bm):
    m = route.shape[0]
    n_blocks = (2 * m) // bm + N_EXPERTS
    counts = cnt[:, 0, :N_EXPERTS].astype(jnp.int32)
    sizes = jnp.sum(counts, axis=0)
    padded = (sizes + bm - 1) // bm * bm
    pad_end = jnp.cumsum(padded)
    base = (pad_end - padded)[None, :] + jnp.cumsum(counts, axis=0) - counts
    base_tok = jnp.repeat(base, tm, axis=0)
    e = route[:, 0:2].astype(jnp.int32)
    rank = route[:, 4:6].astype(jnp.int32)
    sel = e[:, :, None] == jnp.arange(N_EXPERTS, dtype=jnp.int32)[None, None, :]
    slots = jnp.sum(jnp.where(sel, base_tok[:, None, :], 0), axis=-1) + rank
    n_valid = (pad_end[-1] // bm).astype(jnp.int32)
    blk = jnp.minimum(jnp.arange(n_blocks, dtype=jnp.int32), n_valid - 1)
    block_expert = jnp.minimum(jnp.sum(pad_end[None, :] <= (blk * bm)[:, None], axis=1), N_EXPERTS - 1)
    last_blk = jnp.where(padded > 0, pad_end - bm, -1)
    tail = n_valid + jnp.arange(N_EXPERTS, dtype=jnp.int32)
    tail = jnp.where(tail < n_blocks, tail * bm, -1)
    zero_starts = jnp.concatenate([last_blk, tail]).astype(jnp.int32)
    return slots.reshape(-1), block_expert.astype(jnp.int32), n_valid.reshape(1), zero_starts, n_blocks * bm


def _stream(x, pool_hist, k_hist, v_hist, p, *, tq, tk, group, tn, tm_in, tm_mix, tm_fin, bm):
    b, n, d = x.shape
    m = b * n
    past = 0 if k_hist is None else k_hist.shape[1]
    x2d = x.reshape(m, d)
    u, q, kb, vb, k, v, gp, gs = _inproj(x2d, p['norm_mix'], p['w_in'], _fit(m, tm_in))
    dp = u.shape[1]
    diff = _pool_diff(pool_hist, u.reshape(b, n, dp), past, tn).reshape(m, dp)
    shp = (b, n, kb.shape[1])
    if k_hist is None:
        o = _attention_self(q.reshape(shp), kb.reshape(shp), vb.reshape(shp), tq, ATTN_PAIRS)
    else:
        o = _attention_recent_first(q.reshape(shp), kb.reshape(shp), vb.reshape(shp), k_hist, v_hist, tk, group)
    o = o.reshape(m, -1)
    x1, hp, route, cnt = _mix(diff, o, gp, gs, x2d, p['w_pool'], p['pool_scale'], p['w_sb_out'], p['w_out'],
                              p['norm_ffn'], p['w_r'], p['b_r'], tm_mix)
    slots, block_expert, n_valid, zero_starts, n_slots = _routing_tables(route, cnt, tm_mix, bm)
    xs = _dispatch(slots, zero_starts, hp, n_slots, tm_mix, bm)
    y = _ffn(block_expert, n_valid, xs, p['w_g'], p['w_u'], p['w_d'], bm)
    out = _final(slots, x1, route, p['norm_final'], y, tm_fin)
    return out.reshape(b, n, d), u.reshape(b, n, dp), k, v


def kernel(x_prompt, x_sample, cache_sb_k, cache_sb_v, state_pool, norm_mix, w_in, w_pool, pool_scale, w_sb_out,
           w_out, norm_ffn, w_router_group, b_router_group, w_router_expert, b_router_expert, w_exp_gate,
           w_exp_up, w_exp_down, norm_final):
    depth = w_in.shape[0]
    assert depth == 1
    bp, sp, d = x_prompt.shape
    bs, ss, _ = x_sample.shape
    past = cache_sb_k.shape[2]
    heads, hd = cache_sb_k.shape[3], cache_sb_k.shape[4]
    assert hd == HEAD_DIM
    dp = state_pool.shape[3]
    n_state = state_pool.shape[2]

    w_r = jnp.concatenate([w_router_group[0], w_router_expert[0]], axis=1)
    w_r = jnp.pad(w_r, ((0, 0), (0, LANES - w_r.shape[1])))
    w_r_hi = w_r.astype(BF16)
    b_r = jnp.concatenate([b_router_group[0], b_router_expert[0]])
    p = dict(
        norm_mix=norm_mix[0][None, :], w_in=w_in[0].astype(BF16), w_pool=w_pool[0].astype(BF16),
        pool_scale=pool_scale[0][None, :], w_sb_out=w_sb_out[0].astype(BF16), w_out=w_out[0].astype(BF16),
        norm_ffn=norm_ffn[0][None, :],
        w_r=jnp.concatenate([w_r_hi, (w_r - w_r_hi.astype(F32)).astype(BF16)], axis=1),
        b_r=jnp.pad(b_r, (0, LANES - b_r.shape[0]))[None, :].astype(F32),
        w_g=w_exp_gate[0].astype(BF16), w_u=w_exp_up[0].astype(BF16), w_d=w_exp_down[0].astype(BF16),
        norm_final=norm_final[None, :])

    hist_p = jnp.zeros((bp, POOL_HIST, dp), F32)
    yp, up, kp, vp = _stream(x_prompt, hist_p, None, None, p, tq=256, tk=256, group=2, tn=min(sp, 512),
                             tm_in=1024, tm_mix=256, tm_fin=256, bm=256)
    hist_s = jnp.pad(state_pool[0], ((0, 0), (POOL_HIST - n_state, 0), (0, 0)))
    ys, us, ks, vs = _stream(x_sample, hist_s, cache_sb_k[0], cache_sb_v[0], p, tq=ss, tk=256, group=1, tn=ss,
                             tm_in=1024, tm_mix=256, tm_fin=256, bm=256)

    def pool_state(hist, u):
        return jnp.concatenate([hist[:, POOL_HIST - n_state:], u], axis=1)[:, -n_state:][None]

    return (yp, ys,
            kp.reshape(1, bp, sp, heads, hd), vp.reshape(1, bp, sp, heads, hd), pool_state(hist_p, up),
            ks.reshape(1, bs, ss, heads, hd), vs.reshape(1, bs, ss, heads, hd), pool_state(hist_s, us))
```

```python
import functools

import jax
import jax.numpy as jnp
from jax import lax
from jax.experimental import pallas as pl
from jax.experimental.pallas import tpu as pltpu

F32 = jnp.float32
BF16 = jnp.bfloat16

EPS = 1e-6
HEAD_DIM = 64
LANES = 128
POOL_WINDOWS = (2, 4, 8, 16)
POOL_HIST = 16
N_GROUPS = 4
PER_GROUP = 8
N_EXPERTS = N_GROUPS * PER_GROUP
VMEM_LIMIT = 58 * 1024 * 1024
NEG = -1e30
LOG2E = 1.4426950408889634
ATTN_PAIRS = 4
MIX_PARTS = 2
FFN_PARTS = 2
SKIP_MASS = 160.0


def _fit(m, tile):
    while m % tile:
        tile //= 2
    return tile


def _cparams(sem, **kw):
    return pltpu.CompilerParams(dimension_semantics=sem, vmem_limit_bytes=VMEM_LIMIT, **kw)


def _proj_main_body(h_ref, w_ref, u_ref, q_ref, kb_ref, vb_ref, k_hbm, v_hbm, kv_buf, sem):
    i = pl.program_id(0)
    j = pl.program_id(1)
    last = pl.num_programs(0) - 1
    tm = h_ref.shape[0]
    heads = k_hbm.shape[1]

    def proj():
        return jnp.dot(h_ref[...], w_ref[...], preferred_element_type=F32)

    def head_copies(slot, dst_hbm, step):
        row0 = pl.multiple_of(step * tm, tm)
        return [pltpu.make_async_copy(kv_buf.at[slot, hh], dst_hbm.at[pl.ds(row0, tm), hh, :], sem.at[slot])
                for hh in range(heads)]

    @pl.when(j == 0)
    def _():
        u_ref[...] = proj()

    @pl.when(j == 1)
    def _():
        q_ref[...] = (proj() * (HEAD_DIM ** -0.5)).astype(BF16)

    for jj, slot, dense_ref, dst_hbm in ((2, 0, kb_ref, k_hbm), (3, 1, vb_ref, v_hbm)):
        @pl.when(j == jj)
        def _(slot=slot, dense_ref=dense_ref, dst_hbm=dst_hbm):
            @pl.when(i > 0)
            def _():
                for c in head_copies(slot, dst_hbm, i - 1):
                    c.wait()
            acc = proj()
            dense_ref[...] = acc.astype(BF16)
            for hh in range(heads):
                kv_buf[slot, hh] = acc[:, hh * HEAD_DIM:(hh + 1) * HEAD_DIM]
            for c in head_copies(slot, dst_hbm, i):
                c.start()

    @pl.when((i == last) & (j == 3))
    def _():
        for slot, dst_hbm in ((0, k_hbm), (1, v_hbm)):
            for c in head_copies(slot, dst_hbm, i):
                c.wait()


def _proj_gates_body(x_ref, g_ref, w_ref, h_ref, gp_ref, gs_ref, *, tn):
    j = pl.program_id(1)

    @pl.when(j == 0)
    def _():
        x = x_ref[...]
        r = lax.rsqrt(jnp.mean(x * x, axis=-1, keepdims=True) + EPS)
        h_ref[...] = ((x * r) * g_ref[...]).astype(BF16)

    for jj, ref in ((0, gp_ref), (2, gs_ref)):
        for half in range(2):
            @pl.when(j == jj + half)
            def _(ref=ref, half=half):
                a = jnp.dot(h_ref[...], w_ref[...], preferred_element_type=F32)
                ref[:, half * tn:(half + 1) * tn] = (0.5 * jnp.tanh(0.5 * a) + 0.5).astype(BF16)


def _inproj(x, g, w_bf, tm):
    m, d = x.shape
    tn = d // 2
    assert w_bf.shape == (d, 8 * tn) and m % tm == 0
    row = lambda i, j: (i, 0)
    h_spec = pl.BlockSpec((tm, d), row)
    h, gp, gs = pl.pallas_call(
        functools.partial(_proj_gates_body, tn=tn),
        out_shape=[jax.ShapeDtypeStruct((m, d), BF16)] * 3,
        grid=(m // tm, 4),
        in_specs=[h_spec, pl.BlockSpec((1, d), lambda i, j: (0, 0)),
                  pl.BlockSpec((d, tn), lambda i, j: (0, j + 4))],
        out_specs=[h_spec] * 3,
        compiler_params=_cparams(("parallel", "arbitrary")),
        name="inproj_gates",
    )(x, g, w_bf)
    dense = lambda dt: jax.ShapeDtypeStruct((m, tn), dt)
    cache = jax.ShapeDtypeStruct((m, tn // HEAD_DIM, HEAD_DIM), F32)
    any_spec = pl.BlockSpec(memory_space=pl.ANY)
    u, q, kb, vb, k, v = pl.pallas_call(
        _proj_main_body,
        out_shape=[dense(F32), dense(BF16), dense(BF16), dense(BF16), cache, cache],
        grid=(m // tm, 4),
        in_specs=[h_spec, pl.BlockSpec((d, tn), lambda i, j: (0, j))],
        out_specs=[pl.BlockSpec((tm, tn), row)] * 4 + [any_spec] * 2,
        scratch_shapes=[pltpu.VMEM((2, tn // HEAD_DIM, tm, HEAD_DIM), F32), pltpu.SemaphoreType.DMA((2,))],
        compiler_params=_cparams(("arbitrary", "arbitrary"), has_side_effects=True),
        name="inproj_main",
    )(h, w_bf)
    return u, q, kb, vb, k, v, gp, gs


def _pool_windows(ext_ref, tn, pos):
    group = ext_ref.shape[1] // len(POOL_WINDOWS)
    slabs = []
    for g, w in enumerate(POOL_WINDOWS):
        lo, hi = g * group, (g + 1) * group
        cur = ext_ref[POOL_HIST:POOL_HIST + tn, lo:hi]
        tot = cur
        for dlt in range(1, w):
            tot = tot + ext_ref[POOL_HIST - dlt:POOL_HIST - dlt + tn, lo:hi]
        cnt = jnp.minimum(pos + 1, w).astype(F32)
        slabs.append((tot / cnt - cur).astype(BF16))
    return slabs


def _pool_body(hist_ref, u_ref, o_ref, ext_scr, *, pos0, tn):
    s = pl.program_id(1)

    @pl.when(s == 0)
    def _():
        ext_scr[0:POOL_HIST, :] = hist_ref[0]

    ext_scr[POOL_HIST:POOL_HIST + tn, :] = u_ref[0]
    pos = pos0 + s * tn + lax.broadcasted_iota(jnp.int32, (tn, 1), 0)
    group = u_ref.shape[2] // len(POOL_WINDOWS)
    for g, slab in enumerate(_pool_windows(ext_scr, tn, pos)):
        o_ref[0, :, g * group:(g + 1) * group] = slab
    ext_scr[0:POOL_HIST, :] = ext_scr[tn:tn + POOL_HIST, :]


def _pool_diff(hist, u, pos0, tn):
    b, n, dp = u.shape
    assert n % tn == 0 and hist.shape == (b, POOL_HIST, dp)
    return pl.pallas_call(
        functools.partial(_pool_body, pos0=pos0, tn=tn),
        out_shape=jax.ShapeDtypeStruct((b, n, dp), BF16),
        grid=(b, n // tn),
        in_specs=[pl.BlockSpec((1, POOL_HIST, dp), lambda i, s: (i, 0, 0)),
                  pl.BlockSpec((1, tn, dp), lambda i, s: (i, s, 0))],
        out_specs=pl.BlockSpec((1, tn, dp), lambda i, s: (i, s, 0)),
        scratch_shapes=[pltpu.VMEM((POOL_HIST + tn, dp), F32)],
        compiler_params=_cparams(("parallel", "arbitrary")),
        name="pool_diff",
    )(hist, u)


def _softplus2(z2):
    neg_abs = pltpu.bitcast(pltpu.bitcast(z2, jnp.uint32) | jnp.uint32(0x80000000), F32)
    return jnp.maximum(z2, 0.0) + jnp.log2(1.0 + jnp.exp2(neg_abs))


def _suffix_matrix(n):
    r = lax.broadcasted_iota(jnp.int32, (n, n), 0)
    c = lax.broadcasted_iota(jnp.int32, (n, n), 1)
    return jnp.where(r >= c, 1.0, 0.0).astype(BF16)


def _stack_heads(q2):
    lane = lax.broadcasted_iota(jnp.int32, q2.shape, 1)
    zero = jnp.zeros_like(q2)
    return jnp.concatenate([jnp.where(lane < HEAD_DIM, q2, zero), jnp.where(lane >= HEAD_DIM, q2, zero)], axis=0)


def _unstack_heads(acc, t):
    lane = lax.broadcasted_iota(jnp.int32, (t, LANES), 1)
    return jnp.where(lane < HEAD_DIM, acc[:t], acc[t:])


def _causal_mask(t):
    r = lax.broadcasted_iota(jnp.int32, (t, t), 0)
    c = lax.broadcasted_iota(jnp.int32, (t, t), 1)
    m = c < r
    return jnp.concatenate([m, m], axis=0)


def _sb_group(q_st, k_blocks, v_blocks, suffix, carry, masks, transposed_keys):
    (out, carry), = _sb_groups([q_st], [k_blocks], [v_blocks], suffix, [carry], masks, transposed_keys)
    return out, carry


def _sb_groups(q_sts, k_blocks, v_blocks, suffix, carries, masks, transposed_keys):
    dn = (((1,), (0,)), ((), ())) if transposed_keys else (((1,), (1,)), ((), ()))
    zs = [[lax.dot_general(q_st, kb, dn, preferred_element_type=F32) * LOG2E for kb in kbs]
          for q_st, kbs in zip(q_sts, k_blocks)]
    n_blk = len(masks)
    cs = _sb_masses([z for zq in zs for z in zq], masks * len(q_sts), [suffix] * (n_blk * len(q_sts)))
    results = []
    for i, (zq, vbs, carry) in enumerate(zip(zs, v_blocks, carries)):
        out = None
        for j, (z, vb, mask) in enumerate(zip(zq, vbs, masks)):
            a, carry = _sb_weights(z, cs[i * n_blk + j], carry, mask)
            o = jnp.dot(a, vb, preferred_element_type=F32)
            out = o if out is None else out + o
        results.append((out, carry))
    return results


def _sb_masses(zs, masks, suffixes):
    splits = []
    for z, mask in zip(zs, masks):
        sp = _softplus2(z)
        if mask is not None:
            sp = jnp.where(mask, sp, 0.0)
        hi = pltpu.bitcast(pltpu.bitcast(sp, jnp.uint32) & jnp.uint32(0xFFFF0000), F32)
        splits.append((hi.astype(BF16), (sp - hi).astype(BF16)))
    return [jnp.dot(hi, sfx, preferred_element_type=F32) + jnp.dot(lo, sfx, preferred_element_type=F32)
            for (hi, lo), sfx in zip(splits, suffixes)]


def _sb_weights(z, c, carry, mask):
    arg = z - c - carry
    if mask is not None:
        arg = jnp.where(mask, arg, NEG)
    return jnp.exp2(arg).astype(BF16), carry + c[:, 0:1]


def _attn_self_body(q_ref, k_ref, v_ref, o_ref, acc_scr, car_scr, min_scr, *, t):
    qi = pl.program_id(2)
    n_pairs = q_ref.shape[2] // LANES
    lanes = lambda p: slice(p * LANES, (p + 1) * LANES)
    block = lambda ref, p, b: ref[0, pl.ds(pl.multiple_of(b * t, t), t), lanes(p)]

    q_sts = [_stack_heads(q_ref[0, :, lanes(p)]) for p in range(n_pairs)]
    suffix = _suffix_matrix(t)
    mask = _causal_mask(t)

    def run(blocks, masks, first):
        carries = [jnp.zeros((2 * t, 1), F32) if first else car_scr[p, :, 0:1] for p in range(n_pairs)]
        results = _sb_groups(q_sts, [[block(k_ref, p, b) for b in blocks] for p in range(n_pairs)],
                             [[block(v_ref, p, b) for b in blocks] for p in range(n_pairs)], suffix, carries, masks,
                             False)
        low = None
        for p, (out, carry) in enumerate(results):
            acc_scr[p] = out if first else acc_scr[p] + out
            car_scr[p] = jnp.broadcast_to(carry, car_scr.shape[1:])
            low = jnp.min(carry) if low is None else jnp.minimum(low, jnp.min(carry))
        min_scr[0] = low

    @pl.when(qi == 0)
    def _():
        run([0], [mask], True)

    @pl.when(qi > 0)
    def _():
        run([qi, qi - 1], [mask, None], True)

    rest = jnp.maximum(qi - 1, 0)

    def more(it):
        return (it < rest // 2) & (min_scr[0] < SKIP_MASS)

    def pair(it):
        b0 = qi - 2 - 2 * it
        run([b0, b0 - 1], [None, None], False)
        return it + 1

    lax.while_loop(more, pair, 0)

    @pl.when((rest % 2 == 1) & (min_scr[0] < SKIP_MASS))
    def _():
        run([0], [None], False)

    o_ref[0] = jnp.concatenate([_unstack_heads(acc_scr[p], t) for p in range(n_pairs)],
                               axis=1).astype(o_ref.dtype)


def _attn_hist_body(q_ref, kn_ref, vn_ref, kh_ref, vh_ref, o_ref, left_ref, acc_scr, car_scr, min_scr, *, tk,
                    group):
    t = q_ref.shape[1]
    past = kh_ref.shape[1]
    q_st = _stack_heads(q_ref[0])

    def keep(out, carry, first):
        acc_scr[...] = out if first else acc_scr[...] + out
        car_scr[...] = jnp.broadcast_to(carry, car_scr.shape)
        min_scr[0] = jnp.min(carry)

    keep(*_sb_group(q_st, [kn_ref[0].astype(BF16)], [vn_ref[0].astype(BF16)], _suffix_matrix(t),
                    jnp.zeros((2 * t, 1), F32), [_causal_mask(t)], False), True)
    suffix = _suffix_matrix(tk)
    for top in range(past // tk, 0, -group):
        @pl.when(min_scr[0] < SKIP_MASS)
        def _(top=top):
            blocks = range(top - 1, top - 1 - group, -1)
            keep(*_sb_group(q_st, [kh_ref[0, b * tk:(b + 1) * tk, :].astype(BF16) for b in blocks],
                            [vh_ref[0, b * tk:(b + 1) * tk, :].astype(BF16) for b in blocks],
                            suffix, car_scr[:, 0:1], [None] * group, False), False)
    o_ref[0] = _unstack_heads(acc_scr[...], t).astype(o_ref.dtype)
    left_ref[0, 0] = jnp.full(left_ref.shape[2:], min_scr[0], F32)


def _attn_recent_body(q_ref, kn_ref, vn_ref, kc_ref, vc_ref, o_ref, left_ref, *, tk):
    t, dm = q_ref.shape[1:]
    n_blk = kc_ref.shape[1] // tk
    pair = lambda ref, rows, p: ref[0, rows, p * LANES:(p + 1) * LANES].astype(BF16)
    score = lambda qs, kb: lax.dot_general(qs, kb, (((1,), (1,)), ((), ())), preferred_element_type=F32)
    own = slice(None)
    blocks = [pl.ds(j * tk, tk) for j in range(n_blk - 1, -1, -1)]
    mask, sfx_new, sfx_old = _causal_mask(t), _suffix_matrix(t), _suffix_matrix(tk)
    zs, masks, sfx = [], [], []
    for p in range(dm // LANES):
        q_st = _stack_heads(pair(q_ref, own, p))
        zs.append(score(q_st, pair(kn_ref, own, p)) * LOG2E)
        masks.append(mask)
        sfx.append(sfx_new)
        for rows in blocks:
            zs.append(score(q_st, pair(kc_ref, rows, p)) * LOG2E)
            masks.append(None)
            sfx.append(sfx_old)
    cs = _sb_masses(zs, masks, sfx)
    outs, left = [], None
    chain = 1 + n_blk
    for p in range(dm // LANES):
        carry = jnp.zeros((2 * t, 1), F32)
        out = None
        for i in range(chain):
            a, carry = _sb_weights(zs[p * chain + i], cs[p * chain + i], carry, masks[p * chain + i])
            vb = pair(vn_ref, own, p) if i == 0 else pair(vc_ref, blocks[i - 1], p)
            o = jnp.dot(a, vb, preferred_element_type=F32)
            out = o if out is None else out + o
        outs.append(_unstack_heads(out, t))
        low = jnp.min(carry)
        left = low if left is None else jnp.minimum(left, low)
    o_ref[0] = jnp.concatenate(outs, axis=1).astype(o_ref.dtype)
    left_ref[0] = jnp.full(left_ref.shape[1:], left, F32)


def _attn_specs(t):
    tile = pl.BlockSpec((1, t, LANES), lambda i, p, s: (i, s, p))
    seq = lambda rows: pl.BlockSpec((1, rows, LANES), lambda i, p, s: (i, 0, p))
    state = [pltpu.VMEM((2 * t, LANES), F32), pltpu.VMEM((2 * t, LANES), F32), pltpu.SMEM((1,), F32)]
    return tile, seq, state


def _attention_self(q, k, v, t, pairs):
    b, n, dm = q.shape
    width = pairs * LANES
    assert n % t == 0 and dm % width == 0
    tile = pl.BlockSpec((1, t, width), lambda i, p, s: (i, s, p))
    seq = pl.BlockSpec((1, n, width), lambda i, p, s: (i, 0, p))
    return pl.pallas_call(
        functools.partial(_attn_self_body, t=t),
        out_shape=jax.ShapeDtypeStruct((b, n, dm), BF16),
        grid=(b, dm // width, n // t),
        in_specs=[tile, seq, seq],
        out_specs=tile,
        scratch_shapes=[pltpu.VMEM((pairs, 2 * t, LANES), F32), pltpu.VMEM((pairs, 2 * t, LANES), F32),
                        pltpu.SMEM((1,), F32)],
        compiler_params=_cparams(("parallel", "parallel", "arbitrary")),
        name="sb_attention",
    )(q, k, v)


def _attention_cached(q, k_new, v_new, k_hist, v_hist, tk, group):
    b, t, dm = q.shape
    past = k_hist.shape[1]
    assert dm % LANES == 0 and past % (tk * group) == 0
    tile, seq, state = _attn_specs(t)
    return pl.pallas_call(
        functools.partial(_attn_hist_body, tk=tk, group=group),
        out_shape=[jax.ShapeDtypeStruct((b, t, dm), BF16), jax.ShapeDtypeStruct((b, dm // LANES, 8, LANES), F32)],
        grid=(b, dm // LANES, 1),
        in_specs=[tile, seq(t), seq(t), seq(past), seq(past)],
        out_specs=[tile, pl.BlockSpec((1, 1, 8, LANES), lambda i, p, s: (i, p, 0, 0))],
        scratch_shapes=state,
        compiler_params=_cparams(("parallel", "parallel", "arbitrary")),
        name="sb_attention_cached",
    )(q, k_new, v_new, k_hist, v_hist)


def _attention_recent_first(q, k_new, v_new, cache_k, cache_v, tk, group):
    b, past, heads, hd = cache_k.shape
    t, dm = q.shape[1:]
    flat = lambda c: c.reshape(b, c.shape[1], heads * hd)
    walk = lambda: _attention_cached(q, k_new, v_new, flat(cache_k), flat(cache_v), tk, group)[0]
    recent = tk * group
    if past <= recent:
        return walk()
    seq = lambda rows: pl.BlockSpec((1, rows, dm), lambda i: (i, 0, 0))
    o, left = pl.pallas_call(
        functools.partial(_attn_recent_body, tk=tk),
        out_shape=[jax.ShapeDtypeStruct((b, t, dm), BF16), jax.ShapeDtypeStruct((b, 8, LANES), F32)],
        grid=(b,),
        in_specs=[seq(t), seq(t), seq(t), seq(recent), seq(recent)],
        out_specs=[seq(t), pl.BlockSpec((1, 8, LANES), lambda i: (i, 0, 0))],
        compiler_params=_cparams(("parallel",)),
        name="sb_attention_recent",
    )(q, k_new, v_new, flat(cache_k[:, past - recent:]), flat(cache_v[:, past - recent:]))
    return lax.cond(jnp.min(left) < SKIP_MASS, walk, lambda: o)


def _mix_body(*refs, tm, seq_rows):
    if seq_rows:
        u_ref, uprev_ref, *refs, ext_scr = refs
    else:
        diff_ref, *refs = refs
    (o_ref, gp_ref, gs_ref, x_ref, wp_ref, ps_ref, wsb_ref, wo_ref, nf_ref, wr_ref, br_ref,
     x1_ref, hp_ref, route_ref, cnt_ref, logit_scr) = refs

    @pl.when(pl.program_id(0) == 0)
    def _():
        logit_scr[...] = jnp.zeros_like(logit_scr)

    late_logits = logit_scr[...]

    n_pool = wp_ref.shape[0]
    group = wp_ref.shape[1]
    rows = tm // MIX_PARTS
    parts = [pl.ds(i * rows, rows) for i in range(MIX_PARTS)]
    if seq_rows:
        pos0 = (jnp.minimum(pl.program_id(0), pl.num_programs(0) - 2) * tm) % seq_rows
        ext_scr[0:POOL_HIST, :] = jnp.where(pos0 == 0, 0.0, uprev_ref[...])
        ext_scr[POOL_HIST:POOL_HIST + tm, :] = u_ref[...]
        slabs = _pool_windows(ext_scr, tm, pos0 + lax.broadcasted_iota(jnp.int32, (tm, 1), 0))
        diff_of = lambda i, g: slabs[g][i * rows:(i + 1) * rows]
    else:
        diff_of = lambda i, g: diff_ref[parts[i], g * group:(g + 1) * group]
    pools = [jnp.concatenate(
        [jnp.dot(diff_of(i, g), wp_ref[g], preferred_element_type=F32) for g in range(n_pool)],
        axis=-1) * ps_ref[...] for i in range(MIX_PARTS)]
    sbs = [jnp.dot(o_ref[s, :], wsb_ref[...], preferred_element_type=F32) for s in parts]
    mixed = [(gp_ref[s, :].astype(F32) * pool + gs_ref[s, :].astype(F32) * sb).astype(BF16)
             for s, pool, sb in zip(parts, pools, sbs)]
    x1s = [x_ref[s, :] + jnp.dot(mx, wo_ref[...], preferred_element_type=F32) for s, mx in zip(parts, mixed)]
    splits = []
    for s, x1 in zip(parts, x1s):
        x1_ref[s, :] = x1
        h = (x1 * lax.rsqrt(jnp.mean(x1 * x1, axis=-1, keepdims=True) + EPS)) * nf_ref[...]
        d_half = h.shape[1] // 2
        lo_bits = pltpu.bitcast(h[:, :d_half].astype(BF16).astype(F32), jnp.uint32)
        hi_bits = pltpu.bitcast(h[:, d_half:].astype(BF16).astype(F32), jnp.uint32)
        hp_ref[s, :] = (lo_bits >> 16) | (hi_bits & jnp.uint32(0xFFFF0000))
        hh = h.astype(BF16)
        splits.append(jnp.concatenate([hh, (h - hh.astype(F32)).astype(BF16)], axis=0))

    rs = [jnp.dot(sp, wr_ref[...], preferred_element_type=F32) for sp in splits]
    logit_scr[...] = jnp.concatenate(
        [(r[:rows, :LANES] + r[:rows, LANES:]) + (r[rows:, :LANES] + r[rows:, LANES:]) for r in rs],
        axis=0) + br_ref[...]
    _route_tile(late_logits, route_ref, cnt_ref, tm)


def _route_tile(logits, route_ref, cnt_ref, tm):
    lane = lax.broadcasted_iota(jnp.int32, (tm, LANES), 1)
    big = jnp.int32(LANES)

    def first_max(vals):
        m = jnp.max(vals, axis=-1, keepdims=True)
        idx = jnp.min(jnp.where(vals == m, lane, big), axis=-1, keepdims=True)
        return m, idx

    gl = jnp.where(lane < N_GROUPS, logits, NEG)
    gmax, grp = first_max(gl)
    p_grp = 1.0 / jnp.sum(jnp.exp(gl - gmax), axis=-1, keepdims=True)
    e_lo = N_GROUPS + grp * PER_GROUP
    el = jnp.where((lane >= e_lo) & (lane < e_lo + PER_GROUP), logits, NEG)
    m1, i1 = first_max(el)
    m2, i2 = first_max(jnp.where(lane == i1, NEG, el))
    t2 = jnp.exp(m2 - m1)
    w1 = p_grp / (1.0 + t2)
    w2 = w1 * t2
    e1 = i1 - N_GROUPS
    e2 = i2 - N_GROUPS

    oh1 = jnp.where(lane == e1, 1.0, 0.0).astype(BF16)
    oh2 = jnp.where(lane == e2, 1.0, 0.0).astype(BF16)
    rr = lax.broadcasted_iota(jnp.int32, (tm, tm), 0)
    cc = lax.broadcasted_iota(jnp.int32, (tm, tm), 1)
    before = jnp.where(cc < rr, 1.0, 0.0).astype(BF16)
    ones = jnp.ones((8, tm), BF16)
    pre1 = jnp.dot(before, oh1, preferred_element_type=F32)
    pre2 = jnp.dot(before, oh2, preferred_element_type=F32)
    c1 = jnp.dot(ones, oh1, preferred_element_type=F32)
    c2 = jnp.dot(ones, oh2, preferred_element_type=F32)
    rank1 = jnp.sum(jnp.where(lane == e1, pre1, 0.0), axis=-1, keepdims=True)
    rank2 = jnp.sum(jnp.where(lane == e2, pre2 + c1[0:1, :], 0.0), axis=-1, keepdims=True)
    cnt_ref[0] = c1 + c2

    route = jnp.where(lane == 0, e1.astype(F32), 0.0)
    route = jnp.where(lane == 1, e2.astype(F32), route)
    route = jnp.where(lane == 2, w1, route)
    route = jnp.where(lane == 3, w2, route)
    route = jnp.where(lane == 4, rank1, route)
    route = jnp.where(lane == 5, rank2, route)
    route_ref[...] = route


def _mix(pool_in, o, gp, gs, x, wp, ps, wsb, wo, nf, wr, br, tm, seq_rows=0):
    m, d = x.shape
    assert m % tm == 0 and seq_rows % tm == 0
    n = m // tm
    row = lambda i: (jnp.minimum(i, n - 1), 0)
    late = lambda i: (jnp.maximum(i - 1, 0), 0)
    full = lambda a: pl.BlockSpec(a.shape, lambda i: (0,) * a.ndim, pipeline_mode=pl.Buffered(1))
    acts = [pool_in, o, gp, gs, x]
    consts = [wp, ps, wsb, wo, nf, wr, br]
    in_specs = [pl.BlockSpec((tm, a.shape[1]), row) for a in acts] + [full(a) for a in consts]
    scratch = [pltpu.VMEM((tm, LANES), F32)]
    if seq_rows:
        before = lambda i: (jnp.maximum(row(i)[0] * (tm // POOL_HIST) - 1, 0), 0)
        acts.insert(1, pool_in)
        in_specs.insert(1, pl.BlockSpec((POOL_HIST, pool_in.shape[1]), before))
        scratch.append(pltpu.VMEM((POOL_HIST + tm, pool_in.shape[1]), F32))
    return pl.pallas_call(
        functools.partial(_mix_body, tm=tm, seq_rows=seq_rows),
        out_shape=[jax.ShapeDtypeStruct((m, d), F32), jax.ShapeDtypeStruct((m, d // 2), jnp.uint32),
                   jax.ShapeDtypeStruct((m, LANES), F32), jax.ShapeDtypeStruct((n, 8, LANES), F32)],
        grid=(n + 1,),
        in_specs=in_specs,
        out_specs=[pl.BlockSpec((tm, d), row), pl.BlockSpec((tm, d // 2), row),
                   pl.BlockSpec((tm, LANES), late), pl.BlockSpec((1, 8, LANES), lambda i: late(i) + (0,))],
        scratch_shapes=scratch,
        compiler_params=_cparams(("arbitrary",)),
        name="mix_outproj_router",
    )(*acts, *consts)


def _row_wait(src_ref, dst_ref, sem, n_rows):
    pltpu.make_async_copy(src_ref.at[pl.ds(0, n_rows)], dst_ref.at[pl.ds(0, n_rows)], sem).wait()


def _dispatch_body(slot_ref, zero_ref, hp_ref, xs_ref, buf, zbuf, sem, zsem, *, tm):
    i = pl.program_id(0)
    last = pl.num_programs(0) - 1
    par = i % 2

    @pl.when(i == 0)
    def _():
        zbuf[...] = jnp.zeros_like(zbuf)
        bm = zbuf.shape[0]
        for wait in (False, True):
            for z in range(zero_ref.shape[0]):
                @pl.when(zero_ref[z] >= 0)
                def _(z=z, wait=wait):
                    start = pl.multiple_of(jnp.maximum(zero_ref[z], 0), bm)
                    copy = pltpu.make_async_copy(zbuf, xs_ref.at[pl.ds(start, bm)], zsem)
                    copy.wait() if wait else copy.start()

    def drain(slot):
        for _ in range(2):
            _row_wait(buf.at[slot], xs_ref, sem.at[slot], tm)

    @pl.when(i >= 2)
    def _():
        drain(par)

    buf[par] = hp_ref[...]
    base = i * tm

    def body(t, _):
        src = buf.at[par, pl.ds(t, 1)]
        for kk in range(2):
            dst = xs_ref.at[pl.ds(slot_ref[2 * (base + t) + kk], 1)]
            pltpu.make_async_copy(src, dst, sem.at[par]).start()
        return 0

    lax.fori_loop(0, tm, body, 0, unroll=8)

    @pl.when(i == last)
    def _():
        drain(par)

        @pl.when(i >= 1)
        def _():
            drain(1 - par)


def _dispatch(slots_flat, zero_starts, hp, n_slots, tm, bm):
    m, dh = hp.shape
    assert m % tm == 0
    return pl.pallas_call(
        functools.partial(_dispatch_body, tm=tm),
        out_shape=jax.ShapeDtypeStruct((n_slots, dh), hp.dtype),
        grid_spec=pltpu.PrefetchScalarGridSpec(
            num_scalar_prefetch=2, grid=(m // tm,),
            in_specs=[pl.BlockSpec((tm, dh), lambda i, s, z: (i, 0))],
            out_specs=pl.BlockSpec(memory_space=pl.ANY),
            scratch_shapes=[pltpu.VMEM((2, tm, dh), hp.dtype), pltpu.VMEM((bm, dh), hp.dtype),
                            pltpu.SemaphoreType.DMA((2,)), pltpu.SemaphoreType.DMA]),
        compiler_params=_cparams(("arbitrary",), disable_bounds_checks=True, has_side_effects=True),
        name="moe_dispatch",
    )(slots_flat, zero_starts, hp)


def _ffn_body(be_ref, nv_ref, xs_ref, wg_ref, wu_ref, wd_ref, y_ref):
    del be_ref

    @pl.when(pl.program_id(0) >= nv_ref[0])
    def _():
        y_ref[...] = jnp.zeros_like(y_ref)

    @pl.when(pl.program_id(0) < nv_ref[0])
    def _():
        d_half = xs_ref.shape[1]
        rows = xs_ref.shape[0] // FFN_PARTS
        hids = []
        for i in range(FFN_PARTS):
            words = xs_ref[pl.ds(i * rows, rows), :]
            x_lo = pltpu.bitcast(words << 16, F32).astype(BF16)
            x_hi = pltpu.bitcast(words & jnp.uint32(0xFFFF0000), F32).astype(BF16)

            def proj(w_ref):
                return (jnp.dot(x_lo, w_ref[0, :d_half, :], preferred_element_type=F32)
                        + jnp.dot(x_hi, w_ref[0, d_half:, :], preferred_element_type=F32))

            hids.append((jax.nn.silu(proj(wg_ref)) * proj(wu_ref)).astype(BF16))
        for i, hid in enumerate(hids):
            y_ref[pl.ds(i * rows, rows), :] = jnp.dot(hid, wd_ref[0], preferred_element_type=F32)


def _ffn(block_expert, n_valid, xs, wg, wu, wd, bm):
    n_slots, d_half = xs.shape
    d = 2 * d_half
    de = wg.shape[2]
    live = lambda b, be, nv: (jnp.minimum(b, nv[0] - 1), 0)
    wsel = lambda b, be, nv: (be[b], 0, 0)
    return pl.pallas_call(
        _ffn_body,
        out_shape=jax.ShapeDtypeStruct((n_slots, d), F32),
        grid_spec=pltpu.PrefetchScalarGridSpec(
            num_scalar_prefetch=2, grid=(n_slots // bm,),
            in_specs=[pl.BlockSpec((bm, d_half), live), pl.BlockSpec((1, d, de), wsel),
                      pl.BlockSpec((1, d, de), wsel), pl.BlockSpec((1, de, d), wsel)],
            out_specs=pl.BlockSpec((bm, d), lambda b, be, nv: (b, 0))),
        compiler_params=_cparams(("arbitrary",)),
        name="moe_ffn",
    )(block_expert, n_valid, xs, wg, wu, wd)


def _final_body(slot_ref, x1_ref, route_ref, g_ref, y_hbm, o_ref, buf, sem, *, tm):
    i = pl.program_id(0)
    n_steps = pl.num_programs(0)

    def issue(step, par):
        def body(t, _):
            for kk in range(2):
                src = y_hbm.at[pl.ds(slot_ref[2 * (step * tm + t) + kk], 1)]
                pltpu.make_async_copy(src, buf.at[par, kk, pl.ds(t, 1)], sem.at[par]).start()
            return 0
        lax.fori_loop(0, tm, body, 0, unroll=8)

    @pl.when(i == 0)
    def _():
        issue(0, 0)

    @pl.when(i + 1 < n_steps)
    def _():
        issue(i + 1, (i + 1) % 2)

    par = i % 2
    for kk in range(2):
        _row_wait(y_hbm, buf.at[par, kk], sem.at[par], tm)
    route = route_ref[...]
    x2 = x1_ref[...] + route[:, 2:3] * buf[par, 0] + route[:, 3:4] * buf[par, 1]
    o_ref[...] = (x2 * lax.rsqrt(jnp.mean(x2 * x2, axis=-1, keepdims=True) + EPS)) * g_ref[...]


def _final(slots_flat, x1, route, g, y, tm):
    m, d = x1.shape
    assert m % tm == 0
    row = lambda i, s: (i, 0)
    return pl.pallas_call(
        functools.partial(_final_body, tm=tm),
        out_shape=jax.ShapeDtypeStruct((m, d), F32),
        grid_spec=pltpu.PrefetchScalarGridSpec(
            num_scalar_prefetch=1, grid=(m // tm,),
            in_specs=[pl.BlockSpec((tm, d), row), pl.BlockSpec((tm, LANES), row),
                      pl.BlockSpec((1, d), lambda i, s: (0, 0)), pl.BlockSpec(memory_space=pl.ANY)],
            out_specs=pl.BlockSpec((tm, d), row),
            scratch_shapes=[pltpu.VMEM((2, 2, tm, d), F32), pltpu.SemaphoreType.DMA((2,))]),
        compiler_params=_cparams(("arbitrary",), disable_bounds_checks=True),
        name="moe_combine_final_norm",
    )(slots_flat, x1, route, g, y)


def _routing_tables(route, cnt, tm, bm):
    m = route.shape[0]
    n_blocks = (2 * m) // bm + N_EXPERTS
    counts = cnt[:, 0, :N_EXPERTS].astype(jnp.int32)
    sizes = jnp.sum(counts, axis=0)
    padded = (sizes + bm - 1) // bm * bm
    pad_end = jnp.cumsum(padded)
    base = (pad_end - padded)[None, :] + jnp.cumsum(counts, axis=0) - counts
    base_tok = jnp.repeat(base, tm, axis=0)
    e = route[:, 0:2].astype(jnp.int32)
    rank = route[:, 4:6].astype(jnp.int32)
    sel = e[:, :, None] == jnp.arange(N_EXPERTS, dtype=jnp.int32)[None, None, :]
    slots = jnp.sum(jnp.where(sel, base_tok[:, None, :], 0), axis=-1) + rank
    n_valid = (pad_end[-1] // bm).astype(jnp.int32)
    blk = jnp.minimum(jnp.arange(n_blocks, dtype=jnp.int32), n_valid - 1)
    block_expert = jnp.minimum(jnp.sum(pad_end[None, :] <= (blk * bm)[:, None], axis=1), N_EXPERTS - 1)
    last_blk = jnp.where(padded > 0, pad_end - bm, -1)
    tail = n_valid + jnp.arange(N_EXPERTS, dtype=jnp.int32)
    tail = jnp.where(tail < n_blocks, tail * bm, -1)
    zero_starts = jnp.concatenate([last_blk, tail]).astype(jnp.int32)
    return slots.reshape(-1), block_expert.astype(jnp.int32), n_valid.reshape(1), zero_starts, n_blocks * bm


def _stream(x, pool_hist, k_hist, v_hist, p, *, tq, tk, group, tn, tm_in, tm_mix, tm_fin, bm):
    b, n, d = x.shape
    m = b * n
    past = 0 if k_hist is None else k_hist.shape[1]
    x2d = x.reshape(m, d)
    u, q, kb, vb, k, v, gp, gs = _inproj(x2d, p['norm_mix'], p['w_in'], _fit(m, tm_in))
    dp = u.shape[1]
    fused_pool = pool_hist is None and n % tm_mix == 0
    if fused_pool:
        pool_in = u
    else:
        hist = jnp.zeros((b, POOL_HIST, dp), F32) if pool_hist is None else pool_hist
        pool_in = _pool_diff(hist, u.reshape(b, n, dp), past, tn).reshape(m, dp)
    shp = (b, n, kb.shape[1])
    if k_hist is None:
        o = _attention_self(q.reshape(shp), kb.reshape(shp), vb.reshape(shp), tq, ATTN_PAIRS)
    else:
        o = _attention_recent_first(q.reshape(shp), kb.reshape(shp), vb.reshape(shp), k_hist, v_hist, tk, group)
    o = o.reshape(m, -1)
    x1, hp, route, cnt = _mix(pool_in, o, gp, gs, x2d, p['w_pool'], p['pool_scale'], p['w_sb_out'], p['w_out'],
                              p['norm_ffn'], p['w_r'], p['b_r'], tm_mix, n if fused_pool else 0)
    slots, block_expert, n_valid, zero_starts, n_slots = _routing_tables(route, cnt, tm_mix, bm)
    xs = _dispatch(slots, zero_starts, hp, n_slots, tm_mix, bm)
    y = _ffn(block_expert, n_valid, xs, p['w_g'], p['w_u'], p['w_d'], bm)
    out = _final(slots, x1, route, p['norm_final'], y, tm_fin)
    return out.reshape(b, n, d), u.reshape(b, n, dp), k, v


def kernel(x_prompt, x_sample, cache_sb_k, cache_sb_v, state_pool, norm_mix, w_in, w_pool, pool_scale, w_sb_out,
           w_out, norm_ffn, w_router_group, b_router_group, w_router_expert, b_router_expert, w_exp_gate,
           w_exp_up, w_exp_down, norm_final):
    depth = w_in.shape[0]
    assert depth == 1
    bp, sp, d = x_prompt.shape
    bs, ss, _ = x_sample.shape
    past = cache_sb_k.shape[2]
    heads, hd = cache_sb_k.shape[3], cache_sb_k.shape[4]
    assert hd == HEAD_DIM
    dp = state_pool.shape[3]
    n_state = state_pool.shape[2]

    w_r = jnp.concatenate([w_router_group[0], w_router_expert[0]], axis=1)
    w_r = jnp.pad(w_r, ((0, 0), (0, LANES - w_r.shape[1])))
    w_r_hi = w_r.astype(BF16)
    b_r = jnp.concatenate([b_router_group[0], b_router_expert[0]])
    p = dict(
        norm_mix=norm_mix[0][None, :], w_in=w_in[0].astype(BF16), w_pool=w_pool[0].astype(BF16),
        pool_scale=pool_scale[0][None, :], w_sb_out=w_sb_out[0].astype(BF16), w_out=w_out[0].astype(BF16),
        norm_ffn=norm_ffn[0][None, :],
        w_r=jnp.concatenate([w_r_hi, (w_r - w_r_hi.astype(F32)).astype(BF16)], axis=1),
        b_r=jnp.pad(b_r, (0, LANES - b_r.shape[0]))[None, :].astype(F32),
        w_g=w_exp_gate[0].astype(BF16), w_u=w_exp_up[0].astype(BF16), w_d=w_exp_down[0].astype(BF16),
        norm_final=norm_final[None, :])

    hist_p = jnp.zeros((bp, POOL_HIST, dp), F32)
    yp, up, kp, vp = _stream(x_prompt, None, None, None, p, tq=256, tk=256, group=2, tn=min(sp, 512),
                             tm_in=1024, tm_mix=256, tm_fin=256, bm=256)
    hist_s = jnp.pad(state_pool[0], ((0, 0), (POOL_HIST - n_state, 0), (0, 0)))
    ys, us, ks, vs = _stream(x_sample, hist_s, cache_sb_k[0], cache_sb_v[0], p, tq=ss, tk=256, group=1, tn=ss,
                             tm_in=1024, tm_mix=256, tm_fin=256, bm=256)

    def pool_state(hist, u):
        return jnp.concatenate([hist[:, POOL_HIST - n_state:], u], axis=1)[:, -n_state:][None]

    return (yp, ys,
            kp.reshape(1, bp, sp, heads, hd), vp.reshape(1, bp, sp, heads, hd), pool_state(hist_p, up),
            ks.reshape(1, bs, ss, heads, hd), vs.reshape(1, bs, ss, heads, hd), pool_state(hist_s, us))
```

```python
import functools

import jax
import jax.numpy as jnp
from jax import lax
from jax.experimental import pallas as pl
from jax.experimental.pallas import tpu as pltpu

F32 = jnp.float32
BF16 = jnp.bfloat16

EPS = 1e-6
HEAD_DIM = 64
LANES = 128
POOL_WINDOWS = (2, 4, 8, 16)
POOL_HIST = 16
N_GROUPS = 4
PER_GROUP = 8
N_EXPERTS = N_GROUPS * PER_GROUP
VMEM_LIMIT = 58 * 1024 * 1024
NEG = -1e30
LOG2E = 1.4426950408889634
INPROJ_ROWS = 1024
POOL_ROWS = 512
ATTN_TILE = 256
CACHE_FIRST_BLOCKS = 1
MIX_ROWS = 256
FINAL_ROWS = 256
EXPERT_ROWS = 256
ATTN_PAIRS = 4
MIX_PARTS = 2
FFN_PARTS = 2
SKIP_MASS = 160.0


def _fit(m, tile):
    while m % tile:
        tile //= 2
    return tile


def _cparams(sem, **kw):
    return pltpu.CompilerParams(dimension_semantics=sem, vmem_limit_bytes=VMEM_LIMIT, **kw)


def _proj_main_body(h_ref, w_ref, u_ref, q_ref, kb_ref, vb_ref, k_hbm, v_hbm, kv_buf, sem):
    i = pl.program_id(0)
    j = pl.program_id(1)
    last = pl.num_programs(0) - 1
    tm = h_ref.shape[0]
    heads = k_hbm.shape[1]

    def proj():
        return jnp.dot(h_ref[...], w_ref[...], preferred_element_type=F32)

    def head_copies(slot, dst_hbm, step):
        row0 = pl.multiple_of(step * tm, tm)
        return [pltpu.make_async_copy(kv_buf.at[slot, hh], dst_hbm.at[pl.ds(row0, tm), hh, :], sem.at[slot])
                for hh in range(heads)]

    @pl.when(j == 0)
    def _():
        u_ref[...] = proj()

    @pl.when(j == 1)
    def _():
        q_ref[...] = (proj() * (HEAD_DIM ** -0.5)).astype(BF16)

    for jj, slot, dense_ref, dst_hbm in ((2, 0, kb_ref, k_hbm), (3, 1, vb_ref, v_hbm)):
        @pl.when(j == jj)
        def _(slot=slot, dense_ref=dense_ref, dst_hbm=dst_hbm):
            @pl.when(i > 0)
            def _():
                for c in head_copies(slot, dst_hbm, i - 1):
                    c.wait()
            acc = proj()
            dense_ref[...] = acc.astype(BF16)
            for hh in range(heads):
                kv_buf[slot, hh] = acc[:, hh * HEAD_DIM:(hh + 1) * HEAD_DIM]
            for c in head_copies(slot, dst_hbm, i):
                c.start()

    @pl.when((i == last) & (j == 3))
    def _():
        for slot, dst_hbm in ((0, k_hbm), (1, v_hbm)):
            for c in head_copies(slot, dst_hbm, i):
                c.wait()


def _proj_gates_body(x_ref, g_ref, w_ref, h_ref, gp_ref, gs_ref, *, tn):
    j = pl.program_id(1)

    @pl.when(j == 0)
    def _():
        x = x_ref[...]
        r = lax.rsqrt(jnp.mean(x * x, axis=-1, keepdims=True) + EPS)
        h_ref[...] = ((x * r) * g_ref[...]).astype(BF16)

    for jj, ref in ((0, gp_ref), (2, gs_ref)):
        for half in range(2):
            @pl.when(j == jj + half)
            def _(ref=ref, half=half):
                a = jnp.dot(h_ref[...], w_ref[...], preferred_element_type=F32)
                ref[:, half * tn:(half + 1) * tn] = (0.5 * jnp.tanh(0.5 * a) + 0.5).astype(BF16)


def _inproj(x, g, w_bf, tm):
    m, d = x.shape
    tn = d // 2
    assert w_bf.shape == (d, 8 * tn) and m % tm == 0
    row = lambda i, j: (i, 0)
    h_spec = pl.BlockSpec((tm, d), row)
    h, gp, gs = pl.pallas_call(
        functools.partial(_proj_gates_body, tn=tn),
        out_shape=[jax.ShapeDtypeStruct((m, d), BF16)] * 3,
        grid=(m // tm, 4),
        in_specs=[h_spec, pl.BlockSpec((1, d), lambda i, j: (0, 0)),
                  pl.BlockSpec((d, tn), lambda i, j: (0, j + 4))],
        out_specs=[h_spec] * 3,
        compiler_params=_cparams(("parallel", "arbitrary")),
        name="inproj_gates",
    )(x, g, w_bf)
    dense = lambda dt: jax.ShapeDtypeStruct((m, tn), dt)
    cache = jax.ShapeDtypeStruct((m, tn // HEAD_DIM, HEAD_DIM), F32)
    any_spec = pl.BlockSpec(memory_space=pl.ANY)
    u, q, kb, vb, k, v = pl.pallas_call(
        _proj_main_body,
        out_shape=[dense(F32), dense(BF16), dense(BF16), dense(BF16), cache, cache],
        grid=(m // tm, 4),
        in_specs=[h_spec, pl.BlockSpec((d, tn), lambda i, j: (0, j))],
        out_specs=[pl.BlockSpec((tm, tn), row)] * 4 + [any_spec] * 2,
        scratch_shapes=[pltpu.VMEM((2, tn // HEAD_DIM, tm, HEAD_DIM), F32), pltpu.SemaphoreType.DMA((2,))],
        compiler_params=_cparams(("arbitrary", "arbitrary"), has_side_effects=True),
        name="inproj_main",
    )(h, w_bf)
    return u, q, kb, vb, k, v, gp, gs


def _pool_windows(ext_ref, tn, pos):
    group = ext_ref.shape[1] // len(POOL_WINDOWS)
    slabs = []
    for g, w in enumerate(POOL_WINDOWS):
        lo, hi = g * group, (g + 1) * group
        cur = ext_ref[POOL_HIST:POOL_HIST + tn, lo:hi]
        tot = cur
        for dlt in range(1, w):
            tot = tot + ext_ref[POOL_HIST - dlt:POOL_HIST - dlt + tn, lo:hi]
        cnt = jnp.minimum(pos + 1, w).astype(F32)
        slabs.append((tot / cnt - cur).astype(BF16))
    return slabs


def _pool_body(hist_ref, u_ref, o_ref, ext_scr, *, pos0, tn):
    s = pl.program_id(1)

    @pl.when(s == 0)
    def _():
        ext_scr[0:POOL_HIST, :] = hist_ref[0]

    ext_scr[POOL_HIST:POOL_HIST + tn, :] = u_ref[0]
    pos = pos0 + s * tn + lax.broadcasted_iota(jnp.int32, (tn, 1), 0)
    group = u_ref.shape[2] // len(POOL_WINDOWS)
    for g, slab in enumerate(_pool_windows(ext_scr, tn, pos)):
        o_ref[0, :, g * group:(g + 1) * group] = slab
    ext_scr[0:POOL_HIST, :] = ext_scr[tn:tn + POOL_HIST, :]


def _pool_diff(hist, u, pos0, tn):
    b, n, dp = u.shape
    assert n % tn == 0 and hist.shape == (b, POOL_HIST, dp)
    return pl.pallas_call(
        functools.partial(_pool_body, pos0=pos0, tn=tn),
        out_shape=jax.ShapeDtypeStruct((b, n, dp), BF16),
        grid=(b, n // tn),
        in_specs=[pl.BlockSpec((1, POOL_HIST, dp), lambda i, s: (i, 0, 0)),
                  pl.BlockSpec((1, tn, dp), lambda i, s: (i, s, 0))],
        out_specs=pl.BlockSpec((1, tn, dp), lambda i, s: (i, s, 0)),
        scratch_shapes=[pltpu.VMEM((POOL_HIST + tn, dp), F32)],
        compiler_params=_cparams(("parallel", "arbitrary")),
        name="pool_diff",
    )(hist, u)


def _softplus2(z2):
    neg_abs = pltpu.bitcast(pltpu.bitcast(z2, jnp.uint32) | jnp.uint32(0x80000000), F32)
    return jnp.maximum(z2, 0.0) + jnp.log2(1.0 + jnp.exp2(neg_abs))


def _suffix_matrix(n):
    r = lax.broadcasted_iota(jnp.int32, (n, n), 0)
    c = lax.broadcasted_iota(jnp.int32, (n, n), 1)
    return jnp.where(r >= c, 1.0, 0.0).astype(BF16)


def _stack_heads(q2):
    lane = lax.broadcasted_iota(jnp.int32, q2.shape, 1)
    zero = jnp.zeros_like(q2)
    return jnp.concatenate([jnp.where(lane < HEAD_DIM, q2, zero), jnp.where(lane >= HEAD_DIM, q2, zero)], axis=0)


def _unstack_heads(acc, t):
    lane = lax.broadcasted_iota(jnp.int32, (t, LANES), 1)
    return jnp.where(lane < HEAD_DIM, acc[:t], acc[t:])


def _causal_mask(t):
    r = lax.broadcasted_iota(jnp.int32, (t, t), 0)
    c = lax.broadcasted_iota(jnp.int32, (t, t), 1)
    m = c < r
    return jnp.concatenate([m, m], axis=0)


def _sb_group(q_st, k_blocks, v_blocks, suffix, carry, masks, transposed_keys):
    (out, carry), = _sb_groups([q_st], [k_blocks], [v_blocks], suffix, [carry], masks, transposed_keys)
    return out, carry


def _sb_groups(q_sts, k_blocks, v_blocks, suffix, carries, masks, transposed_keys):
    dn = (((1,), (0,)), ((), ())) if transposed_keys else (((1,), (1,)), ((), ()))
    zs = [[lax.dot_general(q_st, kb, dn, preferred_element_type=F32) * LOG2E for kb in kbs]
          for q_st, kbs in zip(q_sts, k_blocks)]
    n_blk = len(masks)
    cs = _sb_masses([z for zq in zs for z in zq], masks * len(q_sts), [suffix] * (n_blk * len(q_sts)))
    results = []
    for i, (zq, vbs, carry) in enumerate(zip(zs, v_blocks, carries)):
        out = None
        for j, (z, vb, mask) in enumerate(zip(zq, vbs, masks)):
            a, carry = _sb_weights(z, cs[i * n_blk + j], carry, mask)
            o = jnp.dot(a, vb, preferred_element_type=F32)
            out = o if out is None else out + o
        results.append((out, carry))
    return results


def _sb_masses(zs, masks, suffixes):
    splits = []
    for z, mask in zip(zs, masks):
        sp = _softplus2(z)
        if mask is not None:
            sp = jnp.where(mask, sp, 0.0)
        hi = pltpu.bitcast(pltpu.bitcast(sp, jnp.uint32) & jnp.uint32(0xFFFF0000), F32)
        splits.append((hi.astype(BF16), (sp - hi).astype(BF16)))
    return [jnp.dot(hi, sfx, preferred_element_type=F32) + jnp.dot(lo, sfx, preferred_element_type=F32)
            for (hi, lo), sfx in zip(splits, suffixes)]


def _sb_weights(z, c, carry, mask):
    arg = z - c - carry
    if mask is not None:
        arg = jnp.where(mask, arg, NEG)
    return jnp.exp2(arg).astype(BF16), carry + c[:, 0:1]


def _attn_self_body(q_ref, k_ref, v_ref, o_ref, acc_scr, car_scr, min_scr, *, t):
    qi = pl.program_id(2)
    n_pairs = q_ref.shape[2] // LANES
    lanes = lambda p: slice(p * LANES, (p + 1) * LANES)
    block = lambda ref, p, b: ref[0, pl.ds(pl.multiple_of(b * t, t), t), lanes(p)]

    q_sts = [_stack_heads(q_ref[0, :, lanes(p)]) for p in range(n_pairs)]
    suffix = _suffix_matrix(t)
    mask = _causal_mask(t)

    def run(blocks, masks, first):
        carries = [jnp.zeros((2 * t, 1), F32) if first else car_scr[p, :, 0:1] for p in range(n_pairs)]
        results = _sb_groups(q_sts, [[block(k_ref, p, b) for b in blocks] for p in range(n_pairs)],
                             [[block(v_ref, p, b) for b in blocks] for p in range(n_pairs)], suffix, carries, masks,
                             False)
        low = None
        for p, (out, carry) in enumerate(results):
            acc_scr[p] = out if first else acc_scr[p] + out
            car_scr[p] = jnp.broadcast_to(carry, car_scr.shape[1:])
            low = jnp.min(carry) if low is None else jnp.minimum(low, jnp.min(carry))
        min_scr[0] = low

    @pl.when(qi == 0)
    def _():
        run([0], [mask], True)

    @pl.when(qi > 0)
    def _():
        run([qi, qi - 1], [mask, None], True)

    rest = jnp.maximum(qi - 1, 0)

    def more(it):
        return (it < rest // 2) & (min_scr[0] < SKIP_MASS)

    def pair(it):
        b0 = qi - 2 - 2 * it
        run([b0, b0 - 1], [None, None], False)
        return it + 1

    lax.while_loop(more, pair, 0)

    @pl.when((rest % 2 == 1) & (min_scr[0] < SKIP_MASS))
    def _():
        run([0], [None], False)

    o_ref[0] = jnp.concatenate([_unstack_heads(acc_scr[p], t) for p in range(n_pairs)],
                               axis=1).astype(o_ref.dtype)


def _attn_hist_body(q_ref, kn_ref, vn_ref, kh_ref, vh_ref, o_ref, left_ref, acc_scr, car_scr, min_scr, *, tk,
                    group):
    t = q_ref.shape[1]
    past = kh_ref.shape[1]
    q_st = _stack_heads(q_ref[0])

    def keep(out, carry, first):
        acc_scr[...] = out if first else acc_scr[...] + out
        car_scr[...] = jnp.broadcast_to(carry, car_scr.shape)
        min_scr[0] = jnp.min(carry)

    keep(*_sb_group(q_st, [kn_ref[0].astype(BF16)], [vn_ref[0].astype(BF16)], _suffix_matrix(t),
                    jnp.zeros((2 * t, 1), F32), [_causal_mask(t)], False), True)
    suffix = _suffix_matrix(tk)
    for top in range(past // tk, 0, -group):
        @pl.when(min_scr[0] < SKIP_MASS)
        def _(top=top):
            blocks = range(top - 1, top - 1 - group, -1)
            keep(*_sb_group(q_st, [kh_ref[0, b * tk:(b + 1) * tk, :].astype(BF16) for b in blocks],
                            [vh_ref[0, b * tk:(b + 1) * tk, :].astype(BF16) for b in blocks],
                            suffix, car_scr[:, 0:1], [None] * group, False), False)
    o_ref[0] = _unstack_heads(acc_scr[...], t).astype(o_ref.dtype)
    left_ref[0, 0] = jnp.full(left_ref.shape[2:], min_scr[0], F32)


def _attn_recent_body(q_ref, kn_ref, vn_ref, kc_ref, vc_ref, o_ref, left_ref, *, tk):
    t, dm = q_ref.shape[1:]
    n_blk = kc_ref.shape[1] // tk
    pair = lambda ref, rows, p: ref[0, rows, p * LANES:(p + 1) * LANES].astype(BF16)
    score = lambda qs, kb: lax.dot_general(qs, kb, (((1,), (1,)), ((), ())), preferred_element_type=F32)
    own = slice(None)
    blocks = [pl.ds(j * tk, tk) for j in range(n_blk - 1, -1, -1)]
    mask, sfx_new, sfx_old = _causal_mask(t), _suffix_matrix(t), _suffix_matrix(tk)
    zs, masks, sfx = [], [], []
    for p in range(dm // LANES):
        q_st = _stack_heads(pair(q_ref, own, p))
        zs.append(score(q_st, pair(kn_ref, own, p)) * LOG2E)
        masks.append(mask)
        sfx.append(sfx_new)
        for rows in blocks:
            zs.append(score(q_st, pair(kc_ref, rows, p)) * LOG2E)
            masks.append(None)
            sfx.append(sfx_old)
    cs = _sb_masses(zs, masks, sfx)
    outs, left = [], None
    chain = 1 + n_blk
    for p in range(dm // LANES):
        carry = jnp.zeros((2 * t, 1), F32)
        out = None
        for i in range(chain):
            a, carry = _sb_weights(zs[p * chain + i], cs[p * chain + i], carry, masks[p * chain + i])
            vb = pair(vn_ref, own, p) if i == 0 else pair(vc_ref, blocks[i - 1], p)
            o = jnp.dot(a, vb, preferred_element_type=F32)
            out = o if out is None else out + o
        outs.append(_unstack_heads(out, t))
        low = jnp.min(carry)
        left = low if left is None else jnp.minimum(left, low)
    o_ref[0] = jnp.concatenate(outs, axis=1).astype(o_ref.dtype)
    left_ref[0] = jnp.full(left_ref.shape[1:], left, F32)


def _attn_specs(t):
    tile = pl.BlockSpec((1, t, LANES), lambda i, p, s: (i, s, p))
    seq = lambda rows: pl.BlockSpec((1, rows, LANES), lambda i, p, s: (i, 0, p))
    state = [pltpu.VMEM((2 * t, LANES), F32), pltpu.VMEM((2 * t, LANES), F32), pltpu.SMEM((1,), F32)]
    return tile, seq, state


def _attention_self(q, k, v, t, pairs):
    b, n, dm = q.shape
    width = pairs * LANES
    assert n % t == 0 and dm % width == 0
    tile = pl.BlockSpec((1, t, width), lambda i, p, s: (i, s, p))
    seq = pl.BlockSpec((1, n, width), lambda i, p, s: (i, 0, p))
    return pl.pallas_call(
        functools.partial(_attn_self_body, t=t),
        out_shape=jax.ShapeDtypeStruct((b, n, dm), BF16),
        grid=(b, dm // width, n // t),
        in_specs=[tile, seq, seq],
        out_specs=tile,
        scratch_shapes=[pltpu.VMEM((pairs, 2 * t, LANES), F32), pltpu.VMEM((pairs, 2 * t, LANES), F32),
                        pltpu.SMEM((1,), F32)],
        compiler_params=_cparams(("parallel", "parallel", "arbitrary")),
        name="sb_attention",
    )(q, k, v)


def _attention_cached(q, k_new, v_new, k_hist, v_hist, tk, group):
    b, t, dm = q.shape
    past = k_hist.shape[1]
    assert dm % LANES == 0 and past % (tk * group) == 0
    tile, seq, state = _attn_specs(t)
    return pl.pallas_call(
        functools.partial(_attn_hist_body, tk=tk, group=group),
        out_shape=[jax.ShapeDtypeStruct((b, t, dm), BF16), jax.ShapeDtypeStruct((b, dm // LANES, 8, LANES), F32)],
        grid=(b, dm // LANES, 1),
        in_specs=[tile, seq(t), seq(t), seq(past), seq(past)],
        out_specs=[tile, pl.BlockSpec((1, 1, 8, LANES), lambda i, p, s: (i, p, 0, 0))],
        scratch_shapes=state,
        compiler_params=_cparams(("parallel", "parallel", "arbitrary")),
        name="sb_attention_cached",
    )(q, k_new, v_new, k_hist, v_hist)


def _attention_recent_first(q, k_new, v_new, cache_k, cache_v, tk, group):
    b, past, heads, hd = cache_k.shape
    t, dm = q.shape[1:]
    flat = lambda c: c.reshape(b, c.shape[1], heads * hd)
    walk = lambda: _attention_cached(q, k_new, v_new, flat(cache_k), flat(cache_v), tk, group)[0]
    recent = tk * group
    if past <= recent:
        return walk()
    seq = lambda rows: pl.BlockSpec((1, rows, dm), lambda i: (i, 0, 0))
    o, left = pl.pallas_call(
        functools.partial(_attn_recent_body, tk=tk),
        out_shape=[jax.ShapeDtypeStruct((b, t, dm), BF16), jax.ShapeDtypeStruct((b, 8, LANES), F32)],
        grid=(b,),
        in_specs=[seq(t), seq(t), seq(t), seq(recent), seq(recent)],
        out_specs=[seq(t), pl.BlockSpec((1, 8, LANES), lambda i: (i, 0, 0))],
        compiler_params=_cparams(("parallel",)),
        name="sb_attention_recent",
    )(q, k_new, v_new, flat(cache_k[:, past - recent:]), flat(cache_v[:, past - recent:]))
    return lax.cond(jnp.min(left) < SKIP_MASS, walk, lambda: o)


def _mix_body(*refs, tm, seq_rows):
    if seq_rows:
        u_ref, uprev_ref, *refs, ext_scr = refs
    else:
        diff_ref, *refs = refs
    (o_ref, gp_ref, gs_ref, x_ref, wp_ref, ps_ref, wsb_ref, wo_ref, nf_ref, wr_ref, br_ref,
     x1_ref, hp_ref, route_ref, cnt_ref, logit_scr) = refs

    @pl.when(pl.program_id(0) == 0)
    def _():
        logit_scr[...] = jnp.zeros_like(logit_scr)

    late_logits = logit_scr[...]

    n_pool = wp_ref.shape[0]
    group = wp_ref.shape[1]
    rows = tm // MIX_PARTS
    parts = [pl.ds(i * rows, rows) for i in range(MIX_PARTS)]
    if seq_rows:
        pos0 = (jnp.minimum(pl.program_id(0), pl.num_programs(0) - 2) * tm) % seq_rows
        ext_scr[0:POOL_HIST, :] = jnp.where(pos0 == 0, 0.0, uprev_ref[...])
        ext_scr[POOL_HIST:POOL_HIST + tm, :] = u_ref[...]
        slabs = _pool_windows(ext_scr, tm, pos0 + lax.broadcasted_iota(jnp.int32, (tm, 1), 0))
        diff_of = lambda i, g: slabs[g][i * rows:(i + 1) * rows]
    else:
        diff_of = lambda i, g: diff_ref[parts[i], g * group:(g + 1) * group]
    pools = [jnp.concatenate(
        [jnp.dot(diff_of(i, g), wp_ref[g], preferred_element_type=F32) for g in range(n_pool)],
        axis=-1) * ps_ref[...] for i in range(MIX_PARTS)]
    sbs = [jnp.dot(o_ref[s, :], wsb_ref[...], preferred_element_type=F32) for s in parts]
    mixed = [(gp_ref[s, :].astype(F32) * pool + gs_ref[s, :].astype(F32) * sb).astype(BF16)
             for s, pool, sb in zip(parts, pools, sbs)]
    x1s = [x_ref[s, :] + jnp.dot(mx, wo_ref[...], preferred_element_type=F32) for s, mx in zip(parts, mixed)]
    splits = []
    for s, x1 in zip(parts, x1s):
        x1_ref[s, :] = x1
        h = (x1 * lax.rsqrt(jnp.mean(x1 * x1, axis=-1, keepdims=True) + EPS)) * nf_ref[...]
        d_half = h.shape[1] // 2
        lo_bits = pltpu.bitcast(h[:, :d_half].astype(BF16).astype(F32), jnp.uint32)
        hi_bits = pltpu.bitcast(h[:, d_half:].astype(BF16).astype(F32), jnp.uint32)
        hp_ref[s, :] = (lo_bits >> 16) | (hi_bits & jnp.uint32(0xFFFF0000))
        hh = h.astype(BF16)
        splits.append(jnp.concatenate([hh, (h - hh.astype(F32)).astype(BF16)], axis=0))

    rs = [jnp.dot(sp, wr_ref[...], preferred_element_type=F32) for sp in splits]
    logit_scr[...] = jnp.concatenate(
        [(r[:rows, :LANES] + r[:rows, LANES:]) + (r[rows:, :LANES] + r[rows:, LANES:]) for r in rs],
        axis=0) + br_ref[...]
    _route_tile(late_logits, route_ref, cnt_ref, tm)


def _route_tile(logits, route_ref, cnt_ref, tm):
    lane = lax.broadcasted_iota(jnp.int32, (tm, LANES), 1)
    big = jnp.int32(LANES)

    def first_max(vals):
        m = jnp.max(vals, axis=-1, keepdims=True)
        idx = jnp.min(jnp.where(vals == m, lane, big), axis=-1, keepdims=True)
        return m, idx

    gl = jnp.where(lane < N_GROUPS, logits, NEG)
    gmax, grp = first_max(gl)
    p_grp = 1.0 / jnp.sum(jnp.exp(gl - gmax), axis=-1, keepdims=True)
    e_lo = N_GROUPS + grp * PER_GROUP
    el = jnp.where((lane >= e_lo) & (lane < e_lo + PER_GROUP), logits, NEG)
    m1, i1 = first_max(el)
    m2, i2 = first_max(jnp.where(lane == i1, NEG, el))
    t2 = jnp.exp(m2 - m1)
    w1 = p_grp / (1.0 + t2)
    w2 = w1 * t2
    e1 = i1 - N_GROUPS
    e2 = i2 - N_GROUPS

    oh1 = jnp.where(lane == e1, 1.0, 0.0).astype(BF16)
    oh2 = jnp.where(lane == e2, 1.0, 0.0).astype(BF16)
    rr = lax.broadcasted_iota(jnp.int32, (tm, tm), 0)
    cc = lax.broadcasted_iota(jnp.int32, (tm, tm), 1)
    before = jnp.where(cc < rr, 1.0, 0.0).astype(BF16)
    ones = jnp.ones((8, tm), BF16)
    pre1 = jnp.dot(before, oh1, preferred_element_type=F32)
    pre2 = jnp.dot(before, oh2, preferred_element_type=F32)
    c1 = jnp.dot(ones, oh1, preferred_element_type=F32)
    c2 = jnp.dot(ones, oh2, preferred_element_type=F32)
    rank1 = jnp.sum(jnp.where(lane == e1, pre1, 0.0), axis=-1, keepdims=True)
    rank2 = jnp.sum(jnp.where(lane == e2, pre2 + c1[0:1, :], 0.0), axis=-1, keepdims=True)
    cnt_ref[0] = c1 + c2

    route = jnp.where(lane == 0, e1.astype(F32), 0.0)
    route = jnp.where(lane == 1, e2.astype(F32), route)
    route = jnp.where(lane == 2, w1, route)
    route = jnp.where(lane == 3, w2, route)
    route = jnp.where(lane == 4, rank1, route)
    route = jnp.where(lane == 5, rank2, route)
    route_ref[...] = route


def _mix(pool_in, o, gp, gs, x, wp, ps, wsb, wo, nf, wr, br, tm, seq_rows=0):
    m, d = x.shape
    assert m % tm == 0 and seq_rows % tm == 0
    n = m // tm
    row = lambda i: (jnp.minimum(i, n - 1), 0)
    late = lambda i: (jnp.maximum(i - 1, 0), 0)
    full = lambda a: pl.BlockSpec(a.shape, lambda i: (0,) * a.ndim, pipeline_mode=pl.Buffered(1))
    acts = [pool_in, o, gp, gs, x]
    consts = [wp, ps, wsb, wo, nf, wr, br]
    in_specs = [pl.BlockSpec((tm, a.shape[1]), row) for a in acts] + [full(a) for a in consts]
    scratch = [pltpu.VMEM((tm, LANES), F32)]
    if seq_rows:
        before = lambda i: (jnp.maximum(row(i)[0] * (tm // POOL_HIST) - 1, 0), 0)
        acts.insert(1, pool_in)
        in_specs.insert(1, pl.BlockSpec((POOL_HIST, pool_in.shape[1]), before))
        scratch.append(pltpu.VMEM((POOL_HIST + tm, pool_in.shape[1]), F32))
    return pl.pallas_call(
        functools.partial(_mix_body, tm=tm, seq_rows=seq_rows),
        out_shape=[jax.ShapeDtypeStruct((m, d), F32), jax.ShapeDtypeStruct((m, d // 2), jnp.uint32),
                   jax.ShapeDtypeStruct((m, LANES), F32), jax.ShapeDtypeStruct((n, 8, LANES), F32)],
        grid=(n + 1,),
        in_specs=in_specs,
        out_specs=[pl.BlockSpec((tm, d), row), pl.BlockSpec((tm, d // 2), row),
                   pl.BlockSpec((tm, LANES), late), pl.BlockSpec((1, 8, LANES), lambda i: late(i) + (0,))],
        scratch_shapes=scratch,
        compiler_params=_cparams(("arbitrary",)),
        name="mix_outproj_router",
    )(*acts, *consts)


def _row_wait(src_ref, dst_ref, sem, n_rows):
    pltpu.make_async_copy(src_ref.at[pl.ds(0, n_rows)], dst_ref.at[pl.ds(0, n_rows)], sem).wait()


def _dispatch_body(slot_ref, zero_ref, hp_ref, xs_ref, buf, zbuf, sem, zsem, *, tm):
    i = pl.program_id(0)
    last = pl.num_programs(0) - 1
    par = i % 2

    @pl.when(i == 0)
    def _():
        zbuf[...] = jnp.zeros_like(zbuf)
        bm = zbuf.shape[0]
        for wait in (False, True):
            for z in range(zero_ref.shape[0]):
                @pl.when(zero_ref[z] >= 0)
                def _(z=z, wait=wait):
                    start = pl.multiple_of(jnp.maximum(zero_ref[z], 0), bm)
                    copy = pltpu.make_async_copy(zbuf, xs_ref.at[pl.ds(start, bm)], zsem)
                    copy.wait() if wait else copy.start()

    def drain(slot):
        for _ in range(2):
            _row_wait(buf.at[slot], xs_ref, sem.at[slot], tm)

    @pl.when(i >= 2)
    def _():
        drain(par)

    buf[par] = hp_ref[...]
    base = i * tm

    def body(t, _):
        src = buf.at[par, pl.ds(t, 1)]
        for kk in range(2):
            dst = xs_ref.at[pl.ds(slot_ref[2 * (base + t) + kk], 1)]
            pltpu.make_async_copy(src, dst, sem.at[par]).start()
        return 0

    lax.fori_loop(0, tm, body, 0, unroll=8)

    @pl.when(i == last)
    def _():
        drain(par)

        @pl.when(i >= 1)
        def _():
            drain(1 - par)


def _dispatch(slots_flat, zero_starts, hp, n_slots, tm, bm):
    m, dh = hp.shape
    assert m % tm == 0
    return pl.pallas_call(
        functools.partial(_dispatch_body, tm=tm),
        out_shape=jax.ShapeDtypeStruct((n_slots, dh), hp.dtype),
        grid_spec=pltpu.PrefetchScalarGridSpec(
            num_scalar_prefetch=2, grid=(m // tm,),
            in_specs=[pl.BlockSpec((tm, dh), lambda i, s, z: (i, 0))],
            out_specs=pl.BlockSpec(memory_space=pl.ANY),
            scratch_shapes=[pltpu.VMEM((2, tm, dh), hp.dtype), pltpu.VMEM((bm, dh), hp.dtype),
                            pltpu.SemaphoreType.DMA((2,)), pltpu.SemaphoreType.DMA]),
        compiler_params=_cparams(("arbitrary",), disable_bounds_checks=True, has_side_effects=True),
        name="moe_dispatch",
    )(slots_flat, zero_starts, hp)


def _ffn_body(be_ref, nv_ref, xs_ref, wg_ref, wu_ref, wd_ref, y_ref):
    del be_ref

    @pl.when(pl.program_id(0) >= nv_ref[0])
    def _():
        y_ref[...] = jnp.zeros_like(y_ref)

    @pl.when(pl.program_id(0) < nv_ref[0])
    def _():
        d_half = xs_ref.shape[1]
        rows = xs_ref.shape[0] // FFN_PARTS
        hids = []
        for i in range(FFN_PARTS):
            words = xs_ref[pl.ds(i * rows, rows), :]
            x_lo = pltpu.bitcast(words << 16, F32).astype(BF16)
            x_hi = pltpu.bitcast(words & jnp.uint32(0xFFFF0000), F32).astype(BF16)

            def proj(w_ref):
                return (jnp.dot(x_lo, w_ref[0, :d_half, :], preferred_element_type=F32)
                        + jnp.dot(x_hi, w_ref[0, d_half:, :], preferred_element_type=F32))

            hids.append((jax.nn.silu(proj(wg_ref)) * proj(wu_ref)).astype(BF16))
        for i, hid in enumerate(hids):
            y_ref[pl.ds(i * rows, rows), :] = jnp.dot(hid, wd_ref[0], preferred_element_type=F32)


def _ffn(block_expert, n_valid, xs, wg, wu, wd, bm):
    n_slots, d_half = xs.shape
    d = 2 * d_half
    de = wg.shape[2]
    live = lambda b, be, nv: (jnp.minimum(b, nv[0] - 1), 0)
    wsel = lambda b, be, nv: (be[b], 0, 0)
    return pl.pallas_call(
        _ffn_body,
        out_shape=jax.ShapeDtypeStruct((n_slots, d), F32),
        grid_spec=pltpu.PrefetchScalarGridSpec(
            num_scalar_prefetch=2, grid=(n_slots // bm,),
            in_specs=[pl.BlockSpec((bm, d_half), live), pl.BlockSpec((1, d, de), wsel),
                      pl.BlockSpec((1, d, de), wsel), pl.BlockSpec((1, de, d), wsel)],
            out_specs=pl.BlockSpec((bm, d), lambda b, be, nv: (b, 0))),
        compiler_params=_cparams(("arbitrary",)),
        name="moe_ffn",
    )(block_expert, n_valid, xs, wg, wu, wd)


def _final_body(slot_ref, x1_ref, route_ref, g_ref, y_hbm, o_ref, buf, sem, *, tm):
    i = pl.program_id(0)
    n_steps = pl.num_programs(0)

    def issue(step, par):
        def body(t, _):
            for kk in range(2):
                src = y_hbm.at[pl.ds(slot_ref[2 * (step * tm + t) + kk], 1)]
                pltpu.make_async_copy(src, buf.at[par, kk, pl.ds(t, 1)], sem.at[par]).start()
            return 0
        lax.fori_loop(0, tm, body, 0, unroll=8)

    @pl.when(i == 0)
    def _():
        issue(0, 0)

    @pl.when(i + 1 < n_steps)
    def _():
        issue(i + 1, (i + 1) % 2)

    par = i % 2
    for kk in range(2):
        _row_wait(y_hbm, buf.at[par, kk], sem.at[par], tm)
    route = route_ref[...]
    x2 = x1_ref[...] + route[:, 2:3] * buf[par, 0] + route[:, 3:4] * buf[par, 1]
    o_ref[...] = (x2 * lax.rsqrt(jnp.mean(x2 * x2, axis=-1, keepdims=True) + EPS)) * g_ref[...]


def _final(slots_flat, x1, route, g, y, tm):
    m, d = x1.shape
    assert m % tm == 0
    row = lambda i, s: (i, 0)
    return pl.pallas_call(
        functools.partial(_final_body, tm=tm),
        out_shape=jax.ShapeDtypeStruct((m, d), F32),
        grid_spec=pltpu.PrefetchScalarGridSpec(
            num_scalar_prefetch=1, grid=(m // tm,),
            in_specs=[pl.BlockSpec((tm, d), row), pl.BlockSpec((tm, LANES), row),
                      pl.BlockSpec((1, d), lambda i, s: (0, 0)), pl.BlockSpec(memory_space=pl.ANY)],
            out_specs=pl.BlockSpec((tm, d), row),
            scratch_shapes=[pltpu.VMEM((2, 2, tm, d), F32), pltpu.SemaphoreType.DMA((2,))]),
        compiler_params=_cparams(("arbitrary",), disable_bounds_checks=True),
        name="moe_combine_final_norm",
    )(slots_flat, x1, route, g, y)


def _routing_tables(route, cnt, tm, bm):
    m = route.shape[0]
    n_blocks = (2 * m) // bm + N_EXPERTS
    counts = cnt[:, 0, :N_EXPERTS].astype(jnp.int32)
    sizes = jnp.sum(counts, axis=0)
    padded = (sizes + bm - 1) // bm * bm
    pad_end = jnp.cumsum(padded)
    base = (pad_end - padded)[None, :] + jnp.cumsum(counts, axis=0) - counts
    base_tok = jnp.repeat(base, tm, axis=0)
    e = route[:, 0:2].astype(jnp.int32)
    rank = route[:, 4:6].astype(jnp.int32)
    sel = e[:, :, None] == jnp.arange(N_EXPERTS, dtype=jnp.int32)[None, None, :]
    slots = jnp.sum(jnp.where(sel, base_tok[:, None, :], 0), axis=-1) + rank
    n_valid = (pad_end[-1] // bm).astype(jnp.int32)
    blk = jnp.minimum(jnp.arange(n_blocks, dtype=jnp.int32), n_valid - 1)
    block_expert = jnp.minimum(jnp.sum(pad_end[None, :] <= (blk * bm)[:, None], axis=1), N_EXPERTS - 1)
    last_blk = jnp.where(padded > 0, pad_end - bm, -1)
    tail = n_valid + jnp.arange(N_EXPERTS, dtype=jnp.int32)
    tail = jnp.where(tail < n_blocks, tail * bm, -1)
    zero_starts = jnp.concatenate([last_blk, tail]).astype(jnp.int32)
    return slots.reshape(-1), block_expert.astype(jnp.int32), n_valid.reshape(1), zero_starts, n_blocks * bm


def _stream(x, pool_hist, k_hist, v_hist, p):
    b, n, d = x.shape
    m = b * n
    past = 0 if k_hist is None else k_hist.shape[1]
    x2d = x.reshape(m, d)
    u, q, kb, vb, k, v, gp, gs = _inproj(x2d, p['norm_mix'], p['w_in'], _fit(m, INPROJ_ROWS))
    dp = u.shape[1]
    fused_pool = pool_hist is None and n % MIX_ROWS == 0
    if fused_pool:
        pool_in = u
    else:
        hist = jnp.zeros((b, POOL_HIST, dp), F32) if pool_hist is None else pool_hist
        pool_in = _pool_diff(hist, u.reshape(b, n, dp), past, _fit(n, POOL_ROWS)).reshape(m, dp)
    shp = (b, n, kb.shape[1])
    if k_hist is None:
        o = _attention_self(q.reshape(shp), kb.reshape(shp), vb.reshape(shp), ATTN_TILE, ATTN_PAIRS)
    else:
        o = _attention_recent_first(q.reshape(shp), kb.reshape(shp), vb.reshape(shp), k_hist, v_hist, ATTN_TILE,
                                    CACHE_FIRST_BLOCKS)
    o = o.reshape(m, -1)
    x1, hp, route, cnt = _mix(pool_in, o, gp, gs, x2d, p['w_pool'], p['pool_scale'], p['w_sb_out'], p['w_out'],
                              p['norm_ffn'], p['w_r'], p['b_r'], MIX_ROWS, n if fused_pool else 0)
    bm = EXPERT_ROWS if 2 * m >= N_EXPERTS * EXPERT_ROWS else EXPERT_ROWS // 2
    slots, block_expert, n_valid, zero_starts, n_slots = _routing_tables(route, cnt, MIX_ROWS, bm)
    xs = _dispatch(slots, zero_starts, hp, n_slots, MIX_ROWS, bm)
    y = _ffn(block_expert, n_valid, xs, p['w_g'], p['w_u'], p['w_d'], bm)
    out = _final(slots, x1, route, p['norm_final'], y, FINAL_ROWS)
    return out.reshape(b, n, d), u.reshape(b, n, dp), k, v


def kernel(x_prompt, x_sample, cache_sb_k, cache_sb_v, state_pool, norm_mix, w_in, w_pool, pool_scale, w_sb_out,
           w_out, norm_ffn, w_router_group, b_router_group, w_router_expert, b_router_expert, w_exp_gate,
           w_exp_up, w_exp_down, norm_final):
    depth = w_in.shape[0]
    assert depth == 1
    bp, sp, d = x_prompt.shape
    bs, ss, _ = x_sample.shape
    heads, hd = cache_sb_k.shape[3], cache_sb_k.shape[4]
    assert hd == HEAD_DIM
    dp = state_pool.shape[3]
    n_state = state_pool.shape[2]

    w_r = jnp.concatenate([w_router_group[0], w_router_expert[0]], axis=1)
    w_r = jnp.pad(w_r, ((0, 0), (0, LANES - w_r.shape[1])))
    w_r_hi = w_r.astype(BF16)
    b_r = jnp.concatenate([b_router_group[0], b_router_expert[0]])
    p = dict(
        norm_mix=norm_mix[0][None, :], w_in=w_in[0].astype(BF16), w_pool=w_pool[0].astype(BF16),
        pool_scale=pool_scale[0][None, :], w_sb_out=w_sb_out[0].astype(BF16), w_out=w_out[0].astype(BF16),
        norm_ffn=norm_ffn[0][None, :],
        w_r=jnp.concatenate([w_r_hi, (w_r - w_r_hi.astype(F32)).astype(BF16)], axis=1),
        b_r=jnp.pad(b_r, (0, LANES - b_r.shape[0]))[None, :].astype(F32),
        w_g=w_exp_gate[0].astype(BF16), w_u=w_exp_up[0].astype(BF16), w_d=w_exp_down[0].astype(BF16),
        norm_final=norm_final[None, :])

    hist_p = jnp.zeros((bp, POOL_HIST, dp), F32)
    yp, up, kp, vp = _stream(x_prompt, None, None, None, p)
    hist_s = jnp.pad(state_pool[0], ((0, 0), (POOL_HIST - n_state, 0), (0, 0)))
    ys, us, ks, vs = _stream(x_sample, hist_s, cache_sb_k[0], cache_sb_v[0], p)

    def pool_state(hist, u):
        return jnp.concatenate([hist[:, POOL_HIST - n_state:], u], axis=1)[:, -n_state:][None]

    return (yp, ys,
            kp.reshape(1, bp, sp, heads, hd), vp.reshape(1, bp, sp, heads, hd), pool_state(hist_p, up),
            ks.reshape(1, bs, ss, heads, hd), vs.reshape(1, bs, ss, heads, hd), pool_state(hist_s, us))
```

```python
import functools

import jax
import jax.numpy as jnp
from jax import lax
from jax.experimental import pallas as pl
from jax.experimental.pallas import tpu as pltpu

F32 = jnp.float32
BF16 = jnp.bfloat16

EPS = 1e-6
HEAD_DIM = 64
LANES = 128
POOL_WINDOWS = (2, 4, 8, 16)
POOL_HIST = 16
N_GROUPS = 4
PER_GROUP = 8
N_EXPERTS = N_GROUPS * PER_GROUP
VMEM_LIMIT = 58 * 1024 * 1024
NEG = -1e30
LOG2E = 1.4426950408889634
INPROJ_ROWS = 1024
POOL_ROWS = 512
ATTN_TILE = 256
CACHE_FIRST_BLOCKS = 1
MIX_ROWS = 256
FINAL_ROWS = 256
EXPERT_ROWS = 256
ATTN_PAIRS = 4
MIX_PARTS = 2
FFN_PARTS = 2
SKIP_MASS = 160.0


def _fit(m, tile):
    while m % tile:
        tile //= 2
    return tile


def _cparams(sem, **kw):
    return pltpu.CompilerParams(dimension_semantics=sem, vmem_limit_bytes=VMEM_LIMIT, **kw)


def _proj_main_body(h_ref, w_ref, u_ref, q_ref, kb_ref, vb_ref, k_hbm, v_hbm, kv_buf, sem):
    i = pl.program_id(0)
    j = pl.program_id(1)
    last = pl.num_programs(0) - 1
    tm = h_ref.shape[0]
    heads = k_hbm.shape[1]

    def proj():
        return jnp.dot(h_ref[...], w_ref[...], preferred_element_type=F32)

    def head_copies(slot, dst_hbm, step):
        row0 = pl.multiple_of(step * tm, tm)
        return [pltpu.make_async_copy(kv_buf.at[slot, hh], dst_hbm.at[pl.ds(row0, tm), hh, :], sem.at[slot])
                for hh in range(heads)]

    @pl.when(j == 0)
    def _():
        u_ref[...] = proj()

    @pl.when(j == 1)
    def _():
        q_ref[...] = (proj() * (HEAD_DIM ** -0.5)).astype(BF16)

    for jj, slot, dense_ref, dst_hbm in ((2, 0, kb_ref, k_hbm), (3, 1, vb_ref, v_hbm)):
        @pl.when(j == jj)
        def _(slot=slot, dense_ref=dense_ref, dst_hbm=dst_hbm):
            @pl.when(i > 0)
            def _():
                for c in head_copies(slot, dst_hbm, i - 1):
                    c.wait()
            acc = proj()
            dense_ref[...] = acc.astype(BF16)
            for hh in range(heads):
                kv_buf[slot, hh] = acc[:, hh * HEAD_DIM:(hh + 1) * HEAD_DIM]
            for c in head_copies(slot, dst_hbm, i):
                c.start()

    @pl.when((i == last) & (j == 3))
    def _():
        for slot, dst_hbm in ((0, k_hbm), (1, v_hbm)):
            for c in head_copies(slot, dst_hbm, i):
                c.wait()


def _proj_gates_body(x_ref, g_ref, w_ref, h_ref, gp_ref, gs_ref, *, tn):
    j = pl.program_id(1)

    @pl.when(j == 0)
    def _():
        x = x_ref[...]
        r = lax.rsqrt(jnp.mean(x * x, axis=-1, keepdims=True) + EPS)
        h_ref[...] = ((x * r) * g_ref[...]).astype(BF16)

    for jj, ref in ((0, gp_ref), (2, gs_ref)):
        for half in range(2):
            @pl.when(j == jj + half)
            def _(ref=ref, half=half):
                a = jnp.dot(h_ref[...], w_ref[...], preferred_element_type=F32)
                ref[:, half * tn:(half + 1) * tn] = (0.5 * jnp.tanh(0.5 * a) + 0.5).astype(BF16)


def _inproj(x, g, w_bf, tm):
    m, d = x.shape
    tn = d // 2
    assert w_bf.shape == (d, 8 * tn) and m % tm == 0
    row = lambda i, j: (i, 0)
    h_spec = pl.BlockSpec((tm, d), row)
    h, gp, gs = pl.pallas_call(
        functools.partial(_proj_gates_body, tn=tn),
        out_shape=[jax.ShapeDtypeStruct((m, d), BF16)] * 3,
        grid=(m // tm, 4),
        in_specs=[h_spec, pl.BlockSpec((1, d), lambda i, j: (0, 0)),
                  pl.BlockSpec((d, tn), lambda i, j: (0, j + 4))],
        out_specs=[h_spec] * 3,
        compiler_params=_cparams(("parallel", "arbitrary")),
        name="inproj_gates",
    )(x, g, w_bf)
    dense = lambda dt: jax.ShapeDtypeStruct((m, tn), dt)
    cache = jax.ShapeDtypeStruct((m, tn // HEAD_DIM, HEAD_DIM), F32)
    any_spec = pl.BlockSpec(memory_space=pl.ANY)
    u, q, kb, vb, k, v = pl.pallas_call(
        _proj_main_body,
        out_shape=[dense(F32), dense(BF16), dense(BF16), dense(BF16), cache, cache],
        grid=(m // tm, 4),
        in_specs=[h_spec, pl.BlockSpec((d, tn), lambda i, j: (0, j))],
        out_specs=[pl.BlockSpec((tm, tn), row)] * 4 + [any_spec] * 2,
        scratch_shapes=[pltpu.VMEM((2, tn // HEAD_DIM, tm, HEAD_DIM), F32), pltpu.SemaphoreType.DMA((2,))],
        compiler_params=_cparams(("arbitrary", "arbitrary"), has_side_effects=True),
        name="inproj_main",
    )(h, w_bf)
    return u, q, kb, vb, k, v, gp, gs


def _pool_windows(ext_ref, tn, pos):
    group = ext_ref.shape[1] // len(POOL_WINDOWS)
    slabs = []
    for g, w in enumerate(POOL_WINDOWS):
        lo, hi = g * group, (g + 1) * group
        cur = ext_ref[POOL_HIST:POOL_HIST + tn, lo:hi]
        tot = cur
        for dlt in range(1, w):
            tot = tot + ext_ref[POOL_HIST - dlt:POOL_HIST - dlt + tn, lo:hi]
        cnt = jnp.minimum(pos + 1, w).astype(F32)
        slabs.append((tot / cnt - cur).astype(BF16))
    return slabs


def _pool_body(hist_ref, u_ref, o_ref, ext_scr, *, pos0, tn):
    s = pl.program_id(1)

    @pl.when(s == 0)
    def _():
        ext_scr[0:POOL_HIST, :] = hist_ref[0]

    ext_scr[POOL_HIST:POOL_HIST + tn, :] = u_ref[0]
    pos = pos0 + s * tn + lax.broadcasted_iota(jnp.int32, (tn, 1), 0)
    group = u_ref.shape[2] // len(POOL_WINDOWS)
    for g, slab in enumerate(_pool_windows(ext_scr, tn, pos)):
        o_ref[0, :, g * group:(g + 1) * group] = slab
    ext_scr[0:POOL_HIST, :] = ext_scr[tn:tn + POOL_HIST, :]


def _pool_diff(hist, u, pos0, tn):
    b, n, dp = u.shape
    assert n % tn == 0 and hist.shape == (b, POOL_HIST, dp)
    return pl.pallas_call(
        functools.partial(_pool_body, pos0=pos0, tn=tn),
        out_shape=jax.ShapeDtypeStruct((b, n, dp), BF16),
        grid=(b, n // tn),
        in_specs=[pl.BlockSpec((1, POOL_HIST, dp), lambda i, s: (i, 0, 0)),
                  pl.BlockSpec((1, tn, dp), lambda i, s: (i, s, 0))],
        out_specs=pl.BlockSpec((1, tn, dp), lambda i, s: (i, s, 0)),
        scratch_shapes=[pltpu.VMEM((POOL_HIST + tn, dp), F32)],
        compiler_params=_cparams(("parallel", "arbitrary")),
        name="pool_diff",
    )(hist, u)


def _softplus2(z2):
    neg_abs = pltpu.bitcast(pltpu.bitcast(z2, jnp.uint32) | jnp.uint32(0x80000000), F32)
    return jnp.maximum(z2, 0.0) + jnp.log2(1.0 + jnp.exp2(neg_abs))


def _suffix_matrix(n):
    r = lax.broadcasted_iota(jnp.int32, (n, n), 0)
    c = lax.broadcasted_iota(jnp.int32, (n, n), 1)
    return jnp.where(r >= c, 1.0, 0.0).astype(BF16)


def _stack_heads(q2):
    lane = lax.broadcasted_iota(jnp.int32, q2.shape, 1)
    zero = jnp.zeros_like(q2)
    return jnp.concatenate([jnp.where(lane < HEAD_DIM, q2, zero), jnp.where(lane >= HEAD_DIM, q2, zero)], axis=0)


def _unstack_heads(acc, t):
    lane = lax.broadcasted_iota(jnp.int32, (t, LANES), 1)
    return jnp.where(lane < HEAD_DIM, acc[:t], acc[t:])


def _causal_mask(t):
    r = lax.broadcasted_iota(jnp.int32, (t, t), 0)
    c = lax.broadcasted_iota(jnp.int32, (t, t), 1)
    m = c < r
    return jnp.concatenate([m, m], axis=0)


def _sb_group(q_st, k_blocks, v_blocks, suffix, carry, masks, transposed_keys):
    (out, carry), = _sb_groups([q_st], [k_blocks], [v_blocks], suffix, [carry], masks, transposed_keys)
    return out, carry


def _sb_groups(q_sts, k_blocks, v_blocks, suffix, carries, masks, transposed_keys):
    dn = (((1,), (0,)), ((), ())) if transposed_keys else (((1,), (1,)), ((), ()))
    zs = [[lax.dot_general(q_st, kb, dn, preferred_element_type=F32) * LOG2E for kb in kbs]
          for q_st, kbs in zip(q_sts, k_blocks)]
    n_blk = len(masks)
    cs = _sb_masses([z for zq in zs for z in zq], masks * len(q_sts), [suffix] * (n_blk * len(q_sts)))
    results = []
    for i, (zq, vbs, carry) in enumerate(zip(zs, v_blocks, carries)):
        out = None
        for j, (z, vb, mask) in enumerate(zip(zq, vbs, masks)):
            a, carry = _sb_weights(z, cs[i * n_blk + j], carry, mask)
            o = jnp.dot(a, vb, preferred_element_type=F32)
            out = o if out is None else out + o
        results.append((out, carry))
    return results


def _sb_masses(zs, masks, suffixes):
    splits = []
    for z, mask in zip(zs, masks):
        sp = _softplus2(z)
        if mask is not None:
            sp = jnp.where(mask, sp, 0.0)
        hi = pltpu.bitcast(pltpu.bitcast(sp, jnp.uint32) & jnp.uint32(0xFFFF0000), F32)
        splits.append((hi.astype(BF16), (sp - hi).astype(BF16)))
    return [jnp.dot(hi, sfx, preferred_element_type=F32) + jnp.dot(lo, sfx, preferred_element_type=F32)
            for (hi, lo), sfx in zip(splits, suffixes)]


def _sb_weights(z, c, carry, mask):
    arg = z - c - carry
    if mask is not None:
        arg = jnp.where(mask, arg, NEG)
    return jnp.exp2(arg).astype(BF16), carry + c[:, 0:1]


def _attn_self_body(q_ref, k_ref, v_ref, o_ref, acc_scr, car_scr, min_scr, *, t):
    qi = pl.program_id(2)
    n_pairs = q_ref.shape[2] // LANES
    lanes = lambda p: slice(p * LANES, (p + 1) * LANES)
    block = lambda ref, p, b: ref[0, pl.ds(pl.multiple_of(b * t, t), t), lanes(p)]

    q_sts = [_stack_heads(q_ref[0, :, lanes(p)]) for p in range(n_pairs)]
    suffix = _suffix_matrix(t)
    mask = _causal_mask(t)

    def run(blocks, masks, first):
        carries = [jnp.zeros((2 * t, 1), F32) if first else car_scr[p, :, 0:1] for p in range(n_pairs)]
        results = _sb_groups(q_sts, [[block(k_ref, p, b) for b in blocks] for p in range(n_pairs)],
                             [[block(v_ref, p, b) for b in blocks] for p in range(n_pairs)], suffix, carries, masks,
                             False)
        low = None
        for p, (out, carry) in enumerate(results):
            acc_scr[p] = out if first else acc_scr[p] + out
            car_scr[p] = jnp.broadcast_to(carry, car_scr.shape[1:])
            low = jnp.min(carry) if low is None else jnp.minimum(low, jnp.min(carry))
        min_scr[0] = low

    @pl.when(qi == 0)
    def _():
        run([0], [mask], True)

    @pl.when(qi > 0)
    def _():
        run([qi, qi - 1], [mask, None], True)

    rest = jnp.maximum(qi - 1, 0)

    def more(it):
        return (it < rest // 2) & (min_scr[0] < SKIP_MASS)

    def pair(it):
        b0 = qi - 2 - 2 * it
        run([b0, b0 - 1], [None, None], False)
        return it + 1

    lax.while_loop(more, pair, 0)

    @pl.when((rest % 2 == 1) & (min_scr[0] < SKIP_MASS))
    def _():
        run([0], [None], False)

    o_ref[0] = jnp.concatenate([_unstack_heads(acc_scr[p], t) for p in range(n_pairs)],
                               axis=1).astype(o_ref.dtype)


def _attn_hist_body(q_ref, kn_ref, vn_ref, kh_ref, vh_ref, o_ref, left_ref, acc_scr, car_scr, min_scr, *, tk,
                    group):
    t = q_ref.shape[1]
    past = kh_ref.shape[1]
    q_st = _stack_heads(q_ref[0])

    def keep(out, carry, first):
        acc_scr[...] = out if first else acc_scr[...] + out
        car_scr[...] = jnp.broadcast_to(carry, car_scr.shape)
        min_scr[0] = jnp.min(carry)

    keep(*_sb_group(q_st, [kn_ref[0].astype(BF16)], [vn_ref[0].astype(BF16)], _suffix_matrix(t),
                    jnp.zeros((2 * t, 1), F32), [_causal_mask(t)], False), True)
    suffix = _suffix_matrix(tk)
    for top in range(past // tk, 0, -group):
        @pl.when(min_scr[0] < SKIP_MASS)
        def _(top=top):
            blocks = range(top - 1, top - 1 - group, -1)
            keep(*_sb_group(q_st, [kh_ref[0, b * tk:(b + 1) * tk, :].astype(BF16) for b in blocks],
                            [vh_ref[0, b * tk:(b + 1) * tk, :].astype(BF16) for b in blocks],
                            suffix, car_scr[:, 0:1], [None] * group, False), False)
    o_ref[0] = _unstack_heads(acc_scr[...], t).astype(o_ref.dtype)
    left_ref[0, 0] = jnp.full(left_ref.shape[2:], min_scr[0], F32)


def _attn_recent_body(q_ref, kn_ref, vn_ref, kc_ref, vc_ref, o_ref, left_ref, *, tk):
    t, dm = q_ref.shape[1:]
    n_blk = kc_ref.shape[1] // tk
    pair = lambda ref, rows, p: ref[0, rows, p * LANES:(p + 1) * LANES].astype(BF16)
    score = lambda qs, kb: lax.dot_general(qs, kb, (((1,), (1,)), ((), ())), preferred_element_type=F32)
    own = slice(None)
    blocks = [pl.ds(j * tk, tk) for j in range(n_blk - 1, -1, -1)]
    mask, sfx_new, sfx_old = _causal_mask(t), _suffix_matrix(t), _suffix_matrix(tk)
    zs, masks, sfx = [], [], []
    for p in range(dm // LANES):
        q_st = _stack_heads(pair(q_ref, own, p))
        zs.append(score(q_st, pair(kn_ref, own, p)) * LOG2E)
        masks.append(mask)
        sfx.append(sfx_new)
        for rows in blocks:
            zs.append(score(q_st, pair(kc_ref, rows, p)) * LOG2E)
            masks.append(None)
            sfx.append(sfx_old)
    cs = _sb_masses(zs, masks, sfx)
    outs, left = [], None
    chain = 1 + n_blk
    for p in range(dm // LANES):
        carry = jnp.zeros((2 * t, 1), F32)
        out = None
        for i in range(chain):
            a, carry = _sb_weights(zs[p * chain + i], cs[p * chain + i], carry, masks[p * chain + i])
            vb = pair(vn_ref, own, p) if i == 0 else pair(vc_ref, blocks[i - 1], p)
            o = jnp.dot(a, vb, preferred_element_type=F32)
            out = o if out is None else out + o
        outs.append(_unstack_heads(out, t))
        low = jnp.min(carry)
        left = low if left is None else jnp.minimum(left, low)
    o_ref[0] = jnp.concatenate(outs, axis=1).astype(o_ref.dtype)
    left_ref[0] = jnp.full(left_ref.shape[1:], left, F32)


def _attn_specs(t):
    tile = pl.BlockSpec((1, t, LANES), lambda i, p, s: (i, s, p))
    seq = lambda rows: pl.BlockSpec((1, rows, LANES), lambda i, p, s: (i, 0, p))
    state = [pltpu.VMEM((2 * t, LANES), F32), pltpu.VMEM((2 * t, LANES), F32), pltpu.SMEM((1,), F32)]
    return tile, seq, state


def _attention_self(q, k, v, t, pairs):
    b, n, dm = q.shape
    width = pairs * LANES
    assert n % t == 0 and dm % width == 0
    tile = pl.BlockSpec((1, t, width), lambda i, p, s: (i, s, p))
    seq = pl.BlockSpec((1, n, width), lambda i, p, s: (i, 0, p))
    return pl.pallas_call(
        functools.partial(_attn_self_body, t=t),
        out_shape=jax.ShapeDtypeStruct((b, n, dm), BF16),
        grid=(b, dm // width, n // t),
        in_specs=[tile, seq, seq],
        out_specs=tile,
        scratch_shapes=[pltpu.VMEM((pairs, 2 * t, LANES), F32), pltpu.VMEM((pairs, 2 * t, LANES), F32),
                        pltpu.SMEM((1,), F32)],
        compiler_params=_cparams(("parallel", "parallel", "arbitrary")),
        name="sb_attention",
    )(q, k, v)


def _attention_cached(q, k_new, v_new, k_hist, v_hist, tk, group):
    b, t, dm = q.shape
    past = k_hist.shape[1]
    assert dm % LANES == 0 and past % (tk * group) == 0
    tile, seq, state = _attn_specs(t)
    return pl.pallas_call(
        functools.partial(_attn_hist_body, tk=tk, group=group),
        out_shape=[jax.ShapeDtypeStruct((b, t, dm), BF16), jax.ShapeDtypeStruct((b, dm // LANES, 8, LANES), F32)],
        grid=(b, dm // LANES, 1),
        in_specs=[tile, seq(t), seq(t), seq(past), seq(past)],
        out_specs=[tile, pl.BlockSpec((1, 1, 8, LANES), lambda i, p, s: (i, p, 0, 0))],
        scratch_shapes=state,
        compiler_params=_cparams(("parallel", "parallel", "arbitrary")),
        name="sb_attention_cached",
    )(q, k_new, v_new, k_hist, v_hist)


def _attention_recent_first(q, k_new, v_new, cache_k, cache_v, tk, group):
    b, past, heads, hd = cache_k.shape
    t, dm = q.shape[1:]
    flat = lambda c: c.reshape(b, c.shape[1], heads * hd)
    walk = lambda: _attention_cached(q, k_new, v_new, flat(cache_k), flat(cache_v), tk, group)[0]
    recent = tk * group
    if past <= recent:
        return walk()
    seq = lambda rows: pl.BlockSpec((1, rows, dm), lambda i: (i, 0, 0))
    o, left = pl.pallas_call(
        functools.partial(_attn_recent_body, tk=tk),
        out_shape=[jax.ShapeDtypeStruct((b, t, dm), BF16), jax.ShapeDtypeStruct((b, 8, LANES), F32)],
        grid=(b,),
        in_specs=[seq(t), seq(t), seq(t), seq(recent), seq(recent)],
        out_specs=[seq(t), pl.BlockSpec((1, 8, LANES), lambda i: (i, 0, 0))],
        compiler_params=_cparams(("parallel",)),
        name="sb_attention_recent",
    )(q, k_new, v_new, flat(cache_k[:, past - recent:]), flat(cache_v[:, past - recent:]))
    return lax.cond(jnp.min(left) < SKIP_MASS, walk, lambda: o)


def _mix_body(*refs, tm, seq_rows):
    if seq_rows:
        u_ref, uprev_ref, *refs, ext_scr = refs
    else:
        diff_ref, *refs = refs
    (o_ref, gp_ref, gs_ref, x_ref, wp_ref, ps_ref, wsb_ref, wo_ref, nf_ref, wr_ref, br_ref,
     x1_ref, hp_ref, route_ref, cnt_ref, logit_scr) = refs

    @pl.when(pl.program_id(0) == 0)
    def _():
        logit_scr[...] = jnp.zeros_like(logit_scr)

    late_logits = logit_scr[...]

    n_pool = wp_ref.shape[0]
    group = wp_ref.shape[1]
    rows = tm // MIX_PARTS
    parts = [pl.ds(i * rows, rows) for i in range(MIX_PARTS)]
    if seq_rows:
        pos0 = (jnp.minimum(pl.program_id(0), pl.num_programs(0) - 2) * tm) % seq_rows
        ext_scr[0:POOL_HIST, :] = jnp.where(pos0 == 0, 0.0, uprev_ref[...])
        ext_scr[POOL_HIST:POOL_HIST + tm, :] = u_ref[...]
        slabs = _pool_windows(ext_scr, tm, pos0 + lax.broadcasted_iota(jnp.int32, (tm, 1), 0))
        diff_of = lambda i, g: slabs[g][i * rows:(i + 1) * rows]
    else:
        diff_of = lambda i, g: diff_ref[parts[i], g * group:(g + 1) * group]
    pools = [jnp.concatenate(
        [jnp.dot(diff_of(i, g), wp_ref[g], preferred_element_type=F32) for g in range(n_pool)],
        axis=-1) * ps_ref[...] for i in range(MIX_PARTS)]
    sbs = [jnp.dot(o_ref[s, :], wsb_ref[...], preferred_element_type=F32) for s in parts]
    mixed = [(gp_ref[s, :].astype(F32) * pool + gs_ref[s, :].astype(F32) * sb).astype(BF16)
             for s, pool, sb in zip(parts, pools, sbs)]
    x1s = [x_ref[s, :] + jnp.dot(mx, wo_ref[...], preferred_element_type=F32) for s, mx in zip(parts, mixed)]
    splits = []
    for s, x1 in zip(parts, x1s):
        x1_ref[s, :] = x1
        h = (x1 * lax.rsqrt(jnp.mean(x1 * x1, axis=-1, keepdims=True) + EPS)) * nf_ref[...]
        d_half = h.shape[1] // 2
        lo_bits = pltpu.bitcast(h[:, :d_half].astype(BF16).astype(F32), jnp.uint32)
        hi_bits = pltpu.bitcast(h[:, d_half:].astype(BF16).astype(F32), jnp.uint32)
        hp_ref[s, :] = (lo_bits >> 16) | (hi_bits & jnp.uint32(0xFFFF0000))
        hh = h.astype(BF16)
        splits.append(jnp.concatenate([hh, (h - hh.astype(F32)).astype(BF16)], axis=0))

    rs = [jnp.dot(sp, wr_ref[...], preferred_element_type=F32) for sp in splits]
    logit_scr[...] = jnp.concatenate(
        [(r[:rows, :LANES] + r[:rows, LANES:]) + (r[rows:, :LANES] + r[rows:, LANES:]) for r in rs],
        axis=0) + br_ref[...]
    _route_tile(late_logits, route_ref, cnt_ref, tm)


def _route_tile(logits, route_ref, cnt_ref, tm):
    lane = lax.broadcasted_iota(jnp.int32, (tm, LANES), 1)
    big = jnp.int32(LANES)

    def first_max(vals):
        m = jnp.max(vals, axis=-1, keepdims=True)
        idx = jnp.min(jnp.where(vals == m, lane, big), axis=-1, keepdims=True)
        return m, idx

    gl = jnp.where(lane < N_GROUPS, logits, NEG)
    gmax, grp = first_max(gl)
    p_grp = 1.0 / jnp.sum(jnp.exp(gl - gmax), axis=-1, keepdims=True)
    e_lo = N_GROUPS + grp * PER_GROUP
    el = jnp.where((lane >= e_lo) & (lane < e_lo + PER_GROUP), logits, NEG)
    m1, i1 = first_max(el)
    m2, i2 = first_max(jnp.where(lane == i1, NEG, el))
    t2 = jnp.exp(m2 - m1)
    w1 = p_grp / (1.0 + t2)
    w2 = w1 * t2
    e1 = i1 - N_GROUPS
    e2 = i2 - N_GROUPS

    oh1 = jnp.where(lane == e1, 1.0, 0.0).astype(BF16)
    oh2 = jnp.where(lane == e2, 1.0, 0.0).astype(BF16)
    rr = lax.broadcasted_iota(jnp.int32, (tm, tm), 0)
    cc = lax.broadcasted_iota(jnp.int32, (tm, tm), 1)
    before = jnp.where(cc < rr, 1.0, 0.0).astype(BF16)
    ones = jnp.ones((8, tm), BF16)
    pre1 = jnp.dot(before, oh1, preferred_element_type=F32)
    pre2 = jnp.dot(before, oh2, preferred_element_type=F32)
    c1 = jnp.dot(ones, oh1, preferred_element_type=F32)
    c2 = jnp.dot(ones, oh2, preferred_element_type=F32)
    rank1 = jnp.sum(jnp.where(lane == e1, pre1, 0.0), axis=-1, keepdims=True)
    rank2 = jnp.sum(jnp.where(lane == e2, pre2 + c1[0:1, :], 0.0), axis=-1, keepdims=True)
    cnt_ref[0] = c1 + c2

    route = jnp.where(lane == 0, e1.astype(F32), 0.0)
    route = jnp.where(lane == 1, e2.astype(F32), route)
    route = jnp.where(lane == 2, w1, route)
    route = jnp.where(lane == 3, w2, route)
    route = jnp.where(lane == 4, rank1, route)
    route = jnp.where(lane == 5, rank2, route)
    route_ref[...] = route


def _mix(pool_in, o, gp, gs, x, wp, ps, wsb, wo, nf, wr, br, tm, seq_rows=0):
    m, d = x.shape
    assert m % tm == 0 and seq_rows % tm == 0
    n = m // tm
    row = lambda i: (jnp.minimum(i, n - 1), 0)
    late = lambda i: (jnp.maximum(i - 1, 0), 0)
    full = lambda a: pl.BlockSpec(a.shape, lambda i: (0,) * a.ndim, pipeline_mode=pl.Buffered(1))
    acts = [pool_in, o, gp, gs, x]
    consts = [wp, ps, wsb, wo, nf, wr, br]
    in_specs = [pl.BlockSpec((tm, a.shape[1]), row) for a in acts] + [full(a) for a in consts]
    scratch = [pltpu.VMEM((tm, LANES), F32)]
    if seq_rows:
        before = lambda i: (jnp.maximum(row(i)[0] * (tm // POOL_HIST) - 1, 0), 0)
        acts.insert(1, pool_in)
        in_specs.insert(1, pl.BlockSpec((POOL_HIST, pool_in.shape[1]), before))
        scratch.append(pltpu.VMEM((POOL_HIST + tm, pool_in.shape[1]), F32))
    return pl.pallas_call(
        functools.partial(_mix_body, tm=tm, seq_rows=seq_rows),
        out_shape=[jax.ShapeDtypeStruct((m, d), F32), jax.ShapeDtypeStruct((m, d // 2), jnp.uint32),
                   jax.ShapeDtypeStruct((m, LANES), F32), jax.ShapeDtypeStruct((n, 8, LANES), F32)],
        grid=(n + 1,),
        in_specs=in_specs,
        out_specs=[pl.BlockSpec((tm, d), row), pl.BlockSpec((tm, d // 2), row),
                   pl.BlockSpec((tm, LANES), late), pl.BlockSpec((1, 8, LANES), lambda i: late(i) + (0,))],
        scratch_shapes=scratch,
        compiler_params=_cparams(("arbitrary",)),
        name="mix_outproj_router",
    )(*acts, *consts)


def _row_wait(src_ref, dst_ref, sem, n_rows):
    pltpu.make_async_copy(src_ref.at[pl.ds(0, n_rows)], dst_ref.at[pl.ds(0, n_rows)], sem).wait()


def _dispatch_body(slot_ref, zero_ref, hp_ref, xs_ref, buf, zbuf, sem, zsem, *, tm):
    i = pl.program_id(0)
    last = pl.num_programs(0) - 1
    par = i % 2

    @pl.when(i == 0)
    def _():
        zbuf[...] = jnp.zeros_like(zbuf)
        bm = zbuf.shape[0]
        for wait in (False, True):
            for z in range(zero_ref.shape[0]):
                @pl.when(zero_ref[z] >= 0)
                def _(z=z, wait=wait):
                    start = pl.multiple_of(jnp.maximum(zero_ref[z], 0), bm)
                    copy = pltpu.make_async_copy(zbuf, xs_ref.at[pl.ds(start, bm)], zsem)
                    copy.wait() if wait else copy.start()

    def drain(slot):
        for _ in range(2):
            _row_wait(buf.at[slot], xs_ref, sem.at[slot], tm)

    @pl.when(i >= 2)
    def _():
        drain(par)

    buf[par] = hp_ref[...]
    base = i * tm

    def body(t, _):
        src = buf.at[par, pl.ds(t, 1)]
        for kk in range(2):
            dst = xs_ref.at[pl.ds(slot_ref[2 * (base + t) + kk], 1)]
            pltpu.make_async_copy(src, dst, sem.at[par]).start()
        return 0

    lax.fori_loop(0, tm, body, 0, unroll=8)

    @pl.when(i == last)
    def _():
        drain(par)

        @pl.when(i >= 1)
        def _():
            drain(1 - par)


def _dispatch(slots_flat, zero_starts, hp, n_slots, tm, bm):
    m, dh = hp.shape
    assert m % tm == 0
    return pl.pallas_call(
        functools.partial(_dispatch_body, tm=tm),
        out_shape=jax.ShapeDtypeStruct((n_slots, dh), hp.dtype),
        grid_spec=pltpu.PrefetchScalarGridSpec(
            num_scalar_prefetch=2, grid=(m // tm,),
            in_specs=[pl.BlockSpec((tm, dh), lambda i, s, z: (i, 0))],
            out_specs=pl.BlockSpec(memory_space=pl.ANY),
            scratch_shapes=[pltpu.VMEM((2, tm, dh), hp.dtype), pltpu.VMEM((bm, dh), hp.dtype),
                            pltpu.SemaphoreType.DMA((2,)), pltpu.SemaphoreType.DMA]),
        compiler_params=_cparams(("arbitrary",), disable_bounds_checks=True, has_side_effects=True),
        name="moe_dispatch",
    )(slots_flat, zero_starts, hp)


def _ffn_body(be_ref, nv_ref, xs_ref, wg_ref, wu_ref, wd_ref, y_ref):
    del be_ref

    @pl.when(pl.program_id(0) >= nv_ref[0])
    def _():
        y_ref[...] = jnp.zeros_like(y_ref)

    @pl.when(pl.program_id(0) < nv_ref[0])
    def _():
        d_half = xs_ref.shape[1]
        rows = xs_ref.shape[0] // FFN_PARTS
        hids = []
        for i in range(FFN_PARTS):
            words = xs_ref[pl.ds(i * rows, rows), :]
            x_lo = pltpu.bitcast(words << 16, F32).astype(BF16)
            x_hi = pltpu.bitcast(words & jnp.uint32(0xFFFF0000), F32).astype(BF16)

            def proj(w_ref):
                return (jnp.dot(x_lo, w_ref[0, :d_half, :], preferred_element_type=F32)
                        + jnp.dot(x_hi, w_ref[0, d_half:, :], preferred_element_type=F32))

            hids.append((jax.nn.silu(proj(wg_ref)) * proj(wu_ref)).astype(BF16))
        for i, hid in enumerate(hids):
            y_ref[pl.ds(i * rows, rows), :] = jnp.dot(hid, wd_ref[0], preferred_element_type=F32)


def _ffn(block_expert, n_valid, xs, wg, wu, wd, bm):
    n_slots, d_half = xs.shape
    d = 2 * d_half
    de = wg.shape[2]
    live = lambda b, be, nv: (jnp.minimum(b, nv[0] - 1), 0)
    wsel = lambda b, be, nv: (be[b], 0, 0)
    return pl.pallas_call(
        _ffn_body,
        out_shape=jax.ShapeDtypeStruct((n_slots, d), F32),
        grid_spec=pltpu.PrefetchScalarGridSpec(
            num_scalar_prefetch=2, grid=(n_slots // bm,),
            in_specs=[pl.BlockSpec((bm, d_half), live), pl.BlockSpec((1, d, de), wsel),
                      pl.BlockSpec((1, d, de), wsel), pl.BlockSpec((1, de, d), wsel)],
            out_specs=pl.BlockSpec((bm, d), lambda b, be, nv: (b, 0))),
        compiler_params=_cparams(("arbitrary",)),
        name="moe_ffn",
    )(block_expert, n_valid, xs, wg, wu, wd)


def _final_body(slot_ref, x1_ref, route_ref, g_ref, y_hbm, o_ref, buf, sem, *, tm):
    i = pl.program_id(0)
    n_steps = pl.num_programs(0)

    def issue(step, par):
        def body(t, _):
            for kk in range(2):
                src = y_hbm.at[pl.ds(slot_ref[2 * (step * tm + t) + kk], 1)]
                pltpu.make_async_copy(src, buf.at[par, kk, pl.ds(t, 1)], sem.at[par]).start()
            return 0
        lax.fori_loop(0, tm, body, 0, unroll=8)

    @pl.when(i == 0)
    def _():
        issue(0, 0)

    @pl.when(i + 1 < n_steps)
    def _():
        issue(i + 1, (i + 1) % 2)

    par = i % 2
    for kk in range(2):
        _row_wait(y_hbm, buf.at[par, kk], sem.at[par], tm)
    route = route_ref[...]
    x2 = x1_ref[...] + route[:, 2:3] * buf[par, 0] + route[:, 3:4] * buf[par, 1]
    o_ref[...] = (x2 * lax.rsqrt(jnp.mean(x2 * x2, axis=-1, keepdims=True) + EPS)) * g_ref[...]


def _final(slots_flat, x1, route, g, y, tm):
    m, d = x1.shape
    assert m % tm == 0
    row = lambda i, s: (i, 0)
    return pl.pallas_call(
        functools.partial(_final_body, tm=tm),
        out_shape=jax.ShapeDtypeStruct((m, d), F32),
        grid_spec=pltpu.PrefetchScalarGridSpec(
            num_scalar_prefetch=1, grid=(m // tm,),
            in_specs=[pl.BlockSpec((tm, d), row), pl.BlockSpec((tm, LANES), row),
                      pl.BlockSpec((1, d), lambda i, s: (0, 0)), pl.BlockSpec(memory_space=pl.ANY)],
            out_specs=pl.BlockSpec((tm, d), row),
            scratch_shapes=[pltpu.VMEM((2, 2, tm, d), F32), pltpu.SemaphoreType.DMA((2,))]),
        compiler_params=_cparams(("arbitrary",), disable_bounds_checks=True),
        name="moe_combine_final_norm",
    )(slots_flat, x1, route, g, y)


def _routing_tables(route, cnt, tm, bm):
    m = route.shape[0]
    n_blocks = (2 * m) // bm + N_EXPERTS
    counts = cnt[:, 0, :N_EXPERTS].astype(jnp.int32)
    sizes = jnp.sum(counts, axis=0)
    padded = (sizes + bm - 1) // bm * bm
    pad_end = jnp.cumsum(padded)
    base = (pad_end - padded)[None, :] + jnp.cumsum(counts, axis=0) - counts
    base_tok = jnp.repeat(base, tm, axis=0)
    e = route[:, 0:2].astype(jnp.int32)
    rank = route[:, 4:6].astype(jnp.int32)
    sel = e[:, :, None] == jnp.arange(N_EXPERTS, dtype=jnp.int32)[None, None, :]
    slots = jnp.sum(jnp.where(sel, base_tok[:, None, :], 0), axis=-1) + rank
    n_valid = (pad_end[-1] // bm).astype(jnp.int32)
    blk = jnp.minimum(jnp.arange(n_blocks, dtype=jnp.int32), n_valid - 1)
    block_expert = jnp.minimum(jnp.sum(pad_end[None, :] <= (blk * bm)[:, None], axis=1), N_EXPERTS - 1)
    last_blk = jnp.where(padded > 0, pad_end - bm, -1)
    tail = n_valid + jnp.arange(N_EXPERTS, dtype=jnp.int32)
    tail = jnp.where(tail < n_blocks, tail * bm, -1)
    zero_starts = jnp.concatenate([last_blk, tail]).astype(jnp.int32)
    return slots.reshape(-1), block_expert.astype(jnp.int32), n_valid.reshape(1), zero_starts, n_blocks * bm


def _stream(x, pool_hist, k_hist, v_hist, p):
    b, n, d = x.shape
    m = b * n
    past = 0 if k_hist is None else k_hist.shape[1]
    x2d = x.reshape(m, d)
    u, q, kb, vb, k, v, gp, gs = _inproj(x2d, p['norm_mix'], p['w_in'], _fit(m, INPROJ_ROWS))
    dp = u.shape[1]
    fused_pool = pool_hist is None and n % MIX_ROWS == 0
    if fused_pool:
        pool_in = u
    else:
        hist = jnp.zeros((b, POOL_HIST, dp), F32) if pool_hist is None else pool_hist
        pool_in = _pool_diff(hist, u.reshape(b, n, dp), past, _fit(n, POOL_ROWS)).reshape(m, dp)
    shp = (b, n, kb.shape[1])
    if k_hist is None:
        o = _attention_self(q.reshape(shp), kb.reshape(shp), vb.reshape(shp), ATTN_TILE, ATTN_PAIRS)
    else:
        o = _attention_recent_first(q.reshape(shp), kb.reshape(shp), vb.reshape(shp), k_hist, v_hist, ATTN_TILE,
                                    CACHE_FIRST_BLOCKS)
    o = o.reshape(m, -1)
    x1, hp, route, cnt = _mix(pool_in, o, gp, gs, x2d, p['w_pool'], p['pool_scale'], p['w_sb_out'], p['w_out'],
                              p['norm_ffn'], p['w_r'], p['b_r'], MIX_ROWS, n if fused_pool else 0)
    slots, block_expert, n_valid, zero_starts, n_slots = _routing_tables(route, cnt, MIX_ROWS, EXPERT_ROWS)
    xs = _dispatch(slots, zero_starts, hp, n_slots, MIX_ROWS, EXPERT_ROWS)
    y = _ffn(block_expert, n_valid, xs, p['w_g'], p['w_u'], p['w_d'], EXPERT_ROWS)
    out = _final(slots, x1, route, p['norm_final'], y, FINAL_ROWS)
    return out.reshape(b, n, d), u.reshape(b, n, dp), k, v


def kernel(x_prompt, x_sample, cache_sb_k, cache_sb_v, state_pool, norm_mix, w_in, w_pool, pool_scale, w_sb_out,
           w_out, norm_ffn, w_router_group, b_router_group, w_router_expert, b_router_expert, w_exp_gate,
           w_exp_up, w_exp_down, norm_final):
    depth = w_in.shape[0]
    assert depth == 1
    bp, sp, d = x_prompt.shape
    bs, ss, _ = x_sample.shape
    heads, hd = cache_sb_k.shape[3], cache_sb_k.shape[4]
    assert hd == HEAD_DIM
    dp = state_pool.shape[3]
    n_state = state_pool.shape[2]

    w_r = jnp.concatenate([w_router_group[0], w_router_expert[0]], axis=1)
    w_r = jnp.pad(w_r, ((0, 0), (0, LANES - w_r.shape[1])))
    w_r_hi = w_r.astype(BF16)
    b_r = jnp.concatenate([b_router_group[0], b_router_expert[0]])
    p = dict(
        norm_mix=norm_mix[0][None, :], w_in=w_in[0].astype(BF16), w_pool=w_pool[0].astype(BF16),
        pool_scale=pool_scale[0][None, :], w_sb_out=w_sb_out[0].astype(BF16), w_out=w_out[0].astype(BF16),
        norm_ffn=norm_ffn[0][None, :],
        w_r=jnp.concatenate([w_r_hi, (w_r - w_r_hi.astype(F32)).astype(BF16)], axis=1),
        b_r=jnp.pad(b_r, (0, LANES - b_r.shape[0]))[None, :].astype(F32),
        w_g=w_exp_gate[0].astype(BF16), w_u=w_exp_up[0].astype(BF16), w_d=w_exp_down[0].astype(BF16),
        norm_final=norm_final[None, :])

    hist_p = jnp.zeros((bp, POOL_HIST, dp), F32)
    yp, up, kp, vp = _stream(x_prompt, None, None, None, p)
    hist_s = jnp.pad(state_pool[0], ((0, 0), (POOL_HIST - n_state, 0), (0, 0)))
    ys, us, ks, vs = _stream(x_sample, hist_s, cache_sb_k[0], cache_sb_v[0], p)

    def pool_state(hist, u):
        return jnp.concatenate([hist[:, POOL_HIST - n_state:], u], axis=1)[:, -n_state:][None]

    return (yp, ys,
            kp.reshape(1, bp, sp, heads, hd), vp.reshape(1, bp, sp, heads, hd), pool_state(hist_p, up),
            ks.reshape(1, bs, ss, heads, hd), vs.reshape(1, bs, ss, heads, hd), pool_state(hist_s, us))
```

```python
import functools

import jax
import jax.numpy as jnp
from jax import lax
from jax.experimental import pallas as pl
from jax.experimental.pallas import tpu as pltpu

F32 = jnp.float32
BF16 = jnp.bfloat16

EPS = 1e-6
HEAD_DIM = 64
LANES = 128
POOL_WINDOWS = (2, 4, 8, 16)
POOL_HIST = 16
N_GROUPS = 4
PER_GROUP = 8
N_EXPERTS = N_GROUPS * PER_GROUP
VMEM_LIMIT = 58 * 1024 * 1024
NEG = -1e30
LOG2E = 1.4426950408889634
INPROJ_ROWS = 1024
POOL_ROWS = 512
ATTN_TILE = 256
CACHE_FIRST_BLOCKS = 1
MIX_ROWS = 256
FINAL_ROWS = 256
EXPERT_ROWS = 256
ATTN_PAIRS = 4
MIX_PARTS = 1
FFN_PARTS = 2
SKIP_MASS = 160.0


def _fit(m, tile):
    while m % tile:
        tile //= 2
    return tile


def _cparams(sem, **kw):
    return pltpu.CompilerParams(dimension_semantics=sem, vmem_limit_bytes=VMEM_LIMIT, **kw)


def _proj_main_body(h_ref, w_ref, u_ref, q_ref, kb_ref, vb_ref, k_hbm, v_hbm, kv_buf, sem):
    i = pl.program_id(0)
    j = pl.program_id(1)
    last = pl.num_programs(0) - 1
    tm = h_ref.shape[0]
    heads = k_hbm.shape[1]

    def proj():
        return jnp.dot(h_ref[...], w_ref[...], preferred_element_type=F32)

    def head_copies(slot, dst_hbm, step):
        row0 = pl.multiple_of(step * tm, tm)
        return [pltpu.make_async_copy(kv_buf.at[slot, hh], dst_hbm.at[pl.ds(row0, tm), hh, :], sem.at[slot])
                for hh in range(heads)]

    @pl.when(j == 0)
    def _():
        u_ref[...] = proj()

    @pl.when(j == 1)
    def _():
        q_ref[...] = (proj() * (HEAD_DIM ** -0.5)).astype(BF16)

    for jj, slot, dense_ref, dst_hbm in ((2, 0, kb_ref, k_hbm), (3, 1, vb_ref, v_hbm)):
        @pl.when(j == jj)
        def _(slot=slot, dense_ref=dense_ref, dst_hbm=dst_hbm):
            @pl.when(i > 0)
            def _():
                for c in head_copies(slot, dst_hbm, i - 1):
                    c.wait()
            acc = proj()
            dense_ref[...] = acc.astype(BF16)
            for hh in range(heads):
                kv_buf[slot, hh] = acc[:, hh * HEAD_DIM:(hh + 1) * HEAD_DIM]
            for c in head_copies(slot, dst_hbm, i):
                c.start()

    @pl.when((i == last) & (j == 3))
    def _():
        for slot, dst_hbm in ((0, k_hbm), (1, v_hbm)):
            for c in head_copies(slot, dst_hbm, i):
                c.wait()


def _proj_gates_body(x_ref, g_ref, w_ref, h_ref, gp_ref, gs_ref, *, tn):
    j = pl.program_id(1)

    @pl.when(j == 0)
    def _():
        x = x_ref[...]
        r = lax.rsqrt(jnp.mean(x * x, axis=-1, keepdims=True) + EPS)
        h_ref[...] = ((x * r) * g_ref[...]).astype(BF16)

    for jj, ref in ((0, gp_ref), (2, gs_ref)):
        for half in range(2):
            @pl.when(j == jj + half)
            def _(ref=ref, half=half):
                a = jnp.dot(h_ref[...], w_ref[...], preferred_element_type=F32)
                ref[:, half * tn:(half + 1) * tn] = (0.5 * jnp.tanh(0.5 * a) + 0.5).astype(BF16)


def _inproj(x, g, w_bf, tm):
    m, d = x.shape
    tn = d // 2
    assert w_bf.shape == (d, 8 * tn) and m % tm == 0
    row = lambda i, j: (i, 0)
    h_spec = pl.BlockSpec((tm, d), row)
    h, gp, gs = pl.pallas_call(
        functools.partial(_proj_gates_body, tn=tn),
        out_shape=[jax.ShapeDtypeStruct((m, d), BF16)] * 3,
        grid=(m // tm, 4),
        in_specs=[h_spec, pl.BlockSpec((1, d), lambda i, j: (0, 0)),
                  pl.BlockSpec((d, tn), lambda i, j: (0, j + 4))],
        out_specs=[h_spec] * 3,
        compiler_params=_cparams(("parallel", "arbitrary")),
        name="inproj_gates",
    )(x, g, w_bf)
    dense = lambda dt: jax.ShapeDtypeStruct((m, tn), dt)
    cache = jax.ShapeDtypeStruct((m, tn // HEAD_DIM, HEAD_DIM), F32)
    any_spec = pl.BlockSpec(memory_space=pl.ANY)
    u, q, kb, vb, k, v = pl.pallas_call(
        _proj_main_body,
        out_shape=[dense(F32), dense(BF16), dense(BF16), dense(BF16), cache, cache],
        grid=(m // tm, 4),
        in_specs=[h_spec, pl.BlockSpec((d, tn), lambda i, j: (0, j))],
        out_specs=[pl.BlockSpec((tm, tn), row)] * 4 + [any_spec] * 2,
        scratch_shapes=[pltpu.VMEM((2, tn // HEAD_DIM, tm, HEAD_DIM), F32), pltpu.SemaphoreType.DMA((2,))],
        compiler_params=_cparams(("arbitrary", "arbitrary"), has_side_effects=True),
        name="inproj_main",
    )(h, w_bf)
    return u, q, kb, vb, k, v, gp, gs


def _pool_windows(ext_ref, tn, pos):
    group = ext_ref.shape[1] // len(POOL_WINDOWS)
    slabs = []
    for g, w in enumerate(POOL_WINDOWS):
        lo, hi = g * group, (g + 1) * group
        cur = ext_ref[POOL_HIST:POOL_HIST + tn, lo:hi]
        tot = cur
        for dlt in range(1, w):
            tot = tot + ext_ref[POOL_HIST - dlt:POOL_HIST - dlt + tn, lo:hi]
        cnt = jnp.minimum(pos + 1, w).astype(F32)
        slabs.append((tot / cnt - cur).astype(BF16))
    return slabs


def _pool_body(hist_ref, u_ref, o_ref, ext_scr, *, pos0, tn):
    s = pl.program_id(1)

    @pl.when(s == 0)
    def _():
        ext_scr[0:POOL_HIST, :] = hist_ref[0]

    ext_scr[POOL_HIST:POOL_HIST + tn, :] = u_ref[0]
    pos = pos0 + s * tn + lax.broadcasted_iota(jnp.int32, (tn, 1), 0)
    group = u_ref.shape[2] // len(POOL_WINDOWS)
    for g, slab in enumerate(_pool_windows(ext_scr, tn, pos)):
        o_ref[0, :, g * group:(g + 1) * group] = slab
    ext_scr[0:POOL_HIST, :] = ext_scr[tn:tn + POOL_HIST, :]


def _pool_diff(hist, u, pos0, tn):
    b, n, dp = u.shape
    assert n % tn == 0 and hist.shape == (b, POOL_HIST, dp)
    return pl.pallas_call(
        functools.partial(_pool_body, pos0=pos0, tn=tn),
        out_shape=jax.ShapeDtypeStruct((b, n, dp), BF16),
        grid=(b, n // tn),
        in_specs=[pl.BlockSpec((1, POOL_HIST, dp), lambda i, s: (i, 0, 0)),
                  pl.BlockSpec((1, tn, dp), lambda i, s: (i, s, 0))],
        out_specs=pl.BlockSpec((1, tn, dp), lambda i, s: (i, s, 0)),
        scratch_shapes=[pltpu.VMEM((POOL_HIST + tn, dp), F32)],
        compiler_params=_cparams(("parallel", "arbitrary")),
        name="pool_diff",
    )(hist, u)


def _softplus2(z2):
    neg_abs = pltpu.bitcast(pltpu.bitcast(z2, jnp.uint32) | jnp.uint32(0x80000000), F32)
    return jnp.maximum(z2, 0.0) + jnp.log2(1.0 + jnp.exp2(neg_abs))


def _suffix_matrix(n):
    r = lax.broadcasted_iota(jnp.int32, (n, n), 0)
    c = lax.broadcasted_iota(jnp.int32, (n, n), 1)
    return jnp.where(r >= c, 1.0, 0.0).astype(BF16)


def _stack_heads(q2):
    lane = lax.broadcasted_iota(jnp.int32, q2.shape, 1)
    zero = jnp.zeros_like(q2)
    return jnp.concatenate([jnp.where(lane < HEAD_DIM, q2, zero), jnp.where(lane >= HEAD_DIM, q2, zero)], axis=0)


def _unstack_heads(acc, t):
    lane = lax.broadcasted_iota(jnp.int32, (t, LANES), 1)
    return jnp.where(lane < HEAD_DIM, acc[:t], acc[t:])


def _causal_mask(t):
    r = lax.broadcasted_iota(jnp.int32, (t, t), 0)
    c = lax.broadcasted_iota(jnp.int32, (t, t), 1)
    m = c < r
    return jnp.concatenate([m, m], axis=0)


def _sb_group(q_st, k_blocks, v_blocks, suffix, carry, masks, transposed_keys):
    (out, carry), = _sb_groups([q_st], [k_blocks], [v_blocks], suffix, [carry], masks, transposed_keys)
    return out, carry


def _sb_groups(q_sts, k_blocks, v_blocks, suffix, carries, masks, transposed_keys):
    dn = (((1,), (0,)), ((), ())) if transposed_keys else (((1,), (1,)), ((), ()))
    zs = [[lax.dot_general(q_st, kb, dn, preferred_element_type=F32) * LOG2E for kb in kbs]
          for q_st, kbs in zip(q_sts, k_blocks)]
    n_blk = len(masks)
    cs = _sb_masses([z for zq in zs for z in zq], masks * len(q_sts), [suffix] * (n_blk * len(q_sts)))
    results = []
    for i, (zq, vbs, carry) in enumerate(zip(zs, v_blocks, carries)):
        out = None
        for j, (z, vb, mask) in enumerate(zip(zq, vbs, masks)):
            a, carry = _sb_weights(z, cs[i * n_blk + j], carry, mask)
            o = jnp.dot(a, vb, preferred_element_type=F32)
            out = o if out is None else out + o
        results.append((out, carry))
    return results


def _sb_masses(zs, masks, suffixes):
    splits = []
    for z, mask in zip(zs, masks):
        sp = _softplus2(z)
        if mask is not None:
            sp = jnp.where(mask, sp, 0.0)
        hi = pltpu.bitcast(pltpu.bitcast(sp, jnp.uint32) & jnp.uint32(0xFFFF0000), F32)
        splits.append((hi.astype(BF16), (sp - hi).astype(BF16)))
    return [jnp.dot(hi, sfx, preferred_element_type=F32) + jnp.dot(lo, sfx, preferred_element_type=F32)
            for (hi, lo), sfx in zip(splits, suffixes)]


def _sb_weights(z, c, carry, mask):
    arg = z - c - carry
    if mask is not None:
        arg = jnp.where(mask, arg, NEG)
    return jnp.exp2(arg).astype(BF16), carry + c[:, 0:1]


def _attn_self_body(q_ref, k_ref, v_ref, o_ref, acc_scr, car_scr, min_scr, *, t):
    qi = pl.program_id(2)
    n_pairs = q_ref.shape[2] // LANES
    lanes = lambda p: slice(p * LANES, (p + 1) * LANES)
    block = lambda ref, p, b: ref[0, pl.ds(pl.multiple_of(b * t, t), t), lanes(p)]

    q_sts = [_stack_heads(q_ref[0, :, lanes(p)]) for p in range(n_pairs)]
    suffix = _suffix_matrix(t)
    mask = _causal_mask(t)

    def run(blocks, masks, first):
        carries = [jnp.zeros((2 * t, 1), F32) if first else car_scr[p, :, 0:1] for p in range(n_pairs)]
        results = _sb_groups(q_sts, [[block(k_ref, p, b) for b in blocks] for p in range(n_pairs)],
                             [[block(v_ref, p, b) for b in blocks] for p in range(n_pairs)], suffix, carries, masks,
                             False)
        low = None
        for p, (out, carry) in enumerate(results):
            acc_scr[p] = out if first else acc_scr[p] + out
            car_scr[p] = jnp.broadcast_to(carry, car_scr.shape[1:])
            low = jnp.min(carry) if low is None else jnp.minimum(low, jnp.min(carry))
        min_scr[0] = low

    @pl.when(qi == 0)
    def _():
        run([0], [mask], True)

    @pl.when(qi > 0)
    def _():
        run([qi, qi - 1], [mask, None], True)

    rest = jnp.maximum(qi - 1, 0)

    def more(it):
        return (it < rest // 2) & (min_scr[0] < SKIP_MASS)

    def pair(it):
        b0 = qi - 2 - 2 * it
        run([b0, b0 - 1], [None, None], False)
        return it + 1

    lax.while_loop(more, pair, 0)

    @pl.when((rest % 2 == 1) & (min_scr[0] < SKIP_MASS))
    def _():
        run([0], [None], False)

    o_ref[0] = jnp.concatenate([_unstack_heads(acc_scr[p], t) for p in range(n_pairs)],
                               axis=1).astype(o_ref.dtype)


def _attn_hist_body(q_ref, kn_ref, vn_ref, kh_ref, vh_ref, o_ref, left_ref, acc_scr, car_scr, min_scr, *, tk,
                    group):
    t = q_ref.shape[1]
    past = kh_ref.shape[1]
    q_st = _stack_heads(q_ref[0])

    def keep(out, carry, first):
        acc_scr[...] = out if first else acc_scr[...] + out
        car_scr[...] = jnp.broadcast_to(carry, car_scr.shape)
        min_scr[0] = jnp.min(carry)

    keep(*_sb_group(q_st, [kn_ref[0].astype(BF16)], [vn_ref[0].astype(BF16)], _suffix_matrix(t),
                    jnp.zeros((2 * t, 1), F32), [_causal_mask(t)], False), True)
    suffix = _suffix_matrix(tk)
    for top in range(past // tk, 0, -group):
        @pl.when(min_scr[0] < SKIP_MASS)
        def _(top=top):
            blocks = range(top - 1, top - 1 - group, -1)
            keep(*_sb_group(q_st, [kh_ref[0, b * tk:(b + 1) * tk, :].astype(BF16) for b in blocks],
                            [vh_ref[0, b * tk:(b + 1) * tk, :].astype(BF16) for b in blocks],
                            suffix, car_scr[:, 0:1], [None] * group, False), False)
    o_ref[0] = _unstack_heads(acc_scr[...], t).astype(o_ref.dtype)
    left_ref[0, 0] = jnp.full(left_ref.shape[2:], min_scr[0], F32)


def _attn_recent_body(q_ref, kn_ref, vn_ref, kc_ref, vc_ref, o_ref, left_ref, *, tk):
    t, dm = q_ref.shape[1:]
    n_blk = kc_ref.shape[1] // tk
    pair = lambda ref, rows, p: ref[0, rows, p * LANES:(p + 1) * LANES].astype(BF16)
    score = lambda qs, kb: lax.dot_general(qs, kb, (((1,), (1,)), ((), ())), preferred_element_type=F32)
    own = slice(None)
    blocks = [pl.ds(j * tk, tk) for j in range(n_blk - 1, -1, -1)]
    mask, sfx_new, sfx_old = _causal_mask(t), _suffix_matrix(t), _suffix_matrix(tk)
    zs, masks, sfx = [], [], []
    for p in range(dm // LANES):
        q_st = _stack_heads(pair(q_ref, own, p))
        zs.append(score(q_st, pair(kn_ref, own, p)) * LOG2E)
        masks.append(mask)
        sfx.append(sfx_new)
        for rows in blocks:
            zs.append(score(q_st, pair(kc_ref, rows, p)) * LOG2E)
            masks.append(None)
            sfx.append(sfx_old)
    cs = _sb_masses(zs, masks, sfx)
    outs, left = [], None
    chain = 1 + n_blk
    for p in range(dm // LANES):
        carry = jnp.zeros((2 * t, 1), F32)
        out = None
        for i in range(chain):
            a, carry = _sb_weights(zs[p * chain + i], cs[p * chain + i], carry, masks[p * chain + i])
            vb = pair(vn_ref, own, p) if i == 0 else pair(vc_ref, blocks[i - 1], p)
            o = jnp.dot(a, vb, preferred_element_type=F32)
            out = o if out is None else out + o
        outs.append(_unstack_heads(out, t))
        low = jnp.min(carry)
        left = low if left is None else jnp.minimum(left, low)
    o_ref[0] = jnp.concatenate(outs, axis=1).astype(o_ref.dtype)
    left_ref[0] = jnp.full(left_ref.shape[1:], left, F32)


def _attn_specs(t):
    tile = pl.BlockSpec((1, t, LANES), lambda i, p, s: (i, s, p))
    seq = lambda rows: pl.BlockSpec((1, rows, LANES), lambda i, p, s: (i, 0, p))
    state = [pltpu.VMEM((2 * t, LANES), F32), pltpu.VMEM((2 * t, LANES), F32), pltpu.SMEM((1,), F32)]
    return tile, seq, state


def _attention_self(q, k, v, t, pairs):
    b, n, dm = q.shape
    width = pairs * LANES
    assert n % t == 0 and dm % width == 0
    tile = pl.BlockSpec((1, t, width), lambda i, p, s: (i, s, p))
    seq = pl.BlockSpec((1, n, width), lambda i, p, s: (i, 0, p))
    return pl.pallas_call(
        functools.partial(_attn_self_body, t=t),
        out_shape=jax.ShapeDtypeStruct((b, n, dm), BF16),
        grid=(b, dm // width, n // t),
        in_specs=[tile, seq, seq],
        out_specs=tile,
        scratch_shapes=[pltpu.VMEM((pairs, 2 * t, LANES), F32), pltpu.VMEM((pairs, 2 * t, LANES), F32),
                        pltpu.SMEM((1,), F32)],
        compiler_params=_cparams(("parallel", "parallel", "arbitrary")),
        name="sb_attention",
    )(q, k, v)


def _attention_cached(q, k_new, v_new, k_hist, v_hist, tk, group):
    b, t, dm = q.shape
    past = k_hist.shape[1]
    assert dm % LANES == 0 and past % (tk * group) == 0
    tile, seq, state = _attn_specs(t)
    return pl.pallas_call(
        functools.partial(_attn_hist_body, tk=tk, group=group),
        out_shape=[jax.ShapeDtypeStruct((b, t, dm), BF16), jax.ShapeDtypeStruct((b, dm // LANES, 8, LANES), F32)],
        grid=(b, dm // LANES, 1),
        in_specs=[tile, seq(t), seq(t), seq(past), seq(past)],
        out_specs=[tile, pl.BlockSpec((1, 1, 8, LANES), lambda i, p, s: (i, p, 0, 0))],
        scratch_shapes=state,
        compiler_params=_cparams(("parallel", "parallel", "arbitrary")),
        name="sb_attention_cached",
    )(q, k_new, v_new, k_hist, v_hist)


def _attention_recent_first(q, k_new, v_new, cache_k, cache_v, tk, group):
    b, past, heads, hd = cache_k.shape
    t, dm = q.shape[1:]
    flat = lambda c: c.reshape(b, c.shape[1], heads * hd)
    walk = lambda: _attention_cached(q, k_new, v_new, flat(cache_k), flat(cache_v), tk, group)[0]
    recent = tk * group
    if past <= recent:
        return walk()
    seq = lambda rows: pl.BlockSpec((1, rows, dm), lambda i: (i, 0, 0))
    o, left = pl.pallas_call(
        functools.partial(_attn_recent_body, tk=tk),
        out_shape=[jax.ShapeDtypeStruct((b, t, dm), BF16), jax.ShapeDtypeStruct((b, 8, LANES), F32)],
        grid=(b,),
        in_specs=[seq(t), seq(t), seq(t), seq(recent), seq(recent)],
        out_specs=[seq(t), pl.BlockSpec((1, 8, LANES), lambda i: (i, 0, 0))],
        compiler_params=_cparams(("parallel",)),
        name="sb_attention_recent",
    )(q, k_new, v_new, flat(cache_k[:, past - recent:]), flat(cache_v[:, past - recent:]))
    return lax.cond(jnp.min(left) < SKIP_MASS, walk, lambda: o)


def _mix_body(*refs, tm, seq_rows):
    if seq_rows:
        u_ref, uprev_ref, *refs, ext_scr = refs
    else:
        diff_ref, *refs = refs
    (o_ref, gp_ref, gs_ref, x_ref, wp_ref, ps_ref, wsb_ref, wo_ref, nf_ref, wr_ref, br_ref,
     x1_ref, hp_ref, route_ref, cnt_ref, logit_scr) = refs

    @pl.when(pl.program_id(0) == 0)
    def _():
        logit_scr[...] = jnp.zeros_like(logit_scr)

    late_logits = logit_scr[...]

    n_pool = wp_ref.shape[0]
    group = wp_ref.shape[1]
    rows = tm // MIX_PARTS
    parts = [pl.ds(i * rows, rows) for i in range(MIX_PARTS)]
    if seq_rows:
        pos0 = (jnp.minimum(pl.program_id(0), pl.num_programs(0) - 2) * tm) % seq_rows
        ext_scr[0:POOL_HIST, :] = jnp.where(pos0 == 0, 0.0, uprev_ref[...])
        ext_scr[POOL_HIST:POOL_HIST + tm, :] = u_ref[...]
        slabs = _pool_windows(ext_scr, tm, pos0 + lax.broadcasted_iota(jnp.int32, (tm, 1), 0))
        diff_of = lambda i, g: slabs[g][i * rows:(i + 1) * rows]
    else:
        diff_of = lambda i, g: diff_ref[parts[i], g * group:(g + 1) * group]
    pools = [jnp.concatenate(
        [jnp.dot(diff_of(i, g), wp_ref[g], preferred_element_type=F32) for g in range(n_pool)],
        axis=-1) * ps_ref[...] for i in range(MIX_PARTS)]
    sbs = [jnp.dot(o_ref[s, :], wsb_ref[...], preferred_element_type=F32) for s in parts]
    mixed = [(gp_ref[s, :].astype(F32) * pool + gs_ref[s, :].astype(F32) * sb).astype(BF16)
             for s, pool, sb in zip(parts, pools, sbs)]
    x1s = [x_ref[s, :] + jnp.dot(mx, wo_ref[...], preferred_element_type=F32) for s, mx in zip(parts, mixed)]
    splits = []
    for s, x1 in zip(parts, x1s):
        x1_ref[s, :] = x1
        h = (x1 * lax.rsqrt(jnp.mean(x1 * x1, axis=-1, keepdims=True) + EPS)) * nf_ref[...]
        d_half = h.shape[1] // 2
        lo_bits = pltpu.bitcast(h[:, :d_half].astype(BF16).astype(F32), jnp.uint32)
        hi_bits = pltpu.bitcast(h[:, d_half:].astype(BF16).astype(F32), jnp.uint32)
        hp_ref[s, :] = (lo_bits >> 16) | (hi_bits & jnp.uint32(0xFFFF0000))
        hh = h.astype(BF16)
        splits.append(jnp.concatenate([hh, (h - hh.astype(F32)).astype(BF16)], axis=0))

    rs = [jnp.dot(sp, wr_ref[...], preferred_element_type=F32) for sp in splits]
    logit_scr[...] = jnp.concatenate(
        [(r[:rows, :LANES] + r[:rows, LANES:]) + (r[rows:, :LANES] + r[rows:, LANES:]) for r in rs],
        axis=0) + br_ref[...]
    _route_tile(late_logits, route_ref, cnt_ref, tm)


def _route_tile(logits, route_ref, cnt_ref, tm):
    lane = lax.broadcasted_iota(jnp.int32, (tm, LANES), 1)
    big = jnp.int32(LANES)

    def first_max(vals):
        m = jnp.max(vals, axis=-1, keepdims=True)
        idx = jnp.min(jnp.where(vals == m, lane, big), axis=-1, keepdims=True)
        return m, idx

    gl = jnp.where(lane < N_GROUPS, logits, NEG)
    gmax, grp = first_max(gl)
    p_grp = 1.0 / jnp.sum(jnp.exp(gl - gmax), axis=-1, keepdims=True)
    e_lo = N_GROUPS + grp * PER_GROUP
    el = jnp.where((lane >= e_lo) & (lane < e_lo + PER_GROUP), logits, NEG)
    m1, i1 = first_max(el)
    m2, i2 = first_max(jnp.where(lane == i1, NEG, el))
    t2 = jnp.exp(m2 - m1)
    w1 = p_grp / (1.0 + t2)
    w2 = w1 * t2
    e1 = i1 - N_GROUPS
    e2 = i2 - N_GROUPS

    oh1 = jnp.where(lane == e1, 1.0, 0.0).astype(BF16)
    oh2 = jnp.where(lane == e2, 1.0, 0.0).astype(BF16)
    rr = lax.broadcasted_iota(jnp.int32, (tm, tm), 0)
    cc = lax.broadcasted_iota(jnp.int32, (tm, tm), 1)
    before = jnp.where(cc < rr, 1.0, 0.0).astype(BF16)
    ones = jnp.ones((8, tm), BF16)
    pre1 = jnp.dot(before, oh1, preferred_element_type=F32)
    pre2 = jnp.dot(before, oh2, preferred_element_type=F32)
    c1 = jnp.dot(ones, oh1, preferred_element_type=F32)
    c2 = jnp.dot(ones, oh2, preferred_element_type=F32)
    rank1 = jnp.sum(jnp.where(lane == e1, pre1, 0.0), axis=-1, keepdims=True)
    rank2 = jnp.sum(jnp.where(lane == e2, pre2 + c1[0:1, :], 0.0), axis=-1, keepdims=True)
    cnt_ref[0] = c1 + c2

    route = jnp.where(lane == 0, e1.astype(F32), 0.0)
    route = jnp.where(lane == 1, e2.astype(F32), route)
    route = jnp.where(lane == 2, w1, route)
    route = jnp.where(lane == 3, w2, route)
    route = jnp.where(lane == 4, rank1, route)
    route = jnp.where(lane == 5, rank2, route)
    route_ref[...] = route


def _mix(pool_in, o, gp, gs, x, wp, ps, wsb, wo, nf, wr, br, tm, seq_rows=0):
    m, d = x.shape
    assert m % tm == 0 and seq_rows % tm == 0
    n = m // tm
    row = lambda i: (jnp.minimum(i, n - 1), 0)
    late = lambda i: (jnp.maximum(i - 1, 0), 0)
    full = lambda a: pl.BlockSpec(a.shape, lambda i: (0,) * a.ndim, pipeline_mode=pl.Buffered(1))
    acts = [pool_in, o, gp, gs, x]
    consts = [wp, ps, wsb, wo, nf, wr, br]
    in_specs = [pl.BlockSpec((tm, a.shape[1]), row) for a in acts] + [full(a) for a in consts]
    scratch = [pltpu.VMEM((tm, LANES), F32)]
    if seq_rows:
        before = lambda i: (jnp.maximum(row(i)[0] * (tm // POOL_HIST) - 1, 0), 0)
        acts.insert(1, pool_in)
        in_specs.insert(1, pl.BlockSpec((POOL_HIST, pool_in.shape[1]), before))
        scratch.append(pltpu.VMEM((POOL_HIST + tm, pool_in.shape[1]), F32))
    return pl.pallas_call(
        functools.partial(_mix_body, tm=tm, seq_rows=seq_rows),
        out_shape=[jax.ShapeDtypeStruct((m, d), F32), jax.ShapeDtypeStruct((m, d // 2), jnp.uint32),
                   jax.ShapeDtypeStruct((m, LANES), F32), jax.ShapeDtypeStruct((n, 8, LANES), F32)],
        grid=(n + 1,),
        in_specs=in_specs,
        out_specs=[pl.BlockSpec((tm, d), row), pl.BlockSpec((tm, d // 2), row),
                   pl.BlockSpec((tm, LANES), late), pl.BlockSpec((1, 8, LANES), lambda i: late(i) + (0,))],
        scratch_shapes=scratch,
        compiler_params=_cparams(("arbitrary",)),
        name="mix_outproj_router",
    )(*acts, *consts)


def _row_wait(src_ref, dst_ref, sem, n_rows):
    pltpu.make_async_copy(src_ref.at[pl.ds(0, n_rows)], dst_ref.at[pl.ds(0, n_rows)], sem).wait()


def _dispatch_body(slot_ref, zero_ref, hp_ref, xs_ref, buf, zbuf, sem, zsem, *, tm):
    i = pl.program_id(0)
    last = pl.num_programs(0) - 1
    par = i % 2

    @pl.when(i == 0)
    def _():
        zbuf[...] = jnp.zeros_like(zbuf)
        bm = zbuf.shape[0]
        for wait in (False, True):
            for z in range(zero_ref.shape[0]):
                @pl.when(zero_ref[z] >= 0)
                def _(z=z, wait=wait):
                    start = pl.multiple_of(jnp.maximum(zero_ref[z], 0), bm)
                    copy = pltpu.make_async_copy(zbuf, xs_ref.at[pl.ds(start, bm)], zsem)
                    copy.wait() if wait else copy.start()

    def drain(slot):
        for _ in range(2):
            _row_wait(buf.at[slot], xs_ref, sem.at[slot], tm)

    @pl.when(i >= 2)
    def _():
        drain(par)

    buf[par] = hp_ref[...]
    base = i * tm

    def body(t, _):
        src = buf.at[par, pl.ds(t, 1)]
        for kk in range(2):
            dst = xs_ref.at[pl.ds(slot_ref[2 * (base + t) + kk], 1)]
            pltpu.make_async_copy(src, dst, sem.at[par]).start()
        return 0

    lax.fori_loop(0, tm, body, 0, unroll=8)

    @pl.when(i == last)
    def _():
        drain(par)

        @pl.when(i >= 1)
        def _():
            drain(1 - par)


def _dispatch(slots_flat, zero_starts, hp, n_slots, tm, bm):
    m, dh = hp.shape
    assert m % tm == 0
    return pl.pallas_call(
        functools.partial(_dispatch_body, tm=tm),
        out_shape=jax.ShapeDtypeStruct((n_slots, dh), hp.dtype),
        grid_spec=pltpu.PrefetchScalarGridSpec(
            num_scalar_prefetch=2, grid=(m // tm,),
            in_specs=[pl.BlockSpec((tm, dh), lambda i, s, z: (i, 0))],
            out_specs=pl.BlockSpec(memory_space=pl.ANY),
            scratch_shapes=[pltpu.VMEM((2, tm, dh), hp.dtype), pltpu.VMEM((bm, dh), hp.dtype),
                            pltpu.SemaphoreType.DMA((2,)), pltpu.SemaphoreType.DMA]),
        compiler_params=_cparams(("arbitrary",), disable_bounds_checks=True, has_side_effects=True),
        name="moe_dispatch",
    )(slots_flat, zero_starts, hp)


def _ffn_body(be_ref, nv_ref, xs_ref, wg_ref, wu_ref, wd_ref, y_ref):
    del be_ref

    @pl.when(pl.program_id(0) >= nv_ref[0])
    def _():
        y_ref[...] = jnp.zeros_like(y_ref)

    @pl.when(pl.program_id(0) < nv_ref[0])
    def _():
        d_half = xs_ref.shape[1]
        rows = xs_ref.shape[0] // FFN_PARTS
        hids = []
        for i in range(FFN_PARTS):
            words = xs_ref[pl.ds(i * rows, rows), :]
            x_lo = pltpu.bitcast(words << 16, F32).astype(BF16)
            x_hi = pltpu.bitcast(words & jnp.uint32(0xFFFF0000), F32).astype(BF16)

            def proj(w_ref):
                return (jnp.dot(x_lo, w_ref[0, :d_half, :], preferred_element_type=F32)
                        + jnp.dot(x_hi, w_ref[0, d_half:, :], preferred_element_type=F32))

            hids.append((jax.nn.silu(proj(wg_ref)) * proj(wu_ref)).astype(BF16))
        for i, hid in enumerate(hids):
            y_ref[pl.ds(i * rows, rows), :] = jnp.dot(hid, wd_ref[0], preferred_element_type=F32)


def _ffn(block_expert, n_valid, xs, wg, wu, wd, bm):
    n_slots, d_half = xs.shape
    d = 2 * d_half
    de = wg.shape[2]
    live = lambda b, be, nv: (jnp.minimum(b, nv[0] - 1), 0)
    wsel = lambda b, be, nv: (be[b], 0, 0)
    return pl.pallas_call(
        _ffn_body,
        out_shape=jax.ShapeDtypeStruct((n_slots, d), F32),
        grid_spec=pltpu.PrefetchScalarGridSpec(
            num_scalar_prefetch=2, grid=(n_slots // bm,),
            in_specs=[pl.BlockSpec((bm, d_half), live), pl.BlockSpec((1, d, de), wsel),
                      pl.BlockSpec((1, d, de), wsel), pl.BlockSpec((1, de, d), wsel)],
            out_specs=pl.BlockSpec((bm, d), lambda b, be, nv: (b, 0))),
        compiler_params=_cparams(("arbitrary",)),
        name="moe_ffn",
    )(block_expert, n_valid, xs, wg, wu, wd)


def _final_body(slot_ref, x1_ref, route_ref, g_ref, y_hbm, o_ref, buf, sem, *, tm):
    i = pl.program_id(0)
    n_steps = pl.num_programs(0)

    def issue(step, par):
        def body(t, _):
            for kk in range(2):
                src = y_hbm.at[pl.ds(slot_ref[2 * (step * tm + t) + kk], 1)]
                pltpu.make_async_copy(src, buf.at[par, kk, pl.ds(t, 1)], sem.at[par]).start()
            return 0
        lax.fori_loop(0, tm, body, 0, unroll=8)

    @pl.when(i == 0)
    def _():
        issue(0, 0)

    @pl.when(i + 1 < n_steps)
    def _():
        issue(i + 1, (i + 1) % 2)

    par = i % 2
    for kk in range(2):
        _row_wait(y_hbm, buf.at[par, kk], sem.at[par], tm)
    route = route_ref[...]
    x2 = x1_ref[...] + route[:, 2:3] * buf[par, 0] + route[:, 3:4] * buf[par, 1]
    o_ref[...] = (x2 * lax.rsqrt(jnp.mean(x2 * x2, axis=-1, keepdims=True) + EPS)) * g_ref[...]


def _final(slots_flat, x1, route, g, y, tm):
    m, d = x1.shape
    assert m % tm == 0
    row = lambda i, s: (i, 0)
    return pl.pallas_call(
        functools.partial(_final_body, tm=tm),
        out_shape=jax.ShapeDtypeStruct((m, d), F32),
        grid_spec=pltpu.PrefetchScalarGridSpec(
            num_scalar_prefetch=1, grid=(m // tm,),
            in_specs=[pl.BlockSpec((tm, d), row), pl.BlockSpec((tm, LANES), row),
                      pl.BlockSpec((1, d), lambda i, s: (0, 0)), pl.BlockSpec(memory_space=pl.ANY)],
            out_specs=pl.BlockSpec((tm, d), row),
            scratch_shapes=[pltpu.VMEM((2, 2, tm, d), F32), pltpu.SemaphoreType.DMA((2,))]),
        compiler_params=_cparams(("arbitrary",), disable_bounds_checks=True),
        name="moe_combine_final_norm",
    )(slots_flat, x1, route, g, y)


def _routing_tables(route, cnt, tm, bm):
    m = route.shape[0]
    n_blocks = (2 * m) // bm + N_EXPERTS
    counts = cnt[:, 0, :N_EXPERTS].astype(jnp.int32)
    sizes = jnp.sum(counts, axis=0)
    padded = (sizes + bm - 1) // bm * bm
    pad_end = jnp.cumsum(padded)
    base = (pad_end - padded)[None, :] + jnp.cumsum(counts, axis=0) - counts
    base_tok = jnp.repeat(base, tm, axis=0)
    e = route[:, 0:2].astype(jnp.int32)
    rank = route[:, 4:6].astype(jnp.int32)
    sel = e[:, :, None] == jnp.arange(N_EXPERTS, dtype=jnp.int32)[None, None, :]
    slots = jnp.sum(jnp.where(sel, base_tok[:, None, :], 0), axis=-1) + rank
    n_valid = (pad_end[-1] // bm).astype(jnp.int32)
    blk = jnp.minimum(jnp.arange(n_blocks, dtype=jnp.int32), n_valid - 1)
    block_expert = jnp.minimum(jnp.sum(pad_end[None, :] <= (blk * bm)[:, None], axis=1), N_EXPERTS - 1)
    last_blk = jnp.where(padded > 0, pad_end - bm, -1)
    tail = n_valid + jnp.arange(N_EXPERTS, dtype=jnp.int32)
    tail = jnp.where(tail < n_blocks, tail * bm, -1)
    zero_starts = jnp.concatenate([last_blk, tail]).astype(jnp.int32)
    return slots.reshape(-1), block_expert.astype(jnp.int32), n_valid.reshape(1), zero_starts, n_blocks * bm


def _stream(x, pool_hist, k_hist, v_hist, p):
    b, n, d = x.shape
    m = b * n
    past = 0 if k_hist is None else k_hist.shape[1]
    x2d = x.reshape(m, d)
    u, q, kb, vb, k, v, gp, gs = _inproj(x2d, p['norm_mix'], p['w_in'], _fit(m, INPROJ_ROWS))
    dp = u.shape[1]
    fused_pool = pool_hist is None and n % MIX_ROWS == 0
    if fused_pool:
        pool_in = u
    else:
        hist = jnp.zeros((b, POOL_HIST, dp), F32) if pool_hist is None else pool_hist
        pool_in = _pool_diff(hist, u.reshape(b, n, dp), past, _fit(n, POOL_ROWS)).reshape(m, dp)
    shp = (b, n, kb.shape[1])
    if k_hist is None:
        o = _attention_self(q.reshape(shp), kb.reshape(shp), vb.reshape(shp), ATTN_TILE, ATTN_PAIRS)
    else:
        o = _attention_recent_first(q.reshape(shp), kb.reshape(shp), vb.reshape(shp), k_hist, v_hist, ATTN_TILE,
                                    CACHE_FIRST_BLOCKS)
    o = o.reshape(m, -1)
    x1, hp, route, cnt = _mix(pool_in, o, gp, gs, x2d, p['w_pool'], p['pool_scale'], p['w_sb_out'], p['w_out'],
                              p['norm_ffn'], p['w_r'], p['b_r'], MIX_ROWS, n if fused_pool else 0)
    slots, block_expert, n_valid, zero_starts, n_slots = _routing_tables(route, cnt, MIX_ROWS, EXPERT_ROWS)
    xs = _dispatch(slots, zero_starts, hp, n_slots, MIX_ROWS, EXPERT_ROWS)
    y = _ffn(block_expert, n_valid, xs, p['w_g'], p['w_u'], p['w_d'], EXPERT_ROWS)
    out = _final(slots, x1, route, p['norm_final'], y, FINAL_ROWS)
    return out.reshape(b, n, d), u.reshape(b, n, dp), k, v


def kernel(x_prompt, x_sample, cache_sb_k, cache_sb_v, state_pool, norm_mix, w_in, w_pool, pool_scale, w_sb_out,
           w_out, norm_ffn, w_router_group, b_router_group, w_router_expert, b_router_expert, w_exp_gate,
           w_exp_up, w_exp_down, norm_final):
    depth = w_in.shape[0]
    assert depth == 1
    bp, sp, d = x_prompt.shape
    bs, ss, _ = x_sample.shape
    heads, hd = cache_sb_k.shape[3], cache_sb_k.shape[4]
    assert hd == HEAD_DIM
    dp = state_pool.shape[3]
    n_state = state_pool.shape[2]

    w_r = jnp.concatenate([w_router_group[0], w_router_expert[0]], axis=1)
    w_r = jnp.pad(w_r, ((0, 0), (0, LANES - w_r.shape[1])))
    w_r_hi = w_r.astype(BF16)
    b_r = jnp.concatenate([b_router_group[0], b_router_expert[0]])
    p = dict(
        norm_mix=norm_mix[0][None, :], w_in=w_in[0].astype(BF16), w_pool=w_pool[0].astype(BF16),
        pool_scale=pool_scale[0][None, :], w_sb_out=w_sb_out[0].astype(BF16), w_out=w_out[0].astype(BF16),
        norm_ffn=norm_ffn[0][None, :],
        w_r=jnp.concatenate([w_r_hi, (w_r - w_r_hi.astype(F32)).astype(BF16)], axis=1),
        b_r=jnp.pad(b_r, (0, LANES - b_r.shape[0]))[None, :].astype(F32),
        w_g=w_exp_gate[0].astype(BF16), w_u=w_exp_up[0].astype(BF16), w_d=w_exp_down[0].astype(BF16),
        norm_final=norm_final[None, :])

    hist_p = jnp.zeros((bp, POOL_HIST, dp), F32)
    yp, up, kp, vp = _stream(x_prompt, None, None, None, p)
    hist_s = jnp.pad(state_pool[0], ((0, 0), (POOL_HIST - n_state, 0), (0, 0)))
    ys, us, ks, vs = _stream(x_sample, hist_s, cache_sb_k[0], cache_sb_v[0], p)

    def pool_state(hist, u):
        return jnp.concatenate([hist[:, POOL_HIST - n_state:], u], axis=1)[:, -n_state:][None]

    return (yp, ys,
            kp.reshape(1, bp, sp, heads, hd), vp.reshape(1, bp, sp, heads, hd), pool_state(hist_p, up),
            ks.reshape(1, bs, ss, heads, hd), vs.reshape(1, bs, ss, heads, hd), pool_state(hist_s, us))
```

```python
import functools

import jax
import jax.numpy as jnp
from jax import lax
from jax.experimental import pallas as pl
from jax.experimental.pallas import tpu as pltpu

F32 = jnp.float32
BF16 = jnp.bfloat16

EPS = 1e-6
HEAD_DIM = 64
LANES = 128
POOL_WINDOWS = (2, 4, 8, 16)
POOL_HIST = 16
N_GROUPS = 4
PER_GROUP = 8
N_EXPERTS = N_GROUPS * PER_GROUP
VMEM_LIMIT = 58 * 1024 * 1024
NEG = -1e30
LOG2E = 1.4426950408889634
INPROJ_ROWS = 1024
POOL_ROWS = 512
ATTN_TILE = 256
CACHE_FIRST_BLOCKS = 1
MIX_ROWS = 256
DISPATCH_ROWS = 512
FINAL_ROWS = 256
EXPERT_ROWS = 256
ATTN_PAIRS = 4
MIX_PARTS = 1
FFN_PARTS = 2
SKIP_MASS = 160.0


def _fit(m, tile):
    while m % tile:
        tile //= 2
    return tile


def _cparams(sem, **kw):
    return pltpu.CompilerParams(dimension_semantics=sem, vmem_limit_bytes=VMEM_LIMIT, **kw)


def _proj_main_body(h_ref, w_ref, u_ref, q_ref, kb_ref, vb_ref, k_hbm, v_hbm, kv_buf, sem):
    i = pl.program_id(0)
    j = pl.program_id(1)
    last = pl.num_programs(0) - 1
    tm = h_ref.shape[0]
    heads = k_hbm.shape[1]

    def proj():
        return jnp.dot(h_ref[...], w_ref[...], preferred_element_type=F32)

    def head_copies(slot, dst_hbm, step):
        row0 = pl.multiple_of(step * tm, tm)
        return [pltpu.make_async_copy(kv_buf.at[slot, hh], dst_hbm.at[pl.ds(row0, tm), hh, :], sem.at[slot])
                for hh in range(heads)]

    @pl.when(j == 0)
    def _():
        u_ref[...] = proj()

    @pl.when(j == 1)
    def _():
        q_ref[...] = (proj() * (HEAD_DIM ** -0.5)).astype(BF16)

    for jj, slot, dense_ref, dst_hbm in ((2, 0, kb_ref, k_hbm), (3, 1, vb_ref, v_hbm)):
        @pl.when(j == jj)
        def _(slot=slot, dense_ref=dense_ref, dst_hbm=dst_hbm):
            @pl.when(i > 0)
            def _():
                for c in head_copies(slot, dst_hbm, i - 1):
                    c.wait()
            acc = proj()
            dense_ref[...] = acc.astype(BF16)
            for hh in range(heads):
                kv_buf[slot, hh] = acc[:, hh * HEAD_DIM:(hh + 1) * HEAD_DIM]
            for c in head_copies(slot, dst_hbm, i):
                c.start()

    @pl.when((i == last) & (j == 3))
    def _():
        for slot, dst_hbm in ((0, k_hbm), (1, v_hbm)):
            for c in head_copies(slot, dst_hbm, i):
                c.wait()


def _proj_gates_body(x_ref, g_ref, w_ref, h_ref, gp_ref, gs_ref, *, tn):
    j = pl.program_id(1)

    @pl.when(j == 0)
    def _():
        x = x_ref[...]
        r = lax.rsqrt(jnp.mean(x * x, axis=-1, keepdims=True) + EPS)
        h_ref[...] = ((x * r) * g_ref[...]).astype(BF16)

    for jj, ref in ((0, gp_ref), (2, gs_ref)):
        for half in range(2):
            @pl.when(j == jj + half)
            def _(ref=ref, half=half):
                a = jnp.dot(h_ref[...], w_ref[...], preferred_element_type=F32)
                ref[:, half * tn:(half + 1) * tn] = (0.5 * jnp.tanh(0.5 * a) + 0.5).astype(BF16)


def _inproj(x, g, w_bf, tm):
    m, d = x.shape
    tn = d // 2
    assert w_bf.shape == (d, 8 * tn) and m % tm == 0
    row = lambda i, j: (i, 0)
    h_spec = pl.BlockSpec((tm, d), row)
    h, gp, gs = pl.pallas_call(
        functools.partial(_proj_gates_body, tn=tn),
        out_shape=[jax.ShapeDtypeStruct((m, d), BF16)] * 3,
        grid=(m // tm, 4),
        in_specs=[h_spec, pl.BlockSpec((1, d), lambda i, j: (0, 0)),
                  pl.BlockSpec((d, tn), lambda i, j: (0, j + 4))],
        out_specs=[h_spec] * 3,
        compiler_params=_cparams(("parallel", "arbitrary")),
        name="inproj_gates",
    )(x, g, w_bf)
    dense = lambda dt: jax.ShapeDtypeStruct((m, tn), dt)
    cache = jax.ShapeDtypeStruct((m, tn // HEAD_DIM, HEAD_DIM), F32)
    any_spec = pl.BlockSpec(memory_space=pl.ANY)
    u, q, kb, vb, k, v = pl.pallas_call(
        _proj_main_body,
        out_shape=[dense(F32), dense(BF16), dense(BF16), dense(BF16), cache, cache],
        grid=(m // tm, 4),
        in_specs=[h_spec, pl.BlockSpec((d, tn), lambda i, j: (0, j))],
        out_specs=[pl.BlockSpec((tm, tn), row)] * 4 + [any_spec] * 2,
        scratch_shapes=[pltpu.VMEM((2, tn // HEAD_DIM, tm, HEAD_DIM), F32), pltpu.SemaphoreType.DMA((2,))],
        compiler_params=_cparams(("arbitrary", "arbitrary"), has_side_effects=True),
        name="inproj_main",
    )(h, w_bf)
    return u, q, kb, vb, k, v, gp, gs


def _pool_windows(ext_ref, tn, pos):
    group = ext_ref.shape[1] // len(POOL_WINDOWS)
    slabs = []
    for g, w in enumerate(POOL_WINDOWS):
        lo, hi = g * group, (g + 1) * group
        cur = ext_ref[POOL_HIST:POOL_HIST + tn, lo:hi]
        tot = cur
        for dlt in range(1, w):
            tot = tot + ext_ref[POOL_HIST - dlt:POOL_HIST - dlt + tn, lo:hi]
        cnt = jnp.minimum(pos + 1, w).astype(F32)
        slabs.append((tot / cnt - cur).astype(BF16))
    return slabs


def _pool_body(hist_ref, u_ref, o_ref, ext_scr, *, pos0, tn):
    s = pl.program_id(1)

    @pl.when(s == 0)
    def _():
        ext_scr[0:POOL_HIST, :] = hist_ref[0]

    ext_scr[POOL_HIST:POOL_HIST + tn, :] = u_ref[0]
    pos = pos0 + s * tn + lax.broadcasted_iota(jnp.int32, (tn, 1), 0)
    group = u_ref.shape[2] // len(POOL_WINDOWS)
    for g, slab in enumerate(_pool_windows(ext_scr, tn, pos)):
        o_ref[0, :, g * group:(g + 1) * group] = slab
    ext_scr[0:POOL_HIST, :] = ext_scr[tn:tn + POOL_HIST, :]


def _pool_diff(hist, u, pos0, tn):
    b, n, dp = u.shape
    assert n % tn == 0 and hist.shape == (b, POOL_HIST, dp)
    return pl.pallas_call(
        functools.partial(_pool_body, pos0=pos0, tn=tn),
        out_shape=jax.ShapeDtypeStruct((b, n, dp), BF16),
        grid=(b, n // tn),
        in_specs=[pl.BlockSpec((1, POOL_HIST, dp), lambda i, s: (i, 0, 0)),
                  pl.BlockSpec((1, tn, dp), lambda i, s: (i, s, 0))],
        out_specs=pl.BlockSpec((1, tn, dp), lambda i, s: (i, s, 0)),
        scratch_shapes=[pltpu.VMEM((POOL_HIST + tn, dp), F32)],
        compiler_params=_cparams(("parallel", "arbitrary")),
        name="pool_diff",
    )(hist, u)


def _softplus2(z2):
    neg_abs = pltpu.bitcast(pltpu.bitcast(z2, jnp.uint32) | jnp.uint32(0x80000000), F32)
    return jnp.maximum(z2, 0.0) + jnp.log2(1.0 + jnp.exp2(neg_abs))


def _suffix_matrix(n):
    r = lax.broadcasted_iota(jnp.int32, (n, n), 0)
    c = lax.broadcasted_iota(jnp.int32, (n, n), 1)
    return jnp.where(r >= c, 1.0, 0.0).astype(BF16)


def _stack_heads(q2):
    lane = lax.broadcasted_iota(jnp.int32, q2.shape, 1)
    zero = jnp.zeros_like(q2)
    return jnp.concatenate([jnp.where(lane < HEAD_DIM, q2, zero), jnp.where(lane >= HEAD_DIM, q2, zero)], axis=0)


def _unstack_heads(acc, t):
    lane = lax.broadcasted_iota(jnp.int32, (t, LANES), 1)
    return jnp.where(lane < HEAD_DIM, acc[:t], acc[t:])


def _causal_mask(t):
    r = lax.broadcasted_iota(jnp.int32, (t, t), 0)
    c = lax.broadcasted_iota(jnp.int32, (t, t), 1)
    m = c < r
    return jnp.concatenate([m, m], axis=0)


def _sb_group(q_st, k_blocks, v_blocks, suffix, carry, masks, transposed_keys):
    (out, carry), = _sb_groups([q_st], [k_blocks], [v_blocks], suffix, [carry], masks, transposed_keys)
    return out, carry


def _sb_groups(q_sts, k_blocks, v_blocks, suffix, carries, masks, transposed_keys):
    dn = (((1,), (0,)), ((), ())) if transposed_keys else (((1,), (1,)), ((), ()))
    zs = [[lax.dot_general(q_st, kb, dn, preferred_element_type=F32) * LOG2E for kb in kbs]
          for q_st, kbs in zip(q_sts, k_blocks)]
    n_blk = len(masks)
    cs = _sb_masses([z for zq in zs for z in zq], masks * len(q_sts), [suffix] * (n_blk * len(q_sts)))
    results = []
    for i, (zq, vbs, carry) in enumerate(zip(zs, v_blocks, carries)):
        out = None
        for j, (z, vb, mask) in enumerate(zip(zq, vbs, masks)):
            a, carry = _sb_weights(z, cs[i * n_blk + j], carry, mask)
            o = jnp.dot(a, vb, preferred_element_type=F32)
            out = o if out is None else out + o
        results.append((out, carry))
    return results


def _sb_masses(zs, masks, suffixes):
    splits = []
    for z, mask in zip(zs, masks):
        sp = _softplus2(z)
        if mask is not None:
            sp = jnp.where(mask, sp, 0.0)
        hi = pltpu.bitcast(pltpu.bitcast(sp, jnp.uint32) & jnp.uint32(0xFFFF0000), F32)
        splits.append((hi.astype(BF16), (sp - hi).astype(BF16)))
    return [jnp.dot(hi, sfx, preferred_element_type=F32) + jnp.dot(lo, sfx, preferred_element_type=F32)
            for (hi, lo), sfx in zip(splits, suffixes)]


def _sb_weights(z, c, carry, mask):
    arg = z - c - carry
    if mask is not None:
        arg = jnp.where(mask, arg, NEG)
    return jnp.exp2(arg).astype(BF16), carry + c[:, 0:1]


def _attn_self_body(q_ref, k_ref, v_ref, o_ref, acc_scr, car_scr, min_scr, *, t):
    qi = pl.program_id(2)
    n_pairs = q_ref.shape[2] // LANES
    lanes = lambda p: slice(p * LANES, (p + 1) * LANES)
    block = lambda ref, p, b: ref[0, pl.ds(pl.multiple_of(b * t, t), t), lanes(p)]

    q_sts = [_stack_heads(q_ref[0, :, lanes(p)]) for p in range(n_pairs)]
    suffix = _suffix_matrix(t)
    mask = _causal_mask(t)

    def run(blocks, masks, first):
        carries = [jnp.zeros((2 * t, 1), F32) if first else car_scr[p, :, 0:1] for p in range(n_pairs)]
        results = _sb_groups(q_sts, [[block(k_ref, p, b) for b in blocks] for p in range(n_pairs)],
                             [[block(v_ref, p, b) for b in blocks] for p in range(n_pairs)], suffix, carries, masks,
                             False)
        low = None
        for p, (out, carry) in enumerate(results):
            acc_scr[p] = out if first else acc_scr[p] + out
            car_scr[p] = jnp.broadcast_to(carry, car_scr.shape[1:])
            low = jnp.min(carry) if low is None else jnp.minimum(low, jnp.min(carry))
        min_scr[0] = low

    @pl.when(qi == 0)
    def _():
        run([0], [mask], True)

    @pl.when(qi > 0)
    def _():
        run([qi, qi - 1], [mask, None], True)

    rest = jnp.maximum(qi - 1, 0)

    def more(it):
        return (it < rest // 2) & (min_scr[0] < SKIP_MASS)

    def pair(it):
        b0 = qi - 2 - 2 * it
        run([b0, b0 - 1], [None, None], False)
        return it + 1

    lax.while_loop(more, pair, 0)

    @pl.when((rest % 2 == 1) & (min_scr[0] < SKIP_MASS))
    def _():
        run([0], [None], False)

    o_ref[0] = jnp.concatenate([_unstack_heads(acc_scr[p], t) for p in range(n_pairs)],
                               axis=1).astype(o_ref.dtype)


def _attn_hist_body(q_ref, kn_ref, vn_ref, kh_ref, vh_ref, o_ref, left_ref, acc_scr, car_scr, min_scr, *, tk,
                    group):
    t = q_ref.shape[1]
    past = kh_ref.shape[1]
    q_st = _stack_heads(q_ref[0])

    def keep(out, carry, first):
        acc_scr[...] = out if first else acc_scr[...] + out
        car_scr[...] = jnp.broadcast_to(carry, car_scr.shape)
        min_scr[0] = jnp.min(carry)

    keep(*_sb_group(q_st, [kn_ref[0].astype(BF16)], [vn_ref[0].astype(BF16)], _suffix_matrix(t),
                    jnp.zeros((2 * t, 1), F32), [_causal_mask(t)], False), True)
    suffix = _suffix_matrix(tk)
    for top in range(past // tk, 0, -group):
        @pl.when(min_scr[0] < SKIP_MASS)
        def _(top=top):
            blocks = range(top - 1, top - 1 - group, -1)
            keep(*_sb_group(q_st, [kh_ref[0, b * tk:(b + 1) * tk, :].astype(BF16) for b in blocks],
                            [vh_ref[0, b * tk:(b + 1) * tk, :].astype(BF16) for b in blocks],
                            suffix, car_scr[:, 0:1], [None] * group, False), False)
    o_ref[0] = _unstack_heads(acc_scr[...], t).astype(o_ref.dtype)
    left_ref[0, 0] = jnp.full(left_ref.shape[2:], min_scr[0], F32)


def _attn_recent_body(q_ref, kn_ref, vn_ref, kc_ref, vc_ref, o_ref, left_ref, *, tk):
    t, dm = q_ref.shape[1:]
    n_blk = kc_ref.shape[1] // tk
    pair = lambda ref, rows, p: ref[0, rows, p * LANES:(p + 1) * LANES].astype(BF16)
    score = lambda qs, kb: lax.dot_general(qs, kb, (((1,), (1,)), ((), ())), preferred_element_type=F32)
    own = slice(None)
    blocks = [pl.ds(j * tk, tk) for j in range(n_blk - 1, -1, -1)]
    mask, sfx_new, sfx_old = _causal_mask(t), _suffix_matrix(t), _suffix_matrix(tk)
    zs, masks, sfx = [], [], []
    for p in range(dm // LANES):
        q_st = _stack_heads(pair(q_ref, own, p))
        zs.append(score(q_st, pair(kn_ref, own, p)) * LOG2E)
        masks.append(mask)
        sfx.append(sfx_new)
        for rows in blocks:
            zs.append(score(q_st, pair(kc_ref, rows, p)) * LOG2E)
            masks.append(None)
            sfx.append(sfx_old)
    cs = _sb_masses(zs, masks, sfx)
    outs, left = [], None
    chain = 1 + n_blk
    for p in range(dm // LANES):
        carry = jnp.zeros((2 * t, 1), F32)
        out = None
        for i in range(chain):
            a, carry = _sb_weights(zs[p * chain + i], cs[p * chain + i], carry, masks[p * chain + i])
            vb = pair(vn_ref, own, p) if i == 0 else pair(vc_ref, blocks[i - 1], p)
            o = jnp.dot(a, vb, preferred_element_type=F32)
            out = o if out is None else out + o
        outs.append(_unstack_heads(out, t))
        low = jnp.min(carry)
        left = low if left is None else jnp.minimum(left, low)
    o_ref[0] = jnp.concatenate(outs, axis=1).astype(o_ref.dtype)
    left_ref[0] = jnp.full(left_ref.shape[1:], left, F32)


def _attn_specs(t):
    tile = pl.BlockSpec((1, t, LANES), lambda i, p, s: (i, s, p))
    seq = lambda rows: pl.BlockSpec((1, rows, LANES), lambda i, p, s: (i, 0, p))
    state = [pltpu.VMEM((2 * t, LANES), F32), pltpu.VMEM((2 * t, LANES), F32), pltpu.SMEM((1,), F32)]
    return tile, seq, state


def _attention_self(q, k, v, t, pairs):
    b, n, dm = q.shape
    width = pairs * LANES
    assert n % t == 0 and dm % width == 0
    tile = pl.BlockSpec((1, t, width), lambda i, p, s: (i, s, p))
    seq = pl.BlockSpec((1, n, width), lambda i, p, s: (i, 0, p))
    return pl.pallas_call(
        functools.partial(_attn_self_body, t=t),
        out_shape=jax.ShapeDtypeStruct((b, n, dm), BF16),
        grid=(b, dm // width, n // t),
        in_specs=[tile, seq, seq],
        out_specs=tile,
        scratch_shapes=[pltpu.VMEM((pairs, 2 * t, LANES), F32), pltpu.VMEM((pairs, 2 * t, LANES), F32),
                        pltpu.SMEM((1,), F32)],
        compiler_params=_cparams(("parallel", "parallel", "arbitrary")),
        name="sb_attention",
    )(q, k, v)


def _attention_cached(q, k_new, v_new, k_hist, v_hist, tk, group):
    b, t, dm = q.shape
    past = k_hist.shape[1]
    assert dm % LANES == 0 and past % (tk * group) == 0
    tile, seq, state = _attn_specs(t)
    return pl.pallas_call(
        functools.partial(_attn_hist_body, tk=tk, group=group),
        out_shape=[jax.ShapeDtypeStruct((b, t, dm), BF16), jax.ShapeDtypeStruct((b, dm // LANES, 8, LANES), F32)],
        grid=(b, dm // LANES, 1),
        in_specs=[tile, seq(t), seq(t), seq(past), seq(past)],
        out_specs=[tile, pl.BlockSpec((1, 1, 8, LANES), lambda i, p, s: (i, p, 0, 0))],
        scratch_shapes=state,
        compiler_params=_cparams(("parallel", "parallel", "arbitrary")),
        name="sb_attention_cached",
    )(q, k_new, v_new, k_hist, v_hist)


def _attention_recent_first(q, k_new, v_new, cache_k, cache_v, tk, group):
    b, past, heads, hd = cache_k.shape
    t, dm = q.shape[1:]
    flat = lambda c: c.reshape(b, c.shape[1], heads * hd)
    walk = lambda: _attention_cached(q, k_new, v_new, flat(cache_k), flat(cache_v), tk, group)[0]
    recent = tk * group
    if past <= recent:
        return walk()
    seq = lambda rows: pl.BlockSpec((1, rows, dm), lambda i: (i, 0, 0))
    o, left = pl.pallas_call(
        functools.partial(_attn_recent_body, tk=tk),
        out_shape=[jax.ShapeDtypeStruct((b, t, dm), BF16), jax.ShapeDtypeStruct((b, 8, LANES), F32)],
        grid=(b,),
        in_specs=[seq(t), seq(t), seq(t), seq(recent), seq(recent)],
        out_specs=[seq(t), pl.BlockSpec((1, 8, LANES), lambda i: (i, 0, 0))],
        compiler_params=_cparams(("parallel",)),
        name="sb_attention_recent",
    )(q, k_new, v_new, flat(cache_k[:, past - recent:]), flat(cache_v[:, past - recent:]))
    return lax.cond(jnp.min(left) < SKIP_MASS, walk, lambda: o)


def _mix_body(*refs, tm, seq_rows):
    if seq_rows:
        u_ref, uprev_ref, *refs, ext_scr = refs
    else:
        diff_ref, *refs = refs
    (o_ref, gp_ref, gs_ref, x_ref, wp_ref, ps_ref, wsb_ref, wo_ref, nf_ref, wr_ref, br_ref,
     x1_ref, hp_ref, route_ref, cnt_ref, logit_scr) = refs

    @pl.when(pl.program_id(0) == 0)
    def _():
        logit_scr[...] = jnp.zeros_like(logit_scr)

    late_logits = logit_scr[...]

    n_pool = wp_ref.shape[0]
    group = wp_ref.shape[1]
    rows = tm // MIX_PARTS
    parts = [pl.ds(i * rows, rows) for i in range(MIX_PARTS)]
    if seq_rows:
        pos0 = (jnp.minimum(pl.program_id(0), pl.num_programs(0) - 2) * tm) % seq_rows
        ext_scr[0:POOL_HIST, :] = jnp.where(pos0 == 0, 0.0, uprev_ref[...])
        ext_scr[POOL_HIST:POOL_HIST + tm, :] = u_ref[...]
        slabs = _pool_windows(ext_scr, tm, pos0 + lax.broadcasted_iota(jnp.int32, (tm, 1), 0))
        diff_of = lambda i, g: slabs[g][i * rows:(i + 1) * rows]
    else:
        diff_of = lambda i, g: diff_ref[parts[i], g * group:(g + 1) * group]
    pools = [jnp.concatenate(
        [jnp.dot(diff_of(i, g), wp_ref[g], preferred_element_type=F32) for g in range(n_pool)],
        axis=-1) * ps_ref[...] for i in range(MIX_PARTS)]
    sbs = [jnp.dot(o_ref[s, :], wsb_ref[...], preferred_element_type=F32) for s in parts]
    mixed = [(gp_ref[s, :].astype(F32) * pool + gs_ref[s, :].astype(F32) * sb).astype(BF16)
             for s, pool, sb in zip(parts, pools, sbs)]
    x1s = [x_ref[s, :] + jnp.dot(mx, wo_ref[...], preferred_element_type=F32) for s, mx in zip(parts, mixed)]
    splits = []
    for s, x1 in zip(parts, x1s):
        x1_ref[s, :] = x1
        h = (x1 * lax.rsqrt(jnp.mean(x1 * x1, axis=-1, keepdims=True) + EPS)) * nf_ref[...]
        d_half = h.shape[1] // 2
        lo_bits = pltpu.bitcast(h[:, :d_half].astype(BF16).astype(F32), jnp.uint32)
        hi_bits = pltpu.bitcast(h[:, d_half:].astype(BF16).astype(F32), jnp.uint32)
        hp_ref[s, :] = (lo_bits >> 16) | (hi_bits & jnp.uint32(0xFFFF0000))
        hh = h.astype(BF16)
        splits.append(jnp.concatenate([hh, (h - hh.astype(F32)).astype(BF16)], axis=0))

    rs = [jnp.dot(sp, wr_ref[...], preferred_element_type=F32) for sp in splits]
    logit_scr[...] = jnp.concatenate(
        [(r[:rows, :LANES] + r[:rows, LANES:]) + (r[rows:, :LANES] + r[rows:, LANES:]) for r in rs],
        axis=0) + br_ref[...]
    _route_tile(late_logits, route_ref, cnt_ref, tm)


def _route_tile(logits, route_ref, cnt_ref, tm):
    lane = lax.broadcasted_iota(jnp.int32, (tm, LANES), 1)
    big = jnp.int32(LANES)

    def first_max(vals):
        m = jnp.max(vals, axis=-1, keepdims=True)
        idx = jnp.min(jnp.where(vals == m, lane, big), axis=-1, keepdims=True)
        return m, idx

    gl = jnp.where(lane < N_GROUPS, logits, NEG)
    gmax, grp = first_max(gl)
    p_grp = 1.0 / jnp.sum(jnp.exp(gl - gmax), axis=-1, keepdims=True)
    e_lo = N_GROUPS + grp * PER_GROUP
    el = jnp.where((lane >= e_lo) & (lane < e_lo + PER_GROUP), logits, NEG)
    m1, i1 = first_max(el)
    m2, i2 = first_max(jnp.where(lane == i1, NEG, el))
    t2 = jnp.exp(m2 - m1)
    w1 = p_grp / (1.0 + t2)
    w2 = w1 * t2
    e1 = i1 - N_GROUPS
    e2 = i2 - N_GROUPS

    oh1 = jnp.where(lane == e1, 1.0, 0.0).astype(BF16)
    oh2 = jnp.where(lane == e2, 1.0, 0.0).astype(BF16)
    rr = lax.broadcasted_iota(jnp.int32, (tm, tm), 0)
    cc = lax.broadcasted_iota(jnp.int32, (tm, tm), 1)
    before = jnp.where(cc < rr, 1.0, 0.0).astype(BF16)
    ones = jnp.ones((8, tm), BF16)
    pre1 = jnp.dot(before, oh1, preferred_element_type=F32)
    pre2 = jnp.dot(before, oh2, preferred_element_type=F32)
    c1 = jnp.dot(ones, oh1, preferred_element_type=F32)
    c2 = jnp.dot(ones, oh2, preferred_element_type=F32)
    rank1 = jnp.sum(jnp.where(lane == e1, pre1, 0.0), axis=-1, keepdims=True)
    rank2 = jnp.sum(jnp.where(lane == e2, pre2 + c1[0:1, :], 0.0), axis=-1, keepdims=True)
    cnt_ref[0] = c1 + c2

    route = jnp.where(lane == 0, e1.astype(F32), 0.0)
    route = jnp.where(lane == 1, e2.astype(F32), route)
    route = jnp.where(lane == 2, w1, route)
    route = jnp.where(lane == 3, w2, route)
    route = jnp.where(lane == 4, rank1, route)
    route = jnp.where(lane == 5, rank2, route)
    route_ref[...] = route


def _mix(pool_in, o, gp, gs, x, wp, ps, wsb, wo, nf, wr, br, tm, seq_rows=0):
    m, d = x.shape
    assert m % tm == 0 and seq_rows % tm == 0
    n = m // tm
    row = lambda i: (jnp.minimum(i, n - 1), 0)
    late = lambda i: (jnp.maximum(i - 1, 0), 0)
    full = lambda a: pl.BlockSpec(a.shape, lambda i: (0,) * a.ndim, pipeline_mode=pl.Buffered(1))
    acts = [pool_in, o, gp, gs, x]
    consts = [wp, ps, wsb, wo, nf, wr, br]
    in_specs = [pl.BlockSpec((tm, a.shape[1]), row) for a in acts] + [full(a) for a in consts]
    scratch = [pltpu.VMEM((tm, LANES), F32)]
    if seq_rows:
        before = lambda i: (jnp.maximum(row(i)[0] * (tm // POOL_HIST) - 1, 0), 0)
        acts.insert(1, pool_in)
        in_specs.insert(1, pl.BlockSpec((POOL_HIST, pool_in.shape[1]), before))
        scratch.append(pltpu.VMEM((POOL_HIST + tm, pool_in.shape[1]), F32))
    return pl.pallas_call(
        functools.partial(_mix_body, tm=tm, seq_rows=seq_rows),
        out_shape=[jax.ShapeDtypeStruct((m, d), F32), jax.ShapeDtypeStruct((m, d // 2), jnp.uint32),
                   jax.ShapeDtypeStruct((m, LANES), F32), jax.ShapeDtypeStruct((n, 8, LANES), F32)],
        grid=(n + 1,),
        in_specs=in_specs,
        out_specs=[pl.BlockSpec((tm, d), row), pl.BlockSpec((tm, d // 2), row),
                   pl.BlockSpec((tm, LANES), late), pl.BlockSpec((1, 8, LANES), lambda i: late(i) + (0,))],
        scratch_shapes=scratch,
        compiler_params=_cparams(("arbitrary",)),
        name="mix_outproj_router",
    )(*acts, *consts)


def _row_wait(src_ref, dst_ref, sem, n_rows):
    pltpu.make_async_copy(src_ref.at[pl.ds(0, n_rows)], dst_ref.at[pl.ds(0, n_rows)], sem).wait()


def _dispatch_body(slot_ref, zero_ref, hp_ref, xs_ref, buf, zbuf, sem, zsem, *, tm):
    i = pl.program_id(0)
    last = pl.num_programs(0) - 1
    par = i % 2

    @pl.when(i == 0)
    def _():
        zbuf[...] = jnp.zeros_like(zbuf)
        bm = zbuf.shape[0]
        for wait in (False, True):
            for z in range(zero_ref.shape[0]):
                @pl.when(zero_ref[z] >= 0)
                def _(z=z, wait=wait):
                    start = pl.multiple_of(jnp.maximum(zero_ref[z], 0), bm)
                    copy = pltpu.make_async_copy(zbuf, xs_ref.at[pl.ds(start, bm)], zsem)
                    copy.wait() if wait else copy.start()

    def drain(slot):
        for _ in range(2):
            _row_wait(buf.at[slot], xs_ref, sem.at[slot], tm)

    @pl.when(i >= 2)
    def _():
        drain(par)

    buf[par] = hp_ref[...]
    base = i * tm

    def body(t, _):
        src = buf.at[par, pl.ds(t, 1)]
        for kk in range(2):
            dst = xs_ref.at[pl.ds(slot_ref[2 * (base + t) + kk], 1)]
            pltpu.make_async_copy(src, dst, sem.at[par]).start()
        return 0

    lax.fori_loop(0, tm, body, 0, unroll=8)

    @pl.when(i == last)
    def _():
        drain(par)

        @pl.when(i >= 1)
        def _():
            drain(1 - par)


def _dispatch(slots_flat, zero_starts, hp, n_slots, tm, bm):
    m, dh = hp.shape
    assert m % tm == 0
    return pl.pallas_call(
        functools.partial(_dispatch_body, tm=tm),
        out_shape=jax.ShapeDtypeStruct((n_slots, dh), hp.dtype),
        grid_spec=pltpu.PrefetchScalarGridSpec(
            num_scalar_prefetch=2, grid=(m // tm,),
            in_specs=[pl.BlockSpec((tm, dh), lambda i, s, z: (i, 0))],
            out_specs=pl.BlockSpec(memory_space=pl.ANY),
            scratch_shapes=[pltpu.VMEM((2, tm, dh), hp.dtype), pltpu.VMEM((bm, dh), hp.dtype),
                            pltpu.SemaphoreType.DMA((2,)), pltpu.SemaphoreType.DMA]),
        compiler_params=_cparams(("arbitrary",), disable_bounds_checks=True, has_side_effects=True),
        name="moe_dispatch",
    )(slots_flat, zero_starts, hp)


def _ffn_body(be_ref, nv_ref, xs_ref, wg_ref, wu_ref, wd_ref, y_ref):
    del be_ref

    @pl.when(pl.program_id(0) >= nv_ref[0])
    def _():
        y_ref[...] = jnp.zeros_like(y_ref)

    @pl.when(pl.program_id(0) < nv_ref[0])
    def _():
        d_half = xs_ref.shape[1]
        rows = xs_ref.shape[0] // FFN_PARTS
        hids = []
        for i in range(FFN_PARTS):
            words = xs_ref[pl.ds(i * rows, rows), :]
            x_lo = pltpu.bitcast(words << 16, F32).astype(BF16)
            x_hi = pltpu.bitcast(words & jnp.uint32(0xFFFF0000), F32).astype(BF16)

            def proj(w_ref):
                return (jnp.dot(x_lo, w_ref[0, :d_half, :], preferred_element_type=F32)
                        + jnp.dot(x_hi, w_ref[0, d_half:, :], preferred_element_type=F32))

            hids.append((jax.nn.silu(proj(wg_ref)) * proj(wu_ref)).astype(BF16))
        for i, hid in enumerate(hids):
            y_ref[pl.ds(i * rows, rows), :] = jnp.dot(hid, wd_ref[0], preferred_element_type=F32)


def _ffn(block_expert, n_valid, xs, wg, wu, wd, bm):
    n_slots, d_half = xs.shape
    d = 2 * d_half
    de = wg.shape[2]
    live = lambda b, be, nv: (jnp.minimum(b, nv[0] - 1), 0)
    wsel = lambda b, be, nv: (be[b], 0, 0)
    return pl.pallas_call(
        _ffn_body,
        out_shape=jax.ShapeDtypeStruct((n_slots, d), F32),
        grid_spec=pltpu.PrefetchScalarGridSpec(
            num_scalar_prefetch=2, grid=(n_slots // bm,),
            in_specs=[pl.BlockSpec((bm, d_half), live), pl.BlockSpec((1, d, de), wsel),
                      pl.BlockSpec((1, d, de), wsel), pl.BlockSpec((1, de, d), wsel)],
            out_specs=pl.BlockSpec((bm, d), lambda b, be, nv: (b, 0))),
        compiler_params=_cparams(("arbitrary",)),
        name="moe_ffn",
    )(block_expert, n_valid, xs, wg, wu, wd)


def _final_body(slot_ref, x1_ref, route_ref, g_ref, y_hbm, o_ref, buf, sem, *, tm):
    i = pl.program_id(0)
    n_steps = pl.num_programs(0)

    def issue(step, par):
        def body(t, _):
            for kk in range(2):
                src = y_hbm.at[pl.ds(slot_ref[2 * (step * tm + t) + kk], 1)]
                pltpu.make_async_copy(src, buf.at[par, kk, pl.ds(t, 1)], sem.at[par]).start()
            return 0
        lax.fori_loop(0, tm, body, 0, unroll=8)

    @pl.when(i == 0)
    def _():
        issue(0, 0)

    @pl.when(i + 1 < n_steps)
    def _():
        issue(i + 1, (i + 1) % 2)

    par = i % 2
    for kk in range(2):
        _row_wait(y_hbm, buf.at[par, kk], sem.at[par], tm)
    route = route_ref[...]
    x2 = x1_ref[...] + route[:, 2:3] * buf[par, 0] + route[:, 3:4] * buf[par, 1]
    o_ref[...] = (x2 * lax.rsqrt(jnp.mean(x2 * x2, axis=-1, keepdims=True) + EPS)) * g_ref[...]


def _final(slots_flat, x1, route, g, y, tm):
    m, d = x1.shape
    assert m % tm == 0
    row = lambda i, s: (i, 0)
    return pl.pallas_call(
        functools.partial(_final_body, tm=tm),
        out_shape=jax.ShapeDtypeStruct((m, d), F32),
        grid_spec=pltpu.PrefetchScalarGridSpec(
            num_scalar_prefetch=1, grid=(m // tm,),
            in_specs=[pl.BlockSpec((tm, d), row), pl.BlockSpec((tm, LANES), row),
                      pl.BlockSpec((1, d), lambda i, s: (0, 0)), pl.BlockSpec(memory_space=pl.ANY)],
            out_specs=pl.BlockSpec((tm, d), row),
            scratch_shapes=[pltpu.VMEM((2, 2, tm, d), F32), pltpu.SemaphoreType.DMA((2,))]),
        compiler_params=_cparams(("arbitrary",), disable_bounds_checks=True),
        name="moe_combine_final_norm",
    )(slots_flat, x1, route, g, y)


def _routing_tables(route, cnt, tm, bm):
    m = route.shape[0]
    n_blocks = (2 * m) // bm + N_EXPERTS
    counts = cnt[:, 0, :N_EXPERTS].astype(jnp.int32)
    sizes = jnp.sum(counts, axis=0)
    padded = (sizes + bm - 1) // bm * bm
    pad_end = jnp.cumsum(padded)
    base = (pad_end - padded)[None, :] + jnp.cumsum(counts, axis=0) - counts
    base_tok = jnp.repeat(base, tm, axis=0)
    e = route[:, 0:2].astype(jnp.int32)
    rank = route[:, 4:6].astype(jnp.int32)
    sel = e[:, :, None] == jnp.arange(N_EXPERTS, dtype=jnp.int32)[None, None, :]
    slots = jnp.sum(jnp.where(sel, base_tok[:, None, :], 0), axis=-1) + rank
    n_valid = (pad_end[-1] // bm).astype(jnp.int32)
    blk = jnp.minimum(jnp.arange(n_blocks, dtype=jnp.int32), n_valid - 1)
    block_expert = jnp.minimum(jnp.sum(pad_end[None, :] <= (blk * bm)[:, None], axis=1), N_EXPERTS - 1)
    last_blk = jnp.where(padded > 0, pad_end - bm, -1)
    tail = n_valid + jnp.arange(N_EXPERTS, dtype=jnp.int32)
    tail = jnp.where(tail < n_blocks, tail * bm, -1)
    zero_starts = jnp.concatenate([last_blk, tail]).astype(jnp.int32)
    return slots.reshape(-1), block_expert.astype(jnp.int32), n_valid.reshape(1), zero_starts, n_blocks * bm


def _stream(x, pool_hist, k_hist, v_hist, p):
    b, n, d = x.shape
    m = b * n
    past = 0 if k_hist is None else k_hist.shape[1]
    x2d = x.reshape(m, d)
    u, q, kb, vb, k, v, gp, gs = _inproj(x2d, p['norm_mix'], p['w_in'], _fit(m, INPROJ_ROWS))
    dp = u.shape[1]
    fused_pool = pool_hist is None and n % MIX_ROWS == 0
    if fused_pool:
        pool_in = u
    else:
        hist = jnp.zeros((b, POOL_HIST, dp), F32) if pool_hist is None else pool_hist
        pool_in = _pool_diff(hist, u.reshape(b, n, dp), past, _fit(n, POOL_ROWS)).reshape(m, dp)
    shp = (b, n, kb.shape[1])
    if k_hist is None:
        o = _attention_self(q.reshape(shp), kb.reshape(shp), vb.reshape(shp), ATTN_TILE, ATTN_PAIRS)
    else:
        o = _attention_recent_first(q.reshape(shp), kb.reshape(shp), vb.reshape(shp), k_hist, v_hist, ATTN_TILE,
                                    CACHE_FIRST_BLOCKS)
    o = o.reshape(m, -1)
    x1, hp, route, cnt = _mix(pool_in, o, gp, gs, x2d, p['w_pool'], p['pool_scale'], p['w_sb_out'], p['w_out'],
                              p['norm_ffn'], p['w_r'], p['b_r'], MIX_ROWS, n if fused_pool else 0)
    slots, block_expert, n_valid, zero_starts, n_slots = _routing_tables(route, cnt, MIX_ROWS, EXPERT_ROWS)
    xs = _dispatch(slots, zero_starts, hp, n_slots, _fit(m, DISPATCH_ROWS), EXPERT_ROWS)
    y = _ffn(block_expert, n_valid, xs, p['w_g'], p['w_u'], p['w_d'], EXPERT_ROWS)
    out = _final(slots, x1, route, p['norm_final'], y, FINAL_ROWS)
    return out.reshape(b, n, d), u.reshape(b, n, dp), k, v


def kernel(x_prompt, x_sample, cache_sb_k, cache_sb_v, state_pool, norm_mix, w_in, w_pool, pool_scale, w_sb_out,
           w_out, norm_ffn, w_router_group, b_router_group, w_router_expert, b_router_expert, w_exp_gate,
           w_exp_up, w_exp_down, norm_final):
    depth = w_in.shape[0]
    assert depth == 1
    bp, sp, d = x_prompt.shape
    bs, ss, _ = x_sample.shape
    heads, hd = cache_sb_k.shape[3], cache_sb_k.shape[4]
    assert hd == HEAD_DIM
    dp = state_pool.shape[3]
    n_state = state_pool.shape[2]

    w_r = jnp.concatenate([w_router_group[0], w_router_expert[0]], axis=1)
    w_r = jnp.pad(w_r, ((0, 0), (0, LANES - w_r.shape[1])))
    w_r_hi = w_r.astype(BF16)
    b_r = jnp.concatenate([b_router_group[0], b_router_expert[0]])
    p = dict(
        norm_mix=norm_mix[0][None, :], w_in=w_in[0].astype(BF16), w_pool=w_pool[0].astype(BF16),
        pool_scale=pool_scale[0][None, :], w_sb_out=w_sb_out[0].astype(BF16), w_out=w_out[0].astype(BF16),
        norm_ffn=norm_ffn[0][None, :],
        w_r=jnp.concatenate([w_r_hi, (w_r - w_r_hi.astype(F32)).astype(BF16)], axis=1),
        b_r=jnp.pad(b_r, (0, LANES - b_r.shape[0]))[None, :].astype(F32),
        w_g=w_exp_gate[0].astype(BF16), w_u=w_exp_up[0].astype(BF16), w_d=w_exp_down[0].astype(BF16),
        norm_final=norm_final[None, :])

    hist_p = jnp.zeros((bp, POOL_HIST, dp), F32)
    yp, up, kp, vp = _stream(x_prompt, None, None, None, p)
    hist_s = jnp.pad(state_pool[0], ((0, 0), (POOL_HIST - n_state, 0), (0, 0)))
    ys, us, ks, vs = _stream(x_sample, hist_s, cache_sb_k[0], cache_sb_v[0], p)

    def pool_state(hist, u):
        return jnp.concatenate([hist[:, POOL_HIST - n_state:], u], axis=1)[:, -n_state:][None]

    return (yp, ys,
            kp.reshape(1, bp, sp, heads, hd), vp.reshape(1, bp, sp, heads, hd), pool_state(hist_p, up),
            ks.reshape(1, bs, ss, heads, hd), vs.reshape(1, bs, ss, heads, hd), pool_state(hist_s, us))
```

```python
import functools

import jax
import jax.numpy as jnp
from jax import lax
from jax.experimental import pallas as pl
from jax.experimental.pallas import tpu as pltpu

F32 = jnp.float32
BF16 = jnp.bfloat16

EPS = 1e-6
HEAD_DIM = 64
LANES = 128
POOL_WINDOWS = (2, 4, 8, 16)
POOL_HIST = 16
N_GROUPS = 4
PER_GROUP = 8
N_EXPERTS = N_GROUPS * PER_GROUP
VMEM_LIMIT = 58 * 1024 * 1024
NEG = -1e30
LOG2E = 1.4426950408889634
INPROJ_ROWS = 1024
POOL_ROWS = 512
ATTN_TILE = 256
CACHE_FIRST_BLOCKS = 1
MIX_ROWS = 256
DISPATCH_ROWS = 512
FINAL_ROWS = 256
EXPERT_ROWS = 256
ATTN_PAIRS = 4
MIX_PARTS = 1
FFN_PARTS = 2
SKIP_MASS = 160.0


def _fit(m, tile):
    while m % tile:
        tile //= 2
    return tile


def _cparams(sem, **kw):
    return pltpu.CompilerParams(dimension_semantics=sem, vmem_limit_bytes=VMEM_LIMIT, **kw)


def _proj_main_body(h_ref, w_ref, u_ref, q_ref, kb_ref, vb_ref, k_hbm, v_hbm, kv_buf, sem):
    i = pl.program_id(0)
    j = pl.program_id(1)
    last = pl.num_programs(0) - 1
    tm = h_ref.shape[0]
    heads = k_hbm.shape[1]

    def proj():
        return jnp.dot(h_ref[...], w_ref[...], preferred_element_type=F32)

    def head_copies(slot, dst_hbm, step):
        row0 = pl.multiple_of(step * tm, tm)
        return [pltpu.make_async_copy(kv_buf.at[slot, hh], dst_hbm.at[pl.ds(row0, tm), hh, :], sem.at[slot])
                for hh in range(heads)]

    @pl.when(j == 0)
    def _():
        u_ref[...] = proj()

    @pl.when(j == 1)
    def _():
        q_ref[...] = (proj() * (HEAD_DIM ** -0.5)).astype(BF16)

    for jj, slot, dense_ref, dst_hbm in ((2, 0, kb_ref, k_hbm), (3, 1, vb_ref, v_hbm)):
        @pl.when(j == jj)
        def _(slot=slot, dense_ref=dense_ref, dst_hbm=dst_hbm):
            @pl.when(i > 0)
            def _():
                for c in head_copies(slot, dst_hbm, i - 1):
                    c.wait()
            acc = proj()
            dense_ref[...] = acc.astype(BF16)
            for hh in range(heads):
                kv_buf[slot, hh] = acc[:, hh * HEAD_DIM:(hh + 1) * HEAD_DIM]
            for c in head_copies(slot, dst_hbm, i):
                c.start()

    @pl.when((i == last) & (j == 3))
    def _():
        for slot, dst_hbm in ((0, k_hbm), (1, v_hbm)):
            for c in head_copies(slot, dst_hbm, i):
                c.wait()


def _proj_gates_body(x_ref, g_ref, w_ref, h_ref, gp_ref, gs_ref, *, tn):
    j = pl.program_id(1)

    @pl.when(j == 0)
    def _():
        x = x_ref[...]
        r = lax.rsqrt(jnp.mean(x * x, axis=-1, keepdims=True) + EPS)
        h_ref[...] = ((x * r) * g_ref[...]).astype(BF16)

    for jj, ref in ((0, gp_ref), (2, gs_ref)):
        for half in range(2):
            @pl.when(j == jj + half)
            def _(ref=ref, half=half):
                a = jnp.dot(h_ref[...], w_ref[...], preferred_element_type=F32)
                ref[:, half * tn:(half + 1) * tn] = (0.5 * jnp.tanh(0.5 * a) + 0.5).astype(BF16)


def _inproj(x, g, w_bf, tm):
    m, d = x.shape
    tn = d // 2
    assert w_bf.shape == (d, 8 * tn) and m % tm == 0
    row = lambda i, j: (i, 0)
    h_spec = pl.BlockSpec((tm, d), row)
    h, gp, gs = pl.pallas_call(
        functools.partial(_proj_gates_body, tn=tn),
        out_shape=[jax.ShapeDtypeStruct((m, d), BF16)] * 3,
        grid=(m // tm, 4),
        in_specs=[h_spec, pl.BlockSpec((1, d), lambda i, j: (0, 0)),
                  pl.BlockSpec((d, tn), lambda i, j: (0, j + 4))],
        out_specs=[h_spec] * 3,
        compiler_params=_cparams(("parallel", "arbitrary")),
        name="inproj_gates",
    )(x, g, w_bf)
    dense = lambda dt: jax.ShapeDtypeStruct((m, tn), dt)
    cache = jax.ShapeDtypeStruct((m, tn // HEAD_DIM, HEAD_DIM), F32)
    any_spec = pl.BlockSpec(memory_space=pl.ANY)
    u, q, kb, vb, k, v = pl.pallas_call(
        _proj_main_body,
        out_shape=[dense(F32), dense(BF16), dense(BF16), dense(BF16), cache, cache],
        grid=(m // tm, 4),
        in_specs=[h_spec, pl.BlockSpec((d, tn), lambda i, j: (0, j))],
        out_specs=[pl.BlockSpec((tm, tn), row)] * 4 + [any_spec] * 2,
        scratch_shapes=[pltpu.VMEM((2, tn // HEAD_DIM, tm, HEAD_DIM), F32), pltpu.SemaphoreType.DMA((2,))],
        compiler_params=_cparams(("arbitrary", "arbitrary"), has_side_effects=True),
        name="inproj_main",
    )(h, w_bf)
    return u, q, kb, vb, k, v, gp, gs


def _pool_windows(ext_ref, tn, pos):
    group = ext_ref.shape[1] // len(POOL_WINDOWS)
    slabs = []
    for g, w in enumerate(POOL_WINDOWS):
        lo, hi = g * group, (g + 1) * group
        cur = ext_ref[POOL_HIST:POOL_HIST + tn, lo:hi]
        tot = cur
        for dlt in range(1, w):
            tot = tot + ext_ref[POOL_HIST - dlt:POOL_HIST - dlt + tn, lo:hi]
        cnt = jnp.minimum(pos + 1, w).astype(F32)
        slabs.append((tot / cnt - cur).astype(BF16))
    return slabs


def _pool_body(hist_ref, u_ref, o_ref, ext_scr, *, pos0, tn):
    s = pl.program_id(1)

    @pl.when(s == 0)
    def _():
        ext_scr[0:POOL_HIST, :] = hist_ref[0]

    ext_scr[POOL_HIST:POOL_HIST + tn, :] = u_ref[0]
    pos = pos0 + s * tn + lax.broadcasted_iota(jnp.int32, (tn, 1), 0)
    group = u_ref.shape[2] // len(POOL_WINDOWS)
    for g, slab in enumerate(_pool_windows(ext_scr, tn, pos)):
        o_ref[0, :, g * group:(g + 1) * group] = slab
    ext_scr[0:POOL_HIST, :] = ext_scr[tn:tn + POOL_HIST, :]


def _pool_diff(hist, u, pos0, tn):
    b, n, dp = u.shape
    assert n % tn == 0 and hist.shape == (b, POOL_HIST, dp)
    return pl.pallas_call(
        functools.partial(_pool_body, pos0=pos0, tn=tn),
        out_shape=jax.ShapeDtypeStruct((b, n, dp), BF16),
        grid=(b, n // tn),
        in_specs=[pl.BlockSpec((1, POOL_HIST, dp), lambda i, s: (i, 0, 0)),
                  pl.BlockSpec((1, tn, dp), lambda i, s: (i, s, 0))],
        out_specs=pl.BlockSpec((1, tn, dp), lambda i, s: (i, s, 0)),
        scratch_shapes=[pltpu.VMEM((POOL_HIST + tn, dp), F32)],
        compiler_params=_cparams(("parallel", "arbitrary")),
        name="pool_diff",
    )(hist, u)


def _softplus2(z2):
    neg_abs = pltpu.bitcast(pltpu.bitcast(z2, jnp.uint32) | jnp.uint32(0x80000000), F32)
    return jnp.maximum(z2, 0.0) + jnp.log2(1.0 + jnp.exp2(neg_abs))


def _suffix_matrix(n):
    r = lax.broadcasted_iota(jnp.int32, (n, n), 0)
    c = lax.broadcasted_iota(jnp.int32, (n, n), 1)
    return jnp.where(r >= c, 1.0, 0.0).astype(BF16)


def _stack_heads(q2):
    lane = lax.broadcasted_iota(jnp.int32, q2.shape, 1)
    zero = jnp.zeros_like(q2)
    return jnp.concatenate([jnp.where(lane < HEAD_DIM, q2, zero), jnp.where(lane >= HEAD_DIM, q2, zero)], axis=0)


def _unstack_heads(acc, t):
    lane = lax.broadcasted_iota(jnp.int32, (t, LANES), 1)
    return jnp.where(lane < HEAD_DIM, acc[:t], acc[t:])


def _causal_mask(t):
    r = lax.broadcasted_iota(jnp.int32, (t, t), 0)
    c = lax.broadcasted_iota(jnp.int32, (t, t), 1)
    m = c < r
    return jnp.concatenate([m, m], axis=0)


def _sb_group(q_st, k_blocks, v_blocks, suffix, carry, masks, transposed_keys):
    (out, carry), = _sb_groups([q_st], [k_blocks], [v_blocks], suffix, [carry], masks, transposed_keys)
    return out, carry


def _sb_groups(q_sts, k_blocks, v_blocks, suffix, carries, masks, transposed_keys):
    dn = (((1,), (0,)), ((), ())) if transposed_keys else (((1,), (1,)), ((), ()))
    zs = [[lax.dot_general(q_st, kb, dn, preferred_element_type=F32) * LOG2E for kb in kbs]
          for q_st, kbs in zip(q_sts, k_blocks)]
    n_blk = len(masks)
    cs = _sb_masses([z for zq in zs for z in zq], masks * len(q_sts), [suffix] * (n_blk * len(q_sts)))
    results = []
    for i, (zq, vbs, carry) in enumerate(zip(zs, v_blocks, carries)):
        out = None
        for j, (z, vb, mask) in enumerate(zip(zq, vbs, masks)):
            a, carry = _sb_weights(z, cs[i * n_blk + j], carry, mask)
            o = jnp.dot(a, vb, preferred_element_type=F32)
            out = o if out is None else out + o
        results.append((out, carry))
    return results


def _sb_masses(zs, masks, suffixes):
    splits = []
    for z, mask in zip(zs, masks):
        sp = _softplus2(z)
        if mask is not None:
            sp = jnp.where(mask, sp, 0.0)
        hi = pltpu.bitcast(pltpu.bitcast(sp, jnp.uint32) & jnp.uint32(0xFFFF0000), F32)
        splits.append((hi.astype(BF16), (sp - hi).astype(BF16)))
    return [jnp.dot(hi, sfx, preferred_element_type=F32) + jnp.dot(lo, sfx, preferred_element_type=F32)
            for (hi, lo), sfx in zip(splits, suffixes)]


def _sb_weights(z, c, carry, mask):
    arg = z - c - carry
    if mask is not None:
        arg = jnp.where(mask, arg, NEG)
    return jnp.exp2(arg).astype(BF16), carry + c[:, 0:1]


def _attn_self_body(q_ref, k_ref, v_ref, o_ref, acc_scr, car_scr, min_scr, *, t):
    qi = pl.program_id(2)
    n_pairs = q_ref.shape[2] // LANES
    lanes = lambda p: slice(p * LANES, (p + 1) * LANES)
    block = lambda ref, p, b: ref[0, pl.ds(pl.multiple_of(b * t, t), t), lanes(p)]

    q_sts = [_stack_heads(q_ref[0, :, lanes(p)]) for p in range(n_pairs)]
    suffix = _suffix_matrix(t)
    mask = _causal_mask(t)

    def run(blocks, masks, first):
        carries = [jnp.zeros((2 * t, 1), F32) if first else car_scr[p, :, 0:1] for p in range(n_pairs)]
        results = _sb_groups(q_sts, [[block(k_ref, p, b) for b in blocks] for p in range(n_pairs)],
                             [[block(v_ref, p, b) for b in blocks] for p in range(n_pairs)], suffix, carries, masks,
                             False)
        low = None
        for p, (out, carry) in enumerate(results):
            acc_scr[p] = out if first else acc_scr[p] + out
            car_scr[p] = jnp.broadcast_to(carry, car_scr.shape[1:])
            low = jnp.min(carry) if low is None else jnp.minimum(low, jnp.min(carry))
        min_scr[0] = low

    @pl.when(qi == 0)
    def _():
        run([0], [mask], True)

    @pl.when(qi > 0)
    def _():
        run([qi, qi - 1], [mask, None], True)

    rest = jnp.maximum(qi - 1, 0)

    def more(it):
        return (it < rest // 2) & (min_scr[0] < SKIP_MASS)

    def pair(it):
        b0 = qi - 2 - 2 * it
        run([b0, b0 - 1], [None, None], False)
        return it + 1

    lax.while_loop(more, pair, 0)

    @pl.when((rest % 2 == 1) & (min_scr[0] < SKIP_MASS))
    def _():
        run([0], [None], False)

    o_ref[0] = jnp.concatenate([_unstack_heads(acc_scr[p], t) for p in range(n_pairs)],
                               axis=1).astype(o_ref.dtype)


def _attn_hist_body(q_ref, kn_ref, vn_ref, kh_ref, vh_ref, o_ref, left_ref, acc_scr, car_scr, min_scr, *, tk,
                    group):
    t = q_ref.shape[1]
    past = kh_ref.shape[1]
    q_st = _stack_heads(q_ref[0])

    def keep(out, carry, first):
        acc_scr[...] = out if first else acc_scr[...] + out
        car_scr[...] = jnp.broadcast_to(carry, car_scr.shape)
        min_scr[0] = jnp.min(carry)

    keep(*_sb_group(q_st, [kn_ref[0].astype(BF16)], [vn_ref[0].astype(BF16)], _suffix_matrix(t),
                    jnp.zeros((2 * t, 1), F32), [_causal_mask(t)], False), True)
    suffix = _suffix_matrix(tk)
    for top in range(past // tk, 0, -group):
        @pl.when(min_scr[0] < SKIP_MASS)
        def _(top=top):
            blocks = range(top - 1, top - 1 - group, -1)
            keep(*_sb_group(q_st, [kh_ref[0, b * tk:(b + 1) * tk, :].astype(BF16) for b in blocks],
                            [vh_ref[0, b * tk:(b + 1) * tk, :].astype(BF16) for b in blocks],
                            suffix, car_scr[:, 0:1], [None] * group, False), False)
    o_ref[0] = _unstack_heads(acc_scr[...], t).astype(o_ref.dtype)
    left_ref[0, 0] = jnp.full(left_ref.shape[2:], min_scr[0], F32)


def _attn_recent_body(q_ref, kn_ref, vn_ref, kc_ref, vc_ref, o_ref, left_ref, *, tk):
    t, dm = q_ref.shape[1:]
    n_blk = kc_ref.shape[1] // tk
    pair = lambda ref, rows, p: ref[0, rows, p * LANES:(p + 1) * LANES].astype(BF16)
    score = lambda qs, kb: lax.dot_general(qs, kb, (((1,), (1,)), ((), ())), preferred_element_type=F32)
    own = slice(None)
    blocks = [pl.ds(j * tk, tk) for j in range(n_blk - 1, -1, -1)]
    mask, sfx_new, sfx_old = _causal_mask(t), _suffix_matrix(t), _suffix_matrix(tk)
    zs, masks, sfx = [], [], []
    for p in range(dm // LANES):
        q_st = _stack_heads(pair(q_ref, own, p))
        zs.append(score(q_st, pair(kn_ref, own, p)) * LOG2E)
        masks.append(mask)
        sfx.append(sfx_new)
        for rows in blocks:
            zs.append(score(q_st, pair(kc_ref, rows, p)) * LOG2E)
            masks.append(None)
            sfx.append(sfx_old)
    cs = _sb_masses(zs, masks, sfx)
    outs, left = [], None
    chain = 1 + n_blk
    for p in range(dm // LANES):
        carry = jnp.zeros((2 * t, 1), F32)
        out = None
        for i in range(chain):
            a, carry = _sb_weights(zs[p * chain + i], cs[p * chain + i], carry, masks[p * chain + i])
            vb = pair(vn_ref, own, p) if i == 0 else pair(vc_ref, blocks[i - 1], p)
            o = jnp.dot(a, vb, preferred_element_type=F32)
            out = o if out is None else out + o
        outs.append(_unstack_heads(out, t))
        low = jnp.min(carry)
        left = low if left is None else jnp.minimum(left, low)
    o_ref[0] = jnp.concatenate(outs, axis=1).astype(o_ref.dtype)
    left_ref[0] = jnp.full(left_ref.shape[1:], left, F32)


def _attn_specs(t):
    tile = pl.BlockSpec((1, t, LANES), lambda i, p, s: (i, s, p))
    seq = lambda rows: pl.BlockSpec((1, rows, LANES), lambda i, p, s: (i, 0, p))
    state = [pltpu.VMEM((2 * t, LANES), F32), pltpu.VMEM((2 * t, LANES), F32), pltpu.SMEM((1,), F32)]
    return tile, seq, state


def _attention_self(q, k, v, t, pairs):
    b, n, dm = q.shape
    width = pairs * LANES
    assert n % t == 0 and dm % width == 0
    tile = pl.BlockSpec((1, t, width), lambda i, p, s: (i, s, p))
    seq = pl.BlockSpec((1, n, width), lambda i, p, s: (i, 0, p))
    return pl.pallas_call(
        functools.partial(_attn_self_body, t=t),
        out_shape=jax.ShapeDtypeStruct((b, n, dm), BF16),
        grid=(b, dm // width, n // t),
        in_specs=[tile, seq, seq],
        out_specs=tile,
        scratch_shapes=[pltpu.VMEM((pairs, 2 * t, LANES), F32), pltpu.VMEM((pairs, 2 * t, LANES), F32),
                        pltpu.SMEM((1,), F32)],
        compiler_params=_cparams(("parallel", "parallel", "arbitrary")),
        name="sb_attention",
    )(q, k, v)


def _attention_cached(q, k_new, v_new, k_hist, v_hist, tk, group):
    b, t, dm = q.shape
    past = k_hist.shape[1]
    assert dm % LANES == 0 and past % (tk * group) == 0
    tile, seq, state = _attn_specs(t)
    return pl.pallas_call(
        functools.partial(_attn_hist_body, tk=tk, group=group),
        out_shape=[jax.ShapeDtypeStruct((b, t, dm), BF16), jax.ShapeDtypeStruct((b, dm // LANES, 8, LANES), F32)],
        grid=(b, dm // LANES, 1),
        in_specs=[tile, seq(t), seq(t), seq(past), seq(past)],
        out_specs=[tile, pl.BlockSpec((1, 1, 8, LANES), lambda i, p, s: (i, p, 0, 0))],
        scratch_shapes=state,
        compiler_params=_cparams(("parallel", "parallel", "arbitrary")),
        name="sb_attention_cached",
    )(q, k_new, v_new, k_hist, v_hist)


def _attention_recent_first(q, k_new, v_new, cache_k, cache_v, tk, group):
    b, past, heads, hd = cache_k.shape
    t, dm = q.shape[1:]
    flat = lambda c: c.reshape(b, c.shape[1], heads * hd)
    walk = lambda: _attention_cached(q, k_new, v_new, flat(cache_k), flat(cache_v), tk, group)[0]
    recent = tk * group
    if past <= recent:
        return walk()
    seq = lambda rows: pl.BlockSpec((1, rows, dm), lambda i: (i, 0, 0))
    o, left = pl.pallas_call(
        functools.partial(_attn_recent_body, tk=tk),
        out_shape=[jax.ShapeDtypeStruct((b, t, dm), BF16), jax.ShapeDtypeStruct((b, 8, LANES), F32)],
        grid=(b,),
        in_specs=[seq(t), seq(t), seq(t), seq(recent), seq(recent)],
        out_specs=[seq(t), pl.BlockSpec((1, 8, LANES), lambda i: (i, 0, 0))],
        compiler_params=_cparams(("parallel",)),
        name="sb_attention_recent",
    )(q, k_new, v_new, flat(cache_k[:, past - recent:]), flat(cache_v[:, past - recent:]))
    return lax.cond(jnp.min(left) < SKIP_MASS, walk, lambda: o)


def _mix_body(*refs, tm, seq_rows):
    if seq_rows:
        u_ref, uprev_ref, *refs, ext_scr = refs
    else:
        diff_ref, *refs = refs
    (o_ref, gp_ref, gs_ref, x_ref, wp_ref, ps_ref, wsb_ref, wo_ref, nf_ref, wr_ref, br_ref,
     x1_ref, hp_ref, route_ref, cnt_ref, logit_scr) = refs

    @pl.when(pl.program_id(0) == 0)
    def _():
        logit_scr[...] = jnp.zeros_like(logit_scr)

    late_logits = logit_scr[...]

    n_pool = wp_ref.shape[0]
    group = wp_ref.shape[1]
    rows = tm // MIX_PARTS
    parts = [pl.ds(i * rows, rows) for i in range(MIX_PARTS)]
    if seq_rows:
        pos0 = (jnp.minimum(pl.program_id(0), pl.num_programs(0) - 2) * tm) % seq_rows
        ext_scr[0:POOL_HIST, :] = jnp.where(pos0 == 0, 0.0, uprev_ref[...])
        ext_scr[POOL_HIST:POOL_HIST + tm, :] = u_ref[...]
        slabs = _pool_windows(ext_scr, tm, pos0 + lax.broadcasted_iota(jnp.int32, (tm, 1), 0))
        diff_of = lambda i, g: slabs[g][i * rows:(i + 1) * rows]
    else:
        diff_of = lambda i, g: diff_ref[parts[i], g * group:(g + 1) * group]
    pools = [jnp.concatenate(
        [jnp.dot(diff_of(i, g), wp_ref[g], preferred_element_type=F32) for g in range(n_pool)],
        axis=-1) * ps_ref[...] for i in range(MIX_PARTS)]
    sbs = [jnp.dot(o_ref[s, :], wsb_ref[...], preferred_element_type=F32) for s in parts]
    mixed = [(gp_ref[s, :].astype(F32) * pool + gs_ref[s, :].astype(F32) * sb).astype(BF16)
             for s, pool, sb in zip(parts, pools, sbs)]
    x1s = [x_ref[s, :] + jnp.dot(mx, wo_ref[...], preferred_element_type=F32) for s, mx in zip(parts, mixed)]
    splits = []
    for s, x1 in zip(parts, x1s):
        x1_ref[s, :] = x1
        h = (x1 * lax.rsqrt(jnp.mean(x1 * x1, axis=-1, keepdims=True) + EPS)) * nf_ref[...]
        d_half = h.shape[1] // 2
        lo_bits = pltpu.bitcast(h[:, :d_half].astype(BF16).astype(F32), jnp.uint32)
        hi_bits = pltpu.bitcast(h[:, d_half:].astype(BF16).astype(F32), jnp.uint32)
        hp_ref[s, :] = (lo_bits >> 16) | (hi_bits & jnp.uint32(0xFFFF0000))
        hh = h.astype(BF16)
        splits.append(jnp.concatenate([hh, (h - hh.astype(F32)).astype(BF16)], axis=0))

    rs = [jnp.dot(sp, wr_ref[...], preferred_element_type=F32) for sp in splits]
    logit_scr[...] = jnp.concatenate(
        [(r[:rows, :LANES] + r[:rows, LANES:]) + (r[rows:, :LANES] + r[rows:, LANES:]) for r in rs],
        axis=0) + br_ref[...]
    _route_tile(late_logits, route_ref, cnt_ref, tm)


def _route_tile(logits, route_ref, cnt_ref, tm):
    lane = lax.broadcasted_iota(jnp.int32, (tm, LANES), 1)
    big = jnp.int32(LANES)

    def first_max(vals):
        m = jnp.max(vals, axis=-1, keepdims=True)
        idx = jnp.min(jnp.where(vals == m, lane, big), axis=-1, keepdims=True)
        return m, idx

    gl = jnp.where(lane < N_GROUPS, logits, NEG)
    gmax, grp = first_max(gl)
    p_grp = 1.0 / jnp.sum(jnp.exp(gl - gmax), axis=-1, keepdims=True)
    e_lo = N_GROUPS + grp * PER_GROUP
    el = jnp.where((lane >= e_lo) & (lane < e_lo + PER_GROUP), logits, NEG)
    m1, i1 = first_max(el)
    m2, i2 = first_max(jnp.where(lane == i1, NEG, el))
    t2 = jnp.exp(m2 - m1)
    w1 = p_grp / (1.0 + t2)
    w2 = w1 * t2
    e1 = i1 - N_GROUPS
    e2 = i2 - N_GROUPS

    oh1 = jnp.where(lane == e1, 1.0, 0.0).astype(BF16)
    oh2 = jnp.where(lane == e2, 1.0, 0.0).astype(BF16)
    rr = lax.broadcasted_iota(jnp.int32, (tm, tm), 0)
    cc = lax.broadcasted_iota(jnp.int32, (tm, tm), 1)
    before = jnp.where(cc < rr, 1.0, 0.0).astype(BF16)
    ones = jnp.ones((8, tm), BF16)
    pre1 = jnp.dot(before, oh1, preferred_element_type=F32)
    pre2 = jnp.dot(before, oh2, preferred_element_type=F32)
    c1 = jnp.dot(ones, oh1, preferred_element_type=F32)
    c2 = jnp.dot(ones, oh2, preferred_element_type=F32)
    rank1 = jnp.sum(jnp.where(lane == e1, pre1, 0.0), axis=-1, keepdims=True)
    rank2 = jnp.sum(jnp.where(lane == e2, pre2 + c1[0:1, :], 0.0), axis=-1, keepdims=True)
    cnt_ref[0] = c1 + c2

    route = jnp.where(lane == 0, e1.astype(F32), 0.0)
    route = jnp.where(lane == 1, e2.astype(F32), route)
    route = jnp.where(lane == 2, w1, route)
    route = jnp.where(lane == 3, w2, route)
    route = jnp.where(lane == 4, rank1, route)
    route = jnp.where(lane == 5, rank2, route)
    route_ref[...] = route


def _mix(pool_in, o, gp, gs, x, wp, ps, wsb, wo, nf, wr, br, tm, seq_rows=0):
    m, d = x.shape
    assert m % tm == 0 and seq_rows % tm == 0
    n = m // tm
    row = lambda i: (jnp.minimum(i, n - 1), 0)
    late = lambda i: (jnp.maximum(i - 1, 0), 0)
    full = lambda a: pl.BlockSpec(a.shape, lambda i: (0,) * a.ndim, pipeline_mode=pl.Buffered(1))
    acts = [pool_in, o, gp, gs, x]
    consts = [wp, ps, wsb, wo, nf, wr, br]
    in_specs = [pl.BlockSpec((tm, a.shape[1]), row) for a in acts] + [full(a) for a in consts]
    scratch = [pltpu.VMEM((tm, LANES), F32)]
    if seq_rows:
        before = lambda i: (jnp.maximum(row(i)[0] * (tm // POOL_HIST) - 1, 0), 0)
        acts.insert(1, pool_in)
        in_specs.insert(1, pl.BlockSpec((POOL_HIST, pool_in.shape[1]), before))
        scratch.append(pltpu.VMEM((POOL_HIST + tm, pool_in.shape[1]), F32))
    return pl.pallas_call(
        functools.partial(_mix_body, tm=tm, seq_rows=seq_rows),
        out_shape=[jax.ShapeDtypeStruct((m, d), F32), jax.ShapeDtypeStruct((m, d // 2), jnp.uint32),
                   jax.ShapeDtypeStruct((m, LANES), F32), jax.ShapeDtypeStruct((n, 8, LANES), F32)],
        grid=(n + 1,),
        in_specs=in_specs,
        out_specs=[pl.BlockSpec((tm, d), row), pl.BlockSpec((tm, d // 2), row),
                   pl.BlockSpec((tm, LANES), late), pl.BlockSpec((1, 8, LANES), lambda i: late(i) + (0,))],
        scratch_shapes=scratch,
        compiler_params=_cparams(("arbitrary",)),
        name="mix_outproj_router",
    )(*acts, *consts)


def _row_wait(src_ref, dst_ref, sem, n_rows):
    pltpu.make_async_copy(src_ref.at[pl.ds(0, n_rows)], dst_ref.at[pl.ds(0, n_rows)], sem).wait()


def _dispatch_body(slot_ref, zero_ref, hp_ref, xs_ref, buf, zbuf, sem, zsem, *, tm):
    i = pl.program_id(0)
    last = pl.num_programs(0) - 1
    par = i % 2

    @pl.when(i == 0)
    def _():
        zbuf[...] = jnp.zeros_like(zbuf)
        bm = zbuf.shape[0]
        for wait in (False, True):
            for z in range(zero_ref.shape[0]):
                @pl.when(zero_ref[z] >= 0)
                def _(z=z, wait=wait):
                    start = pl.multiple_of(jnp.maximum(zero_ref[z], 0), bm)
                    copy = pltpu.make_async_copy(zbuf, xs_ref.at[pl.ds(start, bm)], zsem)
                    copy.wait() if wait else copy.start()

    def drain(slot):
        for _ in range(2):
            _row_wait(buf.at[slot], xs_ref, sem.at[slot], tm)

    @pl.when(i >= 2)
    def _():
        drain(par)

    buf[par] = hp_ref[...]
    base = i * tm

    def body(t, _):
        src = buf.at[par, pl.ds(t, 1)]
        for kk in range(2):
            dst = xs_ref.at[pl.ds(slot_ref[2 * (base + t) + kk], 1)]
            pltpu.make_async_copy(src, dst, sem.at[par]).start(priority=kk)
        return 0

    lax.fori_loop(0, tm, body, 0, unroll=8)

    @pl.when(i == last)
    def _():
        drain(par)

        @pl.when(i >= 1)
        def _():
            drain(1 - par)


def _dispatch(slots_flat, zero_starts, hp, n_slots, tm, bm):
    m, dh = hp.shape
    assert m % tm == 0
    return pl.pallas_call(
        functools.partial(_dispatch_body, tm=tm),
        out_shape=jax.ShapeDtypeStruct((n_slots, dh), hp.dtype),
        grid_spec=pltpu.PrefetchScalarGridSpec(
            num_scalar_prefetch=2, grid=(m // tm,),
            in_specs=[pl.BlockSpec((tm, dh), lambda i, s, z: (i, 0))],
            out_specs=pl.BlockSpec(memory_space=pl.ANY),
            scratch_shapes=[pltpu.VMEM((2, tm, dh), hp.dtype), pltpu.VMEM((bm, dh), hp.dtype),
                            pltpu.SemaphoreType.DMA((2,)), pltpu.SemaphoreType.DMA]),
        compiler_params=_cparams(("arbitrary",), disable_bounds_checks=True, has_side_effects=True),
        name="moe_dispatch",
    )(slots_flat, zero_starts, hp)


def _ffn_body(be_ref, nv_ref, xs_ref, wg_ref, wu_ref, wd_ref, y_ref):
    del be_ref

    @pl.when(pl.program_id(0) >= nv_ref[0])
    def _():
        y_ref[...] = jnp.zeros_like(y_ref)

    @pl.when(pl.program_id(0) < nv_ref[0])
    def _():
        d_half = xs_ref.shape[1]
        rows = xs_ref.shape[0] // FFN_PARTS
        hids = []
        for i in range(FFN_PARTS):
            words = xs_ref[pl.ds(i * rows, rows), :]
            x_lo = pltpu.bitcast(words << 16, F32).astype(BF16)
            x_hi = pltpu.bitcast(words & jnp.uint32(0xFFFF0000), F32).astype(BF16)

            def proj(w_ref):
                return (jnp.dot(x_lo, w_ref[0, :d_half, :], preferred_element_type=F32)
                        + jnp.dot(x_hi, w_ref[0, d_half:, :], preferred_element_type=F32))

            hids.append((jax.nn.silu(proj(wg_ref)) * proj(wu_ref)).astype(BF16))
        for i, hid in enumerate(hids):
            y_ref[pl.ds(i * rows, rows), :] = jnp.dot(hid, wd_ref[0], preferred_element_type=F32)


def _ffn(block_expert, n_valid, xs, wg, wu, wd, bm):
    n_slots, d_half = xs.shape
    d = 2 * d_half
    de = wg.shape[2]
    live = lambda b, be, nv: (jnp.minimum(b, nv[0] - 1), 0)
    wsel = lambda b, be, nv: (be[b], 0, 0)
    return pl.pallas_call(
        _ffn_body,
        out_shape=jax.ShapeDtypeStruct((n_slots, d), F32),
        grid_spec=pltpu.PrefetchScalarGridSpec(
            num_scalar_prefetch=2, grid=(n_slots // bm,),
            in_specs=[pl.BlockSpec((bm, d_half), live), pl.BlockSpec((1, d, de), wsel),
                      pl.BlockSpec((1, d, de), wsel), pl.BlockSpec((1, de, d), wsel)],
            out_specs=pl.BlockSpec((bm, d), lambda b, be, nv: (b, 0))),
        compiler_params=_cparams(("arbitrary",)),
        name="moe_ffn",
    )(block_expert, n_valid, xs, wg, wu, wd)


def _final_body(slot_ref, x1_ref, route_ref, g_ref, y_hbm, o_ref, buf, sem, *, tm):
    i = pl.program_id(0)
    n_steps = pl.num_programs(0)

    def issue(step, par):
        def body(t, _):
            for kk in range(2):
                src = y_hbm.at[pl.ds(slot_ref[2 * (step * tm + t) + kk], 1)]
                pltpu.make_async_copy(src, buf.at[par, kk, pl.ds(t, 1)], sem.at[par]).start(priority=kk)
            return 0
        lax.fori_loop(0, tm, body, 0, unroll=8)

    @pl.when(i == 0)
    def _():
        issue(0, 0)

    @pl.when(i + 1 < n_steps)
    def _():
        issue(i + 1, (i + 1) % 2)

    par = i % 2
    for kk in range(2):
        _row_wait(y_hbm, buf.at[par, kk], sem.at[par], tm)
    route = route_ref[...]
    x2 = x1_ref[...] + route[:, 2:3] * buf[par, 0] + route[:, 3:4] * buf[par, 1]
    o_ref[...] = (x2 * lax.rsqrt(jnp.mean(x2 * x2, axis=-1, keepdims=True) + EPS)) * g_ref[...]


def _final(slots_flat, x1, route, g, y, tm):
    m, d = x1.shape
    assert m % tm == 0
    row = lambda i, s: (i, 0)
    return pl.pallas_call(
        functools.partial(_final_body, tm=tm),
        out_shape=jax.ShapeDtypeStruct((m, d), F32),
        grid_spec=pltpu.PrefetchScalarGridSpec(
            num_scalar_prefetch=1, grid=(m // tm,),
            in_specs=[pl.BlockSpec((tm, d), row), pl.BlockSpec((tm, LANES), row),
                      pl.BlockSpec((1, d), lambda i, s: (0, 0)), pl.BlockSpec(memory_space=pl.ANY)],
            out_specs=pl.BlockSpec((tm, d), row),
            scratch_shapes=[pltpu.VMEM((2, 2, tm, d), F32), pltpu.SemaphoreType.DMA((2,))]),
        compiler_params=_cparams(("arbitrary",), disable_bounds_checks=True),
        name="moe_combine_final_norm",
    )(slots_flat, x1, route, g, y)


def _routing_tables(route, cnt, tm, bm):
    m = route.shape[0]
    n_blocks = (2 * m) // bm + N_EXPERTS
    counts = cnt[:, 0, :N_EXPERTS].astype(jnp.int32)
    sizes = jnp.sum(counts, axis=0)
    padded = (sizes + bm - 1) // bm * bm
    pad_end = jnp.cumsum(padded)
    base = (pad_end - padded)[None, :] + jnp.cumsum(counts, axis=0) - counts
    base_tok = jnp.repeat(base, tm, axis=0)
    e = route[:, 0:2].astype(jnp.int32)
    rank = route[:, 4:6].astype(jnp.int32)
    sel = e[:, :, None] == jnp.arange(N_EXPERTS, dtype=jnp.int32)[None, None, :]
    slots = jnp.sum(jnp.where(sel, base_tok[:, None, :], 0), axis=-1) + rank
    n_valid = (pad_end[-1] // bm).astype(jnp.int32)
    blk = jnp.minimum(jnp.arange(n_blocks, dtype=jnp.int32), n_valid - 1)
    block_expert = jnp.minimum(jnp.sum(pad_end[None, :] <= (blk * bm)[:, None], axis=1), N_EXPERTS - 1)
    last_blk = jnp.where(padded > 0, pad_end - bm, -1)
    tail = n_valid + jnp.arange(N_EXPERTS, dtype=jnp.int32)
    tail = jnp.where(tail < n_blocks, tail * bm, -1)
    zero_starts = jnp.concatenate([last_blk, tail]).astype(jnp.int32)
    return slots.reshape(-1), block_expert.astype(jnp.int32), n_valid.reshape(1), zero_starts, n_blocks * bm


def _stream(x, pool_hist, k_hist, v_hist, p):
    b, n, d = x.shape
    m = b * n
    past = 0 if k_hist is None else k_hist.shape[1]
    x2d = x.reshape(m, d)
    u, q, kb, vb, k, v, gp, gs = _inproj(x2d, p['norm_mix'], p['w_in'], _fit(m, INPROJ_ROWS))
    dp = u.shape[1]
    fused_pool = pool_hist is None and n % MIX_ROWS == 0
    if fused_pool:
        pool_in = u
    else:
        hist = jnp.zeros((b, POOL_HIST, dp), F32) if pool_hist is None else pool_hist
        pool_in = _pool_diff(hist, u.reshape(b, n, dp), past, _fit(n, POOL_ROWS)).reshape(m, dp)
    shp = (b, n, kb.shape[1])
    if k_hist is None:
        o = _attention_self(q.reshape(shp), kb.reshape(shp), vb.reshape(shp), ATTN_TILE, ATTN_PAIRS)
    else:
        o = _attention_recent_first(q.reshape(shp), kb.reshape(shp), vb.reshape(shp), k_hist, v_hist, ATTN_TILE,
                                    CACHE_FIRST_BLOCKS)
    o = o.reshape(m, -1)
    x1, hp, route, cnt = _mix(pool_in, o, gp, gs, x2d, p['w_pool'], p['pool_scale'], p['w_sb_out'], p['w_out'],
                              p['norm_ffn'], p['w_r'], p['b_r'], MIX_ROWS, n if fused_pool else 0)
    slots, block_expert, n_valid, zero_starts, n_slots = _routing_tables(route, cnt, MIX_ROWS, EXPERT_ROWS)
    xs = _dispatch(slots, zero_starts, hp, n_slots, _fit(m, DISPATCH_ROWS), EXPERT_ROWS)
    y = _ffn(block_expert, n_valid, xs, p['w_g'], p['w_u'], p['w_d'], EXPERT_ROWS)
    out = _final(slots, x1, route, p['norm_final'], y, FINAL_ROWS)
    return out.reshape(b, n, d), u.reshape(b, n, dp), k, v


def kernel(x_prompt, x_sample, cache_sb_k, cache_sb_v, state_pool, norm_mix, w_in, w_pool, pool_scale, w_sb_out,
           w_out, norm_ffn, w_router_group, b_router_group, w_router_expert, b_router_expert, w_exp_gate,
           w_exp_up, w_exp_down, norm_final):
    depth = w_in.shape[0]
    assert depth == 1
    bp, sp, d = x_prompt.shape
    bs, ss, _ = x_sample.shape
    heads, hd = cache_sb_k.shape[3], cache_sb_k.shape[4]
    assert hd == HEAD_DIM
    dp = state_pool.shape[3]
    n_state = state_pool.shape[2]

    w_r = jnp.concatenate([w_router_group[0], w_router_expert[0]], axis=1)
    w_r = jnp.pad(w_r, ((0, 0), (0, LANES - w_r.shape[1])))
    w_r_hi = w_r.astype(BF16)
    b_r = jnp.concatenate([b_router_group[0], b_router_expert[0]])
    p = dict(
        norm_mix=norm_mix[0][None, :], w_in=w_in[0].astype(BF16), w_pool=w_pool[0].astype(BF16),
        pool_scale=pool_scale[0][None, :], w_sb_out=w_sb_out[0].astype(BF16), w_out=w_out[0].astype(BF16),
        norm_ffn=norm_ffn[0][None, :],
        w_r=jnp.concatenate([w_r_hi, (w_r - w_r_hi.astype(F32)).astype(BF16)], axis=1),
        b_r=jnp.pad(b_r, (0, LANES - b_r.shape[0]))[None, :].astype(F32),
        w_g=w_exp_gate[0].astype(BF16), w_u=w_exp_up[0].astype(BF16), w_d=w_exp_down[0].astype(BF16),
        norm_final=norm_final[None, :])

    hist_p = jnp.zeros((bp, POOL_HIST, dp), F32)
    yp, up, kp, vp = _stream(x_prompt, None, None, None, p)
    hist_s = jnp.pad(state_pool[0], ((0, 0), (POOL_HIST - n_state, 0), (0, 0)))
    ys, us, ks, vs = _stream(x_sample, hist_s, cache_sb_k[0], cache_sb_v[0], p)

    def pool_state(hist, u):
        return jnp.concatenate([hist[:, POOL_HIST - n_state:], u], axis=1)[:, -n_state:][None]

    return (yp, ys,
            kp.reshape(1, bp, sp, heads, hd), vp.reshape(1, bp, sp, heads, hd), pool_state(hist_p, up),
            ks.reshape(1, bs, ss, heads, hd), vs.reshape(1, bs, ss, heads, hd), pool_state(hist_s, us))
```
